```python
import jax, jax.numpy as jnp
from jax import lax
import numpy as np

D_MODEL = 2048
BATCH = 4
SEQ = 8192
DEPTH = 2
DEC_BATCH = 16
DEC_SEQ = 16
PAST_LEN = 1024

CHUNK = 64
CONV_WIDTH = 3
CONV_DIM = D_MODEL // 2
N_HEADS = 8
DV = (D_MODEL // 2) // N_HEADS
DQK = DV // 2
MLSTM_DIM = N_HEADS * DV
QK_DIM = N_HEADS * DQK
N_GROUPS = 4
EXPERTS_PER_GROUP = 8
N_EXPERTS = N_GROUPS * EXPERTS_PER_GROUP
TOP_K = 2
D_EXPERT = D_MODEL // 4
MOE_BLOCK = 128
EPS = 1e-6
M_INIT = -1e30
IN_SIZES = (CONV_DIM, CONV_DIM, CONV_DIM, QK_DIM, QK_DIM, MLSTM_DIM, MLSTM_DIM, N_HEADS, N_HEADS)
D_IN = 3 * CONV_DIM + 2 * QK_DIM + 2 * MLSTM_DIM + 2 * N_HEADS

kernel_name = "hymba_conv_mlstm_hiermoe_stream_step"


def rmsnorm(x, g):
    xf = x.astype(jnp.float32)
    y = xf * lax.rsqrt(jnp.mean(xf * xf, axis=-1, keepdims=True) + EPS)
    return (y * g.astype(jnp.float32)).astype(x.dtype)


def short_conv(u, gc, gb, w, state):
    z = gc * u
    zp = jnp.concatenate([state.astype(z.dtype), z], axis=1)
    T = z.shape[1]
    y = zp[:, 0:T] * w[0]
    for j in range(1, CONV_WIDTH):
        y = y + zp[:, j:j + T] * w[j]
    return gb * y, zp[:, -(CONV_WIDTH - 1):]


def mlstm_chunk(carry, inp):
    C, n, m = carry
    q, k, v, li, lf = inp
    L = q.shape[2]
    F = jnp.cumsum(lf, axis=-1)
    a = m[..., None] + F
    D = F[..., :, None] - F[..., None, :] + li[..., None, :]
    causal = jnp.tril(jnp.ones((L, L), dtype=bool))
    D = jnp.where(causal, D, -jnp.inf)
    m_t = jnp.maximum(a, jnp.max(D, axis=-1))
    w_inter = jnp.exp(a - m_t)
    w_intra = jnp.exp(D - m_t[..., None])
    s = jnp.einsum('bhtk,bhsk->bhts', q, k) * w_intra
    num = w_inter[..., None] * jnp.einsum('bhtk,bhkv->bhtv', q, C) + jnp.einsum('bhts,bhsv->bhtv', s, v)
    den = w_inter * jnp.einsum('bhtk,bhk->bht', q, n) + jnp.sum(s, axis=-1)
    h = num / jnp.maximum(jnp.abs(den), jnp.exp(-m_t))[..., None]
    g_inter = w_inter[..., -1]
    g_s = w_intra[..., -1, :]
    C_new = g_inter[..., None, None] * C + jnp.einsum('bhs,bhsk,bhsv->bhkv', g_s, k, v)
    n_new = g_inter[..., None] * n + jnp.einsum('bhs,bhsk->bhk', g_s, k)
    return (C_new, n_new, m_t[..., -1]), h


def mlstm_seq(q, k, v, li, lf, C0, n0, m0):
    qh, kh, vh = q.transpose(0, 2, 1, 3), k.transpose(0, 2, 1, 3), v.transpose(0, 2, 1, 3)
    lih, lfh = li.transpose(0, 2, 1), lf.transpose(0, 2, 1)
    carry0 = (C0.astype(jnp.float32), n0.astype(jnp.float32), m0.astype(jnp.float32))
    Bsz, H, T = lih.shape
    if T <= CHUNK:
        carry, h = mlstm_chunk(carry0, (qh, kh, vh, lih, lfh))
    else:
        nc = T // CHUNK

        def to_chunks(a):
            return jnp.moveaxis(a.reshape(a.shape[:2] + (nc, CHUNK) + a.shape[3:]), 2, 0)

        carry, h = lax.scan(mlstm_chunk, carry0, tuple(to_chunks(a) for a in (qh, kh, vh, lih, lfh)))
        h = jnp.moveaxis(h, 0, 2).reshape(Bsz, H, T, DV)
    return h.transpose(0, 2, 1, 3), carry


def mixer(xn, conv_state, C0, n0, m0, w_in, b_if, conv_w, head_gain, w_out):
    Bsz, T, _ = xn.shape
    proj = xn @ w_in
    cuts, acc = [], 0
    for sz in IN_SIZES[:-1]:
        acc += sz
        cuts.append(acc)
    u, gc, gb, q, k, v, o, ig, fg = jnp.split(proj, cuts, axis=-1)
    y_conv, conv_new = short_conv(u, gc, gb, conv_w, conv_state)
    gates = jnp.concatenate([ig, fg], axis=-1).astype(jnp.float32) + b_if.astype(jnp.float32)
    li = gates[..., :N_HEADS]
    lf = jax.nn.log_sigmoid(gates[..., N_HEADS:])
    q = q.reshape(Bsz, T, N_HEADS, DQK).astype(jnp.float32)
    k = k.reshape(Bsz, T, N_HEADS, DQK).astype(jnp.float32) * (DQK ** -0.5)
    v = v.reshape(Bsz, T, N_HEADS, DV).astype(jnp.float32)
    h, (Cn, nn_, mn) = mlstm_seq(q, k, v, li, lf, C0, n0, m0)
    h = h * lax.rsqrt(jnp.mean(h * h, axis=-1, keepdims=True) + EPS)
    h = h.reshape(Bsz, T, MLSTM_DIM) * head_gain.astype(jnp.float32)
    h = (h * jax.nn.sigmoid(o.astype(jnp.float32))).astype(xn.dtype)
    y = jnp.concatenate([y_conv, h], axis=-1) @ w_out
    return y, conv_new, Cn, nn_, mn


def hier_moe(xn, wg, bg, we, be, w1, w3, w2):
    Bsz, T, D = xn.shape
    xf = xn.reshape(-1, D)
    N = xf.shape[0]
    rows = jnp.arange(N)
    glog = (xf @ wg).astype(jnp.float32) + bg.astype(jnp.float32)
    pg = jax.nn.softmax(glog, axis=-1)
    gi = jnp.argmax(glog, axis=-1)
    elog = ((xf @ we).astype(jnp.float32) + be.astype(jnp.float32)).reshape(N, N_GROUPS, EXPERTS_PER_GROUP)
    top_v, top_i = lax.top_k(elog[rows, gi], TOP_K)
    gate = jax.nn.softmax(top_v, axis=-1) * pg[rows, gi][:, None]
    eidx = (gi[:, None] * EXPERTS_PER_GROUP + top_i).astype(jnp.int32)
    S = N * TOP_K
    flat_e = eidx.reshape(-1)
    order = jnp.argsort(flat_e)
    sorted_e = flat_e[order]
    counts = jnp.bincount(flat_e, length=N_EXPERTS)
    padded = (counts + MOE_BLOCK - 1) // MOE_BLOCK * MOE_BLOCK
    pad_end = jnp.cumsum(padded)
    pad_start = pad_end - padded
    start = jnp.cumsum(counts) - counts
    dest = pad_start[sorted_e] + jnp.arange(S) - start[sorted_e]
    nb = -(-S // MOE_BLOCK) + N_EXPERTS
    P = nb * MOE_BLOCK
    buf_tok = jnp.zeros((P,), jnp.int32).at[dest].set((order // TOP_K).astype(jnp.int32))
    buf_gate = jnp.zeros((P,), xn.dtype).at[dest].set(gate.reshape(-1)[order].astype(xn.dtype))
    blk_e = jnp.minimum(jnp.searchsorted(pad_end, jnp.arange(nb) * MOE_BLOCK, side='right'), N_EXPERTS - 1)

    def step(acc, blk):
        tok, g, e = blk
        xb = xf[tok]
        hb = jax.nn.silu(xb @ w1[e]) * (xb @ w3[e])
        return acc.at[tok].add((hb @ w2[e]) * g[:, None]), None

    y, _ = lax.scan(step, jnp.zeros_like(xf),
                    (buf_tok.reshape(nb, MOE_BLOCK), buf_gate.reshape(nb, MOE_BLOCK), blk_e))
    return y.reshape(Bsz, T, D)


def trunk(x, conv_s, C_s, n_s, m_s, norm_mix, w_in, b_if, conv_w, head_gain, w_out, norm_ffn,
          w_router_group, b_router_group, w_router_expert, b_router_expert, w1, w3, w2, norm_final):
    convs, Cs, ns, ms = [], [], [], []
    for l in range(DEPTH):
        y, cv, Cn, nn_, mn = mixer(rmsnorm(x, norm_mix[l]), conv_s[l], C_s[l], n_s[l], m_s[l],
                                   w_in[l], b_if[l], conv_w[l], head_gain[l], w_out[l])
        x = x + y
        x = x + hier_moe(rmsnorm(x, norm_ffn[l]), w_router_group[l], b_router_group[l],
                         w_router_expert[l], b_router_expert[l], w1[l], w3[l], w2[l])
        convs.append(cv.astype(x.dtype))
        Cs.append(Cn.astype(x.dtype))
        ns.append(nn_.astype(x.dtype))
        ms.append(mn.astype(x.dtype))
    return rmsnorm(x, norm_final), jnp.stack(convs), jnp.stack(Cs), jnp.stack(ns), jnp.stack(ms)


def setup_inputs(seed: int = 0) -> dict:
    key = jax.random.key(seed)
    ks = jax.random.split(key, 24)
    f32 = jnp.float32
    nrm = lambda k, s: jax.random.normal(k, s, f32)
    fbias = jnp.linspace(3.0, 6.0, N_HEADS, dtype=f32)[None, :] + 0.1 * nrm(ks[7], (DEPTH, N_HEADS))
    ibias = 0.1 * nrm(ks[8], (DEPTH, N_HEADS))
    return {
        "x_prompt": nrm(ks[0], (BATCH, SEQ, D_MODEL)),
        "x_sample": nrm(ks[1], (DEC_BATCH, DEC_SEQ, D_MODEL)),
        "state_conv": nrm(ks[2], (DEPTH, DEC_BATCH, CONV_WIDTH - 1, CONV_DIM)),
        "state_mlstm_C": 0.1 * nrm(ks[3], (DEPTH, DEC_BATCH, N_HEADS, DQK, DV)),
        "state_mlstm_n": 0.1 * nrm(ks[4], (DEPTH, DEC_BATCH, N_HEADS, DQK)),
        "state_mlstm_m": nrm(ks[5], (DEPTH, DEC_BATCH, N_HEADS)),
        "norm_mix": 1.0 + 0.01 * nrm(ks[6], (DEPTH, D_MODEL)),
        "w_in": nrm(ks[9], (DEPTH, D_MODEL, D_IN)) * D_MODEL ** -0.5,
        "b_if": jnp.concatenate([ibias, fbias], axis=-1),
        "conv_w": nrm(ks[10], (DEPTH, CONV_WIDTH, CONV_DIM)) * CONV_WIDTH ** -0.5,
        "head_gain": 1.0 + 0.01 * nrm(ks[11], (DEPTH, MLSTM_DIM)),
        "w_out": nrm(ks[12], (DEPTH, D_MODEL, D_MODEL)) * D_MODEL ** -0.5,
        "norm_ffn": 1.0 + 0.01 * nrm(ks[13], (DEPTH, D_MODEL)),
        "w_router_group": nrm(ks[14], (DEPTH, D_MODEL, N_GROUPS)) * D_MODEL ** -0.5,
        "b_router_group": 0.01 * nrm(ks[15], (DEPTH, N_GROUPS)),
        "w_router_expert": nrm(ks[16], (DEPTH, D_MODEL, N_EXPERTS)) * D_MODEL ** -0.5,
        "b_router_expert": 0.01 * nrm(ks[17], (DEPTH, N_EXPERTS)),
        "w1": nrm(ks[18], (DEPTH, N_EXPERTS, D_MODEL, D_EXPERT)) * D_MODEL ** -0.5,
        "w3": nrm(ks[19], (DEPTH, N_EXPERTS, D_MODEL, D_EXPERT)) * D_MODEL ** -0.5,
        "w2": nrm(ks[20], (DEPTH, N_EXPERTS, D_EXPERT, D_MODEL)) * D_EXPERT ** -0.5,
        "norm_final": 1.0 + 0.01 * nrm(ks[21], (D_MODEL,)),
    }


def reference(x_prompt, x_sample, state_conv, state_mlstm_C, state_mlstm_n, state_mlstm_m,
              norm_mix, w_in, b_if, conv_w, head_gain, w_out, norm_ffn,
              w_router_group, b_router_group, w_router_expert, b_router_expert,
              w1, w3, w2, norm_final):
    B = x_prompt.shape[0]
    conv0 = jnp.zeros((DEPTH, B, CONV_WIDTH - 1, CONV_DIM), x_prompt.dtype)
    C0 = jnp.zeros((DEPTH, B, N_HEADS, DQK, DV), jnp.float32)
    n0 = jnp.zeros((DEPTH, B, N_HEADS, DQK), jnp.float32)
    m0 = jnp.full((DEPTH, B, N_HEADS), M_INIT, jnp.float32)
    y_prompt, conv_p, C_p, n_p, m_p = trunk(
        x_prompt, conv0, C0, n0, m0, norm_mix, w_in, b_if, conv_w, head_gain, w_out, norm_ffn,
        w_router_group, b_router_group, w_router_expert, b_router_expert, w1, w3, w2, norm_final)
    y_sample, conv_s, C_s, n_s, m_s = trunk(
        x_sample, state_conv, state_mlstm_C, state_mlstm_n, state_mlstm_m,
        norm_mix, w_in, b_if, conv_w, head_gain, w_out, norm_ffn,
        w_router_group, b_router_group, w_router_expert, b_router_expert, w1, w3, w2, norm_final)
    return (y_prompt, y_sample, conv_p, C_p, n_p, m_p, conv_s, C_s, n_s, m_s)
```

```python
import functools

import jax
import jax.numpy as jnp
from jax import lax
from jax.experimental import pallas as pl
from jax.experimental.pallas import tpu as pltpu

F32 = jnp.float32
BF16 = jnp.bfloat16

D_MODEL = 2048
CONV_DIM = 1024
N_HEADS = 8
N_PAIRS = N_HEADS // 2
DV = 128
DQK = 64
QK_DIM = N_HEADS * DQK
MLSTM_DIM = N_HEADS * DV
N_GROUPS = 4
EXPERTS_PER_GROUP = 8
N_EXPERTS = 32
TOP_K = 2
D_EXPERT = 512
EPS = 1e-6
M_INIT = -1e30
CONV_WIDTH = 3
LANES = 128
MAIN_COLS = 3 * CONV_DIM + 2 * QK_DIM + 2 * MLSTM_DIM
VMEM_LIMIT = 56 * 1024 * 1024


def _cparams(sem):
    return pltpu.CompilerParams(dimension_semantics=sem, vmem_limit_bytes=VMEM_LIMIT)


def _split3(x):
    hi = x.astype(BF16)
    r1 = x - hi.astype(F32)
    mid = r1.astype(BF16)
    lo = (r1 - mid.astype(F32)).astype(BF16)
    return hi, mid, lo


def _inproj_kernel(x_ref, g_ref, w_ref, wgh_ref, wgl_ref, b_ref, proj_ref, li_ref, lf_ref,
                   xh_ref, xl_ref, *, tm, rows):
    j = pl.program_id(1)

    @pl.when(j == 0)
    def _():
        def body(r, c):
            sl = pl.ds(pl.multiple_of(r * rows, rows), rows)
            x = x_ref[sl, :]
            ms = jnp.mean(x * x, axis=-1, keepdims=True)
            xn = x * lax.rsqrt(ms + EPS) * g_ref[...]
            xh = xn.astype(BF16)
            xh_ref[sl, :] = xh
            xl_ref[sl, :] = (xn - xh.astype(F32)).astype(BF16)
            return c

        lax.fori_loop(0, tm // rows, body, 0)
        xh = xh_ref[...]
        wgh = wgh_ref[...]
        gt = (jnp.dot(xh, wgh, preferred_element_type=F32)
              + jnp.dot(xl_ref[...], wgh, preferred_element_type=F32)
              + jnp.dot(xh, wgl_ref[...], preferred_element_type=F32)) + b_ref[...]
        lane = lax.broadcasted_iota(jnp.int32, gt.shape, 1)
        valid = lane < N_HEADS
        li_ref[...] = jnp.where(valid, gt, 0.0)
        fg = pltpu.roll(gt, LANES - N_HEADS, axis=1)
        lf = jnp.minimum(fg, 0.0) - jnp.log1p(jnp.exp(-jnp.abs(fg)))
        lf_ref[...] = jnp.where(valid, lf, 0.0)

    proj_ref[...] = jnp.dot(xh_ref[...], w_ref[...], preferred_element_type=F32).astype(BF16)


def _inproj(x2d, g, w_main, wg_hi, wg_lo, b_pad):
    n = x2d.shape[0]
    tm = min(n, 1024)
    tn = 1024
    kern = functools.partial(_inproj_kernel, tm=tm, rows=32)
    return pl.pallas_call(
        kern,
        grid=(n // tm, MAIN_COLS // tn),
        in_specs=[
            pl.BlockSpec((tm, D_MODEL), lambda i, j: (i, 0)),
            pl.BlockSpec((1, D_MODEL), lambda i, j: (0, 0)),
            pl.BlockSpec((D_MODEL, tn), lambda i, j: (0, j)),
            pl.BlockSpec((D_MODEL, LANES), lambda i, j: (0, 0)),
            pl.BlockSpec((D_MODEL, LANES), lambda i, j: (0, 0)),
            pl.BlockSpec((1, LANES), lambda i, j: (0, 0)),
        ],
        out_specs=[
            pl.BlockSpec((tm, tn), lambda i, j: (i, j)),
            pl.BlockSpec((tm, LANES), lambda i, j: (i, 0)),
            pl.BlockSpec((tm, LANES), lambda i, j: (i, 0)),
        ],
        out_shape=[
            jax.ShapeDtypeStruct((n, MAIN_COLS), BF16),
            jax.ShapeDtypeStruct((n, LANES), F32),
            jax.ShapeDtypeStruct((n, LANES), F32),
        ],
        scratch_shapes=[pltpu.VMEM((tm, D_MODEL), BF16), pltpu.VMEM((tm, D_MODEL), BF16)],
        compiler_params=_cparams(("arbitrary", "arbitrary")),
        name="inproj",
    )(x2d, g, w_main, wg_hi, wg_lo, b_pad)


def _cummax_rows(x, length):
    row = lax.broadcasted_iota(jnp.int32, x.shape, 0)
    d = 1
    while d < length:
        shifted = pltpu.roll(x, d, axis=0)
        x = jnp.maximum(x, jnp.where(row >= d, shifted, -jnp.inf))
        d *= 2
    return x


def _pad_rows(x, length):
    if length == LANES:
        return x
    return jnp.concatenate([x, jnp.zeros((LANES - length, x.shape[1]), x.dtype)], axis=0)


def _mixer_kernel(u_ref, gc_ref, gb_ref, q_ref, k_ref, v_ref, o_ref, li_ref, lf_ref,
                  cw_ref, hg_ref, conv0_ref, c0_ref, n0_ref, m0_ref,
                  mix_ref, convn_ref, cn_ref, nn_ref, mn_ref,
                  c_sc, n_sc, m_sc, carry_sc, *, tb, L):
    t = pl.program_id(1)
    nt = pl.num_programs(1)

    @pl.when(t == 0)
    def _():
        zero = jnp.zeros((DQK, DV), F32)
        for p in range(N_PAIRS):
            top = jnp.concatenate([c0_ref[2 * p], zero], axis=1)
            bot = jnp.concatenate([zero, c0_ref[2 * p + 1]], axis=1)
            c_sc[p] = jnp.concatenate([top, bot], axis=0)
        n_sc[...] = n0_ref[...]
        m_sc[...] = m0_ref[...]
        carry_sc[...] = jnp.zeros(carry_sc.shape, F32)
        carry_sc[6:8, :] = conv0_ref[...]

    row = lax.broadcasted_iota(jnp.int32, (L, L), 0)
    col = lax.broadcasted_iota(jnp.int32, (L, L), 1)
    causal = col <= row
    tril = jnp.where(causal, 1.0, 0.0).astype(BF16)
    lane_l = lax.broadcasted_iota(jnp.int32, (L, LANES), 1)
    low_l = lane_l < DQK
    row_l = lax.broadcasted_iota(jnp.int32, (L, LANES), 0)
    krow = lax.broadcasted_iota(jnp.int32, (LANES, 2 * DV), 0)
    lane1 = lax.broadcasted_iota(jnp.int32, (1, LANES), 1)
    ones_v = jnp.ones((L, DV), BF16)

    def chunk(c, carry):
        rows = pl.ds(pl.multiple_of(c * L, L), L)

        cw = 256
        for g in range(CONV_DIM // cw):
            cs = slice(g * cw, (g + 1) * cw)
            z = gc_ref[rows, cs].astype(F32) * u_ref[rows, cs].astype(F32)
            prev = carry_sc[:, cs]
            p1 = prev[7:8, :]
            p2 = prev[6:7, :]
            rw = lax.broadcasted_iota(jnp.int32, (L, cw), 0)
            z1 = jnp.where(rw >= 1, pltpu.roll(z, 1, axis=0), p1)
            z2 = jnp.where(rw >= 2, pltpu.roll(z, 2, axis=0), jnp.where(rw == 1, p1, p2))
            y = z2 * cw_ref[0:1, cs] + z1 * cw_ref[1:2, cs] + z * cw_ref[2:3, cs]
            mix_ref[rows, cs] = (gb_ref[rows, cs].astype(F32) * y).astype(BF16)
            carry_sc[6:8, cs] = z[L - 2:L, :]

        li = li_ref[rows, :]
        lf = lf_ref[rows, :]
        hi, mid, lo = _split3(lf)
        F = (jnp.dot(tril, hi, preferred_element_type=F32)
             + jnp.dot(tril, mid, preferred_element_type=F32)
             + jnp.dot(tril, lo, preferred_element_type=F32))
        r = li - F
        cm = _cummax_rows(r, L)
        mprev = m_sc[...]
        mx = jnp.maximum(mprev, cm)
        M = F + mx
        neg_mx = -mx
        w_inter = jnp.exp(mprev - mx)
        em = jnp.exp(-M)
        gs = jnp.exp(r - mx[L - 1:L, :])
        g_inter = w_inter[L - 1:L, :]
        rT = _pad_rows(r, L).T[:, 0:L]
        m_sc[...] = M[L - 1:L, :]

        for p in range(N_PAIRS):
            ps = slice(p * LANES, (p + 1) * LANES)
            q2 = q_ref[rows, ps]
            k2 = k_ref[rows, ps] * jnp.asarray(DQK ** -0.5, BF16)
            q_e = jnp.where(low_l, q2, jnp.zeros_like(q2))
            q_o = jnp.where(low_l, jnp.zeros_like(q2), q2)
            q_st = jnp.concatenate([q_e, q_o], axis=0)
            n_row = n_sc[p:p + 1, :]
            n_b = jnp.broadcast_to(n_row, (LANES, LANES)).astype(BF16)
            k_aug = jnp.concatenate([n_b, k2], axis=0)
            sn = lax.dot_general(q_st, k_aug, (((1,), (1,)), ((), ())),
                                 preferred_element_type=F32)
            c_full = c_sc[p]
            qc = jnp.dot(q_st, c_full.astype(BF16), preferred_element_type=F32)

            for hh in range(2):
                h = 2 * p + hh
                rs = slice(hh * L, (hh + 1) * L)
                hs = slice(h * DV, (h + 1) * DV)
                qn = sn[rs, 0:LANES]
                s_raw = sn[rs, LANES:LANES + L]
                qch = qc[rs, hh * DV:(hh + 1) * DV]
                dmat = neg_mx[:, h:h + 1] + rT[h:h + 1, :]
                w_intra = jnp.exp(jnp.where(causal, dmat, -jnp.inf))
                s16 = (s_raw * w_intra).astype(BF16)
                v_aug = jnp.concatenate([v_ref[rows, hs], ones_v], axis=1)
                intra = jnp.dot(s16, v_aug, preferred_element_type=F32)
                wi = w_inter[:, h:h + 1]
                num = wi * qch + intra[:, 0:DV]
                den = wi * qn + intra[:, DV:2 * DV]
                hv = num / jnp.maximum(jnp.abs(den), em[:, h:h + 1])
                ms = jnp.mean(hv * hv, axis=-1, keepdims=True)
                hn = hv * lax.rsqrt(ms + EPS) * hg_ref[0:1, hs]
                og = o_ref[rows, hs].astype(F32)
                out = hn * (1.0 / (1.0 + jnp.exp(-og)))
                mix_ref[rows, CONV_DIM + h * DV:CONV_DIM + (h + 1) * DV] = out.astype(BF16)

            kgw = jnp.where(low_l, gs[:, 2 * p:2 * p + 1], gs[:, 2 * p + 1:2 * p + 2])
            kg = k2.astype(F32) * kgw
            v2 = v_ref[rows, p * 2 * DV:(p + 1) * 2 * DV]
            upd = jnp.dot(_pad_rows(kg, L).T.astype(BF16), _pad_rows(v2, L),
                          preferred_element_type=F32)
            ge = g_inter[0:1, 2 * p:2 * p + 1]
            go = g_inter[0:1, 2 * p + 1:2 * p + 2]
            c_sc[p] = jnp.where(krow < DQK, ge, go) * c_full + upd
            n_sc[p:p + 1, :] = (jnp.where(lane1 < DQK, ge, go) * n_row
                                + jnp.sum(kg, axis=0, keepdims=True))
        return carry

    lax.fori_loop(0, tb // L, chunk, 0)

    @pl.when(t == nt - 1)
    def _():
        convn_ref[...] = carry_sc[6:8, :]
        for p in range(N_PAIRS):
            cf = c_sc[p]
            cn_ref[2 * p] = cf[0:DQK, 0:DV]
            cn_ref[2 * p + 1] = cf[DQK:2 * DQK, DV:2 * DV]
        nn_ref[...] = n_sc[...]
        mn_ref[...] = m_sc[...]


def _mixer(proj, li, lf, conv_w, head_gain, conv0, c0, n0p, m0p, bsz, seq):
    n = bsz * seq
    L = min(seq, 128)
    tb = min(seq, 512)
    nt = seq // tb
    kern = functools.partial(_mixer_kernel, tb=tb, L=L)
    rb = lambda b, t: b * nt + t
    wide = lambda cb: pl.BlockSpec((tb, 1024), lambda b, t: (rb(b, t), cb))
    half = lambda cb: pl.BlockSpec((tb, 512), lambda b, t: (rb(b, t), cb))
    gate = pl.BlockSpec((tb, LANES), lambda b, t: (rb(b, t), 0))
    st = lambda *shape: pl.BlockSpec((None,) + shape, lambda b, t: (b,) + (0,) * len(shape))
    return pl.pallas_call(
        kern,
        grid=(bsz, nt),
        in_specs=[
            wide(0), wide(1), wide(2), half(6), half(7), wide(4), wide(5), gate, gate,
            pl.BlockSpec((CONV_WIDTH, CONV_DIM), lambda b, t: (0, 0)),
            pl.BlockSpec((1, MLSTM_DIM), lambda b, t: (0, 0)),
            st(2, CONV_DIM), st(N_HEADS, DQK, DV), st(N_PAIRS, LANES), st(1, LANES),
        ],
        out_specs=[
            pl.BlockSpec((tb, D_MODEL), lambda b, t: (rb(b, t), 0)),
            st(2, CONV_DIM), st(N_HEADS, DQK, DV), st(N_PAIRS, LANES), st(1, LANES),
        ],
        out_shape=[
            jax.ShapeDtypeStruct((n, D_MODEL), BF16),
            jax.ShapeDtypeStruct((bsz, 2, CONV_DIM), F32),
            jax.ShapeDtypeStruct((bsz, N_HEADS, DQK, DV), F32),
            jax.ShapeDtypeStruct((bsz, N_PAIRS, LANES), F32),
            jax.ShapeDtypeStruct((bsz, 1, LANES), F32),
        ],
        scratch_shapes=[
            pltpu.VMEM((N_PAIRS, LANES, 2 * DV), F32),
            pltpu.VMEM((N_PAIRS, LANES), F32),
            pltpu.VMEM((1, LANES), F32),
            pltpu.VMEM((8, CONV_DIM), F32),
        ],
        compiler_params=_cparams(("arbitrary", "arbitrary")),
        name="mixer",
    )(proj, proj, proj, proj, proj, proj, proj, li, lf, conv_w, head_gain, conv0, c0, n0p, m0p)


def _outproj_kernel(mix_ref, w_ref, x_ref, g_ref, wrh_ref, wrl_ref, br_ref,
                    xo_ref, xn_ref, ri_ref, rg_ref, cnt_ref, base_sc, *, tm):
    i = pl.program_id(0)

    @pl.when(i == 0)
    def _():
        base_sc[...] = jnp.zeros(base_sc.shape, F32)

    x = x_ref[...] + jnp.dot(mix_ref[...], w_ref[...], preferred_element_type=F32)
    xo_ref[...] = x
    ms = jnp.mean(x * x, axis=-1, keepdims=True)
    xn = x * lax.rsqrt(ms + EPS) * g_ref[...]
    xn_ref[...] = xn
    xh = xn.astype(BF16)
    xl = (xn - xh.astype(F32)).astype(BF16)
    wrh = wrh_ref[...]
    lg = (jnp.dot(xh, wrh, preferred_element_type=F32)
          + jnp.dot(xl, wrh, preferred_element_type=F32)
          + jnp.dot(xh, wrl_ref[...], preferred_element_type=F32)) + br_ref[...]

    lane = lax.broadcasted_iota(jnp.int32, (tm, LANES), 1)
    lanef = lane.astype(F32)
    big = jnp.float32(1e9)
    ninf = -jnp.inf
    is_g = (lane >= N_EXPERTS) & (lane < N_EXPERTS + N_GROUPS)
    glog = jnp.where(is_g, lg, ninf)
    gmax = jnp.max(glog, axis=1, keepdims=True)
    gi = jnp.min(jnp.where(glog == gmax, lanef, big), axis=1, keepdims=True) - N_EXPERTS
    pgi = 1.0 / jnp.sum(jnp.where(is_g, jnp.exp(lg - gmax), 0.0), axis=1, keepdims=True)
    lo = gi * EXPERTS_PER_GROUP
    in_grp = (lanef >= lo) & (lanef < lo + EXPERTS_PER_GROUP)
    el = jnp.where(in_grp, lg, ninf)
    v1 = jnp.max(el, axis=1, keepdims=True)
    i1 = jnp.min(jnp.where(el == v1, lanef, big), axis=1, keepdims=True)
    el2 = jnp.where(lanef == i1, ninf, el)
    v2 = jnp.max(el2, axis=1, keepdims=True)
    i2 = jnp.min(jnp.where(el2 == v2, lanef, big), axis=1, keepdims=True)
    e21 = jnp.exp(v2 - v1)
    g1 = pgi / (1.0 + e21)
    g2 = pgi * e21 / (1.0 + e21)
    sel1 = lanef == i1
    sel2 = lanef == i2
    oh = jnp.where(sel1 | sel2, 1.0, 0.0)
    r_i = lax.broadcasted_iota(jnp.int32, (tm, tm), 0)
    c_i = lax.broadcasted_iota(jnp.int32, (tm, tm), 1)
    stril = jnp.where(c_i < r_i, 1.0, 0.0).astype(BF16)
    tot = jnp.dot(stril, oh.astype(BF16), preferred_element_type=F32) + base_sc[...]
    r1 = jnp.sum(jnp.where(sel1, tot, 0.0), axis=1, keepdims=True)
    r2 = jnp.sum(jnp.where(sel2, tot, 0.0), axis=1, keepdims=True)
    base = base_sc[...] + jnp.sum(oh, axis=0, keepdims=True)
    base_sc[...] = base
    cnt_ref[...] = base
    ri = jnp.where(lane == 0, i1, jnp.where(lane == 1, i2, jnp.where(lane == 2, r1,
                   jnp.where(lane == 3, r2, 0.0))))
    ri_ref[...] = ri.astype(jnp.int32)
    rg_ref[...] = jnp.where(lane == 0, g1, jnp.where(lane == 1, g2, 0.0))


def _outproj(mix, w_out, x2d, g, wr_hi, wr_lo, br):
    n = x2d.shape[0]
    tm = min(n, 256)
    kern = functools.partial(_outproj_kernel, tm=tm)
    rowblk = lambda w: pl.BlockSpec((tm, w), lambda i: (i, 0))
    const = lambda r, c: pl.BlockSpec((r, c), lambda i: (0, 0))
    return pl.pallas_call(
        kern,
        grid=(n // tm,),
        in_specs=[rowblk(D_MODEL), const(D_MODEL, D_MODEL), rowblk(D_MODEL), const(1, D_MODEL),
                  const(D_MODEL, LANES), const(D_MODEL, LANES), const(1, LANES)],
        out_specs=[rowblk(D_MODEL), rowblk(D_MODEL), rowblk(LANES), rowblk(LANES), const(1, LANES)],
        out_shape=[
            jax.ShapeDtypeStruct((n, D_MODEL), F32),
            jax.ShapeDtypeStruct((n, D_MODEL), F32),
            jax.ShapeDtypeStruct((n, LANES), jnp.int32),
            jax.ShapeDtypeStruct((n, LANES), F32),
            jax.ShapeDtypeStruct((1, LANES), F32),
        ],
        scratch_shapes=[pltpu.VMEM((1, LANES), F32)],
        compiler_params=_cparams(("arbitrary",)),
        name="outproj",
    )(mix, w_out, x2d, g, wr_hi, wr_lo, br)


def _row_copy(src_ref, src_row, dst_ref, dst_row, sem):
    return pltpu.make_async_copy(src_ref.at[pl.ds(src_row, 1), :], dst_ref.at[pl.ds(dst_row, 1), :], sem)


def _dispatch_kernel(pad_end_ref, padded_ref, nused_ref, dest_ref, xn_ref, xs_ref, zero_sc, sem_z, sem,
                     *, tm, bm, nb):
    i = pl.program_id(0)

    @pl.when(i == 0)
    def _():
        zero_sc[...] = jnp.zeros(zero_sc.shape, F32)

        def zcopy(start):
            return pltpu.make_async_copy(zero_sc, xs_ref.at[pl.ds(pl.multiple_of(start, bm), bm), :], sem_z)

        def zstart(e, c):
            @pl.when(padded_ref[e] > 0)
            def _():
                zcopy(pad_end_ref[e] - bm).start()
            return c

        def zwait(e, c):
            @pl.when(padded_ref[e] > 0)
            def _():
                zcopy(pad_end_ref[e] - bm).wait()
            return c

        def tstart(b, c):
            zcopy(b * bm).start()
            return c

        def twait(b, c):
            zcopy(b * bm).wait()
            return c

        lax.fori_loop(0, N_EXPERTS, zstart, 0)
        lax.fori_loop(nused_ref[0], nb, tstart, 0)
        lax.fori_loop(0, N_EXPERTS, zwait, 0)
        lax.fori_loop(nused_ref[0], nb, twait, 0)

    def start(j, c):
        _row_copy(xn_ref, j, xs_ref, dest_ref[0, 0, j], sem).start()
        _row_copy(xn_ref, j, xs_ref, dest_ref[0, 0, tm + j], sem).start()
        return c

    def wait(j, c):
        _row_copy(xn_ref, j, xs_ref, dest_ref[0, 0, j], sem).wait()
        _row_copy(xn_ref, j, xs_ref, dest_ref[0, 0, tm + j], sem).wait()
        return c

    lax.fori_loop(0, tm, start, 0)
    lax.fori_loop(0, tm, wait, 0)


def _dispatch(xn, dest3, pad_end, padded, nused, p_rows, bm):
    n = xn.shape[0]
    tm = dest3.shape[2] // 2
    kern = functools.partial(_dispatch_kernel, tm=tm, bm=bm, nb=p_rows // bm)
    return pl.pallas_call(
        kern,
        grid_spec=pltpu.PrefetchScalarGridSpec(
            num_scalar_prefetch=3,
            grid=(n // tm,),
            in_specs=[
                pl.BlockSpec((1, 1, 2 * tm), lambda i, pe, pd, nu: (i, 0, 0), memory_space=pltpu.SMEM),
                pl.BlockSpec((tm, D_MODEL), lambda i, pe, pd, nu: (i, 0)),
            ],
            out_specs=pl.BlockSpec(memory_space=pl.ANY),
            scratch_shapes=[pltpu.VMEM((bm, D_MODEL), F32), pltpu.SemaphoreType.DMA(()),
                            pltpu.SemaphoreType.DMA(())],
        ),
        out_shape=jax.ShapeDtypeStruct((p_rows, D_MODEL), F32),
        compiler_params=_cparams(("arbitrary",)),
        name="dispatch",
    )(pad_end, padded, nused, dest3, xn)


def _experts_kernel(blk_e_ref, nused_ref, xs_ref, w1_ref, w3_ref, w2_ref, ys_ref):
    i = pl.program_id(0)

    @pl.when(i < nused_ref[0])
    def _():
        xb = xs_ref[...].astype(BF16)
        h1 = jnp.dot(xb, w1_ref[...], preferred_element_type=F32)
        h3 = jnp.dot(xb, w3_ref[...], preferred_element_type=F32)
        hb = (h1 * (1.0 / (1.0 + jnp.exp(-h1)))) * h3
        ys_ref[...] = jnp.dot(hb.astype(BF16), w2_ref[...], preferred_element_type=F32)

    @pl.when(i >= nused_ref[0])
    def _():
        ys_ref[...] = jnp.zeros(ys_ref.shape, F32)


def _experts(xs, blk_e, nused, w1, w3, w2, bm):
    p_rows = xs.shape[0]
    nb = p_rows // bm
    rowmap = lambda i, be, nu: (jnp.minimum(i, nu[0] - 1), 0)
    return pl.pallas_call(
        _experts_kernel,
        grid_spec=pltpu.PrefetchScalarGridSpec(
            num_scalar_prefetch=2,
            grid=(nb,),
            in_specs=[
                pl.BlockSpec((bm, D_MODEL), rowmap),
                pl.BlockSpec((None, D_MODEL, D_EXPERT), lambda i, be, nu: (be[i], 0, 0)),
                pl.BlockSpec((None, D_MODEL, D_EXPERT), lambda i, be, nu: (be[i], 0, 0)),
                pl.BlockSpec((None, D_EXPERT, D_MODEL), lambda i, be, nu: (be[i], 0, 0)),
            ],
            out_specs=pl.BlockSpec((bm, D_MODEL), lambda i, be, nu: (i, 0)),
        ),
        out_shape=jax.ShapeDtypeStruct((p_rows, D_MODEL), F32),
        compiler_params=_cparams(("arbitrary",)),
        name="experts",
    )(blk_e, nused, xs, w1, w3, w2)


def _combine_kernel(dest_ref, x_ref, rg_ref, g_ref, ys_ref, out_ref, ybuf, sem, *, tm, rows, final):
    def start(j, c):
        _row_copy(ys_ref, dest_ref[0, 0, j], ybuf.at[0], j, sem).start()
        _row_copy(ys_ref, dest_ref[0, 0, tm + j], ybuf.at[1], j, sem).start()
        return c

    def wait(j, c):
        _row_copy(ys_ref, dest_ref[0, 0, j], ybuf.at[0], j, sem).wait()
        _row_copy(ys_ref, dest_ref[0, 0, tm + j], ybuf.at[1], j, sem).wait()
        return c

    lax.fori_loop(0, tm, start, 0)
    lax.fori_loop(0, tm, wait, 0)

    def body(r, c):
        sl = pl.ds(pl.multiple_of(r * rows, rows), rows)
        rg = rg_ref[sl, :]
        x = x_ref[sl, :] + (rg[:, 0:1] * ybuf[0, sl, :] + rg[:, 1:2] * ybuf[1, sl, :])
        if final:
            ms = jnp.mean(x * x, axis=-1, keepdims=True)
            x = x * lax.rsqrt(ms + EPS) * g_ref[...]
        out_ref[sl, :] = x
        return c

    lax.fori_loop(0, tm // rows, body, 0)


def _combine(x_new, ys, dest3, rg, g_final, final):
    n = x_new.shape[0]
    tm = dest3.shape[2] // 2
    kern = functools.partial(_combine_kernel, tm=tm, rows=16, final=final)
    return pl.pallas_call(
        kern,
        grid=(n // tm,),
        in_specs=[
            pl.BlockSpec((1, 1, 2 * tm), lambda i: (i, 0, 0), memory_space=pltpu.SMEM),
            pl.BlockSpec((tm, D_MODEL), lambda i: (i, 0)),
            pl.BlockSpec((tm, LANES), lambda i: (i, 0)),
            pl.BlockSpec((1, D_MODEL), lambda i: (0, 0)),
            pl.BlockSpec(memory_space=pl.ANY),
        ],
        out_specs=pl.BlockSpec((tm, D_MODEL), lambda i: (i, 0)),
        out_shape=jax.ShapeDtypeStruct((n, D_MODEL), F32),
        scratch_shapes=[pltpu.VMEM((2, tm, D_MODEL), F32), pltpu.SemaphoreType.DMA(())],
        compiler_params=_cparams(("arbitrary",)),
        name="combine",
    )(dest3, x_new, rg, g_final, ys)


def _route_tables(ri, counts, n, tm, bm):
    cnt = counts[0, :N_EXPERTS].astype(jnp.int32)
    padded = (cnt + bm - 1) // bm * bm
    pad_end = jnp.cumsum(padded)
    pad_start = pad_end - padded
    d1 = pad_start[ri[:, 0]] + ri[:, 2]
    d2 = pad_start[ri[:, 1]] + ri[:, 3]
    dest3 = jnp.concatenate([d1.reshape(n // tm, 1, tm), d2.reshape(n // tm, 1, tm)], axis=2)
    nb = -(-(n * TOP_K) // bm) + N_EXPERTS
    nused = (pad_end[-1] // bm).astype(jnp.int32)
    blk = jnp.minimum(jnp.arange(nb, dtype=jnp.int32), nused - 1) * bm
    blk_e = jnp.minimum(jnp.searchsorted(pad_end, blk, side='right'), N_EXPERTS - 1).astype(jnp.int32)
    return dest3, pad_end.astype(jnp.int32), padded.astype(jnp.int32), blk_e, nused.reshape(1), nb * bm


def _layer(x2d, bsz, seq, conv0, c0, n0, m0, wts, g_final, final):
    n = bsz * seq
    proj, li, lf = _inproj(x2d, wts["g_mix"], wts["w_main"], wts["wg_hi"], wts["wg_lo"], wts["b_if"])
    n0p = n0.reshape(bsz, N_PAIRS, LANES)
    m0p = jnp.pad(m0, ((0, 0), (0, LANES - N_HEADS))).reshape(bsz, 1, LANES)
    mix, conv_n, c_n, n_n, m_n = _mixer(proj, li, lf, wts["conv_w"], wts["head_gain"],
                                         conv0, c0, n0p, m0p, bsz, seq)
    x_new, xn2, ri, rg, counts = _outproj(mix, wts["w_out"], x2d, wts["g_ffn"],
                                          wts["wr_hi"], wts["wr_lo"], wts["br"])
    tm = min(n, 256)
    bm = 256 if n >= 4096 else 128
    dest3, pad_end, padded, blk_e, nused, p_rows = _route_tables(ri, counts, n, tm, bm)
    xs = _dispatch(xn2, dest3, pad_end, padded, nused, p_rows, bm)
    ys = _experts(xs, blk_e, nused, wts["w1"], wts["w3"], wts["w2"], bm)
    x_out = _combine(x_new, ys, dest3, rg, g_final, final)
    return (x_out, conv_n, c_n, n_n.reshape(bsz, N_HEADS, DQK), m_n[:, 0, :N_HEADS])


def _hi_lo(w):
    hi = w.astype(BF16)
    return hi, (w - hi.astype(F32)).astype(BF16)


def _prep_weights(l, norm_mix, w_in, b_if, conv_w, head_gain, w_out, norm_ffn, w_router_group,
                  b_router_group, w_router_expert, b_router_expert, w1, w3, w2):
    wg = jnp.pad(w_in[l][:, MAIN_COLS:], ((0, 0), (0, LANES - 2 * N_HEADS)))
    wg_hi, wg_lo = _hi_lo(wg)
    wr = jnp.pad(jnp.concatenate([w_router_expert[l], w_router_group[l]], axis=1),
                 ((0, 0), (0, LANES - N_EXPERTS - N_GROUPS)))
    wr_hi, wr_lo = _hi_lo(wr)
    br = jnp.pad(jnp.concatenate([b_router_expert[l], b_router_group[l]]),
                 (0, LANES - N_EXPERTS - N_GROUPS)).reshape(1, LANES)
    return dict(
        g_mix=norm_mix[l].reshape(1, D_MODEL),
        w_main=w_in[l][:, :MAIN_COLS].astype(BF16),
        wg_hi=wg_hi, wg_lo=wg_lo,
        b_if=jnp.pad(b_if[l], (0, LANES - 2 * N_HEADS)).reshape(1, LANES),
        conv_w=conv_w[l], head_gain=head_gain[l].reshape(1, MLSTM_DIM),
        w_out=w_out[l].astype(BF16),
        g_ffn=norm_ffn[l].reshape(1, D_MODEL),
        wr_hi=wr_hi, wr_lo=wr_lo, br=br,
        w1=w1[l].astype(BF16), w3=w3[l].astype(BF16), w2=w2[l].astype(BF16),
    )


def _trunk(x, conv_s, c_s, n_s, m_s, wts, g_final):
    bsz, seq, _ = x.shape
    depth = len(wts)
    x2d = x.reshape(bsz * seq, D_MODEL)
    convs, cs, ns, ms = [], [], [], []
    for l in range(depth):
        x2d, cv, cn, nn_, mn = _layer(x2d, bsz, seq, conv_s[l], c_s[l], n_s[l], m_s[l],
                                      wts[l], g_final, l == depth - 1)
        convs.append(cv)
        cs.append(cn)
        ns.append(nn_)
        ms.append(mn)
    return (x2d.reshape(bsz, seq, D_MODEL), jnp.stack(convs), jnp.stack(cs), jnp.stack(ns), jnp.stack(ms))


def kernel(x_prompt, x_sample, state_conv, state_mlstm_C, state_mlstm_n, state_mlstm_m,
           norm_mix, w_in, b_if, conv_w, head_gain, w_out, norm_ffn,
           w_router_group, b_router_group, w_router_expert, b_router_expert,
           w1, w3, w2, norm_final):
    depth = w_in.shape[0]
    wts = [_prep_weights(l, norm_mix, w_in, b_if, conv_w, head_gain, w_out, norm_ffn, w_router_group,
                         b_router_group, w_router_expert, b_router_expert, w1, w3, w2)
           for l in range(depth)]
    g_final = norm_final.reshape(1, D_MODEL)
    b = x_prompt.shape[0]
    conv0 = jnp.zeros((depth, b, CONV_WIDTH - 1, CONV_DIM), F32)
    c0 = jnp.zeros((depth, b, N_HEADS, DQK, DV), F32)
    n0 = jnp.zeros((depth, b, N_HEADS, DQK), F32)
    m0 = jnp.full((depth, b, N_HEADS), M_INIT, F32)
    y_p, conv_p, c_p, n_p, m_p = _trunk(x_prompt, conv0, c0, n0, m0, wts, g_final)
    y_s, conv_s, c_s, n_s, m_s = _trunk(x_sample, state_conv, state_mlstm_C, state_mlstm_n,
                                        state_mlstm_m, wts, g_final)
    return (y_p, y_s, conv_p, c_p, n_p, m_p, conv_s, c_s, n_s, m_s)
```

```python
import functools

import jax
import jax.numpy as jnp
from jax import lax
from jax.experimental import pallas as pl
from jax.experimental.pallas import tpu as pltpu

F32 = jnp.float32
BF16 = jnp.bfloat16

D_MODEL = 2048
CONV_DIM = 1024
N_HEADS = 8
N_PAIRS = N_HEADS // 2
DV = 128
DQK = 64
QK_DIM = N_HEADS * DQK
MLSTM_DIM = N_HEADS * DV
N_GROUPS = 4
EXPERTS_PER_GROUP = 8
N_EXPERTS = 32
TOP_K = 2
D_EXPERT = 512
EPS = 1e-6
M_INIT = -1e30
CONV_WIDTH = 3
LANES = 128
MAIN_COLS = 3 * CONV_DIM + 2 * QK_DIM + 2 * MLSTM_DIM
ROW_WORDS = D_MODEL // 2
PROJ_BLOCK = 1024
F32_BLOCKS = 5
CHUNK = 64
VMEM_LIMIT = 56 * 1024 * 1024


def _cparams(sem):
    return pltpu.CompilerParams(dimension_semantics=sem, vmem_limit_bytes=VMEM_LIMIT)


def _pack_halves(x):
    half = x.shape[1] // 2
    lo = pltpu.bitcast(x[:, :half].astype(BF16).astype(F32), jnp.uint32)
    hi = pltpu.bitcast(x[:, half:].astype(BF16).astype(F32), jnp.uint32)
    return (lo >> 16) | (hi & jnp.uint32(0xFFFF0000))


def _unpack_halves(w):
    lo = pltpu.bitcast(w << 16, F32)
    hi = pltpu.bitcast(w & jnp.uint32(0xFFFF0000), F32)
    return lo, hi


def _split3(x):
    hi = x.astype(BF16)
    r1 = x - hi.astype(F32)
    mid = r1.astype(BF16)
    lo = (r1 - mid.astype(F32)).astype(BF16)
    return hi, mid, lo


def _inproj_kernel(x_ref, g_ref, w_ref, wg_ref, b_ref, pf_ref, pb_ref, li_ref, lf_ref, xh_ref,
                   *, tm, rows):
    j = pl.program_id(1)

    @pl.when(j == 0)
    def _():
        def body(r, c):
            sl = pl.ds(pl.multiple_of(r * rows, rows), rows)
            x = x_ref[sl, :]
            ms = jnp.mean(x * x, axis=-1, keepdims=True)
            xh_ref[sl, :] = (x * lax.rsqrt(ms + EPS) * g_ref[...]).astype(BF16)
            return c

        lax.fori_loop(0, tm // rows, body, 0)
        gt = jnp.dot(xh_ref[...], wg_ref[...], preferred_element_type=F32) + b_ref[...]
        lane = lax.broadcasted_iota(jnp.int32, gt.shape, 1)
        valid = lane < N_HEADS
        li_ref[...] = jnp.where(valid, gt, 0.0)
        fg = pltpu.roll(gt, LANES - N_HEADS, axis=1)
        lf = jnp.minimum(fg, 0.0) - jnp.log1p(jnp.exp(-jnp.abs(fg)))
        lf_ref[...] = jnp.where(valid, lf, 0.0)

    @pl.when(j < F32_BLOCKS)
    def _():
        pf_ref[...] = jnp.dot(xh_ref[...], w_ref[...], preferred_element_type=F32)

    @pl.when(j >= F32_BLOCKS)
    def _():
        pb_ref[...] = jnp.dot(xh_ref[...], w_ref[...], preferred_element_type=F32).astype(BF16)


def _inproj(x2d, g, w_main, wg, b_pad):
    n = x2d.shape[0]
    tm = min(n, 1024)
    tn = PROJ_BLOCK
    kern = functools.partial(_inproj_kernel, tm=tm, rows=32)
    return pl.pallas_call(
        kern,
        grid=(n // tm, MAIN_COLS // tn),
        in_specs=[
            pl.BlockSpec((tm, D_MODEL), lambda i, j: (i, 0)),
            pl.BlockSpec((1, D_MODEL), lambda i, j: (0, 0)),
            pl.BlockSpec((D_MODEL, tn), lambda i, j: (0, j)),
            pl.BlockSpec((D_MODEL, LANES), lambda i, j: (0, 0)),
            pl.BlockSpec((1, LANES), lambda i, j: (0, 0)),
        ],
        out_specs=[
            pl.BlockSpec((tm, tn), lambda i, j: (i, jnp.minimum(j, F32_BLOCKS - 1))),
            pl.BlockSpec((tm, tn), lambda i, j: (i, jnp.maximum(j - F32_BLOCKS, 0))),
            pl.BlockSpec((tm, LANES), lambda i, j: (i, 0)),
            pl.BlockSpec((tm, LANES), lambda i, j: (i, 0)),
        ],
        out_shape=[
            jax.ShapeDtypeStruct((n, F32_BLOCKS * tn), F32),
            jax.ShapeDtypeStruct((n, MAIN_COLS - F32_BLOCKS * tn), BF16),
            jax.ShapeDtypeStruct((n, LANES), F32),
            jax.ShapeDtypeStruct((n, LANES), F32),
        ],
        scratch_shapes=[pltpu.VMEM((tm, D_MODEL), BF16)],
        compiler_params=_cparams(("arbitrary", "arbitrary")),
        name="inproj",
    )(x2d, g, w_main, wg, b_pad)


def _cummax_rows(x, length):
    row = lax.broadcasted_iota(jnp.int32, x.shape, 0)
    d = 1
    while d < length:
        shifted = pltpu.roll(x, d, axis=0)
        x = jnp.maximum(x, jnp.where(row >= d, shifted, -jnp.inf))
        d *= 2
    return x


def _pad_rows(x, length):
    if length == LANES:
        return x
    return jnp.concatenate([x, jnp.zeros((LANES - length, x.shape[1]), x.dtype)], axis=0)


def _mixer_kernel(u_ref, gc_ref, gb_ref, q_ref, k_ref, v_ref, o_ref, li_ref, lf_ref,
                  cw_ref, hg_ref, conv0_ref, c0_ref, n0_ref, m0_ref,
                  mix_ref, convn_ref, cn_ref, nn_ref, mn_ref,
                  c_sc, n_sc, m_sc, carry_sc, *, tb, L):
    t = pl.program_id(1)
    nt = pl.num_programs(1)

    @pl.when(t == 0)
    def _():
        zero = jnp.zeros((DQK, DV), F32)
        for p in range(N_PAIRS):
            top = jnp.concatenate([c0_ref[2 * p], zero], axis=1)
            bot = jnp.concatenate([zero, c0_ref[2 * p + 1]], axis=1)
            c_sc[p] = jnp.concatenate([top, bot], axis=0)
        n_sc[...] = n0_ref[...]
        m_sc[...] = m0_ref[...]
        carry_sc[...] = jnp.zeros(carry_sc.shape, F32)
        carry_sc[6:8, :] = conv0_ref[...]

    row = lax.broadcasted_iota(jnp.int32, (L, L), 0)
    col = lax.broadcasted_iota(jnp.int32, (L, L), 1)
    causal = col <= row
    tril = jnp.where(causal, 1.0, 0.0).astype(BF16)
    lane_l = lax.broadcasted_iota(jnp.int32, (L, LANES), 1)
    low_l = lane_l < DQK
    krow = lax.broadcasted_iota(jnp.int32, (LANES, 2 * DV), 0)
    lane1 = lax.broadcasted_iota(jnp.int32, (1, LANES), 1)

    def chunk(c, carry):
        rows = pl.ds(pl.multiple_of(c * L, L), L)

        cw = 256
        for g in range(CONV_DIM // cw):
            cs = slice(g * cw, (g + 1) * cw)
            z = gc_ref[rows, cs] * u_ref[rows, cs]
            prev = carry_sc[:, cs]
            p1 = prev[7:8, :]
            p2 = prev[6:7, :]
            rw = lax.broadcasted_iota(jnp.int32, (L, cw), 0)
            z1 = jnp.where(rw >= 1, pltpu.roll(z, 1, axis=0), p1)
            z2 = jnp.where(rw >= 2, pltpu.roll(z, 2, axis=0), jnp.where(rw == 1, p1, p2))
            y = z2 * cw_ref[0:1, cs] + z1 * cw_ref[1:2, cs] + z * cw_ref[2:3, cs]
            mix_ref[rows, cs] = (gb_ref[rows, cs] * y).astype(BF16)
            carry_sc[6:8, cs] = z[L - 2:L, :]

        li = li_ref[rows, :]
        lf = lf_ref[rows, :]
        hi, mid, lo = _split3(lf)
        F = (jnp.dot(tril, hi, preferred_element_type=F32)
             + jnp.dot(tril, mid, preferred_element_type=F32)
             + jnp.dot(tril, lo, preferred_element_type=F32))
        r = li - F
        cm = _cummax_rows(r, L)
        mprev = m_sc[...]
        mx = jnp.maximum(mprev, cm)
        M = F + mx
        neg_mx = -mx
        w_inter = jnp.exp(mprev - mx)
        em = jnp.exp(-M)
        gs = jnp.exp(r - mx[L - 1:L, :])
        g_inter = w_inter[L - 1:L, :]
        rT = _pad_rows(r, L).T[:, 0:L]
        m_sc[...] = M[L - 1:L, :]

        for p in range(N_PAIRS):
            ps = slice(p * LANES, (p + 1) * LANES)
            q2 = q_ref[rows, ps].astype(BF16)
            kf = k_ref[rows, ps] * DQK ** -0.5
            k2 = kf.astype(BF16)
            q_e = jnp.where(low_l, q2, jnp.zeros_like(q2))
            q_o = jnp.where(low_l, jnp.zeros_like(q2), q2)
            q_st = jnp.concatenate([q_e, q_o], axis=0)
            n_row = n_sc[p:p + 1, :]
            n_b = jnp.broadcast_to(n_row, (LANES, LANES)).astype(BF16)
            k_aug = jnp.concatenate([n_b, k2], axis=0)
            sn = lax.dot_general(q_st, k_aug, (((1,), (1,)), ((), ())),
                                 preferred_element_type=F32)
            c_full = c_sc[p]
            qc = jnp.dot(q_st, c_full.astype(BF16), preferred_element_type=F32)

            for hh in range(2):
                h = 2 * p + hh
                rs = slice(hh * L, (hh + 1) * L)
                hs = slice(h * DV, (h + 1) * DV)
                qn = sn[rs, 0:LANES]
                s_raw = sn[rs, LANES:LANES + L]
                qch = qc[rs, hh * DV:(hh + 1) * DV]
                dmat = neg_mx[:, h:h + 1] + rT[h:h + 1, :]
                w_intra = jnp.exp(jnp.where(causal, dmat, -jnp.inf))
                s = s_raw * w_intra
                intra = jnp.dot(s.astype(BF16), v_ref[rows, hs], preferred_element_type=F32)
                wi = w_inter[:, h:h + 1]
                num = wi * qch + intra
                den = wi * qn + jnp.sum(s, axis=-1, keepdims=True)
                hv = num / jnp.maximum(jnp.abs(den), em[:, h:h + 1])
                ms = jnp.mean(hv * hv, axis=-1, keepdims=True)
                hn = hv * lax.rsqrt(ms + EPS) * hg_ref[0:1, hs]
                out = hn * (1.0 / (1.0 + jnp.exp(-o_ref[rows, hs])))
                mix_ref[rows, CONV_DIM + h * DV:CONV_DIM + (h + 1) * DV] = out.astype(BF16)

            kgw = jnp.where(low_l, gs[:, 2 * p:2 * p + 1], gs[:, 2 * p + 1:2 * p + 2])
            kg = kf * kgw
            v2 = v_ref[rows, p * 2 * DV:(p + 1) * 2 * DV]
            upd = jnp.dot(_pad_rows(kg, L).T.astype(BF16), _pad_rows(v2, L),
                          preferred_element_type=F32)
            ge = g_inter[0:1, 2 * p:2 * p + 1]
            go = g_inter[0:1, 2 * p + 1:2 * p + 2]
            c_sc[p] = jnp.where(krow < DQK, ge, go) * c_full + upd
            kn = k2.astype(F32) * kgw.astype(BF16).astype(F32)
            n_sc[p:p + 1, :] = (jnp.where(lane1 < DQK, ge, go) * n_row
                                + jnp.sum(kn, axis=0, keepdims=True))
        return carry

    lax.fori_loop(0, tb // L, chunk, 0)

    @pl.when(t == nt - 1)
    def _():
        convn_ref[...] = carry_sc[6:8, :]
        for p in range(N_PAIRS):
            cf = c_sc[p]
            cn_ref[2 * p] = cf[0:DQK, 0:DV]
            cn_ref[2 * p + 1] = cf[DQK:2 * DQK, DV:2 * DV]
        nn_ref[...] = n_sc[...]
        mn_ref[...] = m_sc[...]


def _mixer(pf, pb, li, lf, conv_w, head_gain, conv0, c0, n0p, m0p, bsz, seq):
    n = bsz * seq
    L = min(seq, CHUNK)
    tb = min(seq, 512)
    nt = seq // tb
    kern = functools.partial(_mixer_kernel, tb=tb, L=L)
    rb = lambda b, t: b * nt + t
    wide = lambda cb: pl.BlockSpec((tb, 1024), lambda b, t: (rb(b, t), cb))
    half = lambda cb: pl.BlockSpec((tb, 512), lambda b, t: (rb(b, t), cb))
    gate = pl.BlockSpec((tb, LANES), lambda b, t: (rb(b, t), 0))
    st = lambda *shape: pl.BlockSpec((None,) + shape, lambda b, t: (b,) + (0,) * len(shape))
    return pl.pallas_call(
        kern,
        grid=(bsz, nt),
        in_specs=[
            wide(0), wide(1), wide(2), half(9), half(8), wide(0), wide(3), gate, gate,
            pl.BlockSpec((CONV_WIDTH, CONV_DIM), lambda b, t: (0, 0)),
            pl.BlockSpec((1, MLSTM_DIM), lambda b, t: (0, 0)),
            st(2, CONV_DIM), st(N_HEADS, DQK, DV), st(N_PAIRS, LANES), st(1, LANES),
        ],
        out_specs=[
            pl.BlockSpec((tb, D_MODEL), lambda b, t: (rb(b, t), 0)),
            st(2, CONV_DIM), st(N_HEADS, DQK, DV), st(N_PAIRS, LANES), st(1, LANES),
        ],
        out_shape=[
            jax.ShapeDtypeStruct((n, D_MODEL), BF16),
            jax.ShapeDtypeStruct((bsz, 2, CONV_DIM), F32),
            jax.ShapeDtypeStruct((bsz, N_HEADS, DQK, DV), F32),
            jax.ShapeDtypeStruct((bsz, N_PAIRS, LANES), F32),
            jax.ShapeDtypeStruct((bsz, 1, LANES), F32),
        ],
        scratch_shapes=[
            pltpu.VMEM((N_PAIRS, LANES, 2 * DV), F32),
            pltpu.VMEM((N_PAIRS, LANES), F32),
            pltpu.VMEM((1, LANES), F32),
            pltpu.VMEM((8, CONV_DIM), F32),
        ],
        compiler_params=_cparams(("arbitrary", "arbitrary")),
        name="mixer",
    )(pf, pf, pf, pf, pf, pb, pf, li, lf, conv_w, head_gain, conv0, c0, n0p, m0p)


def _outproj_kernel(mix_ref, w_ref, x_ref, g_ref, wr_ref, br_ref,
                    xo_ref, xq_ref, ri_ref, rg_ref, cnt_ref, base_sc, *, tm):
    i = pl.program_id(0)

    @pl.when(i == 0)
    def _():
        base_sc[...] = jnp.zeros(base_sc.shape, F32)

    x = x_ref[...] + jnp.dot(mix_ref[...], w_ref[...], preferred_element_type=F32)
    xo_ref[...] = x
    ms = jnp.mean(x * x, axis=-1, keepdims=True)
    xn = x * lax.rsqrt(ms + EPS) * g_ref[...]
    xq_ref[...] = _pack_halves(xn)
    lg = jnp.dot(xn.astype(BF16), wr_ref[...], preferred_element_type=F32) + br_ref[...]

    lane = lax.broadcasted_iota(jnp.int32, (tm, LANES), 1)
    lanef = lane.astype(F32)
    big = jnp.float32(1e9)
    ninf = -jnp.inf
    is_g = (lane >= N_EXPERTS) & (lane < N_EXPERTS + N_GROUPS)
    glog = jnp.where(is_g, lg, ninf)
    gmax = jnp.max(glog, axis=1, keepdims=True)
    gi = jnp.min(jnp.where(glog == gmax, lanef, big), axis=1, keepdims=True) - N_EXPERTS
    pgi = 1.0 / jnp.sum(jnp.where(is_g, jnp.exp(lg - gmax), 0.0), axis=1, keepdims=True)
    lo = gi * EXPERTS_PER_GROUP
    in_grp = (lanef >= lo) & (lanef < lo + EXPERTS_PER_GROUP)
    el = jnp.where(in_grp, lg, ninf)
    v1 = jnp.max(el, axis=1, keepdims=True)
    i1 = jnp.min(jnp.where(el == v1, lanef, big), axis=1, keepdims=True)
    el2 = jnp.where(lanef == i1, ninf, el)
    v2 = jnp.max(el2, axis=1, keepdims=True)
    i2 = jnp.min(jnp.where(el2 == v2, lanef, big), axis=1, keepdims=True)
    e21 = jnp.exp(v2 - v1)
    g1 = pgi / (1.0 + e21)
    g2 = pgi * e21 / (1.0 + e21)
    sel1 = lanef == i1
    sel2 = lanef == i2
    oh = jnp.where(sel1 | sel2, 1.0, 0.0)
    r_i = lax.broadcasted_iota(jnp.int32, (tm, tm), 0)
    c_i = lax.broadcasted_iota(jnp.int32, (tm, tm), 1)
    stril = jnp.where(c_i < r_i, 1.0, 0.0).astype(BF16)
    tot = jnp.dot(stril, oh.astype(BF16), preferred_element_type=F32) + base_sc[...]
    r1 = jnp.sum(jnp.where(sel1, tot, 0.0), axis=1, keepdims=True)
    r2 = jnp.sum(jnp.where(sel2, tot, 0.0), axis=1, keepdims=True)
    base = base_sc[...] + jnp.sum(oh, axis=0, keepdims=True)
    base_sc[...] = base
    cnt_ref[...] = base
    ri = jnp.where(lane == 0, i1, jnp.where(lane == 1, i2, jnp.where(lane == 2, r1,
                   jnp.where(lane == 3, r2, 0.0))))
    ri_t = jnp.concatenate([ri[c * LANES:(c + 1) * LANES, :].T[0:8, :] for c in range(tm // LANES)], axis=1)
    ri_ref[...] = ri_t.astype(jnp.int32)
    rg_ref[...] = jnp.where(lane == 0, g1, jnp.where(lane == 1, g2, 0.0))


def _outproj(mix, w_out, x2d, g, wr, br):
    n = x2d.shape[0]
    tm = min(n, 256)
    kern = functools.partial(_outproj_kernel, tm=tm)
    rowblk = lambda w: pl.BlockSpec((tm, w), lambda i: (i, 0))
    const = lambda r, c: pl.BlockSpec((r, c), lambda i: (0, 0))
    return pl.pallas_call(
        kern,
        grid=(n // tm,),
        in_specs=[rowblk(D_MODEL), const(D_MODEL, D_MODEL), rowblk(D_MODEL), const(1, D_MODEL),
                  const(D_MODEL, LANES), const(1, LANES)],
        out_specs=[rowblk(D_MODEL), rowblk(D_MODEL // 2), pl.BlockSpec((None, 8, tm), lambda i: (i, 0, 0)),
                   rowblk(LANES), const(1, LANES)],
        out_shape=[
            jax.ShapeDtypeStruct((n, D_MODEL), F32),
            jax.ShapeDtypeStruct((n, D_MODEL // 2), jnp.uint32),
            jax.ShapeDtypeStruct((n // tm, 8, tm), jnp.int32),
            jax.ShapeDtypeStruct((n, LANES), F32),
            jax.ShapeDtypeStruct((1, LANES), F32),
        ],
        scratch_shapes=[pltpu.VMEM((1, LANES), F32)],
        compiler_params=_cparams(("arbitrary",)),
        name="outproj",
    )(mix, w_out, x2d, g, wr, br)


def _row_copy(src_ref, src_row, dst_ref, dst_row, sem):
    return pltpu.make_async_copy(src_ref.at[pl.ds(src_row, 1), :], dst_ref.at[pl.ds(dst_row, 1), :], sem)


def _dispatch_kernel(pad_end_ref, padded_ref, nused_ref, dest_ref, xn_ref, xs_ref, zero_sc, sem_z, sem,
                     *, tm, bm, nb):
    i = pl.program_id(0)

    @pl.when(i == 0)
    def _():
        zero_sc[...] = jnp.zeros(zero_sc.shape, jnp.uint32)

        def zcopy(start):
            return pltpu.make_async_copy(zero_sc, xs_ref.at[pl.ds(pl.multiple_of(start, bm), bm), :], sem_z)

        def zstart(e, c):
            @pl.when(padded_ref[e] > 0)
            def _():
                zcopy(pad_end_ref[e] - bm).start()
            return c

        def zwait(e, c):
            @pl.when(padded_ref[e] > 0)
            def _():
                zcopy(pad_end_ref[e] - bm).wait()
            return c

        def tstart(b, c):
            zcopy(b * bm).start()
            return c

        def twait(b, c):
            zcopy(b * bm).wait()
            return c

        lax.fori_loop(0, N_EXPERTS, zstart, 0)
        lax.fori_loop(nused_ref[0], nb, tstart, 0)
        lax.fori_loop(0, N_EXPERTS, zwait, 0)
        lax.fori_loop(nused_ref[0], nb, twait, 0)

    def start(j, c):
        _row_copy(xn_ref, j, xs_ref, dest_ref[0, 0, j], sem).start()
        _row_copy(xn_ref, j, xs_ref, dest_ref[0, 0, tm + j], sem).start()
        return c

    lax.fori_loop(0, tm, start, 0, unroll=8)
    for _ in range(TOP_K):
        pltpu.make_async_copy(xn_ref, xs_ref.at[pl.ds(0, tm), :], sem).wait()


def _dispatch(xn, dest3, pad_end, padded, nused, p_rows, bm):
    n = xn.shape[0]
    tm = dest3.shape[2] // 2
    kern = functools.partial(_dispatch_kernel, tm=tm, bm=bm, nb=p_rows // bm)
    return pl.pallas_call(
        kern,
        grid_spec=pltpu.PrefetchScalarGridSpec(
            num_scalar_prefetch=3,
            grid=(n // tm,),
            in_specs=[
                pl.BlockSpec((1, 1, 2 * tm), lambda i, pe, pd, nu: (i, 0, 0), memory_space=pltpu.SMEM),
                pl.BlockSpec((tm, ROW_WORDS), lambda i, pe, pd, nu: (i, 0)),
            ],
            out_specs=pl.BlockSpec(memory_space=pl.ANY),
            scratch_shapes=[pltpu.VMEM((bm, ROW_WORDS), jnp.uint32), pltpu.SemaphoreType.DMA(()),
                            pltpu.SemaphoreType.DMA(())],
        ),
        out_shape=jax.ShapeDtypeStruct((p_rows, ROW_WORDS), jnp.uint32),
        compiler_params=_cparams(("arbitrary",)),
        name="dispatch",
    )(pad_end, padded, nused, dest3, xn)


def _experts_kernel(blk_e_ref, nused_ref, xs_ref, w1_ref, w3_ref, w2_ref, ys_ref, w1b, w3b, w2b):
    i = pl.program_id(0)

    @pl.when((i == 0) | (blk_e_ref[i] != blk_e_ref[jnp.maximum(i - 1, 0)]))
    def _():
        def cast_in(r, c):
            sl = pl.ds(pl.multiple_of(r * 256, 256), 256)
            w1b[sl, :] = w1_ref[sl, :].astype(BF16)
            w3b[sl, :] = w3_ref[sl, :].astype(BF16)
            return c

        def cast_out(r, c):
            sl = pl.ds(pl.multiple_of(r * 64, 64), 64)
            w2b[sl, :] = w2_ref[sl, :].astype(BF16)
            return c

        lax.fori_loop(0, D_MODEL // 256, cast_in, 0)
        lax.fori_loop(0, D_EXPERT // 64, cast_out, 0)

    @pl.when(i < nused_ref[0])
    def _():
        lo, hi = _unpack_halves(xs_ref[...])
        lo = lo.astype(BF16)
        hi = hi.astype(BF16)
        h1 = (jnp.dot(lo, w1b[0:ROW_WORDS, :], preferred_element_type=F32)
              + jnp.dot(hi, w1b[ROW_WORDS:D_MODEL, :], preferred_element_type=F32))
        h3 = (jnp.dot(lo, w3b[0:ROW_WORDS, :], preferred_element_type=F32)
              + jnp.dot(hi, w3b[ROW_WORDS:D_MODEL, :], preferred_element_type=F32))
        hb = (h1 * (1.0 / (1.0 + jnp.exp(-h1)))) * h3
        ys_ref[...] = _pack_halves(jnp.dot(hb.astype(BF16), w2b[...], preferred_element_type=F32))

    @pl.when(i >= nused_ref[0])
    def _():
        ys_ref[...] = jnp.zeros(ys_ref.shape, jnp.uint32)


def _experts(xs, blk_e, nused, w1, w3, w2, bm):
    p_rows = xs.shape[0]
    nb = p_rows // bm
    rowmap = lambda i, be, nu: (jnp.minimum(i, nu[0] - 1), 0)
    return pl.pallas_call(
        _experts_kernel,
        grid_spec=pltpu.PrefetchScalarGridSpec(
            num_scalar_prefetch=2,
            grid=(nb,),
            in_specs=[
                pl.BlockSpec((bm, ROW_WORDS), rowmap),
                pl.BlockSpec((None, D_MODEL, D_EXPERT), lambda i, be, nu: (be[i], 0, 0)),
                pl.BlockSpec((None, D_MODEL, D_EXPERT), lambda i, be, nu: (be[i], 0, 0)),
                pl.BlockSpec((None, D_EXPERT, D_MODEL), lambda i, be, nu: (be[i], 0, 0)),
            ],
            out_specs=pl.BlockSpec((bm, ROW_WORDS), lambda i, be, nu: (i, 0)),
            scratch_shapes=[pltpu.VMEM((D_MODEL, D_EXPERT), BF16), pltpu.VMEM((D_MODEL, D_EXPERT), BF16),
                            pltpu.VMEM((D_EXPERT, D_MODEL), BF16)],
        ),
        out_shape=jax.ShapeDtypeStruct((p_rows, ROW_WORDS), jnp.uint32),
        compiler_params=_cparams(("arbitrary",)),
        name="experts",
    )(blk_e, nused, xs, w1, w3, w2)


def _combine_kernel(dest_ref, x_ref, rg_ref, g_ref, ys_ref, out_ref, ybuf, sem, *, tm, rows, final):
    def start(j, c):
        _row_copy(ys_ref, dest_ref[0, 0, j], ybuf.at[0], j, sem).start()
        _row_copy(ys_ref, dest_ref[0, 0, tm + j], ybuf.at[1], j, sem).start()
        return c

    lax.fori_loop(0, tm, start, 0, unroll=8)
    for k in range(TOP_K):
        pltpu.make_async_copy(ys_ref.at[pl.ds(0, tm), :], ybuf.at[k], sem).wait()

    def body(r, c):
        sl = pl.ds(pl.multiple_of(r * rows, rows), rows)
        rg = rg_ref[sl, :]
        g1 = rg[:, 0:1]
        g2 = rg[:, 1:2]
        lo1, hi1 = _unpack_halves(ybuf[0, sl, :])
        lo2, hi2 = _unpack_halves(ybuf[1, sl, :])
        xa = x_ref[sl, 0:ROW_WORDS] + (g1 * lo1 + g2 * lo2)
        xb = x_ref[sl, ROW_WORDS:D_MODEL] + (g1 * hi1 + g2 * hi2)
        if final:
            ss = jnp.sum(xa * xa, axis=-1, keepdims=True) + jnp.sum(xb * xb, axis=-1, keepdims=True)
            sc = lax.rsqrt(ss / D_MODEL + EPS)
            xa = xa * sc * g_ref[:, 0:ROW_WORDS]
            xb = xb * sc * g_ref[:, ROW_WORDS:D_MODEL]
        out_ref[sl, 0:ROW_WORDS] = xa
        out_ref[sl, ROW_WORDS:D_MODEL] = xb
        return c

    lax.fori_loop(0, tm // rows, body, 0)


def _combine(x_new, ys, dest3, rg, g_final, final):
    n = x_new.shape[0]
    tm = dest3.shape[2] // 2
    kern = functools.partial(_combine_kernel, tm=tm, rows=16, final=final)
    return pl.pallas_call(
        kern,
        grid=(n // tm,),
        in_specs=[
            pl.BlockSpec((1, 1, 2 * tm), lambda i: (i, 0, 0), memory_space=pltpu.SMEM),
            pl.BlockSpec((tm, D_MODEL), lambda i: (i, 0)),
            pl.BlockSpec((tm, LANES), lambda i: (i, 0)),
            pl.BlockSpec((1, D_MODEL), lambda i: (0, 0)),
            pl.BlockSpec(memory_space=pl.ANY),
        ],
        out_specs=pl.BlockSpec((tm, D_MODEL), lambda i: (i, 0)),
        out_shape=jax.ShapeDtypeStruct((n, D_MODEL), F32),
        scratch_shapes=[pltpu.VMEM((TOP_K, tm, ROW_WORDS), jnp.uint32), pltpu.SemaphoreType.DMA(())],
        compiler_params=_cparams(("arbitrary",)),
        name="combine",
    )(dest3, x_new, rg, g_final, ys)


def _route_tables(ri_t, counts, n, bm):
    cnt = counts[0, :N_EXPERTS].astype(jnp.int32)
    padded = (cnt + bm - 1) // bm * bm
    pad_end = jnp.cumsum(padded)
    pad_start = pad_end - padded
    e1, e2, r1, r2 = ri_t[:, 0, :], ri_t[:, 1, :], ri_t[:, 2, :], ri_t[:, 3, :]
    s1 = jnp.zeros_like(e1)
    s2 = jnp.zeros_like(e2)
    for e in range(N_EXPERTS):
        s1 = jnp.where(e1 == e, pad_start[e], s1)
        s2 = jnp.where(e2 == e, pad_start[e], s2)
    dest3 = jnp.concatenate([s1 + r1, s2 + r2], axis=1)[:, None, :]
    nb = -(-(n * TOP_K) // bm) + N_EXPERTS
    nused = (pad_end[-1] // bm).astype(jnp.int32)
    blk = jnp.minimum(jnp.arange(nb, dtype=jnp.int32), nused - 1) * bm
    blk_e = jnp.sum((pad_end[None, :] <= blk[:, None]).astype(jnp.int32), axis=1)
    blk_e = jnp.minimum(blk_e, N_EXPERTS - 1)
    return dest3, pad_end.astype(jnp.int32), padded.astype(jnp.int32), blk_e, nused.reshape(1), nb * bm


def _layer(x2d, bsz, seq, conv0, c0, n0, m0, wts, g_final, final):
    n = bsz * seq
    pf, pb, li, lf = _inproj(x2d, wts["g_mix"], wts["w_main"], wts["wg"], wts["b_if"])
    n0p = n0.reshape(bsz, N_PAIRS, LANES)
    m0p = jnp.pad(m0, ((0, 0), (0, LANES - N_HEADS))).reshape(bsz, 1, LANES)
    mix, conv_n, c_n, n_n, m_n = _mixer(pf, pb, li, lf, wts["conv_w"], wts["head_gain"],
                                         conv0, c0, n0p, m0p, bsz, seq)
    x_new, xq, ri_t, rg, counts = _outproj(mix, wts["w_out"], x2d, wts["g_ffn"], wts["wr"], wts["br"])
    bm = 256 if n >= 4096 else 128
    dest3, pad_end, padded, blk_e, nused, p_rows = _route_tables(ri_t, counts, n, bm)
    xs = _dispatch(xq, dest3, pad_end, padded, nused, p_rows, bm)
    ys = _experts(xs, blk_e, nused, wts["w1"], wts["w3"], wts["w2"], bm)
    x_out = _combine(x_new, ys, dest3, rg, g_final, final)
    return (x_out, conv_n, c_n, n_n.reshape(bsz, N_HEADS, DQK), m_n[:, 0, :N_HEADS])


def _prep_weights(l, norm_mix, w_in, b_if, conv_w, head_gain, w_out, norm_ffn, w_router_group,
                  b_router_group, w_router_expert, b_router_expert, w1, w3, w2):
    wi = w_in[l]
    o_start = 3 * CONV_DIM + 2 * QK_DIM + MLSTM_DIM
    q_start, k_start, v_start = 3 * CONV_DIM, 3 * CONV_DIM + QK_DIM, 3 * CONV_DIM + 2 * QK_DIM
    w_main = jnp.concatenate([wi[:, :q_start], wi[:, o_start:MAIN_COLS], wi[:, k_start:v_start],
                              wi[:, q_start:k_start], wi[:, v_start:o_start]], axis=1).astype(BF16)
    wg = jnp.pad(wi[:, MAIN_COLS:], ((0, 0), (0, LANES - 2 * N_HEADS))).astype(BF16)
    wr = jnp.pad(jnp.concatenate([w_router_expert[l], w_router_group[l]], axis=1),
                 ((0, 0), (0, LANES - N_EXPERTS - N_GROUPS))).astype(BF16)
    br = jnp.pad(jnp.concatenate([b_router_expert[l], b_router_group[l]]),
                 (0, LANES - N_EXPERTS - N_GROUPS)).reshape(1, LANES)
    return dict(
        g_mix=norm_mix[l].reshape(1, D_MODEL),
        w_main=w_main, wg=wg,
        b_if=jnp.pad(b_if[l], (0, LANES - 2 * N_HEADS)).reshape(1, LANES),
        conv_w=conv_w[l], head_gain=head_gain[l].reshape(1, MLSTM_DIM),
        w_out=w_out[l].astype(BF16),
        g_ffn=norm_ffn[l].reshape(1, D_MODEL),
        wr=wr, br=br,
        w1=w1[l], w3=w3[l], w2=w2[l],
    )


def _trunk(x, conv_s, c_s, n_s, m_s, wts, g_final):
    bsz, seq, _ = x.shape
    depth = len(wts)
    x2d = x.reshape(bsz * seq, D_MODEL)
    convs, cs, ns, ms = [], [], [], []
    for l in range(depth):
        x2d, cv, cn, nn_, mn = _layer(x2d, bsz, seq, conv_s[l], c_s[l], n_s[l], m_s[l],
                                      wts[l], g_final, l == depth - 1)
        convs.append(cv)
        cs.append(cn)
        ns.append(nn_)
        ms.append(mn)
    return (x2d.reshape(bsz, seq, D_MODEL), jnp.stack(convs), jnp.stack(cs), jnp.stack(ns), jnp.stack(ms))


def kernel(x_prompt, x_sample, state_conv, state_mlstm_C, state_mlstm_n, state_mlstm_m,
           norm_mix, w_in, b_if, conv_w, head_gain, w_out, norm_ffn,
           w_router_group, b_router_group, w_router_expert, b_router_expert,
           w1, w3, w2, norm_final):
    depth = w_in.shape[0]
    wts = [_prep_weights(l, norm_mix, w_in, b_if, conv_w, head_gain, w_out, norm_ffn, w_router_group,
                         b_router_group, w_router_expert, b_router_expert, w1, w3, w2)
           for l in range(depth)]
    g_final = norm_final.reshape(1, D_MODEL)
    b = x_prompt.shape[0]
    conv0 = jnp.zeros((depth, b, CONV_WIDTH - 1, CONV_DIM), F32)
    c0 = jnp.zeros((depth, b, N_HEADS, DQK, DV), F32)
    n0 = jnp.zeros((depth, b, N_HEADS, DQK), F32)
    m0 = jnp.full((depth, b, N_HEADS), M_INIT, F32)
    y_p, conv_p, c_p, n_p, m_p = _trunk(x_prompt, conv0, c0, n0, m0, wts, g_final)
    y_s, conv_s, c_s, n_s, m_s = _trunk(x_sample, state_conv, state_mlstm_C, state_mlstm_n,
                                        state_mlstm_m, wts, g_final)
    return (y_p, y_s, conv_p, c_p, n_p, m_p, conv_s, c_s, n_s, m_s)
```

```python
import functools

import jax
import jax.numpy as jnp
from jax import lax
from jax.experimental import pallas as pl
from jax.experimental.pallas import tpu as pltpu

F32 = jnp.float32
BF16 = jnp.bfloat16

D_MODEL = 2048
CONV_DIM = 1024
N_HEADS = 8
N_PAIRS = N_HEADS // 2
DV = 128
DQK = 64
QK_DIM = N_HEADS * DQK
MLSTM_DIM = N_HEADS * DV
N_GROUPS = 4
EXPERTS_PER_GROUP = 8
N_EXPERTS = 32
TOP_K = 2
D_EXPERT = 512
EPS = 1e-6
M_INIT = -1e30
CONV_WIDTH = 3
LANES = 128
MAIN_COLS = 3 * CONV_DIM + 2 * QK_DIM + 2 * MLSTM_DIM
ROW_WORDS = D_MODEL // 2
PROJ_BLOCK = 1024
F32_BLOCKS = 5
CHUNK = 64
VMEM_LIMIT = 56 * 1024 * 1024


def _cparams(sem):
    return pltpu.CompilerParams(dimension_semantics=sem, vmem_limit_bytes=VMEM_LIMIT)


def _pack_halves(x):
    half = x.shape[1] // 2
    lo = pltpu.bitcast(x[:, :half].astype(BF16).astype(F32), jnp.uint32)
    hi = pltpu.bitcast(x[:, half:].astype(BF16).astype(F32), jnp.uint32)
    return (lo >> 16) | (hi & jnp.uint32(0xFFFF0000))


def _unpack_halves(w):
    lo = pltpu.bitcast(w << 16, F32)
    hi = pltpu.bitcast(w & jnp.uint32(0xFFFF0000), F32)
    return lo, hi


def _split3(x):
    hi = x.astype(BF16)
    r1 = x - hi.astype(F32)
    mid = r1.astype(BF16)
    lo = (r1 - mid.astype(F32)).astype(BF16)
    return hi, mid, lo


def _inproj_kernel(x_ref, g_ref, w_ref, wg_ref, b_ref, pf_ref, pb_ref, li_ref, lf_ref, xh_ref,
                   *, tm, rows):
    j = pl.program_id(1)

    @pl.when(j == 0)
    def _():
        def body(r, c):
            sl = pl.ds(pl.multiple_of(r * rows, rows), rows)
            x = x_ref[sl, :]
            ms = jnp.mean(x * x, axis=-1, keepdims=True)
            xh_ref[sl, :] = (x * lax.rsqrt(ms + EPS) * g_ref[...]).astype(BF16)
            return c

        lax.fori_loop(0, tm // rows, body, 0)
        gt = jnp.dot(xh_ref[...], wg_ref[...], preferred_element_type=F32) + b_ref[...]
        lane = lax.broadcasted_iota(jnp.int32, gt.shape, 1)
        valid = lane < N_HEADS
        li_ref[...] = jnp.where(valid, gt, 0.0)
        fg = pltpu.roll(gt, LANES - N_HEADS, axis=1)
        lf = jnp.minimum(fg, 0.0) - jnp.log1p(jnp.exp(-jnp.abs(fg)))
        lf_ref[...] = jnp.where(valid, lf, 0.0)

    @pl.when(j < F32_BLOCKS)
    def _():
        pf_ref[...] = jnp.dot(xh_ref[...], w_ref[...], preferred_element_type=F32)

    @pl.when(j >= F32_BLOCKS)
    def _():
        pb_ref[...] = jnp.dot(xh_ref[...], w_ref[...], preferred_element_type=F32).astype(BF16)


def _inproj(x2d, g, w_main, wg, b_pad):
    n = x2d.shape[0]
    tm = min(n, 1024)
    tn = PROJ_BLOCK
    kern = functools.partial(_inproj_kernel, tm=tm, rows=32)
    return pl.pallas_call(
        kern,
        grid=(n // tm, MAIN_COLS // tn),
        in_specs=[
            pl.BlockSpec((tm, D_MODEL), lambda i, j: (i, 0)),
            pl.BlockSpec((1, D_MODEL), lambda i, j: (0, 0)),
            pl.BlockSpec((D_MODEL, tn), lambda i, j: (0, jnp.where(j == 3, 5, jnp.where(j > 3, j - 1, j)))),
            pl.BlockSpec((D_MODEL, LANES), lambda i, j: (0, 0)),
            pl.BlockSpec((1, LANES), lambda i, j: (0, 0)),
        ],
        out_specs=[
            pl.BlockSpec((tm, tn), lambda i, j: (i, jnp.minimum(j, F32_BLOCKS - 1))),
            pl.BlockSpec((tm, tn), lambda i, j: (i, jnp.maximum(j - F32_BLOCKS, 0))),
            pl.BlockSpec((tm, LANES), lambda i, j: (i, 0)),
            pl.BlockSpec((tm, LANES), lambda i, j: (i, 0)),
        ],
        out_shape=[
            jax.ShapeDtypeStruct((n, F32_BLOCKS * tn), F32),
            jax.ShapeDtypeStruct((n, MAIN_COLS - F32_BLOCKS * tn), BF16),
            jax.ShapeDtypeStruct((n, LANES), F32),
            jax.ShapeDtypeStruct((n, LANES), F32),
        ],
        scratch_shapes=[pltpu.VMEM((tm, D_MODEL), BF16)],
        compiler_params=_cparams(("arbitrary", "arbitrary")),
        name="inproj",
    )(x2d, g, w_main, wg, b_pad)


def _cummax_rows(x, length):
    row = lax.broadcasted_iota(jnp.int32, x.shape, 0)
    d = 1
    while d < length:
        shifted = pltpu.roll(x, d, axis=0)
        x = jnp.maximum(x, jnp.where(row >= d, shifted, -jnp.inf))
        d *= 2
    return x


def _pad_rows(x, length):
    if length == LANES:
        return x
    return jnp.concatenate([x, jnp.zeros((LANES - length, x.shape[1]), x.dtype)], axis=0)


def _mixer_kernel(u_ref, gc_ref, gb_ref, q_ref, k_ref, v_ref, o_ref, li_ref, lf_ref,
                  cw_ref, hg_ref, conv0_ref, c0_ref, n0_ref, m0_ref,
                  mix_ref, convn_ref, cn_ref, nn_ref, mn_ref,
                  c_sc, n_sc, m_sc, carry_sc, *, tb, L, G):
    t = pl.program_id(1)
    nt = pl.num_programs(1)

    @pl.when(t == 0)
    def _():
        zero = jnp.zeros((DQK, DV), F32)
        for g in range(G):
            for p in range(N_PAIRS):
                top = jnp.concatenate([c0_ref[g, 2 * p], zero], axis=1)
                bot = jnp.concatenate([zero, c0_ref[g, 2 * p + 1]], axis=1)
                c_sc[g, p] = jnp.concatenate([top, bot], axis=0)
        n_sc[...] = n0_ref[...]
        m_sc[...] = m0_ref[...]
        carry_sc[...] = jnp.zeros(carry_sc.shape, F32)
        carry_sc[:, 6:8, :] = conv0_ref[...]

    row = lax.broadcasted_iota(jnp.int32, (L, L), 0)
    col = lax.broadcasted_iota(jnp.int32, (L, L), 1)
    causal = col <= row
    tril = jnp.where(causal, 1.0, 0.0).astype(BF16)
    lane_l = lax.broadcasted_iota(jnp.int32, (L, LANES), 1)
    low_l = lane_l < DQK
    krow = lax.broadcasted_iota(jnp.int32, (LANES, 2 * DV), 0)
    lane1 = lax.broadcasted_iota(jnp.int32, (1, LANES), 1)

    def chunk(c, carry):
        rows = pl.ds(pl.multiple_of(c * L, L), L)
        for g in range(G):
            chunk_one(g, rows)
        return carry

    def chunk_one(g, rows):
        cw = 256
        for cg in range(CONV_DIM // cw):
            cs = slice(cg * cw, (cg + 1) * cw)
            z = gc_ref[g, rows, cs] * u_ref[g, rows, cs]
            prev = carry_sc[g, :, cs]
            p1 = prev[7:8, :]
            p2 = prev[6:7, :]
            rw = lax.broadcasted_iota(jnp.int32, (L, cw), 0)
            z1 = jnp.where(rw >= 1, pltpu.roll(z, 1, axis=0), p1)
            z2 = jnp.where(rw >= 2, pltpu.roll(z, 2, axis=0), jnp.where(rw == 1, p1, p2))
            y = z2 * cw_ref[0:1, cs] + z1 * cw_ref[1:2, cs] + z * cw_ref[2:3, cs]
            mix_ref[g, rows, cs] = (gb_ref[g, rows, cs] * y).astype(BF16)
            carry_sc[g, 6:8, cs] = z[L - 2:L, :]

        li = li_ref[g, rows, :]
        lf = lf_ref[g, rows, :]
        hi, mid, lo = _split3(lf)
        F = (jnp.dot(tril, hi, preferred_element_type=F32)
             + jnp.dot(tril, mid, preferred_element_type=F32)
             + jnp.dot(tril, lo, preferred_element_type=F32))
        r = li - F
        cm = _cummax_rows(r, L)
        mprev = m_sc[g]
        mx = jnp.maximum(mprev, cm)
        M = F + mx
        neg_mx = -mx
        w_inter = jnp.exp(mprev - mx)
        em = jnp.exp(-M)
        gs = jnp.exp(r - mx[L - 1:L, :])
        g_inter = w_inter[L - 1:L, :]
        rT = _pad_rows(r, L).T[:, 0:L]
        m_sc[g] = M[L - 1:L, :]

        for p in range(N_PAIRS):
            ps = slice(p * LANES, (p + 1) * LANES)
            q2 = q_ref[g, rows, ps].astype(BF16)
            kf = k_ref[g, rows, ps] * DQK ** -0.5
            k2 = kf.astype(BF16)
            q_e = jnp.where(low_l, q2, jnp.zeros_like(q2))
            q_o = jnp.where(low_l, jnp.zeros_like(q2), q2)
            q_st = jnp.concatenate([q_e, q_o], axis=0)
            n_row = n_sc[g, p:p + 1, :]
            n_b = jnp.broadcast_to(n_row, (LANES, LANES)).astype(BF16)
            k_aug = jnp.concatenate([n_b, k2], axis=0)
            sn = lax.dot_general(q_st, k_aug, (((1,), (1,)), ((), ())),
                                 preferred_element_type=F32)
            c_full = c_sc[g, p]
            qc = jnp.dot(q_st, c_full.astype(BF16), preferred_element_type=F32)

            for hh in range(2):
                h = 2 * p + hh
                rs = slice(hh * L, (hh + 1) * L)
                hs = slice(h * DV, (h + 1) * DV)
                qn = sn[rs, 0:LANES]
                s_raw = sn[rs, LANES:LANES + L]
                qch = qc[rs, hh * DV:(hh + 1) * DV]
                dmat = neg_mx[:, h:h + 1] + rT[h:h + 1, :]
                w_intra = jnp.exp(jnp.where(causal, dmat, -jnp.inf))
                s = s_raw * w_intra
                intra = jnp.dot(s.astype(BF16), v_ref[g, rows, hs], preferred_element_type=F32)
                wi = w_inter[:, h:h + 1]
                num = wi * qch + intra
                den = wi * qn + jnp.sum(s, axis=-1, keepdims=True)
                hv = num / jnp.maximum(jnp.abs(den), em[:, h:h + 1])
                ms = jnp.mean(hv * hv, axis=-1, keepdims=True)
                hn = hv * lax.rsqrt(ms + EPS) * hg_ref[0:1, hs]
                out = hn * (1.0 / (1.0 + jnp.exp(-o_ref[g, rows, hs])))
                mix_ref[g, rows, CONV_DIM + h * DV:CONV_DIM + (h + 1) * DV] = out.astype(BF16)

            kgw = jnp.where(low_l, gs[:, 2 * p:2 * p + 1], gs[:, 2 * p + 1:2 * p + 2])
            kg = kf * kgw
            v2 = v_ref[g, rows, p * 2 * DV:(p + 1) * 2 * DV]
            upd = jnp.dot(_pad_rows(kg, L).T.astype(BF16), _pad_rows(v2, L),
                          preferred_element_type=F32)
            ge = g_inter[0:1, 2 * p:2 * p + 1]
            go = g_inter[0:1, 2 * p + 1:2 * p + 2]
            c_sc[g, p] = jnp.where(krow < DQK, ge, go) * c_full + upd
            kn = k2.astype(F32) * kgw.astype(BF16).astype(F32)
            n_sc[g, p:p + 1, :] = (jnp.where(lane1 < DQK, ge, go) * n_row
                                   + jnp.sum(kn, axis=0, keepdims=True))

    lax.fori_loop(0, tb // L, chunk, 0)

    @pl.when(t == nt - 1)
    def _():
        convn_ref[...] = carry_sc[:, 6:8, :]
        for g in range(G):
            for p in range(N_PAIRS):
                cf = c_sc[g, p]
                cn_ref[g, 2 * p] = cf[0:DQK, 0:DV]
                cn_ref[g, 2 * p + 1] = cf[DQK:2 * DQK, DV:2 * DV]
        nn_ref[...] = n_sc[...]
        mn_ref[...] = m_sc[...]


def _mixer(pf, pb, li, lf, conv_w, head_gain, conv0, c0, n0p, m0p, bsz, seq):
    G = 2
    L = min(seq, CHUNK)
    tb = min(seq, 256)
    nt = seq // tb
    kern = functools.partial(_mixer_kernel, tb=tb, L=L, G=G)
    wide = lambda cb: pl.BlockSpec((G, tb, 1024), lambda b, t: (b, t, cb))
    half = lambda cb: pl.BlockSpec((G, tb, 512), lambda b, t: (b, t, cb))
    gate = pl.BlockSpec((G, tb, LANES), lambda b, t: (b, t, 0))
    st = lambda *shape: pl.BlockSpec((G,) + shape, lambda b, t: (b,) + (0,) * len(shape))
    return pl.pallas_call(
        kern,
        grid=(bsz // G, nt),
        in_specs=[
            wide(0), wide(1), wide(2), half(8), half(9), wide(0), wide(3), gate, gate,
            pl.BlockSpec((CONV_WIDTH, CONV_DIM), lambda b, t: (0, 0)),
            pl.BlockSpec((1, MLSTM_DIM), lambda b, t: (0, 0)),
            st(2, CONV_DIM), st(N_HEADS, DQK, DV), st(N_PAIRS, LANES), st(1, LANES),
        ],
        out_specs=[
            pl.BlockSpec((G, tb, D_MODEL), lambda b, t: (b, t, 0)),
            st(2, CONV_DIM), st(N_HEADS, DQK, DV), st(N_PAIRS, LANES), st(1, LANES),
        ],
        out_shape=[
            jax.ShapeDtypeStruct((bsz, seq, D_MODEL), BF16),
            jax.ShapeDtypeStruct((bsz, 2, CONV_DIM), F32),
            jax.ShapeDtypeStruct((bsz, N_HEADS, DQK, DV), F32),
            jax.ShapeDtypeStruct((bsz, N_PAIRS, LANES), F32),
            jax.ShapeDtypeStruct((bsz, 1, LANES), F32),
        ],
        scratch_shapes=[
            pltpu.VMEM((G, N_PAIRS, LANES, 2 * DV), F32),
            pltpu.VMEM((G, N_PAIRS, LANES), F32),
            pltpu.VMEM((G, 1, LANES), F32),
            pltpu.VMEM((G, 8, CONV_DIM), F32),
        ],
        compiler_params=_cparams(("arbitrary", "arbitrary")),
        name="mixer",
    )(pf, pf, pf, pf, pf, pb, pf, li, lf, conv_w, head_gain, conv0, c0, n0p, m0p)


def _outproj_kernel(mix_ref, w_ref, x_ref, g_ref, wr_ref, br_ref,
                    xo_ref, xq_ref, ri_ref, rg_ref, cnt_ref, base_sc, *, tm):
    i = pl.program_id(0)

    @pl.when(i == 0)
    def _():
        base_sc[...] = jnp.zeros(base_sc.shape, F32)

    x = x_ref[...] + jnp.dot(mix_ref[...], w_ref[...], preferred_element_type=F32)
    xo_ref[...] = x
    ms = jnp.mean(x * x, axis=-1, keepdims=True)
    xn = x * lax.rsqrt(ms + EPS) * g_ref[...]
    xq_ref[...] = _pack_halves(xn)
    lg = jnp.dot(xn.astype(BF16), wr_ref[...], preferred_element_type=F32) + br_ref[...]

    lane = lax.broadcasted_iota(jnp.int32, (tm, LANES), 1)
    lanef = lane.astype(F32)
    big = jnp.float32(1e9)
    ninf = -jnp.inf
    is_g = (lane >= N_EXPERTS) & (lane < N_EXPERTS + N_GROUPS)
    glog = jnp.where(is_g, lg, ninf)
    gmax = jnp.max(glog, axis=1, keepdims=True)
    gi = jnp.min(jnp.where(glog == gmax, lanef, big), axis=1, keepdims=True) - N_EXPERTS
    pgi = 1.0 / jnp.sum(jnp.where(is_g, jnp.exp(lg - gmax), 0.0), axis=1, keepdims=True)
    lo = gi * EXPERTS_PER_GROUP
    in_grp = (lanef >= lo) & (lanef < lo + EXPERTS_PER_GROUP)
    el = jnp.where(in_grp, lg, ninf)
    v1 = jnp.max(el, axis=1, keepdims=True)
    i1 = jnp.min(jnp.where(el == v1, lanef, big), axis=1, keepdims=True)
    el2 = jnp.where(lanef == i1, ninf, el)
    v2 = jnp.max(el2, axis=1, keepdims=True)
    i2 = jnp.min(jnp.where(el2 == v2, lanef, big), axis=1, keepdims=True)
    e21 = jnp.exp(v2 - v1)
    g1 = pgi / (1.0 + e21)
    g2 = pgi * e21 / (1.0 + e21)
    sel1 = lanef == i1
    sel2 = lanef == i2
    oh = jnp.where(sel1 | sel2, 1.0, 0.0)
    r_i = lax.broadcasted_iota(jnp.int32, (tm, tm), 0)
    c_i = lax.broadcasted_iota(jnp.int32, (tm, tm), 1)
    stril = jnp.where(c_i < r_i, 1.0, 0.0).astype(BF16)
    tot = jnp.dot(stril, oh.astype(BF16), preferred_element_type=F32) + base_sc[...]
    r1 = jnp.sum(jnp.where(sel1, tot, 0.0), axis=1, keepdims=True)
    r2 = jnp.sum(jnp.where(sel2, tot, 0.0), axis=1, keepdims=True)
    base = base_sc[...] + jnp.sum(oh, axis=0, keepdims=True)
    base_sc[...] = base
    cnt_ref[...] = base
    ri = jnp.where(lane == 0, i1, jnp.where(lane == 1, i2, jnp.where(lane == 2, r1,
                   jnp.where(lane == 3, r2, 0.0))))
    ri_t = jnp.concatenate([ri[c * LANES:(c + 1) * LANES, :].T[0:8, :] for c in range(tm // LANES)], axis=1)
    ri_ref[...] = ri_t.astype(jnp.int32)
    rg_ref[...] = jnp.where(lane == 0, g1, jnp.where(lane == 1, g2, 0.0))


def _outproj(mix, w_out, x2d, g, wr, br):
    n = x2d.shape[0]
    tm = min(n, 256)
    kern = functools.partial(_outproj_kernel, tm=tm)
    rowblk = lambda w: pl.BlockSpec((tm, w), lambda i: (i, 0))
    const = lambda r, c: pl.BlockSpec((r, c), lambda i: (0, 0))
    return pl.pallas_call(
        kern,
        grid=(n // tm,),
        in_specs=[rowblk(D_MODEL), const(D_MODEL, D_MODEL), rowblk(D_MODEL), const(1, D_MODEL),
                  const(D_MODEL, LANES), const(1, LANES)],
        out_specs=[rowblk(D_MODEL), rowblk(D_MODEL // 2), pl.BlockSpec((None, 8, tm), lambda i: (i, 0, 0)),
                   rowblk(LANES), const(1, LANES)],
        out_shape=[
            jax.ShapeDtypeStruct((n, D_MODEL), F32),
            jax.ShapeDtypeStruct((n, D_MODEL // 2), jnp.uint32),
            jax.ShapeDtypeStruct((n // tm, 8, tm), jnp.int32),
            jax.ShapeDtypeStruct((n, LANES), F32),
            jax.ShapeDtypeStruct((1, LANES), F32),
        ],
        scratch_shapes=[pltpu.VMEM((1, LANES), F32)],
        compiler_params=_cparams(("arbitrary",)),
        name="outproj",
    )(mix, w_out, x2d, g, wr, br)


def _row_copy(src_ref, src_row, dst_ref, dst_row, sem):
    return pltpu.make_async_copy(src_ref.at[pl.ds(src_row, 1), :], dst_ref.at[pl.ds(dst_row, 1), :], sem)


def _dispatch_kernel(pad_end_ref, padded_ref, nused_ref, dest_ref, xn_ref, xs_ref, zero_sc, sem_z, sem,
                     *, tm, bm, nb):
    i = pl.program_id(0)

    @pl.when(i == 0)
    def _():
        zero_sc[...] = jnp.zeros(zero_sc.shape, jnp.uint32)

        def zcopy(start):
            return pltpu.make_async_copy(zero_sc, xs_ref.at[pl.ds(pl.multiple_of(start, bm), bm), :], sem_z)

        def zstart(e, c):
            @pl.when(padded_ref[e] > 0)
            def _():
                zcopy(pad_end_ref[e] - bm).start()
            return c

        def zwait(e, c):
            @pl.when(padded_ref[e] > 0)
            def _():
                zcopy(pad_end_ref[e] - bm).wait()
            return c

        def tstart(b, c):
            zcopy(b * bm).start()
            return c

        def twait(b, c):
            zcopy(b * bm).wait()
            return c

        lax.fori_loop(0, N_EXPERTS, zstart, 0)
        lax.fori_loop(nused_ref[0], nb, tstart, 0)
        lax.fori_loop(0, N_EXPERTS, zwait, 0)
        lax.fori_loop(nused_ref[0], nb, twait, 0)

    def start(j, c):
        _row_copy(xn_ref, j, xs_ref, dest_ref[0, 0, j], sem).start()
        _row_copy(xn_ref, j, xs_ref, dest_ref[0, 0, tm + j], sem).start()
        return c

    lax.fori_loop(0, tm, start, 0, unroll=8)
    for _ in range(TOP_K):
        pltpu.make_async_copy(xn_ref, xs_ref.at[pl.ds(0, tm), :], sem).wait()


def _dispatch(xn, dest3, pad_end, padded, nused, p_rows, bm):
    n = xn.shape[0]
    tm = dest3.shape[2] // 2
    kern = functools.partial(_dispatch_kernel, tm=tm, bm=bm, nb=p_rows // bm)
    return pl.pallas_call(
        kern,
        grid_spec=pltpu.PrefetchScalarGridSpec(
            num_scalar_prefetch=3,
            grid=(n // tm,),
            in_specs=[
                pl.BlockSpec((1, 1, 2 * tm), lambda i, pe, pd, nu: (i, 0, 0), memory_space=pltpu.SMEM),
                pl.BlockSpec((tm, ROW_WORDS), lambda i, pe, pd, nu: (i, 0)),
            ],
            out_specs=pl.BlockSpec(memory_space=pl.ANY),
            scratch_shapes=[pltpu.VMEM((bm, ROW_WORDS), jnp.uint32), pltpu.SemaphoreType.DMA(()),
                            pltpu.SemaphoreType.DMA(())],
        ),
        out_shape=jax.ShapeDtypeStruct((p_rows, ROW_WORDS), jnp.uint32),
        compiler_params=_cparams(("arbitrary",)),
        name="dispatch",
    )(pad_end, padded, nused, dest3, xn)


def _experts_kernel(blk_e_ref, nused_ref, xs_ref, w1_ref, w3_ref, w2_ref, ys_ref, w1b, w3b, w2b):
    i = pl.program_id(0)

    @pl.when((i == 0) | (blk_e_ref[i] != blk_e_ref[jnp.maximum(i - 1, 0)]))
    def _():
        def cast_in(r, c):
            sl = pl.ds(pl.multiple_of(r * 256, 256), 256)
            w1b[sl, :] = w1_ref[sl, :].astype(BF16)
            w3b[sl, :] = w3_ref[sl, :].astype(BF16)
            return c

        def cast_out(r, c):
            sl = pl.ds(pl.multiple_of(r * 64, 64), 64)
            w2b[sl, :] = w2_ref[sl, :].astype(BF16)
            return c

        lax.fori_loop(0, D_MODEL // 256, cast_in, 0)
        lax.fori_loop(0, D_EXPERT // 64, cast_out, 0)

    @pl.when(i < nused_ref[0])
    def _():
        lo, hi = _unpack_halves(xs_ref[...])
        lo = lo.astype(BF16)
        hi = hi.astype(BF16)
        h1 = (jnp.dot(lo, w1b[0:ROW_WORDS, :], preferred_element_type=F32)
              + jnp.dot(hi, w1b[ROW_WORDS:D_MODEL, :], preferred_element_type=F32))
        h3 = (jnp.dot(lo, w3b[0:ROW_WORDS, :], preferred_element_type=F32)
              + jnp.dot(hi, w3b[ROW_WORDS:D_MODEL, :], preferred_element_type=F32))
        hb = (h1 * (1.0 / (1.0 + jnp.exp(-h1)))) * h3
        ys_ref[...] = _pack_halves(jnp.dot(hb.astype(BF16), w2b[...], preferred_element_type=F32))

    @pl.when(i >= nused_ref[0])
    def _():
        ys_ref[...] = jnp.zeros(ys_ref.shape, jnp.uint32)


def _experts(xs, blk_e, nused, w1, w3, w2, bm):
    p_rows = xs.shape[0]
    nb = p_rows // bm
    rowmap = lambda i, be, nu: (jnp.minimum(i, nu[0] - 1), 0)
    return pl.pallas_call(
        _experts_kernel,
        grid_spec=pltpu.PrefetchScalarGridSpec(
            num_scalar_prefetch=2,
            grid=(nb,),
            in_specs=[
                pl.BlockSpec((bm, ROW_WORDS), rowmap),
                pl.BlockSpec((None, D_MODEL, D_EXPERT), lambda i, be, nu: (be[i], 0, 0)),
                pl.BlockSpec((None, D_MODEL, D_EXPERT), lambda i, be, nu: (be[i], 0, 0)),
                pl.BlockSpec((None, D_EXPERT, D_MODEL), lambda i, be, nu: (be[i], 0, 0)),
            ],
            out_specs=pl.BlockSpec((bm, ROW_WORDS), lambda i, be, nu: (i, 0)),
            scratch_shapes=[pltpu.VMEM((D_MODEL, D_EXPERT), BF16), pltpu.VMEM((D_MODEL, D_EXPERT), BF16),
                            pltpu.VMEM((D_EXPERT, D_MODEL), BF16)],
        ),
        out_shape=jax.ShapeDtypeStruct((p_rows, ROW_WORDS), jnp.uint32),
        compiler_params=_cparams(("arbitrary",)),
        name="experts",
    )(blk_e, nused, xs, w1, w3, w2)


def _combine_kernel(dest_ref, dnext_ref, x_ref, rg_ref, g_ref, ys_ref, out_ref, ybuf, sem, *, tm, rows, final):
    i = pl.program_id(0)
    nt = pl.num_programs(0)
    slot = lax.rem(i, 2)
    other = 1 - slot

    def gather_rows(d_ref, base, buf_slot):
        for jj in range(rows):
            j = base + jj
            _row_copy(ys_ref, d_ref[0, 0, j], ybuf.at[buf_slot, 0], j, sem.at[buf_slot]).start()
            _row_copy(ys_ref, d_ref[0, 0, tm + j], ybuf.at[buf_slot, 1], j, sem.at[buf_slot]).start()

    @pl.when(i == 0)
    def _():
        def first(r, c):
            gather_rows(dest_ref, pl.multiple_of(r * rows, rows), 0)
            return c

        lax.fori_loop(0, tm // rows, first, 0)

    for k in range(TOP_K):
        pltpu.make_async_copy(ys_ref.at[pl.ds(0, tm), :], ybuf.at[slot, k], sem.at[slot]).wait()

    def combine_rows(base):
        sl = pl.ds(base, rows)
        rg = rg_ref[sl, :]
        g1 = rg[:, 0:1]
        g2 = rg[:, 1:2]
        lo1, hi1 = _unpack_halves(ybuf[slot, 0, sl, :])
        lo2, hi2 = _unpack_halves(ybuf[slot, 1, sl, :])
        xa = x_ref[sl, 0:ROW_WORDS] + (g1 * lo1 + g2 * lo2)
        xb = x_ref[sl, ROW_WORDS:D_MODEL] + (g1 * hi1 + g2 * hi2)
        if final:
            ss = jnp.sum(xa * xa, axis=-1, keepdims=True) + jnp.sum(xb * xb, axis=-1, keepdims=True)
            sc = lax.rsqrt(ss / D_MODEL + EPS)
            xa = xa * sc * g_ref[:, 0:ROW_WORDS]
            xb = xb * sc * g_ref[:, ROW_WORDS:D_MODEL]
        out_ref[sl, 0:ROW_WORDS] = xa
        out_ref[sl, ROW_WORDS:D_MODEL] = xb

    @pl.when(i + 1 < nt)
    def _():
        def body(r, c):
            base = pl.multiple_of(r * rows, rows)
            gather_rows(dnext_ref, base, other)
            combine_rows(base)
            return c

        lax.fori_loop(0, tm // rows, body, 0)

    @pl.when(i + 1 == nt)
    def _():
        def body(r, c):
            combine_rows(pl.multiple_of(r * rows, rows))
            return c

        lax.fori_loop(0, tm // rows, body, 0)


def _combine(x_new, ys, dest3, rg, g_final, final):
    n = x_new.shape[0]
    tm = dest3.shape[2] // 2
    nt = n // tm
    kern = functools.partial(_combine_kernel, tm=tm, rows=16, final=final)
    return pl.pallas_call(
        kern,
        grid=(nt,),
        in_specs=[
            pl.BlockSpec((1, 1, 2 * tm), lambda i: (i, 0, 0), memory_space=pltpu.SMEM),
            pl.BlockSpec((1, 1, 2 * tm), lambda i: (jnp.minimum(i + 1, nt - 1), 0, 0), memory_space=pltpu.SMEM),
            pl.BlockSpec((tm, D_MODEL), lambda i: (i, 0)),
            pl.BlockSpec((tm, LANES), lambda i: (i, 0)),
            pl.BlockSpec((1, D_MODEL), lambda i: (0, 0)),
            pl.BlockSpec(memory_space=pl.ANY),
        ],
        out_specs=pl.BlockSpec((tm, D_MODEL), lambda i: (i, 0)),
        out_shape=jax.ShapeDtypeStruct((n, D_MODEL), F32),
        scratch_shapes=[pltpu.VMEM((2, TOP_K, tm, ROW_WORDS), jnp.uint32), pltpu.SemaphoreType.DMA((2,))],
        compiler_params=_cparams(("arbitrary",)),
        name="combine",
    )(dest3, dest3, x_new, rg, g_final, ys)


def _route_tables(ri_t, counts, n, bm):
    cnt = counts[0, :N_EXPERTS].astype(jnp.int32)
    padded = (cnt + bm - 1) // bm * bm
    pad_end = jnp.cumsum(padded)
    pad_start = pad_end - padded
    e1, e2, r1, r2 = ri_t[:, 0, :], ri_t[:, 1, :], ri_t[:, 2, :], ri_t[:, 3, :]
    s1 = jnp.zeros_like(e1)
    s2 = jnp.zeros_like(e2)
    for e in range(N_EXPERTS):
        s1 = jnp.where(e1 == e, pad_start[e], s1)
        s2 = jnp.where(e2 == e, pad_start[e], s2)
    dest3 = jnp.concatenate([s1 + r1, s2 + r2], axis=1)[:, None, :]
    nb = -(-(n * TOP_K) // bm) + N_EXPERTS
    nused = (pad_end[-1] // bm).astype(jnp.int32)
    blk = jnp.minimum(jnp.arange(nb, dtype=jnp.int32), nused - 1) * bm
    blk_e = jnp.sum((pad_end[None, :] <= blk[:, None]).astype(jnp.int32), axis=1)
    blk_e = jnp.minimum(blk_e, N_EXPERTS - 1)
    return dest3, pad_end.astype(jnp.int32), padded.astype(jnp.int32), blk_e, nused.reshape(1), nb * bm


def _layer(x2d, bsz, seq, conv0, c0, n0, m0, wts, g_final, final):
    n = bsz * seq
    pf, pb, li, lf = _inproj(x2d, wts["g_mix"], wts["w_main"], wts["wg"], wts["b_if"])
    n0p = n0.reshape(bsz, N_PAIRS, LANES)
    m0p = jnp.pad(m0, ((0, 0), (0, LANES - N_HEADS))).reshape(bsz, 1, LANES)
    by_row = lambda a: a.reshape(bsz, seq, a.shape[-1])
    mix, conv_n, c_n, n_n, m_n = _mixer(by_row(pf), by_row(pb), by_row(li), by_row(lf), wts["conv_w"],
                                         wts["head_gain"], conv0, c0, n0p, m0p, bsz, seq)
    x_new, xq, ri_t, rg, counts = _outproj(mix.reshape(n, D_MODEL), wts["w_out"], x2d, wts["g_ffn"],
                                           wts["wr"], wts["br"])
    bm = 256 if n >= 4096 else 128
    dest3, pad_end, padded, blk_e, nused, p_rows = _route_tables(ri_t, counts, n, bm)
    xs = _dispatch(xq, dest3, pad_end, padded, nused, p_rows, bm)
    ys = _experts(xs, blk_e, nused, wts["w1"], wts["w3"], wts["w2"], bm)
    x_out = _combine(x_new, ys, dest3, rg, g_final, final)
    return (x_out, conv_n, c_n, n_n.reshape(bsz, N_HEADS, DQK), m_n[:, 0, :N_HEADS])


def _prep_weights(l, norm_mix, w_in, b_if, conv_w, head_gain, w_out, norm_ffn, w_router_group,
                  b_router_group, w_router_expert, b_router_expert, w1, w3, w2):
    wi = w_in[l]
    w_main = wi[:, :MAIN_COLS].astype(BF16)
    wg = jnp.pad(wi[:, MAIN_COLS:], ((0, 0), (0, LANES - 2 * N_HEADS))).astype(BF16)
    wr = jnp.pad(jnp.concatenate([w_router_expert[l], w_router_group[l]], axis=1),
                 ((0, 0), (0, LANES - N_EXPERTS - N_GROUPS))).astype(BF16)
    br = jnp.pad(jnp.concatenate([b_router_expert[l], b_router_group[l]]),
                 (0, LANES - N_EXPERTS - N_GROUPS)).reshape(1, LANES)
    return dict(
        g_mix=norm_mix[l].reshape(1, D_MODEL),
        w_main=w_main, wg=wg,
        b_if=jnp.pad(b_if[l], (0, LANES - 2 * N_HEADS)).reshape(1, LANES),
        conv_w=conv_w[l], head_gain=head_gain[l].reshape(1, MLSTM_DIM),
        w_out=w_out[l].astype(BF16),
        g_ffn=norm_ffn[l].reshape(1, D_MODEL),
        wr=wr, br=br,
        w1=w1[l], w3=w3[l], w2=w2[l],
    )


def _trunk(x, conv_s, c_s, n_s, m_s, wts, g_final):
    bsz, seq, _ = x.shape
    depth = len(wts)
    x2d = x.reshape(bsz * seq, D_MODEL)
    convs, cs, ns, ms = [], [], [], []
    for l in range(depth):
        x2d, cv, cn, nn_, mn = _layer(x2d, bsz, seq, conv_s[l], c_s[l], n_s[l], m_s[l],
                                      wts[l], g_final, l == depth - 1)
        convs.append(cv)
        cs.append(cn)
        ns.append(nn_)
        ms.append(mn)
    return (x2d.reshape(bsz, seq, D_MODEL), jnp.stack(convs), jnp.stack(cs), jnp.stack(ns), jnp.stack(ms))


def kernel(x_prompt, x_sample, state_conv, state_mlstm_C, state_mlstm_n, state_mlstm_m,
           norm_mix, w_in, b_if, conv_w, head_gain, w_out, norm_ffn,
           w_router_group, b_router_group, w_router_expert, b_router_expert,
           w1, w3, w2, norm_final):
    depth = w_in.shape[0]
    wts = [_prep_weights(l, norm_mix, w_in, b_if, conv_w, head_gain, w_out, norm_ffn, w_router_group,
                         b_router_group, w_router_expert, b_router_expert, w1, w3, w2)
           for l in range(depth)]
    g_final = norm_final.reshape(1, D_MODEL)
    b = x_prompt.shape[0]
    conv0 = jnp.zeros((depth, b, CONV_WIDTH - 1, CONV_DIM), F32)
    c0 = jnp.zeros((depth, b, N_HEADS, DQK, DV), F32)
    n0 = jnp.zeros((depth, b, N_HEADS, DQK), F32)
    m0 = jnp.full((depth, b, N_HEADS), M_INIT, F32)
    y_p, conv_p, c_p, n_p, m_p = _trunk(x_prompt, conv0, c0, n0, m0, wts, g_final)
    y_s, conv_s, c_s, n_s, m_s = _trunk(x_sample, state_conv, state_mlstm_C, state_mlstm_n,
                                        state_mlstm_m, wts, g_final)
    return (y_p, y_s, conv_p, c_p, n_p, m_p, conv_s, c_s, n_s, m_s)
```

```python
import functools

import jax
import jax.numpy as jnp
from jax import lax
from jax.experimental import pallas as pl
from jax.experimental.pallas import tpu as pltpu

F32 = jnp.float32
BF16 = jnp.bfloat16

D_MODEL = 2048
CONV_DIM = 1024
N_HEADS = 8
N_PAIRS = N_HEADS // 2
DV = 128
DQK = 64
QK_DIM = N_HEADS * DQK
MLSTM_DIM = N_HEADS * DV
N_GROUPS = 4
EXPERTS_PER_GROUP = 8
N_EXPERTS = 32
TOP_K = 2
D_EXPERT = 512
EPS = 1e-6
M_INIT = -1e30
CONV_WIDTH = 3
LANES = 128
MAIN_COLS = 3 * CONV_DIM + 2 * QK_DIM + 2 * MLSTM_DIM
ROW_WORDS = D_MODEL // 2
PROJ_BLOCK = 1024
F32_BLOCKS = 5
CHUNK = 64
CONV_GROUP_COLS = 256
CONV_GROUPS = CONV_DIM // CONV_GROUP_COLS
VMEM_LIMIT = 56 * 1024 * 1024


def _cparams(sem):
    return pltpu.CompilerParams(dimension_semantics=sem, vmem_limit_bytes=VMEM_LIMIT)


def _pack_halves(x):
    half = x.shape[1] // 2
    lo = pltpu.bitcast(x[:, :half].astype(BF16).astype(F32), jnp.uint32)
    hi = pltpu.bitcast(x[:, half:].astype(BF16).astype(F32), jnp.uint32)
    return (lo >> 16) | (hi & jnp.uint32(0xFFFF0000))


def _unpack_halves(w):
    lo = pltpu.bitcast(w << 16, F32)
    hi = pltpu.bitcast(w & jnp.uint32(0xFFFF0000), F32)
    return lo, hi


def _split3(x):
    hi = x.astype(BF16)
    r1 = x - hi.astype(F32)
    mid = r1.astype(BF16)
    lo = (r1 - mid.astype(F32)).astype(BF16)
    return hi, mid, lo


def _inproj_kernel(x_ref, g_ref, w_ref, wg_ref, b_ref, pf_ref, pb_ref, li_ref, lf_ref, xh_ref,
                   *, tm, rows):
    j = pl.program_id(1)

    @pl.when(j == 0)
    def _():
        def body(r, c):
            sl = pl.ds(pl.multiple_of(r * rows, rows), rows)
            x = x_ref[sl, :]
            ms = jnp.mean(x * x, axis=-1, keepdims=True)
            xh_ref[sl, :] = (x * lax.rsqrt(ms + EPS) * g_ref[...]).astype(BF16)
            return c

        lax.fori_loop(0, tm // rows, body, 0)
        gt = jnp.dot(xh_ref[...], wg_ref[...], preferred_element_type=F32) + b_ref[...]
        lane = lax.broadcasted_iota(jnp.int32, gt.shape, 1)
        valid = lane < N_HEADS
        li_ref[...] = jnp.where(valid, gt, 0.0)
        fg = pltpu.roll(gt, LANES - N_HEADS, axis=1)
        lf = jnp.minimum(fg, 0.0) - jnp.log1p(jnp.exp(-jnp.abs(fg)))
        lf_ref[...] = jnp.where(valid, lf, 0.0)

    @pl.when(j < F32_BLOCKS)
    def _():
        pf_ref[...] = jnp.dot(xh_ref[...], w_ref[...], preferred_element_type=F32)

    @pl.when(j >= F32_BLOCKS)
    def _():
        pb_ref[...] = jnp.dot(xh_ref[...], w_ref[...], preferred_element_type=F32).astype(BF16)


def _inproj(x2d, g, w_main, wg, b_pad):
    n = x2d.shape[0]
    tm = min(n, 1024)
    tn = PROJ_BLOCK
    kern = functools.partial(_inproj_kernel, tm=tm, rows=32)
    return pl.pallas_call(
        kern,
        grid=(n // tm, MAIN_COLS // tn),
        in_specs=[
            pl.BlockSpec((tm, D_MODEL), lambda i, j: (i, 0)),
            pl.BlockSpec((1, D_MODEL), lambda i, j: (0, 0)),
            pl.BlockSpec((D_MODEL, tn), lambda i, j: (0, jnp.where(j == 3, 5, jnp.where(j > 3, j - 1, j)))),
            pl.BlockSpec((D_MODEL, LANES), lambda i, j: (0, 0)),
            pl.BlockSpec((1, LANES), lambda i, j: (0, 0)),
        ],
        out_specs=[
            pl.BlockSpec((tm, tn), lambda i, j: (i, jnp.minimum(j, F32_BLOCKS - 1))),
            pl.BlockSpec((tm, tn), lambda i, j: (i, jnp.maximum(j - F32_BLOCKS, 0))),
            pl.BlockSpec((tm, LANES), lambda i, j: (i, 0)),
            pl.BlockSpec((tm, LANES), lambda i, j: (i, 0)),
        ],
        out_shape=[
            jax.ShapeDtypeStruct((n, F32_BLOCKS * tn), F32),
            jax.ShapeDtypeStruct((n, MAIN_COLS - F32_BLOCKS * tn), BF16),
            jax.ShapeDtypeStruct((n, LANES), F32),
            jax.ShapeDtypeStruct((n, LANES), F32),
        ],
        scratch_shapes=[pltpu.VMEM((tm, D_MODEL), BF16)],
        compiler_params=_cparams(("arbitrary", "arbitrary")),
        name="inproj",
    )(x2d, g, w_main, wg, b_pad)


def _cummax_rows(x, length):
    row = lax.broadcasted_iota(jnp.int32, x.shape, 0)
    d = 1
    while d < length:
        shifted = pltpu.roll(x, d, axis=0)
        x = jnp.maximum(x, jnp.where(row >= d, shifted, -jnp.inf))
        d *= 2
    return x


def _pad_rows(x, length):
    if length == LANES:
        return x
    return jnp.concatenate([x, jnp.zeros((LANES - length, x.shape[1]), x.dtype)], axis=0)


def _mixer_kernel(u_ref, gc_ref, gb_ref, q_ref, k_ref, v_ref, o_ref, li_ref, lf_ref,
                  cw_ref, hg_ref, conv0_ref, c0_ref, n0_ref, m0_ref,
                  mix_ref, convn_ref, cn_ref, nn_ref, mn_ref,
                  *scratch, tb, L, G):
    t = pl.program_id(1)
    nt = pl.num_programs(1)
    n_state = G * N_PAIRS
    c_sc = [scratch[g * N_PAIRS:(g + 1) * N_PAIRS] for g in range(G)]
    n_sc = [scratch[n_state + g * N_PAIRS:n_state + (g + 1) * N_PAIRS] for g in range(G)]
    m_sc = scratch[2 * n_state:2 * n_state + G]
    carry_sc = [scratch[2 * n_state + G + g * CONV_GROUPS:2 * n_state + G + (g + 1) * CONV_GROUPS]
                for g in range(G)]

    @pl.when(t == 0)
    def _():
        zero = jnp.zeros((DQK, DV), F32)
        for g in range(G):
            for p in range(N_PAIRS):
                top = jnp.concatenate([c0_ref[g, 2 * p], zero], axis=1)
                bot = jnp.concatenate([zero, c0_ref[g, 2 * p + 1]], axis=1)
                c_sc[g][p][...] = jnp.concatenate([top, bot], axis=0)
                n_sc[g][p][...] = n0_ref[g, p:p + 1, :]
            m_sc[g][...] = m0_ref[g]
            for cg in range(CONV_GROUPS):
                cs = slice(cg * CONV_GROUP_COLS, (cg + 1) * CONV_GROUP_COLS)
                carry_sc[g][cg][...] = jnp.zeros((8, CONV_GROUP_COLS), F32)
                carry_sc[g][cg][6:8, :] = conv0_ref[g, :, cs]

    row = lax.broadcasted_iota(jnp.int32, (L, L), 0)
    col = lax.broadcasted_iota(jnp.int32, (L, L), 1)
    causal = col <= row
    tril = jnp.where(causal, 1.0, 0.0).astype(BF16)
    lane_l = lax.broadcasted_iota(jnp.int32, (L, LANES), 1)
    low_l = lane_l < DQK
    krow = lax.broadcasted_iota(jnp.int32, (LANES, 2 * DV), 0)
    lane1 = lax.broadcasted_iota(jnp.int32, (1, LANES), 1)

    def chunk(c, carry):
        rows = pl.ds(pl.multiple_of(c * L, L), L)
        units = [(g, p) for g in range(G) for p in range(N_PAIRS)]
        heads = [(g, h) for g in range(G) for h in range(N_HEADS)]
        convs = [conv_chunk(g, rows) for g in range(G)]
        gates = [gate_algebra(g, rows) for g in range(G)]
        outs, states = {}, {}
        for g, p in units:
            pr = pair_scores(g, p, rows)
            v2 = v_ref[g, rows, p * 2 * DV:(p + 1) * 2 * DV]
            for hh in range(2):
                h = 2 * p + hh
                s = intra_weights(gates[g], pr, h)
                intra = jnp.dot(s.astype(BF16), v2[:, hh * DV:(hh + 1) * DV], preferred_element_type=F32)
                o_sig = 1.0 / (1.0 + jnp.exp(-o_ref[g, rows, h * DV:(h + 1) * DV]))
                outs[(g, h)] = head_output(gates[g], pr, s, intra, o_sig, h)
            states[(g, p)] = state_update(gates[g], pr, v2, p)
        for g in range(G):
            for cg in range(CONV_GROUPS):
                y_conv, z_tail = convs[g][cg]
                mix_ref[g, rows, cg * CONV_GROUP_COLS:(cg + 1) * CONV_GROUP_COLS] = y_conv
                carry_sc[g][cg][6:8, :] = z_tail
            m_sc[g][...] = gates[g]["m_new"]
        for g, h in heads:
            mix_ref[g, rows, CONV_DIM + h * DV:CONV_DIM + (h + 1) * DV] = outs[(g, h)]
        for g, p in units:
            c_sc[g][p][...], n_sc[g][p][...] = states[(g, p)]
        return carry

    def conv_chunk(g, rows):
        cw = CONV_GROUP_COLS
        res = []
        for cg in range(CONV_GROUPS):
            cs = slice(cg * cw, (cg + 1) * cw)
            z = gc_ref[g, rows, cs] * u_ref[g, rows, cs]
            prev = carry_sc[g][cg][...]
            p1 = prev[7:8, :]
            p2 = prev[6:7, :]
            rw = lax.broadcasted_iota(jnp.int32, (L, cw), 0)
            z1 = jnp.where(rw >= 1, pltpu.roll(z, 1, axis=0), p1)
            z2 = jnp.where(rw >= 2, pltpu.roll(z, 2, axis=0), jnp.where(rw == 1, p1, p2))
            y = z2 * cw_ref[0:1, cs] + z1 * cw_ref[1:2, cs] + z * cw_ref[2:3, cs]
            res.append(((gb_ref[g, rows, cs] * y).astype(BF16), z[L - 2:L, :]))
        return res

    def gate_algebra(g, rows):
        li = li_ref[g, rows, :]
        lf = lf_ref[g, rows, :]
        hi, mid, lo = _split3(lf)
        F = (jnp.dot(tril, hi, preferred_element_type=F32)
             + jnp.dot(tril, mid, preferred_element_type=F32)
             + jnp.dot(tril, lo, preferred_element_type=F32))
        r = li - F
        cm = _cummax_rows(r, L)
        mprev = m_sc[g][...]
        mx = jnp.maximum(mprev, cm)
        M = F + mx
        neg_mx = -mx
        w_inter = jnp.exp(mprev - mx)
        em = jnp.exp(-M)
        gs = jnp.exp(r - mx[L - 1:L, :])
        g_inter = w_inter[L - 1:L, :]
        rT = _pad_rows(r, L).T[:, 0:L]
        return dict(neg_mx=neg_mx, w_inter=w_inter, em=em, gs=gs, g_inter=g_inter, rT=rT, m_new=M[L - 1:L, :])

    def pair_scores(g, p, rows):
        ps = slice(p * LANES, (p + 1) * LANES)
        q2 = q_ref[g, rows, ps].astype(BF16)
        kf = k_ref[g, rows, ps] * DQK ** -0.5
        k2 = kf.astype(BF16)
        q_e = jnp.where(low_l, q2, jnp.zeros_like(q2))
        q_o = jnp.where(low_l, jnp.zeros_like(q2), q2)
        q_st = jnp.concatenate([q_e, q_o], axis=0)
        n_row = n_sc[g][p][...]
        n_b = jnp.broadcast_to(n_row, (LANES, LANES)).astype(BF16)
        k_aug = jnp.concatenate([n_b, k2], axis=0)
        sn = lax.dot_general(q_st, k_aug, (((1,), (1,)), ((), ())),
                             preferred_element_type=F32)
        c_full = c_sc[g][p][...]
        qc = jnp.dot(q_st, c_full.astype(BF16), preferred_element_type=F32)
        return dict(kf=kf, k2=k2, n_row=n_row, c_full=c_full, sn=sn, qc=qc)

    def intra_weights(gt, pr, h):
        rs = slice((h % 2) * L, (h % 2 + 1) * L)
        dmat = gt["neg_mx"][:, h:h + 1] + gt["rT"][h:h + 1, :]
        return pr["sn"][rs, LANES:LANES + L] * jnp.exp(jnp.where(causal, dmat, -jnp.inf))

    def head_output(gt, pr, s, intra, o_sig, h):
        hh = h % 2
        rs = slice(hh * L, (hh + 1) * L)
        hs = slice(h * DV, (h + 1) * DV)
        wi = gt["w_inter"][:, h:h + 1]
        num = wi * pr["qc"][rs, hh * DV:(hh + 1) * DV] + intra
        den = wi * pr["sn"][rs, 0:LANES] + jnp.sum(s, axis=-1, keepdims=True)
        hv = num / jnp.maximum(jnp.abs(den), gt["em"][:, h:h + 1])
        ms = jnp.mean(hv * hv, axis=-1, keepdims=True)
        hn = hv * lax.rsqrt(ms + EPS) * hg_ref[0:1, hs]
        return (hn * o_sig).astype(BF16)

    def state_update(gt, pr, v2, p):
        gs = gt["gs"]
        kgw = jnp.where(low_l, gs[:, 2 * p:2 * p + 1], gs[:, 2 * p + 1:2 * p + 2])
        kg = pr["kf"] * kgw
        upd = jnp.dot(_pad_rows(kg, L).T.astype(BF16), _pad_rows(v2, L),
                      preferred_element_type=F32)
        ge = gt["g_inter"][0:1, 2 * p:2 * p + 1]
        go = gt["g_inter"][0:1, 2 * p + 1:2 * p + 2]
        c_new = jnp.where(krow < DQK, ge, go) * pr["c_full"] + upd
        kn = pr["k2"].astype(F32) * kgw.astype(BF16).astype(F32)
        n_new = jnp.where(lane1 < DQK, ge, go) * pr["n_row"] + jnp.sum(kn, axis=0, keepdims=True)
        return c_new, n_new

    lax.fori_loop(0, tb // L, chunk, 0)

    @pl.when(t == nt - 1)
    def _():
        for g in range(G):
            for cg in range(CONV_GROUPS):
                cs = slice(cg * CONV_GROUP_COLS, (cg + 1) * CONV_GROUP_COLS)
                convn_ref[g, :, cs] = carry_sc[g][cg][6:8, :]
            for p in range(N_PAIRS):
                cf = c_sc[g][p][...]
                cn_ref[g, 2 * p] = cf[0:DQK, 0:DV]
                cn_ref[g, 2 * p + 1] = cf[DQK:2 * DQK, DV:2 * DV]
                nn_ref[g, p:p + 1, :] = n_sc[g][p][...]
            mn_ref[g] = m_sc[g][...]


def _mixer(pf, pb, li, lf, conv_w, head_gain, conv0, c0, n0p, m0p, bsz, seq):
    G = 1
    L = min(seq, CHUNK)
    tb = min(seq, 512)
    nt = seq // tb
    kern = functools.partial(_mixer_kernel, tb=tb, L=L, G=G)
    wide = lambda cb: pl.BlockSpec((G, tb, 1024), lambda b, t: (b, t, cb))
    half = lambda cb: pl.BlockSpec((G, tb, 512), lambda b, t: (b, t, cb))
    gate = pl.BlockSpec((G, tb, LANES), lambda b, t: (b, t, 0))
    st = lambda *shape: pl.BlockSpec((G,) + shape, lambda b, t: (b,) + (0,) * len(shape))
    return pl.pallas_call(
        kern,
        grid=(bsz // G, nt),
        in_specs=[
            wide(0), wide(1), wide(2), half(8), half(9), wide(0), wide(3), gate, gate,
            pl.BlockSpec((CONV_WIDTH, CONV_DIM), lambda b, t: (0, 0)),
            pl.BlockSpec((1, MLSTM_DIM), lambda b, t: (0, 0)),
            st(2, CONV_DIM), st(N_HEADS, DQK, DV), st(N_PAIRS, LANES), st(1, LANES),
        ],
        out_specs=[
            pl.BlockSpec((G, tb, D_MODEL), lambda b, t: (b, t, 0)),
            st(2, CONV_DIM), st(N_HEADS, DQK, DV), st(N_PAIRS, LANES), st(1, LANES),
        ],
        out_shape=[
            jax.ShapeDtypeStruct((bsz, seq, D_MODEL), BF16),
            jax.ShapeDtypeStruct((bsz, 2, CONV_DIM), F32),
            jax.ShapeDtypeStruct((bsz, N_HEADS, DQK, DV), F32),
            jax.ShapeDtypeStruct((bsz, N_PAIRS, LANES), F32),
            jax.ShapeDtypeStruct((bsz, 1, LANES), F32),
        ],
        scratch_shapes=(
            [pltpu.VMEM((LANES, 2 * DV), F32)] * (G * N_PAIRS)
            + [pltpu.VMEM((1, LANES), F32)] * (G * N_PAIRS)
            + [pltpu.VMEM((1, LANES), F32)] * G
            + [pltpu.VMEM((8, CONV_GROUP_COLS), F32)] * (G * CONV_GROUPS)
        ),
        compiler_params=_cparams(("arbitrary", "arbitrary")),
        name="mixer",
    )(pf, pf, pf, pf, pf, pb, pf, li, lf, conv_w, head_gain, conv0, c0, n0p, m0p)


def _outproj_kernel(mix_ref, w_ref, x_ref, g_ref, wr_ref, br_ref,
                    xo_ref, xq_ref, ri_ref, rg_ref, cnt_ref, base_sc, *, tm):
    i = pl.program_id(0)

    @pl.when(i == 0)
    def _():
        base_sc[...] = jnp.zeros(base_sc.shape, F32)

    x = x_ref[...] + jnp.dot(mix_ref[...], w_ref[...], preferred_element_type=F32)
    xo_ref[...] = x
    ms = jnp.mean(x * x, axis=-1, keepdims=True)
    xn = x * lax.rsqrt(ms + EPS) * g_ref[...]
    xq_ref[...] = _pack_halves(xn)
    lg = jnp.dot(xn.astype(BF16), wr_ref[...], preferred_element_type=F32) + br_ref[...]

    lane = lax.broadcasted_iota(jnp.int32, (tm, LANES), 1)
    lanef = lane.astype(F32)
    big = jnp.float32(1e9)
    ninf = -jnp.inf
    is_g = (lane >= N_EXPERTS) & (lane < N_EXPERTS + N_GROUPS)
    glog = jnp.where(is_g, lg, ninf)
    gmax = jnp.max(glog, axis=1, keepdims=True)
    gi = jnp.min(jnp.where(glog == gmax, lanef, big), axis=1, keepdims=True) - N_EXPERTS
    pgi = 1.0 / jnp.sum(jnp.where(is_g, jnp.exp(lg - gmax), 0.0), axis=1, keepdims=True)
    lo = gi * EXPERTS_PER_GROUP
    in_grp = (lanef >= lo) & (lanef < lo + EXPERTS_PER_GROUP)
    el = jnp.where(in_grp, lg, ninf)
    v1 = jnp.max(el, axis=1, keepdims=True)
    i1 = jnp.min(jnp.where(el == v1, lanef, big), axis=1, keepdims=True)
    el2 = jnp.where(lanef == i1, ninf, el)
    v2 = jnp.max(el2, axis=1, keepdims=True)
    i2 = jnp.min(jnp.where(el2 == v2, lanef, big), axis=1, keepdims=True)
    e21 = jnp.exp(v2 - v1)
    g1 = pgi / (1.0 + e21)
    g2 = pgi * e21 / (1.0 + e21)
    sel1 = lanef == i1
    sel2 = lanef == i2
    oh = jnp.where(sel1 | sel2, 1.0, 0.0)
    r_i = lax.broadcasted_iota(jnp.int32, (tm, tm), 0)
    c_i = lax.broadcasted_iota(jnp.int32, (tm, tm), 1)
    stril = jnp.where(c_i < r_i, 1.0, 0.0).astype(BF16)
    tot = jnp.dot(stril, oh.astype(BF16), preferred_element_type=F32) + base_sc[...]
    r1 = jnp.sum(jnp.where(sel1, tot, 0.0), axis=1, keepdims=True)
    r2 = jnp.sum(jnp.where(sel2, tot, 0.0), axis=1, keepdims=True)
    base = base_sc[...] + jnp.sum(oh, axis=0, keepdims=True)
    base_sc[...] = base
    cnt_ref[...] = base
    ri = jnp.where(lane == 0, i1, jnp.where(lane == 1, i2, jnp.where(lane == 2, r1,
                   jnp.where(lane == 3, r2, 0.0))))
    ri_t = jnp.concatenate([ri[c * LANES:(c + 1) * LANES, :].T[0:8, :] for c in range(tm // LANES)], axis=1)
    ri_ref[...] = ri_t.astype(jnp.int32)
    rg_ref[...] = jnp.where(lane == 0, g1, jnp.where(lane == 1, g2, 0.0))


def _outproj(mix, w_out, x2d, g, wr, br):
    n = x2d.shape[0]
    tm = min(n, 256)
    kern = functools.partial(_outproj_kernel, tm=tm)
    rowblk = lambda w: pl.BlockSpec((tm, w), lambda i: (i, 0))
    const = lambda r, c: pl.BlockSpec((r, c), lambda i: (0, 0))
    return pl.pallas_call(
        kern,
        grid=(n // tm,),
        in_specs=[rowblk(D_MODEL), const(D_MODEL, D_MODEL), rowblk(D_MODEL), const(1, D_MODEL),
                  const(D_MODEL, LANES), const(1, LANES)],
        out_specs=[rowblk(D_MODEL), rowblk(D_MODEL // 2), pl.BlockSpec((None, 8, tm), lambda i: (i, 0, 0)),
                   rowblk(LANES), const(1, LANES)],
        out_shape=[
            jax.ShapeDtypeStruct((n, D_MODEL), F32),
            jax.ShapeDtypeStruct((n, D_MODEL // 2), jnp.uint32),
            jax.ShapeDtypeStruct((n // tm, 8, tm), jnp.int32),
            jax.ShapeDtypeStruct((n, LANES), F32),
            jax.ShapeDtypeStruct((1, LANES), F32),
        ],
        scratch_shapes=[pltpu.VMEM((1, LANES), F32)],
        compiler_params=_cparams(("arbitrary",)),
        name="outproj",
    )(mix, w_out, x2d, g, wr, br)


def _row_copy(src_ref, src_row, dst_ref, dst_row, sem):
    return pltpu.make_async_copy(src_ref.at[pl.ds(src_row, 1), :], dst_ref.at[pl.ds(dst_row, 1), :], sem)


def _dispatch_kernel(pad_end_ref, padded_ref, nused_ref, dest_ref, xn_ref, xs_ref, zero_sc, sem_z, sem,
                     *, tm, bm, nb):
    i = pl.program_id(0)

    @pl.when(i == 0)
    def _():
        zero_sc[...] = jnp.zeros(zero_sc.shape, jnp.uint32)

        def zcopy(start):
            return pltpu.make_async_copy(zero_sc, xs_ref.at[pl.ds(pl.multiple_of(start, bm), bm), :], sem_z)

        def zstart(e, c):
            @pl.when(padded_ref[e] > 0)
            def _():
                zcopy(pad_end_ref[e] - bm).start()
            return c

        def zwait(e, c):
            @pl.when(padded_ref[e] > 0)
            def _():
                zcopy(pad_end_ref[e] - bm).wait()
            return c

        def tstart(b, c):
            zcopy(b * bm).start()
            return c

        def twait(b, c):
            zcopy(b * bm).wait()
            return c

        lax.fori_loop(0, N_EXPERTS, zstart, 0)
        lax.fori_loop(nused_ref[0], nb, tstart, 0)
        lax.fori_loop(0, N_EXPERTS, zwait, 0)
        lax.fori_loop(nused_ref[0], nb, twait, 0)

    def start(j, c):
        _row_copy(xn_ref, j, xs_ref, dest_ref[0, 0, j], sem).start()
        _row_copy(xn_ref, j, xs_ref, dest_ref[0, 0, tm + j], sem).start()
        return c

    lax.fori_loop(0, tm, start, 0, unroll=8)
    for _ in range(TOP_K):
        pltpu.make_async_copy(xn_ref, xs_ref.at[pl.ds(0, tm), :], sem).wait()


def _dispatch(xn, dest3, pad_end, padded, nused, p_rows, bm):
    n = xn.shape[0]
    tm = dest3.shape[2] // 2
    kern = functools.partial(_dispatch_kernel, tm=tm, bm=bm, nb=p_rows // bm)
    return pl.pallas_call(
        kern,
        grid_spec=pltpu.PrefetchScalarGridSpec(
            num_scalar_prefetch=3,
            grid=(n // tm,),
            in_specs=[
                pl.BlockSpec((1, 1, 2 * tm), lambda i, pe, pd, nu: (i, 0, 0), memory_space=pltpu.SMEM),
                pl.BlockSpec((tm, ROW_WORDS), lambda i, pe, pd, nu: (i, 0)),
            ],
            out_specs=pl.BlockSpec(memory_space=pl.ANY),
            scratch_shapes=[pltpu.VMEM((bm, ROW_WORDS), jnp.uint32), pltpu.SemaphoreType.DMA(()),
                            pltpu.SemaphoreType.DMA(())],
        ),
        out_shape=jax.ShapeDtypeStruct((p_rows, ROW_WORDS), jnp.uint32),
        compiler_params=_cparams(("arbitrary",)),
        name="dispatch",
    )(pad_end, padded, nused, dest3, xn)


def _experts_kernel(blk_e_ref, nused_ref, xs_ref, w1_ref, w3_ref, w2_ref, ys_ref, w1b, w3b, w2b):
    i = pl.program_id(0)

    @pl.when((i == 0) | (blk_e_ref[i] != blk_e_ref[jnp.maximum(i - 1, 0)]))
    def _():
        def cast_in(r, c):
            sl = pl.ds(pl.multiple_of(r * 256, 256), 256)
            w1b[sl, :] = w1_ref[sl, :].astype(BF16)
            w3b[sl, :] = w3_ref[sl, :].astype(BF16)
            return c

        def cast_out(r, c):
            sl = pl.ds(pl.multiple_of(r * 64, 64), 64)
            w2b[sl, :] = w2_ref[sl, :].astype(BF16)
            return c

        lax.fori_loop(0, D_MODEL // 256, cast_in, 0)
        lax.fori_loop(0, D_EXPERT // 64, cast_out, 0)

    @pl.when(i < nused_ref[0])
    def _():
        lo, hi = _unpack_halves(xs_ref[...])
        lo = lo.astype(BF16)
        hi = hi.astype(BF16)
        h1 = (jnp.dot(lo, w1b[0:ROW_WORDS, :], preferred_element_type=F32)
              + jnp.dot(hi, w1b[ROW_WORDS:D_MODEL, :], preferred_element_type=F32))
        h3 = (jnp.dot(lo, w3b[0:ROW_WORDS, :], preferred_element_type=F32)
              + jnp.dot(hi, w3b[ROW_WORDS:D_MODEL, :], preferred_element_type=F32))
        hb = (h1 * (1.0 / (1.0 + jnp.exp(-h1)))) * h3
        ys_ref[...] = _pack_halves(jnp.dot(hb.astype(BF16), w2b[...], preferred_element_type=F32))

    @pl.when(i >= nused_ref[0])
    def _():
        ys_ref[...] = jnp.zeros(ys_ref.shape, jnp.uint32)


def _experts(xs, blk_e, nused, w1, w3, w2, layer, bm):
    p_rows = xs.shape[0]
    nb = p_rows // bm
    rowmap = lambda i, be, nu: (jnp.minimum(i, nu[0] - 1), 0)
    wmap = lambda i, be, nu: (layer, be[i], 0, 0)
    return pl.pallas_call(
        _experts_kernel,
        grid_spec=pltpu.PrefetchScalarGridSpec(
            num_scalar_prefetch=2,
            grid=(nb,),
            in_specs=[
                pl.BlockSpec((bm, ROW_WORDS), rowmap),
                pl.BlockSpec((None, None, D_MODEL, D_EXPERT), wmap),
                pl.BlockSpec((None, None, D_MODEL, D_EXPERT), wmap),
                pl.BlockSpec((None, None, D_EXPERT, D_MODEL), wmap),
            ],
            out_specs=pl.BlockSpec((bm, ROW_WORDS), lambda i, be, nu: (i, 0)),
            scratch_shapes=[pltpu.VMEM((D_MODEL, D_EXPERT), BF16), pltpu.VMEM((D_MODEL, D_EXPERT), BF16),
                            pltpu.VMEM((D_EXPERT, D_MODEL), BF16)],
        ),
        out_shape=jax.ShapeDtypeStruct((p_rows, ROW_WORDS), jnp.uint32),
        compiler_params=_cparams(("arbitrary",)),
        name="experts",
    )(blk_e, nused, xs, w1, w3, w2)


def _combine_kernel(dest_ref, dnext_ref, x_ref, rg_ref, g_ref, ys_ref, out_ref, ybuf, sem, *, tm, rows, final):
    i = pl.program_id(0)
    nt = pl.num_programs(0)
    slot = lax.rem(i, 2)
    other = 1 - slot

    def gather_rows(d_ref, base, buf_slot):
        for jj in range(rows):
            j = base + jj
            _row_copy(ys_ref, d_ref[0, 0, j], ybuf.at[buf_slot, 0], j, sem.at[buf_slot]).start()
            _row_copy(ys_ref, d_ref[0, 0, tm + j], ybuf.at[buf_slot, 1], j, sem.at[buf_slot]).start()

    @pl.when(i == 0)
    def _():
        def first(r, c):
            gather_rows(dest_ref, pl.multiple_of(r * rows, rows), 0)
            return c

        lax.fori_loop(0, tm // rows, first, 0)

    for k in range(TOP_K):
        pltpu.make_async_copy(ys_ref.at[pl.ds(0, tm), :], ybuf.at[slot, k], sem.at[slot]).wait()

    def combine_rows(base):
        sl = pl.ds(base, rows)
        rg = rg_ref[sl, :]
        g1 = rg[:, 0:1]
        g2 = rg[:, 1:2]
        lo1, hi1 = _unpack_halves(ybuf[slot, 0, sl, :])
        lo2, hi2 = _unpack_halves(ybuf[slot, 1, sl, :])
        xa = x_ref[sl, 0:ROW_WORDS] + (g1 * lo1 + g2 * lo2)
        xb = x_ref[sl, ROW_WORDS:D_MODEL] + (g1 * hi1 + g2 * hi2)
        if final:
            ss = jnp.sum(xa * xa, axis=-1, keepdims=True) + jnp.sum(xb * xb, axis=-1, keepdims=True)
            sc = lax.rsqrt(ss / D_MODEL + EPS)
            xa = xa * sc * g_ref[:, 0:ROW_WORDS]
            xb = xb * sc * g_ref[:, ROW_WORDS:D_MODEL]
        out_ref[sl, 0:ROW_WORDS] = xa
        out_ref[sl, ROW_WORDS:D_MODEL] = xb

    @pl.when(i + 1 < nt)
    def _():
        def body(r, c):
            base = pl.multiple_of(r * rows, rows)
            gather_rows(dnext_ref, base, other)
            combine_rows(base)
            return c

        lax.fori_loop(0, tm // rows, body, 0)

    @pl.when(i + 1 == nt)
    def _():
        def body(r, c):
            combine_rows(pl.multiple_of(r * rows, rows))
            return c

        lax.fori_loop(0, tm // rows, body, 0)


def _combine(x_new, ys, dest3, rg, g_final, final):
    n = x_new.shape[0]
    tm = dest3.shape[2] // 2
    nt = n // tm
    kern = functools.partial(_combine_kernel, tm=tm, rows=16, final=final)
    return pl.pallas_call(
        kern,
        grid=(nt,),
        in_specs=[
            pl.BlockSpec((1, 1, 2 * tm), lambda i: (i, 0, 0), memory_space=pltpu.SMEM),
            pl.BlockSpec((1, 1, 2 * tm), lambda i: (jnp.minimum(i + 1, nt - 1), 0, 0), memory_space=pltpu.SMEM),
            pl.BlockSpec((tm, D_MODEL), lambda i: (i, 0)),
            pl.BlockSpec((tm, LANES), lambda i: (i, 0)),
            pl.BlockSpec((1, D_MODEL), lambda i: (0, 0)),
            pl.BlockSpec(memory_space=pl.ANY),
        ],
        out_specs=pl.BlockSpec((tm, D_MODEL), lambda i: (i, 0)),
        out_shape=jax.ShapeDtypeStruct((n, D_MODEL), F32),
        scratch_shapes=[pltpu.VMEM((2, TOP_K, tm, ROW_WORDS), jnp.uint32), pltpu.SemaphoreType.DMA((2,))],
        compiler_params=_cparams(("arbitrary",)),
        name="combine",
    )(dest3, dest3, x_new, rg, g_final, ys)


def _route_tables(ri_t, counts, n, bm):
    cnt = counts[0, :N_EXPERTS].astype(jnp.int32)
    padded = (cnt + bm - 1) // bm * bm
    pad_end = jnp.cumsum(padded)
    pad_start = pad_end - padded
    e1, e2, r1, r2 = ri_t[:, 0, :], ri_t[:, 1, :], ri_t[:, 2, :], ri_t[:, 3, :]
    s1 = jnp.zeros_like(e1)
    s2 = jnp.zeros_like(e2)
    for e in range(N_EXPERTS):
        s1 = jnp.where(e1 == e, pad_start[e], s1)
        s2 = jnp.where(e2 == e, pad_start[e], s2)
    dest3 = jnp.concatenate([s1 + r1, s2 + r2], axis=1)[:, None, :]
    nb = -(-(n * TOP_K) // bm) + N_EXPERTS
    nused = (pad_end[-1] // bm).astype(jnp.int32)
    blk = jnp.minimum(jnp.arange(nb, dtype=jnp.int32), nused - 1) * bm
    blk_e = jnp.sum((pad_end[None, :] <= blk[:, None]).astype(jnp.int32), axis=1)
    blk_e = jnp.minimum(blk_e, N_EXPERTS - 1)
    return dest3, pad_end.astype(jnp.int32), padded.astype(jnp.int32), blk_e, nused.reshape(1), nb * bm


def _layer(x2d, bsz, seq, conv0, c0, n0, m0, wts, g_final, final):
    n = bsz * seq
    pf, pb, li, lf = _inproj(x2d, wts["g_mix"], wts["w_main"], wts["wg"], wts["b_if"])
    n0p = n0.reshape(bsz, N_PAIRS, LANES)
    m0p = jnp.pad(m0, ((0, 0), (0, LANES - N_HEADS))).reshape(bsz, 1, LANES)
    by_row = lambda a: a.reshape(bsz, seq, a.shape[-1])
    mix, conv_n, c_n, n_n, m_n = _mixer(by_row(pf), by_row(pb), by_row(li), by_row(lf), wts["conv_w"],
                                         wts["head_gain"], conv0, c0, n0p, m0p, bsz, seq)
    x_new, xq, ri_t, rg, counts = _outproj(mix.reshape(n, D_MODEL), wts["w_out"], x2d, wts["g_ffn"],
                                           wts["wr"], wts["br"])
    bm = 256 if n >= 4096 else 128
    dest3, pad_end, padded, blk_e, nused, p_rows = _route_tables(ri_t, counts, n, bm)
    xs = _dispatch(xq, dest3, pad_end, padded, nused, p_rows, bm)
    ys = _experts(xs, blk_e, nused, wts["w1"], wts["w3"], wts["w2"], wts["layer"], bm)
    x_out = _combine(x_new, ys, dest3, rg, g_final, final)
    return (x_out, conv_n, c_n, n_n.reshape(bsz, N_HEADS, DQK), m_n[:, 0, :N_HEADS])


def _prep_weights(l, norm_mix, w_in, b_if, conv_w, head_gain, w_out, norm_ffn, w_router_group,
                  b_router_group, w_router_expert, b_router_expert, w1, w3, w2):
    wi = w_in[l]
    w_main = wi[:, :MAIN_COLS].astype(BF16)
    wg = jnp.pad(wi[:, MAIN_COLS:], ((0, 0), (0, LANES - 2 * N_HEADS))).astype(BF16)
    wr = jnp.pad(jnp.concatenate([w_router_expert[l], w_router_group[l]], axis=1),
                 ((0, 0), (0, LANES - N_EXPERTS - N_GROUPS))).astype(BF16)
    br = jnp.pad(jnp.concatenate([b_router_expert[l], b_router_group[l]]),
                 (0, LANES - N_EXPERTS - N_GROUPS)).reshape(1, LANES)
    return dict(
        g_mix=norm_mix[l].reshape(1, D_MODEL),
        w_main=w_main, wg=wg,
        b_if=jnp.pad(b_if[l], (0, LANES - 2 * N_HEADS)).reshape(1, LANES),
        conv_w=conv_w[l], head_gain=head_gain[l].reshape(1, MLSTM_DIM),
        w_out=w_out[l].astype(BF16),
        g_ffn=norm_ffn[l].reshape(1, D_MODEL),
        wr=wr, br=br,
        w1=w1, w3=w3, w2=w2, layer=l,
    )


def _trunk(x, conv_s, c_s, n_s, m_s, wts, g_final):
    bsz, seq, _ = x.shape
    depth = len(wts)
    x2d = x.reshape(bsz * seq, D_MODEL)
    convs, cs, ns, ms = [], [], [], []
    for l in range(depth):
        x2d, cv, cn, nn_, mn = _layer(x2d, bsz, seq, conv_s[l], c_s[l], n_s[l], m_s[l],
                                      wts[l], g_final, l == depth - 1)
        convs.append(cv)
        cs.append(cn)
        ns.append(nn_)
        ms.append(mn)
    return (x2d.reshape(bsz, seq, D_MODEL), jnp.stack(convs), jnp.stack(cs), jnp.stack(ns), jnp.stack(ms))


def kernel(x_prompt, x_sample, state_conv, state_mlstm_C, state_mlstm_n, state_mlstm_m,
           norm_mix, w_in, b_if, conv_w, head_gain, w_out, norm_ffn,
           w_router_group, b_router_group, w_router_expert, b_router_expert,
           w1, w3, w2, norm_final):
    depth = w_in.shape[0]
    wts = [_prep_weights(l, norm_mix, w_in, b_if, conv_w, head_gain, w_out, norm_ffn, w_router_group,
                         b_router_group, w_router_expert, b_router_expert, w1, w3, w2)
           for l in range(depth)]
    g_final = norm_final.reshape(1, D_MODEL)
    b = x_prompt.shape[0]
    conv0 = jnp.zeros((depth, b, CONV_WIDTH - 1, CONV_DIM), F32)
    c0 = jnp.zeros((depth, b, N_HEADS, DQK, DV), F32)
    n0 = jnp.zeros((depth, b, N_HEADS, DQK), F32)
    m0 = jnp.full((depth, b, N_HEADS), M_INIT, F32)
    y_p, conv_p, c_p, n_p, m_p = _trunk(x_prompt, conv0, c0, n0, m0, wts, g_final)
    y_s, conv_s, c_s, n_s, m_s = _trunk(x_sample, state_conv, state_mlstm_C, state_mlstm_n,
                                        state_mlstm_m, wts, g_final)
    return (y_p, y_s, conv_p, c_p, n_p, m_p, conv_s, c_s, n_s, m_s)
```

```python
import functools

import jax
import jax.numpy as jnp
from jax import lax
from jax.experimental import pallas as pl
from jax.experimental.pallas import tpu as pltpu

F32 = jnp.float32
BF16 = jnp.bfloat16

D_MODEL = 2048
CONV_DIM = 1024
N_HEADS = 8
N_PAIRS = N_HEADS // 2
DV = 128
DQK = 64
QK_DIM = N_HEADS * DQK
MLSTM_DIM = N_HEADS * DV
N_GROUPS = 4
EXPERTS_PER_GROUP = 8
N_EXPERTS = 32
TOP_K = 2
D_EXPERT = 512
EPS = 1e-6
M_INIT = -1e30
CONV_WIDTH = 3
LANES = 128
MAIN_COLS = 3 * CONV_DIM + 2 * QK_DIM + 2 * MLSTM_DIM
ROW_WORDS = D_MODEL // 2
PROJ_BLOCK = 1024
F32_BLOCKS = 5
CHUNK = 64
EXPERT_BLOCK = 256
CONV_GROUP_COLS = 256
CONV_GROUPS = CONV_DIM // CONV_GROUP_COLS
VMEM_LIMIT = 56 * 1024 * 1024


def _cparams(sem):
    return pltpu.CompilerParams(dimension_semantics=sem, vmem_limit_bytes=VMEM_LIMIT)


def _pack_halves(x):
    half = x.shape[1] // 2
    lo = pltpu.bitcast(x[:, :half].astype(BF16).astype(F32), jnp.uint32)
    hi = pltpu.bitcast(x[:, half:].astype(BF16).astype(F32), jnp.uint32)
    return (lo >> 16) | (hi & jnp.uint32(0xFFFF0000))


def _unpack_halves(w):
    lo = pltpu.bitcast(w << 16, F32)
    hi = pltpu.bitcast(w & jnp.uint32(0xFFFF0000), F32)
    return lo, hi


def _split3(x):
    hi = x.astype(BF16)
    r1 = x - hi.astype(F32)
    mid = r1.astype(BF16)
    lo = (r1 - mid.astype(F32)).astype(BF16)
    return hi, mid, lo


def _inproj_kernel(x_ref, g_ref, w_ref, wg_ref, b_ref, pf_ref, pb_ref, li_ref, lf_ref, xh_ref,
                   *, tm, rows):
    j = pl.program_id(1)

    @pl.when(j == 0)
    def _():
        def body(r, c):
            sl = pl.ds(pl.multiple_of(r * rows, rows), rows)
            x = x_ref[sl, :]
            ms = jnp.mean(x * x, axis=-1, keepdims=True)
            xh_ref[sl, :] = (x * lax.rsqrt(ms + EPS) * g_ref[...]).astype(BF16)
            return c

        lax.fori_loop(0, tm // rows, body, 0)
        gt = jnp.dot(xh_ref[...], wg_ref[...], preferred_element_type=F32) + b_ref[...]
        lane = lax.broadcasted_iota(jnp.int32, gt.shape, 1)
        valid = lane < N_HEADS
        li_ref[...] = jnp.where(valid, gt, 0.0)
        fg = pltpu.roll(gt, LANES - N_HEADS, axis=1)
        lf = jnp.minimum(fg, 0.0) - jnp.log1p(jnp.exp(-jnp.abs(fg)))
        lf_ref[...] = jnp.where(valid, lf, 0.0)

    @pl.when(j < F32_BLOCKS)
    def _():
        pf_ref[...] = jnp.dot(xh_ref[...], w_ref[...], preferred_element_type=F32)

    @pl.when(j >= F32_BLOCKS)
    def _():
        pb_ref[...] = jnp.dot(xh_ref[...], w_ref[...], preferred_element_type=F32).astype(BF16)


def _inproj(x2d, g, w_main, wg, b_pad):
    n = x2d.shape[0]
    tm = min(n, 1024)
    tn = PROJ_BLOCK
    kern = functools.partial(_inproj_kernel, tm=tm, rows=32)
    return pl.pallas_call(
        kern,
        grid=(n // tm, MAIN_COLS // tn),
        in_specs=[
            pl.BlockSpec((tm, D_MODEL), lambda i, j: (i, 0)),
            pl.BlockSpec((1, D_MODEL), lambda i, j: (0, 0)),
            pl.BlockSpec((D_MODEL, tn), lambda i, j: (0, jnp.where(j == 3, 5, jnp.where(j > 3, j - 1, j)))),
            pl.BlockSpec((D_MODEL, LANES), lambda i, j: (0, 0)),
            pl.BlockSpec((1, LANES), lambda i, j: (0, 0)),
        ],
        out_specs=[
            pl.BlockSpec((tm, tn), lambda i, j: (i, jnp.minimum(j, F32_BLOCKS - 1))),
            pl.BlockSpec((tm, tn), lambda i, j: (i, jnp.maximum(j - F32_BLOCKS, 0))),
            pl.BlockSpec((tm, LANES), lambda i, j: (i, 0)),
            pl.BlockSpec((tm, LANES), lambda i, j: (i, 0)),
        ],
        out_shape=[
            jax.ShapeDtypeStruct((n, F32_BLOCKS * tn), F32),
            jax.ShapeDtypeStruct((n, MAIN_COLS - F32_BLOCKS * tn), BF16),
            jax.ShapeDtypeStruct((n, LANES), F32),
            jax.ShapeDtypeStruct((n, LANES), F32),
        ],
        scratch_shapes=[pltpu.VMEM((tm, D_MODEL), BF16)],
        compiler_params=_cparams(("arbitrary", "arbitrary")),
        name="inproj",
    )(x2d, g, w_main, wg, b_pad)


def _cummax_rows(x, length):
    row = lax.broadcasted_iota(jnp.int32, x.shape, 0)
    d = 1
    while d < length:
        shifted = pltpu.roll(x, d, axis=0)
        x = jnp.maximum(x, jnp.where(row >= d, shifted, -jnp.inf))
        d *= 2
    return x


def _pad_rows(x, length):
    if length == LANES:
        return x
    return jnp.concatenate([x, jnp.zeros((LANES - length, x.shape[1]), x.dtype)], axis=0)


def _mixer_kernel(u_ref, gc_ref, gb_ref, q_ref, k_ref, v_ref, o_ref, li_ref, lf_ref,
                  cw_ref, hg_ref, conv0_ref, c0_ref, n0_ref, m0_ref,
                  mix_ref, convn_ref, cn_ref, nn_ref, mn_ref,
                  *scratch, tb, L, G):
    t = pl.program_id(1)
    nt = pl.num_programs(1)
    n_state = G * N_PAIRS
    c_sc = [scratch[g * N_PAIRS:(g + 1) * N_PAIRS] for g in range(G)]
    n_sc = [scratch[n_state + g * N_PAIRS:n_state + (g + 1) * N_PAIRS] for g in range(G)]
    m_sc = scratch[2 * n_state:2 * n_state + G]
    carry_sc = [scratch[2 * n_state + G + g * CONV_GROUPS:2 * n_state + G + (g + 1) * CONV_GROUPS]
                for g in range(G)]

    @pl.when(t == 0)
    def _():
        zero = jnp.zeros((DQK, DV), F32)
        for g in range(G):
            for p in range(N_PAIRS):
                top = jnp.concatenate([c0_ref[g, 2 * p], zero], axis=1)
                bot = jnp.concatenate([zero, c0_ref[g, 2 * p + 1]], axis=1)
                c_sc[g][p][...] = jnp.concatenate([top, bot], axis=0)
                n_sc[g][p][...] = n0_ref[g, p:p + 1, :]
            m_sc[g][...] = m0_ref[g]
            for cg in range(CONV_GROUPS):
                cs = slice(cg * CONV_GROUP_COLS, (cg + 1) * CONV_GROUP_COLS)
                carry_sc[g][cg][...] = jnp.zeros((8, CONV_GROUP_COLS), F32)
                carry_sc[g][cg][6:8, :] = conv0_ref[g, :, cs]

    row = lax.broadcasted_iota(jnp.int32, (L, L), 0)
    col = lax.broadcasted_iota(jnp.int32, (L, L), 1)
    causal = col <= row
    tril = jnp.where(causal, 1.0, 0.0).astype(BF16)
    lane_l = lax.broadcasted_iota(jnp.int32, (L, LANES), 1)
    low_l = lane_l < DQK
    krow = lax.broadcasted_iota(jnp.int32, (LANES, 2 * DV), 0)
    lane1 = lax.broadcasted_iota(jnp.int32, (1, LANES), 1)

    def chunk(c, carry):
        rows = pl.ds(pl.multiple_of(c * L, L), L)
        units = [(g, p) for g in range(G) for p in range(N_PAIRS)]
        heads = [(g, h) for g in range(G) for h in range(N_HEADS)]
        convs = [conv_chunk(g, rows) for g in range(G)]
        gates = [gate_algebra(g, rows) for g in range(G)]
        outs, states = {}, {}
        for g, p in units:
            pr = pair_scores(g, p, rows)
            v2 = v_ref[g, rows, p * 2 * DV:(p + 1) * 2 * DV]
            for hh in range(2):
                h = 2 * p + hh
                s = intra_weights(gates[g], pr, h)
                intra = jnp.dot(s.astype(BF16), v2[:, hh * DV:(hh + 1) * DV], preferred_element_type=F32)
                o_sig = 1.0 / (1.0 + jnp.exp(-o_ref[g, rows, h * DV:(h + 1) * DV]))
                outs[(g, h)] = head_output(gates[g], pr, s, intra, o_sig, h)
            states[(g, p)] = state_update(gates[g], pr, v2, p)
        for g in range(G):
            for cg in range(CONV_GROUPS):
                y_conv, z_tail = convs[g][cg]
                mix_ref[g, rows, cg * CONV_GROUP_COLS:(cg + 1) * CONV_GROUP_COLS] = y_conv
                carry_sc[g][cg][6:8, :] = z_tail
            m_sc[g][...] = gates[g]["m_new"]
        for g, h in heads:
            mix_ref[g, rows, CONV_DIM + h * DV:CONV_DIM + (h + 1) * DV] = outs[(g, h)]
        for g, p in units:
            c_sc[g][p][...], n_sc[g][p][...] = states[(g, p)]
        return carry

    def conv_chunk(g, rows):
        cw = CONV_GROUP_COLS
        res = []
        for cg in range(CONV_GROUPS):
            cs = slice(cg * cw, (cg + 1) * cw)
            z = gc_ref[g, rows, cs] * u_ref[g, rows, cs]
            prev = carry_sc[g][cg][...]
            p1 = prev[7:8, :]
            p2 = prev[6:7, :]
            rw = lax.broadcasted_iota(jnp.int32, (L, cw), 0)
            z1 = jnp.where(rw >= 1, pltpu.roll(z, 1, axis=0), p1)
            z2 = jnp.where(rw >= 2, pltpu.roll(z, 2, axis=0), jnp.where(rw == 1, p1, p2))
            y = z2 * cw_ref[0:1, cs] + z1 * cw_ref[1:2, cs] + z * cw_ref[2:3, cs]
            res.append(((gb_ref[g, rows, cs] * y).astype(BF16), z[L - 2:L, :]))
        return res

    def gate_algebra(g, rows):
        li = li_ref[g, rows, :]
        lf = lf_ref[g, rows, :]
        hi, mid, lo = _split3(lf)
        F = (jnp.dot(tril, hi, preferred_element_type=F32)
             + jnp.dot(tril, mid, preferred_element_type=F32)
             + jnp.dot(tril, lo, preferred_element_type=F32))
        r = li - F
        cm = _cummax_rows(r, L)
        mprev = m_sc[g][...]
        mx = jnp.maximum(mprev, cm)
        M = F + mx
        neg_mx = -mx
        w_inter = jnp.exp(mprev - mx)
        em = jnp.exp(-M)
        gs = jnp.exp(r - mx[L - 1:L, :])
        g_inter = w_inter[L - 1:L, :]
        rT = _pad_rows(r, L).T[:, 0:L]
        return dict(neg_mx=neg_mx, w_inter=w_inter, em=em, gs=gs, g_inter=g_inter, rT=rT, m_new=M[L - 1:L, :])

    def pair_scores(g, p, rows):
        ps = slice(p * LANES, (p + 1) * LANES)
        q2 = q_ref[g, rows, ps].astype(BF16)
        kf = k_ref[g, rows, ps] * DQK ** -0.5
        k2 = kf.astype(BF16)
        q_e = jnp.where(low_l, q2, jnp.zeros_like(q2))
        q_o = jnp.where(low_l, jnp.zeros_like(q2), q2)
        q_st = jnp.concatenate([q_e, q_o], axis=0)
        n_row = n_sc[g][p][...]
        n_b = jnp.broadcast_to(n_row, (LANES, LANES)).astype(BF16)
        k_aug = jnp.concatenate([n_b, k2], axis=0)
        sn = lax.dot_general(q_st, k_aug, (((1,), (1,)), ((), ())),
                             preferred_element_type=F32)
        c_full = c_sc[g][p][...]
        qc = jnp.dot(q_st, c_full.astype(BF16), preferred_element_type=F32)
        return dict(kf=kf, k2=k2, n_row=n_row, c_full=c_full, sn=sn, qc=qc)

    def intra_weights(gt, pr, h):
        rs = slice((h % 2) * L, (h % 2 + 1) * L)
        dmat = gt["neg_mx"][:, h:h + 1] + gt["rT"][h:h + 1, :]
        return pr["sn"][rs, LANES:LANES + L] * jnp.exp(jnp.where(causal, dmat, -jnp.inf))

    def head_output(gt, pr, s, intra, o_sig, h):
        hh = h % 2
        rs = slice(hh * L, (hh + 1) * L)
        hs = slice(h * DV, (h + 1) * DV)
        wi = gt["w_inter"][:, h:h + 1]
        num = wi * pr["qc"][rs, hh * DV:(hh + 1) * DV] + intra
        den = wi * pr["sn"][rs, 0:LANES] + jnp.sum(s, axis=-1, keepdims=True)
        hv = num / jnp.maximum(jnp.abs(den), gt["em"][:, h:h + 1])
        ms = jnp.mean(hv * hv, axis=-1, keepdims=True)
        hn = hv * lax.rsqrt(ms + EPS) * hg_ref[0:1, hs]
        return (hn * o_sig).astype(BF16)

    def state_update(gt, pr, v2, p):
        gs = gt["gs"]
        kgw = jnp.where(low_l, gs[:, 2 * p:2 * p + 1], gs[:, 2 * p + 1:2 * p + 2])
        kg = pr["kf"] * kgw
        upd = jnp.dot(_pad_rows(kg, L).T.astype(BF16), _pad_rows(v2, L),
                      preferred_element_type=F32)
        ge = gt["g_inter"][0:1, 2 * p:2 * p + 1]
        go = gt["g_inter"][0:1, 2 * p + 1:2 * p + 2]
        c_new = jnp.where(krow < DQK, ge, go) * pr["c_full"] + upd
        kn = pr["k2"].astype(F32) * kgw.astype(BF16).astype(F32)
        n_new = jnp.where(lane1 < DQK, ge, go) * pr["n_row"] + jnp.sum(kn, axis=0, keepdims=True)
        return c_new, n_new

    lax.fori_loop(0, tb // L, chunk, 0)

    @pl.when(t == nt - 1)
    def _():
        for g in range(G):
            for cg in range(CONV_GROUPS):
                cs = slice(cg * CONV_GROUP_COLS, (cg + 1) * CONV_GROUP_COLS)
                convn_ref[g, :, cs] = carry_sc[g][cg][6:8, :]
            for p in range(N_PAIRS):
                cf = c_sc[g][p][...]
                cn_ref[g, 2 * p] = cf[0:DQK, 0:DV]
                cn_ref[g, 2 * p + 1] = cf[DQK:2 * DQK, DV:2 * DV]
                nn_ref[g, p:p + 1, :] = n_sc[g][p][...]
            mn_ref[g] = m_sc[g][...]


def _mixer(pf, pb, li, lf, conv_w, head_gain, conv0, c0, n0p, m0p, bsz, seq):
    G = 1
    L = min(seq, CHUNK)
    tb = min(seq, 512)
    nt = seq // tb
    kern = functools.partial(_mixer_kernel, tb=tb, L=L, G=G)
    wide = lambda cb: pl.BlockSpec((G, tb, 1024), lambda b, t: (b, t, cb))
    half = lambda cb: pl.BlockSpec((G, tb, 512), lambda b, t: (b, t, cb))
    gate = pl.BlockSpec((G, tb, LANES), lambda b, t: (b, t, 0))
    st = lambda *shape: pl.BlockSpec((G,) + shape, lambda b, t: (b,) + (0,) * len(shape))
    return pl.pallas_call(
        kern,
        grid=(bsz // G, nt),
        in_specs=[
            wide(0), wide(1), wide(2), half(8), half(9), wide(0), wide(3), gate, gate,
            pl.BlockSpec((CONV_WIDTH, CONV_DIM), lambda b, t: (0, 0)),
            pl.BlockSpec((1, MLSTM_DIM), lambda b, t: (0, 0)),
            st(2, CONV_DIM), st(N_HEADS, DQK, DV), st(N_PAIRS, LANES), st(1, LANES),
        ],
        out_specs=[
            pl.BlockSpec((G, tb, D_MODEL), lambda b, t: (b, t, 0)),
            st(2, CONV_DIM), st(N_HEADS, DQK, DV), st(N_PAIRS, LANES), st(1, LANES),
        ],
        out_shape=[
            jax.ShapeDtypeStruct((bsz, seq, D_MODEL), BF16),
            jax.ShapeDtypeStruct((bsz, 2, CONV_DIM), F32),
            jax.ShapeDtypeStruct((bsz, N_HEADS, DQK, DV), F32),
            jax.ShapeDtypeStruct((bsz, N_PAIRS, LANES), F32),
            jax.ShapeDtypeStruct((bsz, 1, LANES), F32),
        ],
        scratch_shapes=(
            [pltpu.VMEM((LANES, 2 * DV), F32)] * (G * N_PAIRS)
            + [pltpu.VMEM((1, LANES), F32)] * (G * N_PAIRS)
            + [pltpu.VMEM((1, LANES), F32)] * G
            + [pltpu.VMEM((8, CONV_GROUP_COLS), F32)] * (G * CONV_GROUPS)
        ),
        compiler_params=_cparams(("arbitrary", "arbitrary")),
        name="mixer",
    )(pf, pf, pf, pf, pf, pb, pf, li, lf, conv_w, head_gain, conv0, c0, n0p, m0p)


def _outproj_kernel(mix_ref, w_ref, x_ref, g_ref, wr_ref, br_ref,
                    xo_ref, xq_ref, ri_ref, rg_ref, cnt_ref, base_sc, *, tm):
    i = pl.program_id(0)

    @pl.when(i == 0)
    def _():
        base_sc[...] = jnp.zeros(base_sc.shape, F32)

    x = x_ref[...] + jnp.dot(mix_ref[...], w_ref[...], preferred_element_type=F32)
    xo_ref[...] = x
    ms = jnp.mean(x * x, axis=-1, keepdims=True)
    xn = x * lax.rsqrt(ms + EPS) * g_ref[...]
    xq_ref[...] = _pack_halves(xn)
    lg = jnp.dot(xn.astype(BF16), wr_ref[...], preferred_element_type=F32) + br_ref[...]

    lane = lax.broadcasted_iota(jnp.int32, (tm, LANES), 1)
    lanef = lane.astype(F32)
    big = jnp.float32(1e9)
    ninf = -jnp.inf
    is_g = (lane >= N_EXPERTS) & (lane < N_EXPERTS + N_GROUPS)
    glog = jnp.where(is_g, lg, ninf)
    gmax = jnp.max(glog, axis=1, keepdims=True)
    gi = jnp.min(jnp.where(glog == gmax, lanef, big), axis=1, keepdims=True) - N_EXPERTS
    pgi = 1.0 / jnp.sum(jnp.where(is_g, jnp.exp(lg - gmax), 0.0), axis=1, keepdims=True)
    lo = gi * EXPERTS_PER_GROUP
    in_grp = (lanef >= lo) & (lanef < lo + EXPERTS_PER_GROUP)
    el = jnp.where(in_grp, lg, ninf)
    v1 = jnp.max(el, axis=1, keepdims=True)
    i1 = jnp.min(jnp.where(el == v1, lanef, big), axis=1, keepdims=True)
    el2 = jnp.where(lanef == i1, ninf, el)
    v2 = jnp.max(el2, axis=1, keepdims=True)
    i2 = jnp.min(jnp.where(el2 == v2, lanef, big), axis=1, keepdims=True)
    e21 = jnp.exp(v2 - v1)
    g1 = pgi / (1.0 + e21)
    g2 = pgi * e21 / (1.0 + e21)
    sel1 = lanef == i1
    sel2 = lanef == i2
    oh = jnp.where(sel1 | sel2, 1.0, 0.0)
    r_i = lax.broadcasted_iota(jnp.int32, (tm, tm), 0)
    c_i = lax.broadcasted_iota(jnp.int32, (tm, tm), 1)
    stril = jnp.where(c_i < r_i, 1.0, 0.0).astype(BF16)
    tot = jnp.dot(stril, oh.astype(BF16), preferred_element_type=F32) + base_sc[...]
    r1 = jnp.sum(jnp.where(sel1, tot, 0.0), axis=1, keepdims=True)
    r2 = jnp.sum(jnp.where(sel2, tot, 0.0), axis=1, keepdims=True)
    base = base_sc[...] + jnp.sum(oh, axis=0, keepdims=True)
    base_sc[...] = base
    cnt_ref[...] = base
    ri = jnp.where(lane == 0, i1, jnp.where(lane == 1, i2, jnp.where(lane == 2, r1,
                   jnp.where(lane == 3, r2, 0.0))))
    ri_t = jnp.concatenate([ri[c * LANES:(c + 1) * LANES, :].T[0:8, :] for c in range(tm // LANES)], axis=1)
    ri_ref[...] = ri_t.astype(jnp.int32)
    rg_ref[...] = jnp.where(lane == 0, g1, jnp.where(lane == 1, g2, 0.0))


def _outproj(mix, w_out, x2d, g, wr, br):
    n = x2d.shape[0]
    tm = min(n, 256)
    kern = functools.partial(_outproj_kernel, tm=tm)
    rowblk = lambda w: pl.BlockSpec((tm, w), lambda i: (i, 0))
    const = lambda r, c: pl.BlockSpec((r, c), lambda i: (0, 0))
    return pl.pallas_call(
        kern,
        grid=(n // tm,),
        in_specs=[rowblk(D_MODEL), const(D_MODEL, D_MODEL), rowblk(D_MODEL), const(1, D_MODEL),
                  const(D_MODEL, LANES), const(1, LANES)],
        out_specs=[rowblk(D_MODEL), rowblk(D_MODEL // 2), pl.BlockSpec((None, 8, tm), lambda i: (i, 0, 0)),
                   rowblk(LANES), const(1, LANES)],
        out_shape=[
            jax.ShapeDtypeStruct((n, D_MODEL), F32),
            jax.ShapeDtypeStruct((n, D_MODEL // 2), jnp.uint32),
            jax.ShapeDtypeStruct((n // tm, 8, tm), jnp.int32),
            jax.ShapeDtypeStruct((n, LANES), F32),
            jax.ShapeDtypeStruct((1, LANES), F32),
        ],
        scratch_shapes=[pltpu.VMEM((1, LANES), F32)],
        compiler_params=_cparams(("arbitrary",)),
        name="outproj",
    )(mix, w_out, x2d, g, wr, br)


def _row_copy(src_ref, src_row, dst_ref, dst_row, sem):
    return pltpu.make_async_copy(src_ref.at[pl.ds(src_row, 1), :], dst_ref.at[pl.ds(dst_row, 1), :], sem)


def _dispatch_kernel(pad_end_ref, padded_ref, nused_ref, dest_ref, xa_ref, xb_ref, xs_ref, zero_sc, sem_z, sem,
                     *, tm, bm, nb, tiles_a):
    i = pl.program_id(0)

    @pl.when(i == 0)
    def _():
        zero_sc[...] = jnp.zeros(zero_sc.shape, jnp.uint32)

        def zcopy(start):
            return pltpu.make_async_copy(zero_sc, xs_ref.at[pl.ds(pl.multiple_of(start, bm), bm), :], sem_z)

        def zstart(e, c):
            @pl.when(padded_ref[e] > 0)
            def _():
                zcopy(pad_end_ref[e] - bm).start()
            return c

        def zwait(e, c):
            @pl.when(padded_ref[e] > 0)
            def _():
                zcopy(pad_end_ref[e] - bm).wait()
            return c

        def tstart(b, c):
            zcopy(b * bm).start()
            return c

        def twait(b, c):
            zcopy(b * bm).wait()
            return c

        lax.fori_loop(0, N_EXPERTS, zstart, 0)
        lax.fori_loop(nused_ref[0], nb, tstart, 0)
        lax.fori_loop(0, N_EXPERTS, zwait, 0)
        lax.fori_loop(nused_ref[0], nb, twait, 0)

    def scatter_tile(src_ref):
        def start(j, c):
            _row_copy(src_ref, j, xs_ref, dest_ref[0, 0, j], sem).start()
            _row_copy(src_ref, j, xs_ref, dest_ref[0, 0, tm + j], sem).start()
            return c

        lax.fori_loop(0, tm, start, 0, unroll=8)
        for _ in range(TOP_K):
            pltpu.make_async_copy(src_ref, xs_ref.at[pl.ds(0, tm), :], sem).wait()

    @pl.when(i < tiles_a)
    def _():
        scatter_tile(xa_ref)

    @pl.when(i >= tiles_a)
    def _():
        scatter_tile(xb_ref)


def _dispatch(xq_a, xq_b, dest3, pad_end, padded, nused, p_rows, bm):
    tm = dest3.shape[2] // 2
    tiles_a, tiles_b = xq_a.shape[0] // tm, xq_b.shape[0] // tm
    kern = functools.partial(_dispatch_kernel, tm=tm, bm=bm, nb=p_rows // bm, tiles_a=tiles_a)
    return pl.pallas_call(
        kern,
        grid_spec=pltpu.PrefetchScalarGridSpec(
            num_scalar_prefetch=3,
            grid=(tiles_a + tiles_b,),
            in_specs=[
                pl.BlockSpec((1, 1, 2 * tm), lambda i, pe, pd, nu: (i, 0, 0), memory_space=pltpu.SMEM),
                pl.BlockSpec((tm, ROW_WORDS), lambda i, pe, pd, nu: (jnp.minimum(i, tiles_a - 1), 0)),
                pl.BlockSpec((tm, ROW_WORDS), lambda i, pe, pd, nu: (jnp.maximum(i - tiles_a, 0), 0)),
            ],
            out_specs=pl.BlockSpec(memory_space=pl.ANY),
            scratch_shapes=[pltpu.VMEM((bm, ROW_WORDS), jnp.uint32), pltpu.SemaphoreType.DMA(()),
                            pltpu.SemaphoreType.DMA(())],
        ),
        out_shape=jax.ShapeDtypeStruct((p_rows, ROW_WORDS), jnp.uint32),
        compiler_params=_cparams(("arbitrary",)),
        name="dispatch",
    )(pad_end, padded, nused, dest3, xq_a, xq_b)


def _experts_kernel(blk_e_ref, nused_ref, xs_ref, w1_ref, w3_ref, w2_ref, ys_ref, w1b, w3b, w2b):
    i = pl.program_id(0)

    @pl.when((i == 0) | (blk_e_ref[i] != blk_e_ref[jnp.maximum(i - 1, 0)]))
    def _():
        def cast_in(r, c):
            sl = pl.ds(pl.multiple_of(r * 256, 256), 256)
            w1b[sl, :] = w1_ref[sl, :].astype(BF16)
            w3b[sl, :] = w3_ref[sl, :].astype(BF16)
            return c

        def cast_out(r, c):
            sl = pl.ds(pl.multiple_of(r * 64, 64), 64)
            w2b[sl, :] = w2_ref[sl, :].astype(BF16)
            return c

        lax.fori_loop(0, D_MODEL // 256, cast_in, 0)
        lax.fori_loop(0, D_EXPERT // 64, cast_out, 0)

    @pl.when(i < nused_ref[0])
    def _():
        lo, hi = _unpack_halves(xs_ref[...])
        lo = lo.astype(BF16)
        hi = hi.astype(BF16)
        h1 = (jnp.dot(lo, w1b[0:ROW_WORDS, :], preferred_element_type=F32)
              + jnp.dot(hi, w1b[ROW_WORDS:D_MODEL, :], preferred_element_type=F32))
        h3 = (jnp.dot(lo, w3b[0:ROW_WORDS, :], preferred_element_type=F32)
              + jnp.dot(hi, w3b[ROW_WORDS:D_MODEL, :], preferred_element_type=F32))
        hb = (h1 * (1.0 / (1.0 + jnp.exp(-h1)))) * h3
        ys_ref[...] = _pack_halves(jnp.dot(hb.astype(BF16), w2b[...], preferred_element_type=F32))

    @pl.when(i >= nused_ref[0])
    def _():
        ys_ref[...] = jnp.zeros(ys_ref.shape, jnp.uint32)


def _experts(xs, blk_e, nused, w1, w3, w2, layer, bm):
    p_rows = xs.shape[0]
    nb = p_rows // bm
    rowmap = lambda i, be, nu: (jnp.minimum(i, nu[0] - 1), 0)
    wmap = lambda i, be, nu: (layer, be[i], 0, 0)
    return pl.pallas_call(
        _experts_kernel,
        grid_spec=pltpu.PrefetchScalarGridSpec(
            num_scalar_prefetch=2,
            grid=(nb,),
            in_specs=[
                pl.BlockSpec((bm, ROW_WORDS), rowmap),
                pl.BlockSpec((None, None, D_MODEL, D_EXPERT), wmap),
                pl.BlockSpec((None, None, D_MODEL, D_EXPERT), wmap),
                pl.BlockSpec((None, None, D_EXPERT, D_MODEL), wmap),
            ],
            out_specs=pl.BlockSpec((bm, ROW_WORDS), lambda i, be, nu: (i, 0)),
            scratch_shapes=[pltpu.VMEM((D_MODEL, D_EXPERT), BF16), pltpu.VMEM((D_MODEL, D_EXPERT), BF16),
                            pltpu.VMEM((D_EXPERT, D_MODEL), BF16)],
        ),
        out_shape=jax.ShapeDtypeStruct((p_rows, ROW_WORDS), jnp.uint32),
        compiler_params=_cparams(("arbitrary",)),
        name="experts",
    )(blk_e, nused, xs, w1, w3, w2)


def _combine_kernel(dest_ref, dnext_ref, x_ref, rg_ref, g_ref, ys_ref, out_ref, ybuf, sem, *, tm, rows, final):
    i = pl.program_id(0)
    nt = pl.num_programs(0)
    slot = lax.rem(i, 2)
    other = 1 - slot

    def gather_rows(d_ref, base, buf_slot):
        for jj in range(rows):
            j = base + jj
            _row_copy(ys_ref, d_ref[0, 0, j], ybuf.at[buf_slot, 0], j, sem.at[buf_slot]).start()
            _row_copy(ys_ref, d_ref[0, 0, tm + j], ybuf.at[buf_slot, 1], j, sem.at[buf_slot]).start()

    @pl.when(i == 0)
    def _():
        def first(r, c):
            gather_rows(dest_ref, pl.multiple_of(r * rows, rows), 0)
            return c

        lax.fori_loop(0, tm // rows, first, 0)

    for k in range(TOP_K):
        pltpu.make_async_copy(ys_ref.at[pl.ds(0, tm), :], ybuf.at[slot, k], sem.at[slot]).wait()

    def combine_rows(base):
        sl = pl.ds(base, rows)
        rg = rg_ref[sl, :]
        g1 = rg[:, 0:1]
        g2 = rg[:, 1:2]
        lo1, hi1 = _unpack_halves(ybuf[slot, 0, sl, :])
        lo2, hi2 = _unpack_halves(ybuf[slot, 1, sl, :])
        xa = x_ref[sl, 0:ROW_WORDS] + (g1 * lo1 + g2 * lo2)
        xb = x_ref[sl, ROW_WORDS:D_MODEL] + (g1 * hi1 + g2 * hi2)
        if final:
            ss = jnp.sum(xa * xa, axis=-1, keepdims=True) + jnp.sum(xb * xb, axis=-1, keepdims=True)
            sc = lax.rsqrt(ss / D_MODEL + EPS)
            xa = xa * sc * g_ref[:, 0:ROW_WORDS]
            xb = xb * sc * g_ref[:, ROW_WORDS:D_MODEL]
        out_ref[sl, 0:ROW_WORDS] = xa
        out_ref[sl, ROW_WORDS:D_MODEL] = xb

    @pl.when(i + 1 < nt)
    def _():
        def body(r, c):
            base = pl.multiple_of(r * rows, rows)
            gather_rows(dnext_ref, base, other)
            combine_rows(base)
            return c

        lax.fori_loop(0, tm // rows, body, 0, unroll=2)

    @pl.when(i + 1 == nt)
    def _():
        def body(r, c):
            combine_rows(pl.multiple_of(r * rows, rows))
            return c

        lax.fori_loop(0, tm // rows, body, 0, unroll=2)


def _combine(x_new, ys, dest3, rg, g_final, final):
    n = x_new.shape[0]
    tm = dest3.shape[2] // 2
    nt = n // tm
    kern = functools.partial(_combine_kernel, tm=tm, rows=16, final=final)
    return pl.pallas_call(
        kern,
        grid=(nt,),
        in_specs=[
            pl.BlockSpec((1, 1, 2 * tm), lambda i: (i, 0, 0), memory_space=pltpu.SMEM),
            pl.BlockSpec((1, 1, 2 * tm), lambda i: (jnp.minimum(i + 1, nt - 1), 0, 0), memory_space=pltpu.SMEM),
            pl.BlockSpec((tm, D_MODEL), lambda i: (i, 0)),
            pl.BlockSpec((tm, LANES), lambda i: (i, 0)),
            pl.BlockSpec((1, D_MODEL), lambda i: (0, 0)),
            pl.BlockSpec(memory_space=pl.ANY),
        ],
        out_specs=pl.BlockSpec((tm, D_MODEL), lambda i: (i, 0)),
        out_shape=jax.ShapeDtypeStruct((n, D_MODEL), F32),
        scratch_shapes=[pltpu.VMEM((2, TOP_K, tm, ROW_WORDS), jnp.uint32), pltpu.SemaphoreType.DMA((2,))],
        compiler_params=_cparams(("arbitrary",)),
        name="combine",
    )(dest3, dest3, x_new, rg, g_final, ys)


def _route_tables(ri_ts, counts, n_total, bm):
    cnts = [c[0, :N_EXPERTS].astype(jnp.int32) for c in counts]
    cnt = sum(cnts)
    padded = (cnt + bm - 1) // bm * bm
    pad_end = jnp.cumsum(padded)
    pad_start = pad_end - padded
    dests, first = [], pad_start
    for ri_t, c in zip(ri_ts, cnts):
        e1, e2, r1, r2 = ri_t[:, 0, :], ri_t[:, 1, :], ri_t[:, 2, :], ri_t[:, 3, :]
        s1 = jnp.zeros_like(e1)
        s2 = jnp.zeros_like(e2)
        for e in range(N_EXPERTS):
            s1 = jnp.where(e1 == e, first[e], s1)
            s2 = jnp.where(e2 == e, first[e], s2)
        dests.append(jnp.concatenate([s1 + r1, s2 + r2], axis=1)[:, None, :])
        first = first + c
    nb = -(-(n_total * TOP_K) // bm) + N_EXPERTS
    nused = (pad_end[-1] // bm).astype(jnp.int32)
    blk = jnp.minimum(jnp.arange(nb, dtype=jnp.int32), nused - 1) * bm
    blk_e = jnp.sum((pad_end[None, :] <= blk[:, None]).astype(jnp.int32), axis=1)
    blk_e = jnp.minimum(blk_e, N_EXPERTS - 1)
    return dests, pad_end.astype(jnp.int32), padded.astype(jnp.int32), blk_e, nused.reshape(1), nb * bm


def _mix_and_route(x2d, bsz, seq, conv0, c0, n0, m0, wts):
    n = bsz * seq
    pf, pb, li, lf = _inproj(x2d, wts["g_mix"], wts["w_main"], wts["wg"], wts["b_if"])
    n0p = n0.reshape(bsz, N_PAIRS, LANES)
    m0p = jnp.pad(m0, ((0, 0), (0, LANES - N_HEADS))).reshape(bsz, 1, LANES)
    by_row = lambda a: a.reshape(bsz, seq, a.shape[-1])
    mix, conv_n, c_n, n_n, m_n = _mixer(by_row(pf), by_row(pb), by_row(li), by_row(lf), wts["conv_w"],
                                         wts["head_gain"], conv0, c0, n0p, m0p, bsz, seq)
    x_new, xq, ri_t, rg, counts = _outproj(mix.reshape(n, D_MODEL), wts["w_out"], x2d, wts["g_ffn"],
                                           wts["wr"], wts["br"])
    states = (conv_n, c_n, n_n.reshape(bsz, N_HEADS, DQK), m_n[:, 0, :N_HEADS])
    return dict(x_new=x_new, xq=xq, ri_t=ri_t, rg=rg, counts=counts, states=states)


def _moe(groups, wts, g_final, final):
    bm = EXPERT_BLOCK
    n_total = sum(g["x_new"].shape[0] for g in groups)
    dests, pad_end, padded, blk_e, nused, p_rows = _route_tables(
        [g["ri_t"] for g in groups], [g["counts"] for g in groups], n_total, bm)
    xs = _dispatch(groups[0]["xq"], groups[1]["xq"], jnp.concatenate(dests, axis=0),
                   pad_end, padded, nused, p_rows, bm)
    ys = _experts(xs, blk_e, nused, wts["w1"], wts["w3"], wts["w2"], wts["layer"], bm)
    return [_combine(g["x_new"], ys, d, g["rg"], g_final, final) for g, d in zip(groups, dests)]


def _prep_weights(l, norm_mix, w_in, b_if, conv_w, head_gain, w_out, norm_ffn, w_router_group,
                  b_router_group, w_router_expert, b_router_expert, w1, w3, w2):
    wi = w_in[l]
    w_main = wi[:, :MAIN_COLS].astype(BF16)
    wg = jnp.pad(wi[:, MAIN_COLS:], ((0, 0), (0, LANES - 2 * N_HEADS))).astype(BF16)
    wr = jnp.pad(jnp.concatenate([w_router_expert[l], w_router_group[l]], axis=1),
                 ((0, 0), (0, LANES - N_EXPERTS - N_GROUPS))).astype(BF16)
    br = jnp.pad(jnp.concatenate([b_router_expert[l], b_router_group[l]]),
                 (0, LANES - N_EXPERTS - N_GROUPS)).reshape(1, LANES)
    return dict(
        g_mix=norm_mix[l].reshape(1, D_MODEL),
        w_main=w_main, wg=wg,
        b_if=jnp.pad(b_if[l], (0, LANES - 2 * N_HEADS)).reshape(1, LANES),
        conv_w=conv_w[l], head_gain=head_gain[l].reshape(1, MLSTM_DIM),
        w_out=w_out[l].astype(BF16),
        g_ffn=norm_ffn[l].reshape(1, D_MODEL),
        wr=wr, br=br,
        w1=w1, w3=w3, w2=w2, layer=l,
    )


def _trunks(xs_in, states_in, wts, g_final):
    depth = len(wts)
    shapes = [x.shape for x in xs_in]
    x2ds = [x.reshape(x.shape[0] * x.shape[1], D_MODEL) for x in xs_in]
    new_states = [[] for _ in xs_in]
    for l in range(depth):
        groups = []
        for gi, (x2d, shp, st) in enumerate(zip(x2ds, shapes, states_in)):
            grp = _mix_and_route(x2d, shp[0], shp[1], st[0][l], st[1][l], st[2][l], st[3][l], wts[l])
            new_states[gi].append(grp["states"])
            groups.append(grp)
        x2ds = _moe(groups, wts[l], g_final, l == depth - 1)
    outs = []
    for x2d, shp, sts in zip(x2ds, shapes, new_states):
        outs.append((x2d.reshape(shp),) + tuple(jnp.stack([s[k] for s in sts]) for k in range(4)))
    return outs


def kernel(x_prompt, x_sample, state_conv, state_mlstm_C, state_mlstm_n, state_mlstm_m,
           norm_mix, w_in, b_if, conv_w, head_gain, w_out, norm_ffn,
           w_router_group, b_router_group, w_router_expert, b_router_expert,
           w1, w3, w2, norm_final):
    depth = w_in.shape[0]
    wts = [_prep_weights(l, norm_mix, w_in, b_if, conv_w, head_gain, w_out, norm_ffn, w_router_group,
                         b_router_group, w_router_expert, b_router_expert, w1, w3, w2)
           for l in range(depth)]
    g_final = norm_final.reshape(1, D_MODEL)
    b = x_prompt.shape[0]
    conv0 = jnp.zeros((depth, b, CONV_WIDTH - 1, CONV_DIM), F32)
    c0 = jnp.zeros((depth, b, N_HEADS, DQK, DV), F32)
    n0 = jnp.zeros((depth, b, N_HEADS, DQK), F32)
    m0 = jnp.full((depth, b, N_HEADS), M_INIT, F32)
    (y_p, conv_p, c_p, n_p, m_p), (y_s, conv_s, c_s, n_s, m_s) = _trunks(
        [x_prompt, x_sample],
        [(conv0, c0, n0, m0), (state_conv, state_mlstm_C, state_mlstm_n, state_mlstm_m)],
        wts, g_final)
    return (y_p, y_s, conv_p, c_p, n_p, m_p, conv_s, c_s, n_s, m_s)
```

```python
import functools

import jax
import jax.numpy as jnp
from jax import lax
from jax.experimental import pallas as pl
from jax.experimental.pallas import tpu as pltpu

F32 = jnp.float32
BF16 = jnp.bfloat16

D_MODEL = 2048
CONV_DIM = 1024
N_HEADS = 8
N_PAIRS = N_HEADS // 2
DV = 128
DQK = 64
QK_DIM = N_HEADS * DQK
MLSTM_DIM = N_HEADS * DV
N_GROUPS = 4
EXPERTS_PER_GROUP = 8
N_EXPERTS = 32
TOP_K = 2
D_EXPERT = 512
EPS = 1e-6
M_INIT = -1e30
CONV_WIDTH = 3
LANES = 128
MAIN_COLS = 3 * CONV_DIM + 2 * QK_DIM + 2 * MLSTM_DIM
ROW_WORDS = D_MODEL // 2
PROJ_BLOCK = 1024
F32_SOURCE_BLOCKS = (0, 1, 2, 5, 3)
V_SOURCE_BLOCK = 4
CHUNK = 64
EXPERT_BLOCK = 256
CONV_GROUP_COLS = 256
CONV_GROUPS = CONV_DIM // CONV_GROUP_COLS
VMEM_LIMIT = 56 * 1024 * 1024


def _cparams(sem):
    return pltpu.CompilerParams(dimension_semantics=sem, vmem_limit_bytes=VMEM_LIMIT)


def _pack_halves(x):
    half = x.shape[1] // 2
    lo = pltpu.bitcast(x[:, :half].astype(BF16).astype(F32), jnp.uint32)
    hi = pltpu.bitcast(x[:, half:].astype(BF16).astype(F32), jnp.uint32)
    return (lo >> 16) | (hi & jnp.uint32(0xFFFF0000))


def _unpack_halves(w):
    lo = pltpu.bitcast(w << 16, F32)
    hi = pltpu.bitcast(w & jnp.uint32(0xFFFF0000), F32)
    return lo, hi


def _split3(x):
    hi = x.astype(BF16)
    r1 = x - hi.astype(F32)
    mid = r1.astype(BF16)
    lo = (r1 - mid.astype(F32)).astype(BF16)
    return hi, mid, lo


def _inproj_kernel(x_ref, g_ref, w_ref, wg_ref, b_ref, pf_ref, pb_ref, li_ref, lf_ref, xh_ref,
                   *, tm, rows):
    def body(r, c):
        sl = pl.ds(pl.multiple_of(r * rows, rows), rows)
        x = x_ref[sl, :]
        ms = jnp.mean(x * x, axis=-1, keepdims=True)
        xh_ref[sl, :] = (x * lax.rsqrt(ms + EPS) * g_ref[...]).astype(BF16)
        return c

    lax.fori_loop(0, tm // rows, body, 0, unroll=4)
    xh = xh_ref[...]
    gt = jnp.dot(xh, wg_ref[...], preferred_element_type=F32) + b_ref[...]
    lane = lax.broadcasted_iota(jnp.int32, gt.shape, 1)
    valid = lane < N_HEADS
    li_ref[...] = jnp.where(valid, gt, 0.0)
    fg = pltpu.roll(gt, LANES - N_HEADS, axis=1)
    lf = jnp.minimum(fg, 0.0) - jnp.log1p(jnp.exp(-jnp.abs(fg)))
    lf_ref[...] = jnp.where(valid, lf, 0.0)

    def block(src):
        return jnp.dot(xh, w_ref[:, src * PROJ_BLOCK:(src + 1) * PROJ_BLOCK], preferred_element_type=F32)

    for dst, src in enumerate(F32_SOURCE_BLOCKS):
        pf_ref[:, dst * PROJ_BLOCK:(dst + 1) * PROJ_BLOCK] = block(src)
    pb_ref[...] = block(V_SOURCE_BLOCK).astype(BF16)


def _inproj(x2d, g, w_all, wg, b_pad):
    n = x2d.shape[0]
    tm = min(n, 256)
    kern = functools.partial(_inproj_kernel, tm=tm, rows=32)
    const = lambda shape: pl.BlockSpec(shape, lambda i: (0, 0))
    return pl.pallas_call(
        kern,
        grid=(n // tm,),
        in_specs=[
            pl.BlockSpec((tm, D_MODEL), lambda i: (i, 0)),
            const((1, D_MODEL)),
            pl.BlockSpec(w_all.shape, lambda i: (0, 0), pipeline_mode=pl.Buffered(1)),
            const((D_MODEL, LANES)),
            const((1, LANES)),
        ],
        out_specs=[
            pl.BlockSpec((tm, len(F32_SOURCE_BLOCKS) * PROJ_BLOCK), lambda i: (i, 0)),
            pl.BlockSpec((tm, PROJ_BLOCK), lambda i: (i, 0)),
            pl.BlockSpec((tm, LANES), lambda i: (i, 0)),
            pl.BlockSpec((tm, LANES), lambda i: (i, 0)),
        ],
        out_shape=[
            jax.ShapeDtypeStruct((n, len(F32_SOURCE_BLOCKS) * PROJ_BLOCK), F32),
            jax.ShapeDtypeStruct((n, PROJ_BLOCK), BF16),
            jax.ShapeDtypeStruct((n, LANES), F32),
            jax.ShapeDtypeStruct((n, LANES), F32),
        ],
        scratch_shapes=[pltpu.VMEM((tm, D_MODEL), BF16)],
        compiler_params=_cparams(("arbitrary",)),
        name="inproj",
    )(x2d, g, w_all, wg, b_pad)


def _cummax_rows(x, length):
    row = lax.broadcasted_iota(jnp.int32, x.shape, 0)
    d = 1
    while d < length:
        shifted = pltpu.roll(x, d, axis=0)
        x = jnp.maximum(x, jnp.where(row >= d, shifted, -jnp.inf))
        d *= 2
    return x


def _pad_rows(x, length):
    if length == LANES:
        return x
    return jnp.concatenate([x, jnp.zeros((LANES - length, x.shape[1]), x.dtype)], axis=0)


def _mixer_kernel(u_ref, gc_ref, gb_ref, q_ref, k_ref, v_ref, o_ref, li_ref, lf_ref,
                  cw_ref, hg_ref, conv0_ref, c0_ref, n0_ref, m0_ref,
                  mix_ref, convn_ref, cn_ref, nn_ref, mn_ref,
                  *scratch, tb, L, G):
    t = pl.program_id(1)
    nt = pl.num_programs(1)
    n_state = G * N_PAIRS
    c_sc = [scratch[g * N_PAIRS:(g + 1) * N_PAIRS] for g in range(G)]
    n_sc = [scratch[n_state + g * N_PAIRS:n_state + (g + 1) * N_PAIRS] for g in range(G)]
    m_sc = scratch[2 * n_state:2 * n_state + G]
    carry_sc = [scratch[2 * n_state + G + g * CONV_GROUPS:2 * n_state + G + (g + 1) * CONV_GROUPS]
                for g in range(G)]

    @pl.when(t == 0)
    def _():
        zero = jnp.zeros((DQK, DV), F32)
        for g in range(G):
            for p in range(N_PAIRS):
                top = jnp.concatenate([c0_ref[g, 2 * p], zero], axis=1)
                bot = jnp.concatenate([zero, c0_ref[g, 2 * p + 1]], axis=1)
                c_sc[g][p][...] = jnp.concatenate([top, bot], axis=0)
                n_sc[g][p][...] = n0_ref[g, p:p + 1, :]
            m_sc[g][...] = m0_ref[g]
            for cg in range(CONV_GROUPS):
                cs = slice(cg * CONV_GROUP_COLS, (cg + 1) * CONV_GROUP_COLS)
                carry_sc[g][cg][...] = jnp.zeros((8, CONV_GROUP_COLS), F32)
                carry_sc[g][cg][6:8, :] = conv0_ref[g, :, cs]

    row = lax.broadcasted_iota(jnp.int32, (L, L), 0)
    col = lax.broadcasted_iota(jnp.int32, (L, L), 1)
    causal = col <= row
    tril = jnp.where(causal, 1.0, 0.0).astype(BF16)
    lane_l = lax.broadcasted_iota(jnp.int32, (L, LANES), 1)
    low_l = lane_l < DQK
    krow = lax.broadcasted_iota(jnp.int32, (LANES, 2 * DV), 0)
    lane1 = lax.broadcasted_iota(jnp.int32, (1, LANES), 1)

    def chunk(c, carry):
        rows = pl.ds(pl.multiple_of(c * L, L), L)
        units = [(g, p) for g in range(G) for p in range(N_PAIRS)]
        heads = [(g, h) for g in range(G) for h in range(N_HEADS)]
        convs = [conv_chunk(g, rows) for g in range(G)]
        gates = [gate_algebra(g, rows) for g in range(G)]
        outs, states = {}, {}
        for g, p in units:
            pr = pair_scores(g, p, rows)
            v2 = v_ref[g, rows, p * 2 * DV:(p + 1) * 2 * DV]
            for hh in range(2):
                h = 2 * p + hh
                s = intra_weights(gates[g], pr, h)
                intra = jnp.dot(s.astype(BF16), v2[:, hh * DV:(hh + 1) * DV], preferred_element_type=F32)
                o_sig = 1.0 / (1.0 + jnp.exp(-o_ref[g, rows, h * DV:(h + 1) * DV]))
                outs[(g, h)] = head_output(gates[g], pr, s, intra, o_sig, h)
            states[(g, p)] = state_update(gates[g], pr, v2, p)
        for g in range(G):
            for cg in range(CONV_GROUPS):
                y_conv, z_tail = convs[g][cg]
                mix_ref[g, rows, cg * CONV_GROUP_COLS:(cg + 1) * CONV_GROUP_COLS] = y_conv
                carry_sc[g][cg][6:8, :] = z_tail
            m_sc[g][...] = gates[g]["m_new"]
        for g, h in heads:
            mix_ref[g, rows, CONV_DIM + h * DV:CONV_DIM + (h + 1) * DV] = outs[(g, h)]
        for g, p in units:
            c_sc[g][p][...], n_sc[g][p][...] = states[(g, p)]
        return carry

    def conv_chunk(g, rows):
        cw = CONV_GROUP_COLS
        res = []
        for cg in range(CONV_GROUPS):
            cs = slice(cg * cw, (cg + 1) * cw)
            z = gc_ref[g, rows, cs] * u_ref[g, rows, cs]
            prev = carry_sc[g][cg][...]
            p1 = prev[7:8, :]
            p2 = prev[6:7, :]
            rw = lax.broadcasted_iota(jnp.int32, (L, cw), 0)
            z1 = jnp.where(rw >= 1, pltpu.roll(z, 1, axis=0), p1)
            z2 = jnp.where(rw >= 2, pltpu.roll(z, 2, axis=0), jnp.where(rw == 1, p1, p2))
            y = z2 * cw_ref[0:1, cs] + z1 * cw_ref[1:2, cs] + z * cw_ref[2:3, cs]
            res.append(((gb_ref[g, rows, cs] * y).astype(BF16), z[L - 2:L, :]))
        return res

    def gate_algebra(g, rows):
        li = li_ref[g, rows, :]
        lf = lf_ref[g, rows, :]
        hi, mid, lo = _split3(lf)
        F = (jnp.dot(tril, hi, preferred_element_type=F32)
             + jnp.dot(tril, mid, preferred_element_type=F32)
             + jnp.dot(tril, lo, preferred_element_type=F32))
        r = li - F
        cm = _cummax_rows(r, L)
        mprev = m_sc[g][...]
        mx = jnp.maximum(mprev, cm)
        M = F + mx
        neg_mx = -mx
        w_inter = jnp.exp(mprev - mx)
        em = jnp.exp(-M)
        gs = jnp.exp(r - mx[L - 1:L, :])
        g_inter = w_inter[L - 1:L, :]
        rT = _pad_rows(r, L).T[:, 0:L]
        return dict(neg_mx=neg_mx, w_inter=w_inter, em=em, gs=gs, g_inter=g_inter, rT=rT, m_new=M[L - 1:L, :])

    def pair_scores(g, p, rows):
        ps = slice(p * LANES, (p + 1) * LANES)
        q2 = q_ref[g, rows, ps].astype(BF16)
        kf = k_ref[g, rows, ps] * DQK ** -0.5
        k2 = kf.astype(BF16)
        q_e = jnp.where(low_l, q2, jnp.zeros_like(q2))
        q_o = jnp.where(low_l, jnp.zeros_like(q2), q2)
        q_st = jnp.concatenate([q_e, q_o], axis=0)
        n_row = n_sc[g][p][...]
        n_b = jnp.broadcast_to(n_row, (LANES, LANES)).astype(BF16)
        k_aug = jnp.concatenate([n_b, k2], axis=0)
        sn = lax.dot_general(q_st, k_aug, (((1,), (1,)), ((), ())),
                             preferred_element_type=F32)
        c_full = c_sc[g][p][...]
        qc = jnp.dot(q_st, c_full.astype(BF16), preferred_element_type=F32)
        return dict(kf=kf, k2=k2, n_row=n_row, c_full=c_full, sn=sn, qc=qc)

    def intra_weights(gt, pr, h):
        rs = slice((h % 2) * L, (h % 2 + 1) * L)
        dmat = gt["neg_mx"][:, h:h + 1] + gt["rT"][h:h + 1, :]
        return pr["sn"][rs, LANES:LANES + L] * jnp.exp(jnp.where(causal, dmat, -jnp.inf))

    def head_output(gt, pr, s, intra, o_sig, h):
        hh = h % 2
        rs = slice(hh * L, (hh + 1) * L)
        hs = slice(h * DV, (h + 1) * DV)
        wi = gt["w_inter"][:, h:h + 1]
        num = wi * pr["qc"][rs, hh * DV:(hh + 1) * DV] + intra
        den = wi * pr["sn"][rs, 0:LANES] + jnp.sum(s, axis=-1, keepdims=True)
        hv = num / jnp.maximum(jnp.abs(den), gt["em"][:, h:h + 1])
        ms = jnp.mean(hv * hv, axis=-1, keepdims=True)
        hn = hv * lax.rsqrt(ms + EPS) * hg_ref[0:1, hs]
        return (hn * o_sig).astype(BF16)

    def state_update(gt, pr, v2, p):
        gs = gt["gs"]
        kgw = jnp.where(low_l, gs[:, 2 * p:2 * p + 1], gs[:, 2 * p + 1:2 * p + 2])
        kg = pr["kf"] * kgw
        upd = jnp.dot(_pad_rows(kg, L).T.astype(BF16), _pad_rows(v2, L),
                      preferred_element_type=F32)
        ge = gt["g_inter"][0:1, 2 * p:2 * p + 1]
        go = gt["g_inter"][0:1, 2 * p + 1:2 * p + 2]
        c_new = jnp.where(krow < DQK, ge, go) * pr["c_full"] + upd
        kn = pr["k2"].astype(F32) * kgw.astype(BF16).astype(F32)
        n_new = jnp.where(lane1 < DQK, ge, go) * pr["n_row"] + jnp.sum(kn, axis=0, keepdims=True)
        return c_new, n_new

    lax.fori_loop(0, tb // L, chunk, 0)

    @pl.when(t == nt - 1)
    def _():
        for g in range(G):
            for cg in range(CONV_GROUPS):
                cs = slice(cg * CONV_GROUP_COLS, (cg + 1) * CONV_GROUP_COLS)
                convn_ref[g, :, cs] = carry_sc[g][cg][6:8, :]
            for p in range(N_PAIRS):
                cf = c_sc[g][p][...]
                cn_ref[g, 2 * p] = cf[0:DQK, 0:DV]
                cn_ref[g, 2 * p + 1] = cf[DQK:2 * DQK, DV:2 * DV]
                nn_ref[g, p:p + 1, :] = n_sc[g][p][...]
            mn_ref[g] = m_sc[g][...]


def _mixer(pf, pb, li, lf, conv_w, head_gain, conv0, c0, n0p, m0p, bsz, seq):
    G = 1
    L = min(seq, CHUNK)
    tb = min(seq, 512)
    nt = seq // tb
    kern = functools.partial(_mixer_kernel, tb=tb, L=L, G=G)
    wide = lambda cb: pl.BlockSpec((G, tb, 1024), lambda b, t: (b, t, cb))
    half = lambda cb: pl.BlockSpec((G, tb, 512), lambda b, t: (b, t, cb))
    gate = pl.BlockSpec((G, tb, LANES), lambda b, t: (b, t, 0))
    st = lambda *shape: pl.BlockSpec((G,) + shape, lambda b, t: (b,) + (0,) * len(shape))
    return pl.pallas_call(
        kern,
        grid=(bsz // G, nt),
        in_specs=[
            wide(0), wide(1), wide(2), half(8), half(9), wide(0), wide(3), gate, gate,
            pl.BlockSpec((CONV_WIDTH, CONV_DIM), lambda b, t: (0, 0)),
            pl.BlockSpec((1, MLSTM_DIM), lambda b, t: (0, 0)),
            st(2, CONV_DIM), st(N_HEADS, DQK, DV), st(N_PAIRS, LANES), st(1, LANES),
        ],
        out_specs=[
            pl.BlockSpec((G, tb, D_MODEL), lambda b, t: (b, t, 0)),
            st(2, CONV_DIM), st(N_HEADS, DQK, DV), st(N_PAIRS, LANES), st(1, LANES),
        ],
        out_shape=[
            jax.ShapeDtypeStruct((bsz, seq, D_MODEL), BF16),
            jax.ShapeDtypeStruct((bsz, 2, CONV_DIM), F32),
            jax.ShapeDtypeStruct((bsz, N_HEADS, DQK, DV), F32),
            jax.ShapeDtypeStruct((bsz, N_PAIRS, LANES), F32),
            jax.ShapeDtypeStruct((bsz, 1, LANES), F32),
        ],
        scratch_shapes=(
            [pltpu.VMEM((LANES, 2 * DV), F32)] * (G * N_PAIRS)
            + [pltpu.VMEM((1, LANES), F32)] * (G * N_PAIRS)
            + [pltpu.VMEM((1, LANES), F32)] * G
            + [pltpu.VMEM((8, CONV_GROUP_COLS), F32)] * (G * CONV_GROUPS)
        ),
        compiler_params=_cparams(("arbitrary", "arbitrary")),
        name="mixer",
    )(pf, pf, pf, pf, pf, pb, pf, li, lf, conv_w, head_gain, conv0, c0, n0p, m0p)


def _outproj_kernel(mix_ref, w_ref, x_ref, g_ref, wr_ref, br_ref,
                    xo_ref, xq_ref, ri_ref, rg_ref, cnt_ref, base_sc, *, tm):
    i = pl.program_id(0)

    @pl.when(i == 0)
    def _():
        base_sc[...] = jnp.zeros(base_sc.shape, F32)

    x = x_ref[...] + jnp.dot(mix_ref[...], w_ref[...], preferred_element_type=F32)
    xo_ref[...] = x
    ms = jnp.mean(x * x, axis=-1, keepdims=True)
    xn = x * lax.rsqrt(ms + EPS) * g_ref[...]
    xq_ref[...] = _pack_halves(xn)
    lg = jnp.dot(xn.astype(BF16), wr_ref[...], preferred_element_type=F32) + br_ref[...]

    lane = lax.broadcasted_iota(jnp.int32, (tm, LANES), 1)
    lanef = lane.astype(F32)
    big = jnp.float32(1e9)
    ninf = -jnp.inf
    is_g = (lane >= N_EXPERTS) & (lane < N_EXPERTS + N_GROUPS)
    glog = jnp.where(is_g, lg, ninf)
    gmax = jnp.max(glog, axis=1, keepdims=True)
    gi = jnp.min(jnp.where(glog == gmax, lanef, big), axis=1, keepdims=True) - N_EXPERTS
    pgi = 1.0 / jnp.sum(jnp.where(is_g, jnp.exp(lg - gmax), 0.0), axis=1, keepdims=True)
    lo = gi * EXPERTS_PER_GROUP
    in_grp = (lanef >= lo) & (lanef < lo + EXPERTS_PER_GROUP)
    el = jnp.where(in_grp, lg, ninf)
    v1 = jnp.max(el, axis=1, keepdims=True)
    i1 = jnp.min(jnp.where(el == v1, lanef, big), axis=1, keepdims=True)
    el2 = jnp.where(lanef == i1, ninf, el)
    v2 = jnp.max(el2, axis=1, keepdims=True)
    i2 = jnp.min(jnp.where(el2 == v2, lanef, big), axis=1, keepdims=True)
    e21 = jnp.exp(v2 - v1)
    g1 = pgi / (1.0 + e21)
    g2 = pgi * e21 / (1.0 + e21)
    sel1 = lanef == i1
    sel2 = lanef == i2
    oh = jnp.where(sel1 | sel2, 1.0, 0.0)
    r_i = lax.broadcasted_iota(jnp.int32, (tm, tm), 0)
    c_i = lax.broadcasted_iota(jnp.int32, (tm, tm), 1)
    stril = jnp.where(c_i < r_i, 1.0, 0.0).astype(BF16)
    tot = jnp.dot(stril, oh.astype(BF16), preferred_element_type=F32) + base_sc[...]
    r1 = jnp.sum(jnp.where(sel1, tot, 0.0), axis=1, keepdims=True)
    r2 = jnp.sum(jnp.where(sel2, tot, 0.0), axis=1, keepdims=True)
    base = base_sc[...] + jnp.sum(oh, axis=0, keepdims=True)
    base_sc[...] = base
    cnt_ref[...] = base
    ri = jnp.where(lane == 0, i1, jnp.where(lane == 1, i2, jnp.where(lane == 2, r1,
                   jnp.where(lane == 3, r2, 0.0))))
    ri_t = jnp.concatenate([ri[c * LANES:(c + 1) * LANES, :].T[0:8, :] for c in range(tm // LANES)], axis=1)
    ri_ref[...] = ri_t.astype(jnp.int32)
    rg_ref[...] = jnp.where(lane == 0, g1, jnp.where(lane == 1, g2, 0.0))


def _outproj(mix, w_out, x2d, g, wr, br):
    n = x2d.shape[0]
    tm = min(n, 256)
    assert tm % LANES == 0
    kern = functools.partial(_outproj_kernel, tm=tm)
    rowblk = lambda w: pl.BlockSpec((tm, w), lambda i: (i, 0))
    const = lambda r, c: pl.BlockSpec((r, c), lambda i: (0, 0))
    return pl.pallas_call(
        kern,
        grid=(n // tm,),
        in_specs=[rowblk(D_MODEL), const(D_MODEL, D_MODEL), rowblk(D_MODEL), const(1, D_MODEL),
                  const(D_MODEL, LANES), const(1, LANES)],
        out_specs=[rowblk(D_MODEL), rowblk(D_MODEL // 2), pl.BlockSpec((None, 8, tm), lambda i: (i, 0, 0)),
                   rowblk(LANES), const(1, LANES)],
        out_shape=[
            jax.ShapeDtypeStruct((n, D_MODEL), F32),
            jax.ShapeDtypeStruct((n, D_MODEL // 2), jnp.uint32),
            jax.ShapeDtypeStruct((n // tm, 8, tm), jnp.int32),
            jax.ShapeDtypeStruct((n, LANES), F32),
            jax.ShapeDtypeStruct((1, LANES), F32),
        ],
        scratch_shapes=[pltpu.VMEM((1, LANES), F32)],
        compiler_params=_cparams(("arbitrary",)),
        name="outproj",
    )(mix, w_out, x2d, g, wr, br)


def _row_copy(src_ref, src_row, dst_ref, dst_row, sem):
    return pltpu.make_async_copy(src_ref.at[pl.ds(src_row, 1), :], dst_ref.at[pl.ds(dst_row, 1), :], sem)


def _dispatch_kernel(pad_end_ref, padded_ref, nused_ref, dest_ref, xa_ref, xb_ref, xs_ref, zero_sc, sem_z, sem,
                     *, tm, bm, nb, tiles_a):
    i = pl.program_id(0)

    @pl.when(i == 0)
    def _():
        zero_sc[...] = jnp.zeros(zero_sc.shape, jnp.uint32)

        def zcopy(start):
            return pltpu.make_async_copy(zero_sc, xs_ref.at[pl.ds(pl.multiple_of(start, bm), bm), :], sem_z)

        def zstart(e, c):
            @pl.when(padded_ref[e] > 0)
            def _():
                zcopy(pad_end_ref[e] - bm).start()
            return c

        def zwait(e, c):
            @pl.when(padded_ref[e] > 0)
            def _():
                zcopy(pad_end_ref[e] - bm).wait()
            return c

        def tstart(b, c):
            zcopy(b * bm).start()
            return c

        def twait(b, c):
            zcopy(b * bm).wait()
            return c

        lax.fori_loop(0, N_EXPERTS, zstart, 0)
        lax.fori_loop(nused_ref[0], nb, tstart, 0)
        lax.fori_loop(0, N_EXPERTS, zwait, 0)
        lax.fori_loop(nused_ref[0], nb, twait, 0)

    def scatter_tile(src_ref):
        def start(j, c):
            _row_copy(src_ref, j, xs_ref, dest_ref[0, 0, j], sem).start()
            _row_copy(src_ref, j, xs_ref, dest_ref[0, 0, tm + j], sem).start()
            return c

        lax.fori_loop(0, tm, start, 0, unroll=8)
        for _ in range(TOP_K):
            pltpu.make_async_copy(src_ref, xs_ref.at[pl.ds(0, tm), :], sem).wait()

    @pl.when(i < tiles_a)
    def _():
        scatter_tile(xa_ref)

    @pl.when(i >= tiles_a)
    def _():
        scatter_tile(xb_ref)


def _dispatch(xq_a, xq_b, dest3, pad_end, padded, nused, p_rows, bm):
    tm = dest3.shape[2] // 2
    tiles_a, tiles_b = xq_a.shape[0] // tm, xq_b.shape[0] // tm
    kern = functools.partial(_dispatch_kernel, tm=tm, bm=bm, nb=p_rows // bm, tiles_a=tiles_a)
    return pl.pallas_call(
        kern,
        grid_spec=pltpu.PrefetchScalarGridSpec(
            num_scalar_prefetch=3,
            grid=(tiles_a + tiles_b,),
            in_specs=[
                pl.BlockSpec((1, 1, 2 * tm), lambda i, pe, pd, nu: (i, 0, 0), memory_space=pltpu.SMEM),
                pl.BlockSpec((tm, ROW_WORDS), lambda i, pe, pd, nu: (jnp.minimum(i, tiles_a - 1), 0)),
                pl.BlockSpec((tm, ROW_WORDS), lambda i, pe, pd, nu: (jnp.maximum(i - tiles_a, 0), 0)),
            ],
            out_specs=pl.BlockSpec(memory_space=pl.ANY),
            scratch_shapes=[pltpu.VMEM((bm, ROW_WORDS), jnp.uint32), pltpu.SemaphoreType.DMA(()),
                            pltpu.SemaphoreType.DMA(())],
        ),
        out_shape=jax.ShapeDtypeStruct((p_rows, ROW_WORDS), jnp.uint32),
        compiler_params=_cparams(("arbitrary",)),
        name="dispatch",
    )(pad_end, padded, nused, dest3, xq_a, xq_b)


def _experts_kernel(blk_e_ref, nused_ref, xs_ref, w1_ref, w3_ref, w2_ref, ys_ref, w1b, w3b, w2b):
    i = pl.program_id(0)

    @pl.when((i == 0) | (blk_e_ref[i] != blk_e_ref[jnp.maximum(i - 1, 0)]))
    def _():
        def cast_in(r, c):
            sl = pl.ds(pl.multiple_of(r * 256, 256), 256)
            w1b[sl, :] = w1_ref[sl, :].astype(BF16)
            w3b[sl, :] = w3_ref[sl, :].astype(BF16)
            return c

        def cast_out(r, c):
            sl = pl.ds(pl.multiple_of(r * 64, 64), 64)
            w2b[sl, :] = w2_ref[sl, :].astype(BF16)
            return c

        lax.fori_loop(0, D_MODEL // 256, cast_in, 0)
        lax.fori_loop(0, D_EXPERT // 64, cast_out, 0)

    @pl.when(i < nused_ref[0])
    def _():
        lo, hi = _unpack_halves(xs_ref[...])
        lo = lo.astype(BF16)
        hi = hi.astype(BF16)
        h1 = (jnp.dot(lo, w1b[0:ROW_WORDS, :], preferred_element_type=F32)
              + jnp.dot(hi, w1b[ROW_WORDS:D_MODEL, :], preferred_element_type=F32))
        h3 = (jnp.dot(lo, w3b[0:ROW_WORDS, :], preferred_element_type=F32)
              + jnp.dot(hi, w3b[ROW_WORDS:D_MODEL, :], preferred_element_type=F32))
        hb = (h1 * (1.0 / (1.0 + jnp.exp(-h1)))) * h3
        ys_ref[...] = _pack_halves(jnp.dot(hb.astype(BF16), w2b[...], preferred_element_type=F32))

    @pl.when(i >= nused_ref[0])
    def _():
        ys_ref[...] = jnp.zeros(ys_ref.shape, jnp.uint32)


def _experts(xs, blk_e, nused, w1, w3, w2, layer, bm):
    p_rows = xs.shape[0]
    nb = p_rows // bm
    rowmap = lambda i, be, nu: (jnp.minimum(i, nu[0] - 1), 0)
    wmap = lambda i, be, nu: (layer, be[i], 0, 0)
    return pl.pallas_call(
        _experts_kernel,
        grid_spec=pltpu.PrefetchScalarGridSpec(
            num_scalar_prefetch=2,
            grid=(nb,),
            in_specs=[
                pl.BlockSpec((bm, ROW_WORDS), rowmap),
                pl.BlockSpec((None, None, D_MODEL, D_EXPERT), wmap),
                pl.BlockSpec((None, None, D_MODEL, D_EXPERT), wmap),
                pl.BlockSpec((None, None, D_EXPERT, D_MODEL), wmap),
            ],
            out_specs=pl.BlockSpec((bm, ROW_WORDS), lambda i, be, nu: (i, 0)),
            scratch_shapes=[pltpu.VMEM((D_MODEL, D_EXPERT), BF16), pltpu.VMEM((D_MODEL, D_EXPERT), BF16),
                            pltpu.VMEM((D_EXPERT, D_MODEL), BF16)],
        ),
        out_shape=jax.ShapeDtypeStruct((p_rows, ROW_WORDS), jnp.uint32),
        compiler_params=_cparams(("arbitrary",)),
        name="experts",
    )(blk_e, nused, xs, w1, w3, w2)


def _combine_kernel(dest_ref, dnext_ref, x_ref, rg_ref, g_ref, ys_ref, out_ref, ybuf, sem, *, tm, rows, final):
    i = pl.program_id(0)
    nt = pl.num_programs(0)
    slot = lax.rem(i, 2)
    other = 1 - slot

    def gather_rows(d_ref, base, buf_slot):
        for jj in range(rows):
            j = base + jj
            _row_copy(ys_ref, d_ref[0, 0, j], ybuf.at[buf_slot, 0], j, sem.at[buf_slot]).start()
            _row_copy(ys_ref, d_ref[0, 0, tm + j], ybuf.at[buf_slot, 1], j, sem.at[buf_slot]).start()

    @pl.when(i == 0)
    def _():
        def first(r, c):
            gather_rows(dest_ref, pl.multiple_of(r * rows, rows), 0)
            return c

        lax.fori_loop(0, tm // rows, first, 0)

    for k in range(TOP_K):
        pltpu.make_async_copy(ys_ref.at[pl.ds(0, tm), :], ybuf.at[slot, k], sem.at[slot]).wait()

    def combine_rows(base):
        sl = pl.ds(base, rows)
        rg = rg_ref[sl, :]
        g1 = rg[:, 0:1]
        g2 = rg[:, 1:2]
        lo1, hi1 = _unpack_halves(ybuf[slot, 0, sl, :])
        lo2, hi2 = _unpack_halves(ybuf[slot, 1, sl, :])
        xa = x_ref[sl, 0:ROW_WORDS] + (g1 * lo1 + g2 * lo2)
        xb = x_ref[sl, ROW_WORDS:D_MODEL] + (g1 * hi1 + g2 * hi2)
        if final:
            ss = jnp.sum(xa * xa, axis=-1, keepdims=True) + jnp.sum(xb * xb, axis=-1, keepdims=True)
            sc = lax.rsqrt(ss / D_MODEL + EPS)
            xa = xa * sc * g_ref[:, 0:ROW_WORDS]
            xb = xb * sc * g_ref[:, ROW_WORDS:D_MODEL]
        out_ref[sl, 0:ROW_WORDS] = xa
        out_ref[sl, ROW_WORDS:D_MODEL] = xb

    @pl.when(i + 1 < nt)
    def _():
        def body(r, c):
            base = pl.multiple_of(r * rows, rows)
            gather_rows(dnext_ref, base, other)
            combine_rows(base)
            return c

        lax.fori_loop(0, tm // rows, body, 0, unroll=2)

    @pl.when(i + 1 == nt)
    def _():
        def body(r, c):
            combine_rows(pl.multiple_of(r * rows, rows))
            return c

        lax.fori_loop(0, tm // rows, body, 0, unroll=2)


def _combine(x_new, ys, dest3, rg, g_final, final):
    n = x_new.shape[0]
    tm = dest3.shape[2] // 2
    nt = n // tm
    kern = functools.partial(_combine_kernel, tm=tm, rows=16, final=final)
    return pl.pallas_call(
        kern,
        grid=(nt,),
        in_specs=[
            pl.BlockSpec((1, 1, 2 * tm), lambda i: (i, 0, 0), memory_space=pltpu.SMEM),
            pl.BlockSpec((1, 1, 2 * tm), lambda i: (jnp.minimum(i + 1, nt - 1), 0, 0), memory_space=pltpu.SMEM),
            pl.BlockSpec((tm, D_MODEL), lambda i: (i, 0)),
            pl.BlockSpec((tm, LANES), lambda i: (i, 0)),
            pl.BlockSpec((1, D_MODEL), lambda i: (0, 0)),
            pl.BlockSpec(memory_space=pl.ANY),
        ],
        out_specs=pl.BlockSpec((tm, D_MODEL), lambda i: (i, 0)),
        out_shape=jax.ShapeDtypeStruct((n, D_MODEL), F32),
        scratch_shapes=[pltpu.VMEM((2, TOP_K, tm, ROW_WORDS), jnp.uint32), pltpu.SemaphoreType.DMA((2,))],
        compiler_params=_cparams(("arbitrary",)),
        name="combine",
    )(dest3, dest3, x_new, rg, g_final, ys)


def _dest_kernel(first_ref, ri_ref, dest_ref, *, tiles, tm):
    for t in range(tiles):
        e1, e2 = ri_ref[t, 0:1, :], ri_ref[t, 1:2, :]
        s1 = jnp.zeros_like(e1)
        s2 = jnp.zeros_like(e2)
        for e in range(N_EXPERTS):
            s1 = jnp.where(e1 == e, first_ref[e], s1)
            s2 = jnp.where(e2 == e, first_ref[e], s2)
        dest_ref[t, :, 0:tm] = s1 + ri_ref[t, 2:3, :]
        dest_ref[t, :, tm:2 * tm] = s2 + ri_ref[t, 3:4, :]


def _dest_rows(first, ri_t):
    n_tiles, _, tm = ri_t.shape
    tiles = min(n_tiles, 16)
    return pl.pallas_call(
        functools.partial(_dest_kernel, tiles=tiles, tm=tm),
        grid_spec=pltpu.PrefetchScalarGridSpec(
            num_scalar_prefetch=1,
            grid=(n_tiles // tiles,),
            in_specs=[pl.BlockSpec((tiles, 8, tm), lambda i, f: (i, 0, 0))],
            out_specs=pl.BlockSpec((tiles, 1, 2 * tm), lambda i, f: (i, 0, 0)),
        ),
        out_shape=jax.ShapeDtypeStruct((n_tiles, 1, 2 * tm), jnp.int32),
        compiler_params=_cparams(("arbitrary",)),
        name="dest_rows",
    )(first, ri_t)


def _route_tables(ri_ts, counts, n_total, bm):
    cnts = [c[0, :N_EXPERTS].astype(jnp.int32) for c in counts]
    cnt = sum(cnts)
    padded = (cnt + bm - 1) // bm * bm
    pad_end = jnp.cumsum(padded)
    pad_start = pad_end - padded
    dests, first = [], pad_start
    for ri_t, c in zip(ri_ts, cnts):
        dests.append(_dest_rows(first.astype(jnp.int32), ri_t))
        first = first + c
    nb = -(-(n_total * TOP_K) // bm) + N_EXPERTS
    nused = (pad_end[-1] // bm).astype(jnp.int32)
    blk = jnp.minimum(jnp.arange(nb, dtype=jnp.int32), nused - 1) * bm
    blk_e = jnp.sum((pad_end[None, :] <= blk[:, None]).astype(jnp.int32), axis=1)
    blk_e = jnp.minimum(blk_e, N_EXPERTS - 1)
    return dests, pad_end.astype(jnp.int32), padded.astype(jnp.int32), blk_e, nused.reshape(1), nb * bm


def _mix_and_route(x2d, bsz, seq, conv0, c0, n0, m0, wts):
    n = bsz * seq
    pf, pb, li, lf = _inproj(x2d, wts["g_mix"], wts["w_main"], wts["wg"], wts["b_if"])
    n0p = n0.reshape(bsz, N_PAIRS, LANES)
    m0p = jnp.pad(m0, ((0, 0), (0, LANES - N_HEADS))).reshape(bsz, 1, LANES)
    by_row = lambda a: a.reshape(bsz, seq, a.shape[-1])
    mix, conv_n, c_n, n_n, m_n = _mixer(by_row(pf), by_row(pb), by_row(li), by_row(lf), wts["conv_w"],
                                         wts["head_gain"], conv0, c0, n0p, m0p, bsz, seq)
    x_new, xq, ri_t, rg, counts = _outproj(mix.reshape(n, D_MODEL), wts["w_out"], x2d, wts["g_ffn"],
                                           wts["wr"], wts["br"])
    states = (conv_n, c_n, n_n.reshape(bsz, N_HEADS, DQK), m_n[:, 0, :N_HEADS])
    return dict(x_new=x_new, xq=xq, ri_t=ri_t, rg=rg, counts=counts, states=states)


def _moe(groups, wts, g_final, final):
    bm = EXPERT_BLOCK
    n_total = sum(g["x_new"].shape[0] for g in groups)
    dests, pad_end, padded, blk_e, nused, p_rows = _route_tables(
        [g["ri_t"] for g in groups], [g["counts"] for g in groups], n_total, bm)
    xs = _dispatch(groups[0]["xq"], groups[1]["xq"], jnp.concatenate(dests, axis=0),
                   pad_end, padded, nused, p_rows, bm)
    ys = _experts(xs, blk_e, nused, wts["w1"], wts["w3"], wts["w2"], wts["layer"], bm)
    return [_combine(g["x_new"], ys, d, g["rg"], g_final, final) for g, d in zip(groups, dests)]


def _prep_weights(l, norm_mix, w_in, b_if, conv_w, head_gain, w_out, norm_ffn, w_router_group,
                  b_router_group, w_router_expert, b_router_expert, w1, w3, w2):
    wi = w_in[l]
    w_main = wi.astype(BF16)
    wg = jnp.pad(wi[:, MAIN_COLS:], ((0, 0), (0, LANES - 2 * N_HEADS))).astype(BF16)
    wr = jnp.pad(jnp.concatenate([w_router_expert[l], w_router_group[l]], axis=1),
                 ((0, 0), (0, LANES - N_EXPERTS - N_GROUPS))).astype(BF16)
    br = jnp.pad(jnp.concatenate([b_router_expert[l], b_router_group[l]]),
                 (0, LANES - N_EXPERTS - N_GROUPS)).reshape(1, LANES)
    return dict(
        g_mix=norm_mix[l].reshape(1, D_MODEL),
        w_main=w_main, wg=wg,
        b_if=jnp.pad(b_if[l], (0, LANES - 2 * N_HEADS)).reshape(1, LANES),
        conv_w=conv_w[l], head_gain=head_gain[l].reshape(1, MLSTM_DIM),
        w_out=w_out[l].astype(BF16),
        g_ffn=norm_ffn[l].reshape(1, D_MODEL),
        wr=wr, br=br,
        w1=w1, w3=w3, w2=w2, layer=l,
    )


def _trunks(xs_in, states_in, wts, g_final):
    depth = len(wts)
    shapes = [x.shape for x in xs_in]
    x2ds = [x.reshape(x.shape[0] * x.shape[1], D_MODEL) for x in xs_in]
    new_states = [[] for _ in xs_in]
    for l in range(depth):
        groups = []
        for gi, (x2d, shp, st) in enumerate(zip(x2ds, shapes, states_in)):
            grp = _mix_and_route(x2d, shp[0], shp[1], st[0][l], st[1][l], st[2][l], st[3][l], wts[l])
            new_states[gi].append(grp["states"])
            groups.append(grp)
        x2ds = _moe(groups, wts[l], g_final, l == depth - 1)
    outs = []
    for x2d, shp, sts in zip(x2ds, shapes, new_states):
        outs.append((x2d.reshape(shp),) + tuple(jnp.stack([s[k] for s in sts]) for k in range(4)))
    return outs


def kernel(x_prompt, x_sample, state_conv, state_mlstm_C, state_mlstm_n, state_mlstm_m,
           norm_mix, w_in, b_if, conv_w, head_gain, w_out, norm_ffn,
           w_router_group, b_router_group, w_router_expert, b_router_expert,
           w1, w3, w2, norm_final):
    depth = w_in.shape[0]
    wts = [_prep_weights(l, norm_mix, w_in, b_if, conv_w, head_gain, w_out, norm_ffn, w_router_group,
                         b_router_group, w_router_expert, b_router_expert, w1, w3, w2)
           for l in range(depth)]
    g_final = norm_final.reshape(1, D_MODEL)
    b = x_prompt.shape[0]
    conv0 = jnp.zeros((depth, b, CONV_WIDTH - 1, CONV_DIM), F32)
    c0 = jnp.zeros((depth, b, N_HEADS, DQK, DV), F32)
    n0 = jnp.zeros((depth, b, N_HEADS, DQK), F32)
    m0 = jnp.full((depth, b, N_HEADS), M_INIT, F32)
    (y_p, conv_p, c_p, n_p, m_p), (y_s, conv_s, c_s, n_s, m_s) = _trunks(
        [x_prompt, x_sample],
        [(conv0, c0, n0, m0), (state_conv, state_mlstm_C, state_mlstm_n, state_mlstm_m)],
        wts, g_final)
    return (y_p, y_s, conv_p, c_p, n_p, m_p, conv_s, c_s, n_s, m_s)
```

```python
import functools

import jax
import jax.numpy as jnp
from jax import lax
from jax.experimental import pallas as pl
from jax.experimental.pallas import tpu as pltpu

F32 = jnp.float32
BF16 = jnp.bfloat16

D_MODEL = 2048
CONV_DIM = 1024
N_HEADS = 8
N_PAIRS = N_HEADS // 2
DV = 128
DQK = 64
QK_DIM = N_HEADS * DQK
MLSTM_DIM = N_HEADS * DV
N_GROUPS = 4
EXPERTS_PER_GROUP = 8
N_EXPERTS = 32
TOP_K = 2
D_EXPERT = 512
EPS = 1e-6
M_INIT = -1e30
CONV_WIDTH = 3
LANES = 128
MAIN_COLS = 3 * CONV_DIM + 2 * QK_DIM + 2 * MLSTM_DIM
ROW_WORDS = D_MODEL // 2
PROJ_BLOCK = 1024
F32_SOURCE_BLOCKS = (0, 1, 2, 5, 3)
V_SOURCE_BLOCK = 4
CHUNK = 64
EXPERT_BLOCK = 256
CONV_GROUP_COLS = 256
CONV_GROUPS = CONV_DIM // CONV_GROUP_COLS
VMEM_LIMIT = 56 * 1024 * 1024


def _cparams(sem):
    return pltpu.CompilerParams(dimension_semantics=sem, vmem_limit_bytes=VMEM_LIMIT)


def _pack_halves(x):
    half = x.shape[1] // 2
    lo = pltpu.bitcast(x[:, :half].astype(BF16).astype(F32), jnp.uint32)
    hi = pltpu.bitcast(x[:, half:].astype(BF16).astype(F32), jnp.uint32)
    return (lo >> 16) | (hi & jnp.uint32(0xFFFF0000))


def _unpack_halves(w):
    lo = pltpu.bitcast(w << 16, F32)
    hi = pltpu.bitcast(w & jnp.uint32(0xFFFF0000), F32)
    return lo, hi


def _split3(x):
    hi = x.astype(BF16)
    r1 = x - hi.astype(F32)
    mid = r1.astype(BF16)
    lo = (r1 - mid.astype(F32)).astype(BF16)
    return hi, mid, lo


def _inproj_kernel(x_ref, g_ref, w_ref, wg_ref, b_ref, pf_ref, pb_ref, li_ref, lf_ref, xh_ref,
                   *, tm, rows):
    def body(r, c):
        sl = pl.ds(pl.multiple_of(r * rows, rows), rows)
        x = x_ref[sl, :]
        ms = jnp.mean(x * x, axis=-1, keepdims=True)
        xh_ref[sl, :] = (x * lax.rsqrt(ms + EPS) * g_ref[...]).astype(BF16)
        return c

    lax.fori_loop(0, tm // rows, body, 0, unroll=4)
    xh = xh_ref[...]
    gt = jnp.dot(xh, wg_ref[...], preferred_element_type=F32) + b_ref[...]
    lane = lax.broadcasted_iota(jnp.int32, gt.shape, 1)
    valid = lane < N_HEADS
    li_ref[...] = jnp.where(valid, gt, 0.0)
    fg = pltpu.roll(gt, LANES - N_HEADS, axis=1)
    lf = jnp.minimum(fg, 0.0) - jnp.log1p(jnp.exp(-jnp.abs(fg)))
    lf_ref[...] = jnp.where(valid, lf, 0.0)

    def block(src):
        return jnp.dot(xh, w_ref[:, src * PROJ_BLOCK:(src + 1) * PROJ_BLOCK], preferred_element_type=F32)

    for dst, src in enumerate(F32_SOURCE_BLOCKS):
        pf_ref[:, dst * PROJ_BLOCK:(dst + 1) * PROJ_BLOCK] = block(src)
    pb_ref[...] = block(V_SOURCE_BLOCK).astype(BF16)


def _inproj(x2d, g, w_all, wg, b_pad):
    n = x2d.shape[0]
    tm = min(n, 256)
    kern = functools.partial(_inproj_kernel, tm=tm, rows=32)
    const = lambda shape: pl.BlockSpec(shape, lambda i: (0, 0))
    return pl.pallas_call(
        kern,
        grid=(n // tm,),
        in_specs=[
            pl.BlockSpec((tm, D_MODEL), lambda i: (i, 0)),
            const((1, D_MODEL)),
            pl.BlockSpec(w_all.shape, lambda i: (0, 0), pipeline_mode=pl.Buffered(1)),
            const((D_MODEL, LANES)),
            const((1, LANES)),
        ],
        out_specs=[
            pl.BlockSpec((tm, len(F32_SOURCE_BLOCKS) * PROJ_BLOCK), lambda i: (i, 0)),
            pl.BlockSpec((tm, PROJ_BLOCK), lambda i: (i, 0)),
            pl.BlockSpec((tm, LANES), lambda i: (i, 0)),
            pl.BlockSpec((tm, LANES), lambda i: (i, 0)),
        ],
        out_shape=[
            jax.ShapeDtypeStruct((n, len(F32_SOURCE_BLOCKS) * PROJ_BLOCK), F32),
            jax.ShapeDtypeStruct((n, PROJ_BLOCK), BF16),
            jax.ShapeDtypeStruct((n, LANES), F32),
            jax.ShapeDtypeStruct((n, LANES), F32),
        ],
        scratch_shapes=[pltpu.VMEM((tm, D_MODEL), BF16)],
        compiler_params=_cparams(("arbitrary",)),
        name="inproj",
    )(x2d, g, w_all, wg, b_pad)


def _cummax_rows(x, length):
    row = lax.broadcasted_iota(jnp.int32, x.shape, 0)
    d = 1
    while d < length:
        shifted = pltpu.roll(x, d, axis=0)
        x = jnp.maximum(x, jnp.where(row >= d, shifted, -jnp.inf))
        d *= 2
    return x


def _pad_rows(x, length):
    if length == LANES:
        return x
    return jnp.concatenate([x, jnp.zeros((LANES - length, x.shape[1]), x.dtype)], axis=0)


def _mixer_kernel(u_ref, gc_ref, gb_ref, q_ref, k_ref, v_ref, o_ref, li_ref, lf_ref,
                  cw_ref, hg_ref, conv0_ref, c0_ref, n0_ref, m0_ref,
                  mix_ref, convn_ref, cn_ref, nn_ref, mn_ref,
                  *scratch, tb, L, G):
    t = pl.program_id(1)
    nt = pl.num_programs(1)
    n_state = G * N_PAIRS
    c_sc = [scratch[g * N_PAIRS:(g + 1) * N_PAIRS] for g in range(G)]
    n_sc = [scratch[n_state + g * N_PAIRS:n_state + (g + 1) * N_PAIRS] for g in range(G)]
    m_sc = scratch[2 * n_state:2 * n_state + G]
    carry_sc = [scratch[2 * n_state + G + g * CONV_GROUPS:2 * n_state + G + (g + 1) * CONV_GROUPS]
                for g in range(G)]

    @pl.when(t == 0)
    def _():
        zero = jnp.zeros((DQK, DV), F32)
        for g in range(G):
            for p in range(N_PAIRS):
                top = jnp.concatenate([c0_ref[g, 2 * p], zero], axis=1)
                bot = jnp.concatenate([zero, c0_ref[g, 2 * p + 1]], axis=1)
                c_sc[g][p][...] = jnp.concatenate([top, bot], axis=0)
                n_sc[g][p][...] = n0_ref[g, p:p + 1, :]
            m_sc[g][...] = m0_ref[g]
            for cg in range(CONV_GROUPS):
                cs = slice(cg * CONV_GROUP_COLS, (cg + 1) * CONV_GROUP_COLS)
                carry_sc[g][cg][...] = jnp.zeros((8, CONV_GROUP_COLS), F32)
                carry_sc[g][cg][6:8, :] = conv0_ref[g, :, cs]

    row = lax.broadcasted_iota(jnp.int32, (L, L), 0)
    col = lax.broadcasted_iota(jnp.int32, (L, L), 1)
    causal = col <= row
    tril = jnp.where(causal, 1.0, 0.0).astype(BF16)
    lane_l = lax.broadcasted_iota(jnp.int32, (L, LANES), 1)
    low_l = lane_l < DQK
    krow = lax.broadcasted_iota(jnp.int32, (LANES, 2 * DV), 0)
    lane1 = lax.broadcasted_iota(jnp.int32, (1, LANES), 1)

    def chunk(c, carry):
        rows = pl.ds(pl.multiple_of(c * L, L), L)
        units = [(g, p) for g in range(G) for p in range(N_PAIRS)]
        heads = [(g, h) for g in range(G) for h in range(N_HEADS)]
        convs = [conv_chunk(g, rows) for g in range(G)]
        gates = [gate_algebra(g, rows) for g in range(G)]
        outs, states = {}, {}
        for g, p in units:
            pr = pair_scores(g, p, rows)
            v2 = v_ref[g, rows, p * 2 * DV:(p + 1) * 2 * DV]
            for hh in range(2):
                h = 2 * p + hh
                s = intra_weights(gates[g], pr, h)
                intra = jnp.dot(s.astype(BF16), v2[:, hh * DV:(hh + 1) * DV], preferred_element_type=F32)
                o_sig = 1.0 / (1.0 + jnp.exp(-o_ref[g, rows, h * DV:(h + 1) * DV]))
                outs[(g, h)] = head_output(gates[g], pr, s, intra, o_sig, h)
            states[(g, p)] = state_update(gates[g], pr, v2, p)
        for g in range(G):
            for cg in range(CONV_GROUPS):
                y_conv, z_tail = convs[g][cg]
                mix_ref[g, rows, cg * CONV_GROUP_COLS:(cg + 1) * CONV_GROUP_COLS] = y_conv
                carry_sc[g][cg][6:8, :] = z_tail
            m_sc[g][...] = gates[g]["m_new"]
        for g, h in heads:
            mix_ref[g, rows, CONV_DIM + h * DV:CONV_DIM + (h + 1) * DV] = outs[(g, h)]
        for g, p in units:
            c_sc[g][p][...], n_sc[g][p][...] = states[(g, p)]
        return carry

    def conv_chunk(g, rows):
        cw = CONV_GROUP_COLS
        res = []
        for cg in range(CONV_GROUPS):
            cs = slice(cg * cw, (cg + 1) * cw)
            z = gc_ref[g, rows, cs] * u_ref[g, rows, cs]
            prev = carry_sc[g][cg][...]
            p1 = prev[7:8, :]
            p2 = prev[6:7, :]
            rw = lax.broadcasted_iota(jnp.int32, (L, cw), 0)
            z1 = jnp.where(rw >= 1, pltpu.roll(z, 1, axis=0), p1)
            z2 = jnp.where(rw >= 2, pltpu.roll(z, 2, axis=0), jnp.where(rw == 1, p1, p2))
            y = z2 * cw_ref[0:1, cs] + z1 * cw_ref[1:2, cs] + z * cw_ref[2:3, cs]
            res.append(((gb_ref[g, rows, cs] * y).astype(BF16), z[L - 2:L, :]))
        return res

    def gate_algebra(g, rows):
        li = li_ref[g, rows, :]
        lf = lf_ref[g, rows, :]
        hi, mid, lo = _split3(lf)
        F = (jnp.dot(tril, hi, preferred_element_type=F32)
             + jnp.dot(tril, mid, preferred_element_type=F32)
             + jnp.dot(tril, lo, preferred_element_type=F32))
        r = li - F
        cm = _cummax_rows(r, L)
        mprev = m_sc[g][...]
        mx = jnp.maximum(mprev, cm)
        M = F + mx
        neg_mx = -mx
        w_inter = jnp.exp(mprev - mx)
        em = jnp.exp(-M)
        gs = jnp.exp(r - mx[L - 1:L, :])
        g_inter = w_inter[L - 1:L, :]
        rT = _pad_rows(r, L).T[:, 0:L]
        return dict(neg_mx=neg_mx, w_inter=w_inter, em=em, gs=gs, g_inter=g_inter, rT=rT, m_new=M[L - 1:L, :])

    def pair_scores(g, p, rows):
        ps = slice(p * LANES, (p + 1) * LANES)
        q2 = q_ref[g, rows, ps].astype(BF16)
        kf = k_ref[g, rows, ps] * DQK ** -0.5
        k2 = kf.astype(BF16)
        q_e = jnp.where(low_l, q2, jnp.zeros_like(q2))
        q_o = jnp.where(low_l, jnp.zeros_like(q2), q2)
        q_st = jnp.concatenate([q_e, q_o], axis=0)
        n_row = n_sc[g][p][...]
        n_b = jnp.broadcast_to(n_row, (LANES, LANES)).astype(BF16)
        k_aug = jnp.concatenate([n_b, k2], axis=0)
        sn = lax.dot_general(q_st, k_aug, (((1,), (1,)), ((), ())),
                             preferred_element_type=F32)
        c_full = c_sc[g][p][...]
        qc = jnp.dot(q_st, c_full.astype(BF16), preferred_element_type=F32)
        return dict(kf=kf, k2=k2, n_row=n_row, c_full=c_full, sn=sn, qc=qc)

    def intra_weights(gt, pr, h):
        rs = slice((h % 2) * L, (h % 2 + 1) * L)
        dmat = gt["neg_mx"][:, h:h + 1] + gt["rT"][h:h + 1, :]
        return pr["sn"][rs, LANES:LANES + L] * jnp.exp(jnp.where(causal, dmat, -jnp.inf))

    def head_output(gt, pr, s, intra, o_sig, h):
        hh = h % 2
        rs = slice(hh * L, (hh + 1) * L)
        hs = slice(h * DV, (h + 1) * DV)
        wi = gt["w_inter"][:, h:h + 1]
        num = wi * pr["qc"][rs, hh * DV:(hh + 1) * DV] + intra
        den = wi * pr["sn"][rs, 0:LANES] + jnp.sum(s, axis=-1, keepdims=True)
        hv = num / jnp.maximum(jnp.abs(den), gt["em"][:, h:h + 1])
        ms = jnp.mean(hv * hv, axis=-1, keepdims=True)
        hn = hv * lax.rsqrt(ms + EPS) * hg_ref[0:1, hs]
        return (hn * o_sig).astype(BF16)

    def state_update(gt, pr, v2, p):
        gs = gt["gs"]
        kgw = jnp.where(low_l, gs[:, 2 * p:2 * p + 1], gs[:, 2 * p + 1:2 * p + 2])
        kg = pr["kf"] * kgw
        upd = jnp.dot(_pad_rows(kg, L).T.astype(BF16), _pad_rows(v2, L),
                      preferred_element_type=F32)
        ge = gt["g_inter"][0:1, 2 * p:2 * p + 1]
        go = gt["g_inter"][0:1, 2 * p + 1:2 * p + 2]
        c_new = jnp.where(krow < DQK, ge, go) * pr["c_full"] + upd
        kn = pr["k2"].astype(F32) * kgw.astype(BF16).astype(F32)
        n_new = jnp.where(lane1 < DQK, ge, go) * pr["n_row"] + jnp.sum(kn, axis=0, keepdims=True)
        return c_new, n_new

    lax.fori_loop(0, tb // L, chunk, 0)

    @pl.when(t == nt - 1)
    def _():
        for g in range(G):
            for cg in range(CONV_GROUPS):
                cs = slice(cg * CONV_GROUP_COLS, (cg + 1) * CONV_GROUP_COLS)
                convn_ref[g, :, cs] = carry_sc[g][cg][6:8, :]
            for p in range(N_PAIRS):
                cf = c_sc[g][p][...]
                cn_ref[g, 2 * p] = cf[0:DQK, 0:DV]
                cn_ref[g, 2 * p + 1] = cf[DQK:2 * DQK, DV:2 * DV]
                nn_ref[g, p:p + 1, :] = n_sc[g][p][...]
            mn_ref[g] = m_sc[g][...]


def _mixer(pf, pb, li, lf, conv_w, head_gain, conv0, c0, n0p, m0p, bsz, seq):
    G = 1
    L = min(seq, CHUNK)
    tb = min(seq, 512)
    nt = seq // tb
    kern = functools.partial(_mixer_kernel, tb=tb, L=L, G=G)
    wide = lambda cb: pl.BlockSpec((G, tb, 1024), lambda b, t: (b, t, cb))
    half = lambda cb: pl.BlockSpec((G, tb, 512), lambda b, t: (b, t, cb))
    gate = pl.BlockSpec((G, tb, LANES), lambda b, t: (b, t, 0))
    st = lambda *shape: pl.BlockSpec((G,) + shape, lambda b, t: (b,) + (0,) * len(shape))
    return pl.pallas_call(
        kern,
        grid=(bsz // G, nt),
        in_specs=[
            wide(0), wide(1), wide(2), half(8), half(9), wide(0), wide(3), gate, gate,
            pl.BlockSpec((CONV_WIDTH, CONV_DIM), lambda b, t: (0, 0)),
            pl.BlockSpec((1, MLSTM_DIM), lambda b, t: (0, 0)),
            st(2, CONV_DIM), st(N_HEADS, DQK, DV), st(N_PAIRS, LANES), st(1, LANES),
        ],
        out_specs=[
            pl.BlockSpec((G, tb, D_MODEL), lambda b, t: (b, t, 0)),
            st(2, CONV_DIM), st(N_HEADS, DQK, DV), st(N_PAIRS, LANES), st(1, LANES),
        ],
        out_shape=[
            jax.ShapeDtypeStruct((bsz, seq, D_MODEL), BF16),
            jax.ShapeDtypeStruct((bsz, 2, CONV_DIM), F32),
            jax.ShapeDtypeStruct((bsz, N_HEADS, DQK, DV), F32),
            jax.ShapeDtypeStruct((bsz, N_PAIRS, LANES), F32),
            jax.ShapeDtypeStruct((bsz, 1, LANES), F32),
        ],
        scratch_shapes=(
            [pltpu.VMEM((LANES, 2 * DV), F32)] * (G * N_PAIRS)
            + [pltpu.VMEM((1, LANES), F32)] * (G * N_PAIRS)
            + [pltpu.VMEM((1, LANES), F32)] * G
            + [pltpu.VMEM((8, CONV_GROUP_COLS), F32)] * (G * CONV_GROUPS)
        ),
        compiler_params=_cparams(("arbitrary", "arbitrary")),
        name="mixer",
    )(pf, pf, pf, pf, pf, pb, pf, li, lf, conv_w, head_gain, conv0, c0, n0p, m0p)


def _outproj_kernel(mix_ref, w_ref, x_ref, g_ref, wr_ref, br_ref,
                    xo_ref, xq_ref, ri_ref, rg_ref, cnt_ref, base_sc, *, tm):
    i = pl.program_id(0)

    @pl.when(i == 0)
    def _():
        base_sc[...] = jnp.zeros(base_sc.shape, F32)

    x = x_ref[...] + jnp.dot(mix_ref[...], w_ref[...], preferred_element_type=F32)
    xo_ref[...] = x
    ms = jnp.mean(x * x, axis=-1, keepdims=True)
    xn = x * lax.rsqrt(ms + EPS) * g_ref[...]
    xq_ref[...] = _pack_halves(xn)
    lg = jnp.dot(xn.astype(BF16), wr_ref[...], preferred_element_type=F32) + br_ref[...]

    lane = lax.broadcasted_iota(jnp.int32, (tm, LANES), 1)
    lanef = lane.astype(F32)
    big = jnp.float32(1e9)
    ninf = -jnp.inf
    is_g = (lane >= N_EXPERTS) & (lane < N_EXPERTS + N_GROUPS)
    glog = jnp.where(is_g, lg, ninf)
    gmax = jnp.max(glog, axis=1, keepdims=True)
    gi = jnp.min(jnp.where(glog == gmax, lanef, big), axis=1, keepdims=True) - N_EXPERTS
    pgi = 1.0 / jnp.sum(jnp.where(is_g, jnp.exp(lg - gmax), 0.0), axis=1, keepdims=True)
    lo = gi * EXPERTS_PER_GROUP
    in_grp = (lanef >= lo) & (lanef < lo + EXPERTS_PER_GROUP)
    el = jnp.where(in_grp, lg, ninf)
    v1 = jnp.max(el, axis=1, keepdims=True)
    i1 = jnp.min(jnp.where(el == v1, lanef, big), axis=1, keepdims=True)
    el2 = jnp.where(lanef == i1, ninf, el)
    v2 = jnp.max(el2, axis=1, keepdims=True)
    i2 = jnp.min(jnp.where(el2 == v2, lanef, big), axis=1, keepdims=True)
    e21 = jnp.exp(v2 - v1)
    g1 = pgi / (1.0 + e21)
    g2 = pgi * e21 / (1.0 + e21)
    sel1 = lanef == i1
    sel2 = lanef == i2
    oh = jnp.where(sel1 | sel2, 1.0, 0.0)
    r_i = lax.broadcasted_iota(jnp.int32, (tm, tm), 0)
    c_i = lax.broadcasted_iota(jnp.int32, (tm, tm), 1)
    stril = jnp.where(c_i < r_i, 1.0, 0.0).astype(BF16)
    tot = jnp.dot(stril, oh.astype(BF16), preferred_element_type=F32) + base_sc[...]
    r1 = jnp.sum(jnp.where(sel1, tot, 0.0), axis=1, keepdims=True)
    r2 = jnp.sum(jnp.where(sel2, tot, 0.0), axis=1, keepdims=True)
    base = base_sc[...] + jnp.sum(oh, axis=0, keepdims=True)
    base_sc[...] = base
    cnt_ref[...] = base
    ri = jnp.where(lane == 0, i1, jnp.where(lane == 1, i2, jnp.where(lane == 2, r1,
                   jnp.where(lane == 3, r2, 0.0))))
    ri_t = jnp.concatenate([ri[c * LANES:(c + 1) * LANES, :].T[0:8, :] for c in range(tm // LANES)], axis=1)
    ri_ref[...] = ri_t.astype(jnp.int32)
    rg_ref[...] = jnp.where(lane == 0, g1, jnp.where(lane == 1, g2, 0.0))


def _outproj(mix, w_out, x2d, g, wr, br):
    n = x2d.shape[0]
    tm = min(n, 256)
    assert tm % LANES == 0
    kern = functools.partial(_outproj_kernel, tm=tm)
    rowblk = lambda w: pl.BlockSpec((tm, w), lambda i: (i, 0))
    const = lambda r, c: pl.BlockSpec((r, c), lambda i: (0, 0))
    return pl.pallas_call(
        kern,
        grid=(n // tm,),
        in_specs=[rowblk(D_MODEL), const(D_MODEL, D_MODEL), rowblk(D_MODEL), const(1, D_MODEL),
                  const(D_MODEL, LANES), const(1, LANES)],
        out_specs=[rowblk(D_MODEL), rowblk(D_MODEL // 2), pl.BlockSpec((None, 8, tm), lambda i: (i, 0, 0)),
                   rowblk(LANES), const(1, LANES)],
        out_shape=[
            jax.ShapeDtypeStruct((n, D_MODEL), F32),
            jax.ShapeDtypeStruct((n, D_MODEL // 2), jnp.uint32),
            jax.ShapeDtypeStruct((n // tm, 8, tm), jnp.int32),
            jax.ShapeDtypeStruct((n, LANES), F32),
            jax.ShapeDtypeStruct((1, LANES), F32),
        ],
        scratch_shapes=[pltpu.VMEM((1, LANES), F32)],
        compiler_params=_cparams(("arbitrary",)),
        name="outproj",
    )(mix, w_out, x2d, g, wr, br)


def _row_copy(src_ref, src_row, dst_ref, dst_row, sem):
    return pltpu.make_async_copy(src_ref.at[pl.ds(src_row, 1), :], dst_ref.at[pl.ds(dst_row, 1), :], sem)


def _dispatch_kernel(pad_end_ref, padded_ref, nused_ref, dest_ref, xa_ref, xb_ref, xs_ref, zero_sc, sem_z, sem,
                     *, tm, bm, nb, tiles_a):
    i = pl.program_id(0)

    @pl.when(i == 0)
    def _():
        zero_sc[...] = jnp.zeros(zero_sc.shape, jnp.uint32)

        def zcopy(start):
            return pltpu.make_async_copy(zero_sc, xs_ref.at[pl.ds(pl.multiple_of(start, bm), bm), :], sem_z)

        def zstart(e, c):
            @pl.when(padded_ref[e] > 0)
            def _():
                zcopy(pad_end_ref[e] - bm).start()
            return c

        def zwait(e, c):
            @pl.when(padded_ref[e] > 0)
            def _():
                zcopy(pad_end_ref[e] - bm).wait()
            return c

        def tstart(b, c):
            zcopy(b * bm).start()
            return c

        def twait(b, c):
            zcopy(b * bm).wait()
            return c

        lax.fori_loop(0, N_EXPERTS, zstart, 0)
        lax.fori_loop(nused_ref[0], nb, tstart, 0)
        lax.fori_loop(0, N_EXPERTS, zwait, 0)
        lax.fori_loop(nused_ref[0], nb, twait, 0)

    def scatter_tile(src_ref):
        def start(j, c):
            _row_copy(src_ref, j, xs_ref, dest_ref[0, 0, j], sem).start(priority=0)
            _row_copy(src_ref, j, xs_ref, dest_ref[0, 0, tm + j], sem).start(priority=1)
            return c

        lax.fori_loop(0, tm, start, 0, unroll=8)
        for _ in range(TOP_K):
            pltpu.make_async_copy(src_ref, xs_ref.at[pl.ds(0, tm), :], sem).wait()

    @pl.when(i < tiles_a)
    def _():
        scatter_tile(xa_ref)

    @pl.when(i >= tiles_a)
    def _():
        scatter_tile(xb_ref)


def _dispatch(xq_a, xq_b, dest3, pad_end, padded, nused, p_rows, bm):
    tm = dest3.shape[2] // 2
    tiles_a, tiles_b = xq_a.shape[0] // tm, xq_b.shape[0] // tm
    kern = functools.partial(_dispatch_kernel, tm=tm, bm=bm, nb=p_rows // bm, tiles_a=tiles_a)
    return pl.pallas_call(
        kern,
        grid_spec=pltpu.PrefetchScalarGridSpec(
            num_scalar_prefetch=3,
            grid=(tiles_a + tiles_b,),
            in_specs=[
                pl.BlockSpec((1, 1, 2 * tm), lambda i, pe, pd, nu: (i, 0, 0), memory_space=pltpu.SMEM),
                pl.BlockSpec((tm, ROW_WORDS), lambda i, pe, pd, nu: (jnp.minimum(i, tiles_a - 1), 0)),
                pl.BlockSpec((tm, ROW_WORDS), lambda i, pe, pd, nu: (jnp.maximum(i - tiles_a, 0), 0)),
            ],
            out_specs=pl.BlockSpec(memory_space=pl.ANY),
            scratch_shapes=[pltpu.VMEM((bm, ROW_WORDS), jnp.uint32), pltpu.SemaphoreType.DMA(()),
                            pltpu.SemaphoreType.DMA(())],
        ),
        out_shape=jax.ShapeDtypeStruct((p_rows, ROW_WORDS), jnp.uint32),
        compiler_params=_cparams(("arbitrary",)),
        name="dispatch",
    )(pad_end, padded, nused, dest3, xq_a, xq_b)


def _experts_kernel(blk_e_ref, nused_ref, xs_ref, w1_ref, w3_ref, w2_ref, ys_ref, w1b, w3b, w2b):
    i = pl.program_id(0)

    @pl.when((i == 0) | (blk_e_ref[i] != blk_e_ref[jnp.maximum(i - 1, 0)]))
    def _():
        def cast_in(r, c):
            sl = pl.ds(pl.multiple_of(r * 256, 256), 256)
            w1b[sl, :] = w1_ref[sl, :].astype(BF16)
            w3b[sl, :] = w3_ref[sl, :].astype(BF16)
            return c

        def cast_out(r, c):
            sl = pl.ds(pl.multiple_of(r * 64, 64), 64)
            w2b[sl, :] = w2_ref[sl, :].astype(BF16)
            return c

        lax.fori_loop(0, D_MODEL // 256, cast_in, 0)
        lax.fori_loop(0, D_EXPERT // 64, cast_out, 0)

    @pl.when(i < nused_ref[0])
    def _():
        lo, hi = _unpack_halves(xs_ref[...])
        lo = lo.astype(BF16)
        hi = hi.astype(BF16)
        h1 = (jnp.dot(lo, w1b[0:ROW_WORDS, :], preferred_element_type=F32)
              + jnp.dot(hi, w1b[ROW_WORDS:D_MODEL, :], preferred_element_type=F32))
        h3 = (jnp.dot(lo, w3b[0:ROW_WORDS, :], preferred_element_type=F32)
              + jnp.dot(hi, w3b[ROW_WORDS:D_MODEL, :], preferred_element_type=F32))
        hb = (h1 * (1.0 / (1.0 + jnp.exp(-h1)))) * h3
        ys_ref[...] = _pack_halves(jnp.dot(hb.astype(BF16), w2b[...], preferred_element_type=F32))

    @pl.when(i >= nused_ref[0])
    def _():
        ys_ref[...] = jnp.zeros(ys_ref.shape, jnp.uint32)


def _experts(xs, blk_e, nused, w1, w3, w2, layer, bm):
    p_rows = xs.shape[0]
    nb = p_rows // bm
    rowmap = lambda i, be, nu: (jnp.minimum(i, nu[0] - 1), 0)
    wmap = lambda i, be, nu: (layer, be[i], 0, 0)
    return pl.pallas_call(
        _experts_kernel,
        grid_spec=pltpu.PrefetchScalarGridSpec(
            num_scalar_prefetch=2,
            grid=(nb,),
            in_specs=[
                pl.BlockSpec((bm, ROW_WORDS), rowmap),
                pl.BlockSpec((None, None, D_MODEL, D_EXPERT), wmap),
                pl.BlockSpec((None, None, D_MODEL, D_EXPERT), wmap),
                pl.BlockSpec((None, None, D_EXPERT, D_MODEL), wmap),
            ],
            out_specs=pl.BlockSpec((bm, ROW_WORDS), lambda i, be, nu: (i, 0)),
            scratch_shapes=[pltpu.VMEM((D_MODEL, D_EXPERT), BF16), pltpu.VMEM((D_MODEL, D_EXPERT), BF16),
                            pltpu.VMEM((D_EXPERT, D_MODEL), BF16)],
        ),
        out_shape=jax.ShapeDtypeStruct((p_rows, ROW_WORDS), jnp.uint32),
        compiler_params=_cparams(("arbitrary",)),
        name="experts",
    )(blk_e, nused, xs, w1, w3, w2)


def _combine_kernel(dest_ref, dnext_ref, x_ref, rg_ref, g_ref, ys_ref, out_ref, ybuf, sem, *, tm, rows, final):
    i = pl.program_id(0)
    nt = pl.num_programs(0)
    slot = lax.rem(i, 2)
    other = 1 - slot

    def gather_rows(d_ref, base, buf_slot):
        for jj in range(rows):
            j = base + jj
            _row_copy(ys_ref, d_ref[0, 0, j], ybuf.at[buf_slot, 0], j, sem.at[buf_slot]).start(priority=0)
            _row_copy(ys_ref, d_ref[0, 0, tm + j], ybuf.at[buf_slot, 1], j, sem.at[buf_slot]).start(priority=1)

    @pl.when(i == 0)
    def _():
        def first(r, c):
            gather_rows(dest_ref, pl.multiple_of(r * rows, rows), 0)
            return c

        lax.fori_loop(0, tm // rows, first, 0)

    for k in range(TOP_K):
        pltpu.make_async_copy(ys_ref.at[pl.ds(0, tm), :], ybuf.at[slot, k], sem.at[slot]).wait()

    def combine_rows(base):
        sl = pl.ds(base, rows)
        rg = rg_ref[sl, :]
        g1 = rg[:, 0:1]
        g2 = rg[:, 1:2]
        lo1, hi1 = _unpack_halves(ybuf[slot, 0, sl, :])
        lo2, hi2 = _unpack_halves(ybuf[slot, 1, sl, :])
        xa = x_ref[sl, 0:ROW_WORDS] + (g1 * lo1 + g2 * lo2)
        xb = x_ref[sl, ROW_WORDS:D_MODEL] + (g1 * hi1 + g2 * hi2)
        if final:
            ss = jnp.sum(xa * xa, axis=-1, keepdims=True) + jnp.sum(xb * xb, axis=-1, keepdims=True)
            sc = lax.rsqrt(ss / D_MODEL + EPS)
            xa = xa * sc * g_ref[:, 0:ROW_WORDS]
            xb = xb * sc * g_ref[:, ROW_WORDS:D_MODEL]
        out_ref[sl, 0:ROW_WORDS] = xa
        out_ref[sl, ROW_WORDS:D_MODEL] = xb

    @pl.when(i + 1 < nt)
    def _():
        def body(r, c):
            base = pl.multiple_of(r * rows, rows)
            gather_rows(dnext_ref, base, other)
            combine_rows(base)
            return c

        lax.fori_loop(0, tm // rows, body, 0, unroll=2)

    @pl.when(i + 1 == nt)
    def _():
        def body(r, c):
            combine_rows(pl.multiple_of(r * rows, rows))
            return c

        lax.fori_loop(0, tm // rows, body, 0, unroll=2)


def _combine(x_new, ys, dest3, rg, g_final, final):
    n = x_new.shape[0]
    tm = dest3.shape[2] // 2
    nt = n // tm
    kern = functools.partial(_combine_kernel, tm=tm, rows=16, final=final)
    return pl.pallas_call(
        kern,
        grid=(nt,),
        in_specs=[
            pl.BlockSpec((1, 1, 2 * tm), lambda i: (i, 0, 0), memory_space=pltpu.SMEM),
            pl.BlockSpec((1, 1, 2 * tm), lambda i: (jnp.minimum(i + 1, nt - 1), 0, 0), memory_space=pltpu.SMEM),
            pl.BlockSpec((tm, D_MODEL), lambda i: (i, 0)),
            pl.BlockSpec((tm, LANES), lambda i: (i, 0)),
            pl.BlockSpec((1, D_MODEL), lambda i: (0, 0)),
            pl.BlockSpec(memory_space=pl.ANY),
        ],
        out_specs=pl.BlockSpec((tm, D_MODEL), lambda i: (i, 0)),
        out_shape=jax.ShapeDtypeStruct((n, D_MODEL), F32),
        scratch_shapes=[pltpu.VMEM((2, TOP_K, tm, ROW_WORDS), jnp.uint32), pltpu.SemaphoreType.DMA((2,))],
        compiler_params=_cparams(("arbitrary",)),
        name="combine",
    )(dest3, dest3, x_new, rg, g_final, ys)


def _dest_kernel(first_ref, ri_ref, dest_ref, *, tiles, tm):
    for t in range(tiles):
        e1, e2 = ri_ref[t, 0:1, :], ri_ref[t, 1:2, :]
        s1 = jnp.zeros_like(e1)
        s2 = jnp.zeros_like(e2)
        for e in range(N_EXPERTS):
            s1 = jnp.where(e1 == e, first_ref[e], s1)
            s2 = jnp.where(e2 == e, first_ref[e], s2)
        dest_ref[t, :, 0:tm] = s1 + ri_ref[t, 2:3, :]
        dest_ref[t, :, tm:2 * tm] = s2 + ri_ref[t, 3:4, :]


def _dest_rows(first, ri_t):
    n_tiles, _, tm = ri_t.shape
    tiles = min(n_tiles, 16)
    return pl.pallas_call(
        functools.partial(_dest_kernel, tiles=tiles, tm=tm),
        grid_spec=pltpu.PrefetchScalarGridSpec(
            num_scalar_prefetch=1,
            grid=(n_tiles // tiles,),
            in_specs=[pl.BlockSpec((tiles, 8, tm), lambda i, f: (i, 0, 0))],
            out_specs=pl.BlockSpec((tiles, 1, 2 * tm), lambda i, f: (i, 0, 0)),
        ),
        out_shape=jax.ShapeDtypeStruct((n_tiles, 1, 2 * tm), jnp.int32),
        compiler_params=_cparams(("arbitrary",)),
        name="dest_rows",
    )(first, ri_t)


def _route_tables(ri_ts, counts, n_total, bm):
    cnts = [c[0, :N_EXPERTS].astype(jnp.int32) for c in counts]
    cnt = sum(cnts)
    padded = (cnt + bm - 1) // bm * bm
    pad_end = jnp.cumsum(padded)
    pad_start = pad_end - padded
    dests, first = [], pad_start
    for ri_t, c in zip(ri_ts, cnts):
        dests.append(_dest_rows(first.astype(jnp.int32), ri_t))
        first = first + c
    nb = -(-(n_total * TOP_K) // bm) + N_EXPERTS
    nused = (pad_end[-1] // bm).astype(jnp.int32)
    blk = jnp.minimum(jnp.arange(nb, dtype=jnp.int32), nused - 1) * bm
    blk_e = jnp.sum((pad_end[None, :] <= blk[:, None]).astype(jnp.int32), axis=1)
    blk_e = jnp.minimum(blk_e, N_EXPERTS - 1)
    return dests, pad_end.astype(jnp.int32), padded.astype(jnp.int32), blk_e, nused.reshape(1), nb * bm


def _mix_and_route(x2d, bsz, seq, conv0, c0, n0, m0, wts):
    n = bsz * seq
    pf, pb, li, lf = _inproj(x2d, wts["g_mix"], wts["w_main"], wts["wg"], wts["b_if"])
    n0p = n0.reshape(bsz, N_PAIRS, LANES)
    m0p = jnp.pad(m0, ((0, 0), (0, LANES - N_HEADS))).reshape(bsz, 1, LANES)
    by_row = lambda a: a.reshape(bsz, seq, a.shape[-1])
    mix, conv_n, c_n, n_n, m_n = _mixer(by_row(pf), by_row(pb), by_row(li), by_row(lf), wts["conv_w"],
                                         wts["head_gain"], conv0, c0, n0p, m0p, bsz, seq)
    x_new, xq, ri_t, rg, counts = _outproj(mix.reshape(n, D_MODEL), wts["w_out"], x2d, wts["g_ffn"],
                                           wts["wr"], wts["br"])
    states = (conv_n, c_n, n_n.reshape(bsz, N_HEADS, DQK), m_n[:, 0, :N_HEADS])
    return dict(x_new=x_new, xq=xq, ri_t=ri_t, rg=rg, counts=counts, states=states)


def _moe(groups, wts, g_final, final):
    bm = EXPERT_BLOCK
    n_total = sum(g["x_new"].shape[0] for g in groups)
    dests, pad_end, padded, blk_e, nused, p_rows = _route_tables(
        [g["ri_t"] for g in groups], [g["counts"] for g in groups], n_total, bm)
    xs = _dispatch(groups[0]["xq"], groups[1]["xq"], jnp.concatenate(dests, axis=0),
                   pad_end, padded, nused, p_rows, bm)
    ys = _experts(xs, blk_e, nused, wts["w1"], wts["w3"], wts["w2"], wts["layer"], bm)
    return [_combine(g["x_new"], ys, d, g["rg"], g_final, final) for g, d in zip(groups, dests)]


def _prep_weights(l, norm_mix, w_in, b_if, conv_w, head_gain, w_out, norm_ffn, w_router_group,
                  b_router_group, w_router_expert, b_router_expert, w1, w3, w2):
    wi = w_in[l]
    w_main = wi.astype(BF16)
    wg = jnp.pad(wi[:, MAIN_COLS:], ((0, 0), (0, LANES - 2 * N_HEADS))).astype(BF16)
    wr = jnp.pad(jnp.concatenate([w_router_expert[l], w_router_group[l]], axis=1),
                 ((0, 0), (0, LANES - N_EXPERTS - N_GROUPS))).astype(BF16)
    br = jnp.pad(jnp.concatenate([b_router_expert[l], b_router_group[l]]),
                 (0, LANES - N_EXPERTS - N_GROUPS)).reshape(1, LANES)
    return dict(
        g_mix=norm_mix[l].reshape(1, D_MODEL),
        w_main=w_main, wg=wg,
        b_if=jnp.pad(b_if[l], (0, LANES - 2 * N_HEADS)).reshape(1, LANES),
        conv_w=conv_w[l], head_gain=head_gain[l].reshape(1, MLSTM_DIM),
        w_out=w_out[l].astype(BF16),
        g_ffn=norm_ffn[l].reshape(1, D_MODEL),
        wr=wr, br=br,
        w1=w1, w3=w3, w2=w2, layer=l,
    )


def _trunks(xs_in, states_in, wts, g_final):
    depth = len(wts)
    shapes = [x.shape for x in xs_in]
    x2ds = [x.reshape(x.shape[0] * x.shape[1], D_MODEL) for x in xs_in]
    new_states = [[] for _ in xs_in]
    for l in range(depth):
        groups = []
        for gi, (x2d, shp, st) in enumerate(zip(x2ds, shapes, states_in)):
            grp = _mix_and_route(x2d, shp[0], shp[1], st[0][l], st[1][l], st[2][l], st[3][l], wts[l])
            new_states[gi].append(grp["states"])
            groups.append(grp)
        x2ds = _moe(groups, wts[l], g_final, l == depth - 1)
    outs = []
    for x2d, shp, sts in zip(x2ds, shapes, new_states):
        outs.append((x2d.reshape(shp),) + tuple(jnp.stack([s[k] for s in sts]) for k in range(4)))
    return outs


def kernel(x_prompt, x_sample, state_conv, state_mlstm_C, state_mlstm_n, state_mlstm_m,
           norm_mix, w_in, b_if, conv_w, head_gain, w_out, norm_ffn,
           w_router_group, b_router_group, w_router_expert, b_router_expert,
           w1, w3, w2, norm_final):
    depth = w_in.shape[0]
    wts = [_prep_weights(l, norm_mix, w_in, b_if, conv_w, head_gain, w_out, norm_ffn, w_router_group,
                         b_router_group, w_router_expert, b_router_expert, w1, w3, w2)
           for l in range(depth)]
    g_final = norm_final.reshape(1, D_MODEL)
    b = x_prompt.shape[0]
    conv0 = jnp.zeros((depth, b, CONV_WIDTH - 1, CONV_DIM), F32)
    c0 = jnp.zeros((depth, b, N_HEADS, DQK, DV), F32)
    n0 = jnp.zeros((depth, b, N_HEADS, DQK), F32)
    m0 = jnp.full((depth, b, N_HEADS), M_INIT, F32)
    (y_p, conv_p, c_p, n_p, m_p), (y_s, conv_s, c_s, n_s, m_s) = _trunks(
        [x_prompt, x_sample],
        [(conv0, c0, n0, m0), (state_conv, state_mlstm_C, state_mlstm_n, state_mlstm_m)],
        wts, g_final)
    return (y_p, y_s, conv_p, c_p, n_p, m_p, conv_s, c_s, n_s, m_s)
```

```python
import functools

import jax
import jax.numpy as jnp
from jax import lax
from jax.experimental import pallas as pl
from jax.experimental.pallas import tpu as pltpu

F32 = jnp.float32
BF16 = jnp.bfloat16

D_MODEL = 2048
CONV_DIM = 1024
N_HEADS = 8
N_PAIRS = N_HEADS // 2
DV = 128
DQK = 64
QK_DIM = N_HEADS * DQK
MLSTM_DIM = N_HEADS * DV
N_GROUPS = 4
EXPERTS_PER_GROUP = 8
N_EXPERTS = 32
TOP_K = 2
D_EXPERT = 512
EPS = 1e-6
M_INIT = -1e30
CONV_WIDTH = 3
LANES = 128
MAIN_COLS = 3 * CONV_DIM + 2 * QK_DIM + 2 * MLSTM_DIM
ROW_WORDS = D_MODEL // 2
PROJ_BLOCK = 1024
F32_SOURCE_BLOCKS = (0, 1, 2, 5, 3)
V_SOURCE_BLOCK = 4
CHUNK = 64
EXPERT_BLOCK = 256
TOKEN_TILE = 512
CONV_GROUP_COLS = 256
CONV_GROUPS = CONV_DIM // CONV_GROUP_COLS
VMEM_LIMIT = 56 * 1024 * 1024


def _cparams(sem):
    return pltpu.CompilerParams(dimension_semantics=sem, vmem_limit_bytes=VMEM_LIMIT)


def _pack_halves(x):
    half = x.shape[1] // 2
    lo = pltpu.bitcast(x[:, :half].astype(BF16).astype(F32), jnp.uint32)
    hi = pltpu.bitcast(x[:, half:].astype(BF16).astype(F32), jnp.uint32)
    return (lo >> 16) | (hi & jnp.uint32(0xFFFF0000))


def _unpack_halves(w):
    lo = pltpu.bitcast(w << 16, F32)
    hi = pltpu.bitcast(w & jnp.uint32(0xFFFF0000), F32)
    return lo, hi


def _split3(x):
    hi = x.astype(BF16)
    r1 = x - hi.astype(F32)
    mid = r1.astype(BF16)
    lo = (r1 - mid.astype(F32)).astype(BF16)
    return hi, mid, lo


def _inproj_kernel(x_ref, g_ref, w_ref, wg_ref, b_ref, pf_ref, pb_ref, li_ref, lf_ref, xh_ref,
                   *, tm, rows):
    def body(r, c):
        sl = pl.ds(pl.multiple_of(r * rows, rows), rows)
        x = x_ref[sl, :]
        ms = jnp.mean(x * x, axis=-1, keepdims=True)
        xh_ref[sl, :] = (x * lax.rsqrt(ms + EPS) * g_ref[...]).astype(BF16)
        return c

    lax.fori_loop(0, tm // rows, body, 0, unroll=4)
    xh = xh_ref[...]
    gt = jnp.dot(xh, wg_ref[...], preferred_element_type=F32) + b_ref[...]
    lane = lax.broadcasted_iota(jnp.int32, gt.shape, 1)
    valid = lane < N_HEADS
    li_ref[...] = jnp.where(valid, gt, 0.0)
    fg = pltpu.roll(gt, LANES - N_HEADS, axis=1)
    lf = jnp.minimum(fg, 0.0) - jnp.log1p(jnp.exp(-jnp.abs(fg)))
    lf_ref[...] = jnp.where(valid, lf, 0.0)

    def block(src):
        return jnp.dot(xh, w_ref[:, src * PROJ_BLOCK:(src + 1) * PROJ_BLOCK], preferred_element_type=F32)

    for dst, src in enumerate(F32_SOURCE_BLOCKS):
        pf_ref[:, dst * PROJ_BLOCK:(dst + 1) * PROJ_BLOCK] = block(src)
    pb_ref[...] = block(V_SOURCE_BLOCK).astype(BF16)


def _inproj(x2d, g, w_all, wg, b_pad):
    n = x2d.shape[0]
    tm = min(n, 256)
    kern = functools.partial(_inproj_kernel, tm=tm, rows=32)
    const = lambda shape: pl.BlockSpec(shape, lambda i: (0, 0))
    return pl.pallas_call(
        kern,
        grid=(n // tm,),
        in_specs=[
            pl.BlockSpec((tm, D_MODEL), lambda i: (i, 0)),
            const((1, D_MODEL)),
            pl.BlockSpec(w_all.shape, lambda i: (0, 0), pipeline_mode=pl.Buffered(1)),
            const((D_MODEL, LANES)),
            const((1, LANES)),
        ],
        out_specs=[
            pl.BlockSpec((tm, len(F32_SOURCE_BLOCKS) * PROJ_BLOCK), lambda i: (i, 0)),
            pl.BlockSpec((tm, PROJ_BLOCK), lambda i: (i, 0)),
            pl.BlockSpec((tm, LANES), lambda i: (i, 0)),
            pl.BlockSpec((tm, LANES), lambda i: (i, 0)),
        ],
        out_shape=[
            jax.ShapeDtypeStruct((n, len(F32_SOURCE_BLOCKS) * PROJ_BLOCK), F32),
            jax.ShapeDtypeStruct((n, PROJ_BLOCK), BF16),
            jax.ShapeDtypeStruct((n, LANES), F32),
            jax.ShapeDtypeStruct((n, LANES), F32),
        ],
        scratch_shapes=[pltpu.VMEM((tm, D_MODEL), BF16)],
        compiler_params=_cparams(("arbitrary",)),
        name="inproj",
    )(x2d, g, w_all, wg, b_pad)


def _cummax_rows(x, length):
    row = lax.broadcasted_iota(jnp.int32, x.shape, 0)
    d = 1
    while d < length:
        shifted = pltpu.roll(x, d, axis=0)
        x = jnp.maximum(x, jnp.where(row >= d, shifted, -jnp.inf))
        d *= 2
    return x


def _pad_rows(x, length):
    if length == LANES:
        return x
    return jnp.concatenate([x, jnp.zeros((LANES - length, x.shape[1]), x.dtype)], axis=0)


def _mixer_kernel(u_ref, gc_ref, gb_ref, q_ref, k_ref, v_ref, o_ref, li_ref, lf_ref,
                  cw_ref, hg_ref, conv0_ref, c0_ref, n0_ref, m0_ref,
                  mix_ref, convn_ref, cn_ref, nn_ref, mn_ref,
                  *scratch, tb, L, G):
    t = pl.program_id(1)
    nt = pl.num_programs(1)
    n_state = G * N_PAIRS
    c_sc = [scratch[g * N_PAIRS:(g + 1) * N_PAIRS] for g in range(G)]
    n_sc = [scratch[n_state + g * N_PAIRS:n_state + (g + 1) * N_PAIRS] for g in range(G)]
    m_sc = scratch[2 * n_state:2 * n_state + G]
    carry_sc = [scratch[2 * n_state + G + g * CONV_GROUPS:2 * n_state + G + (g + 1) * CONV_GROUPS]
                for g in range(G)]

    @pl.when(t == 0)
    def _():
        zero = jnp.zeros((DQK, DV), F32)
        for g in range(G):
            for p in range(N_PAIRS):
                top = jnp.concatenate([c0_ref[g, 2 * p], zero], axis=1)
                bot = jnp.concatenate([zero, c0_ref[g, 2 * p + 1]], axis=1)
                c_sc[g][p][...] = jnp.concatenate([top, bot], axis=0)
                n_sc[g][p][...] = n0_ref[g, p:p + 1, :]
            m_sc[g][...] = m0_ref[g]
            for cg in range(CONV_GROUPS):
                cs = slice(cg * CONV_GROUP_COLS, (cg + 1) * CONV_GROUP_COLS)
                carry_sc[g][cg][...] = jnp.zeros((8, CONV_GROUP_COLS), F32)
                carry_sc[g][cg][6:8, :] = conv0_ref[g, :, cs]

    row = lax.broadcasted_iota(jnp.int32, (L, L), 0)
    col = lax.broadcasted_iota(jnp.int32, (L, L), 1)
    causal = col <= row
    tril = jnp.where(causal, 1.0, 0.0).astype(BF16)
    lane_l = lax.broadcasted_iota(jnp.int32, (L, LANES), 1)
    low_l = lane_l < DQK
    krow = lax.broadcasted_iota(jnp.int32, (LANES, 2 * DV), 0)
    lane1 = lax.broadcasted_iota(jnp.int32, (1, LANES), 1)

    def chunk(c, carry):
        rows = pl.ds(pl.multiple_of(c * L, L), L)
        units = [(g, p) for g in range(G) for p in range(N_PAIRS)]
        heads = [(g, h) for g in range(G) for h in range(N_HEADS)]
        convs = [conv_chunk(g, rows) for g in range(G)]
        gates = [gate_algebra(g, rows) for g in range(G)]
        outs, states = {}, {}
        for g, p in units:
            pr = pair_scores(g, p, rows)
            v2 = v_ref[g, rows, p * 2 * DV:(p + 1) * 2 * DV]
            for hh in range(2):
                h = 2 * p + hh
                s = intra_weights(gates[g], pr, h)
                intra = jnp.dot(s.astype(BF16), v2[:, hh * DV:(hh + 1) * DV], preferred_element_type=F32)
                o_sig = 1.0 / (1.0 + jnp.exp(-o_ref[g, rows, h * DV:(h + 1) * DV]))
                outs[(g, h)] = head_output(gates[g], pr, s, intra, o_sig, h)
            states[(g, p)] = state_update(gates[g], pr, v2, p)
        for g in range(G):
            for cg in range(CONV_GROUPS):
                y_conv, z_tail = convs[g][cg]
                mix_ref[g, rows, cg * CONV_GROUP_COLS:(cg + 1) * CONV_GROUP_COLS] = y_conv
                carry_sc[g][cg][6:8, :] = z_tail
            m_sc[g][...] = gates[g]["m_new"]
        for g, h in heads:
            mix_ref[g, rows, CONV_DIM + h * DV:CONV_DIM + (h + 1) * DV] = outs[(g, h)]
        for g, p in units:
            c_sc[g][p][...], n_sc[g][p][...] = states[(g, p)]
        return carry

    def conv_chunk(g, rows):
        cw = CONV_GROUP_COLS
        res = []
        for cg in range(CONV_GROUPS):
            cs = slice(cg * cw, (cg + 1) * cw)
            z = gc_ref[g, rows, cs] * u_ref[g, rows, cs]
            prev = carry_sc[g][cg][...]
            p1 = prev[7:8, :]
            p2 = prev[6:7, :]
            rw = lax.broadcasted_iota(jnp.int32, (L, cw), 0)
            z1 = jnp.where(rw >= 1, pltpu.roll(z, 1, axis=0), p1)
            z2 = jnp.where(rw >= 2, pltpu.roll(z, 2, axis=0), jnp.where(rw == 1, p1, p2))
            y = z2 * cw_ref[0:1, cs] + z1 * cw_ref[1:2, cs] + z * cw_ref[2:3, cs]
            res.append(((gb_ref[g, rows, cs] * y).astype(BF16), z[L - 2:L, :]))
        return res

    def gate_algebra(g, rows):
        li = li_ref[g, rows, :]
        lf = lf_ref[g, rows, :]
        hi, mid, lo = _split3(lf)
        F = (jnp.dot(tril, hi, preferred_element_type=F32)
             + jnp.dot(tril, mid, preferred_element_type=F32)
             + jnp.dot(tril, lo, preferred_element_type=F32))
        r = li - F
        cm = _cummax_rows(r, L)
        mprev = m_sc[g][...]
        mx = jnp.maximum(mprev, cm)
        M = F + mx
        neg_mx = -mx
        w_inter = jnp.exp(mprev - mx)
        em = jnp.exp(-M)
        gs = jnp.exp(r - mx[L - 1:L, :])
        g_inter = w_inter[L - 1:L, :]
        rT = _pad_rows(r, L).T[:, 0:L]
        return dict(neg_mx=neg_mx, w_inter=w_inter, em=em, gs=gs, g_inter=g_inter, rT=rT, m_new=M[L - 1:L, :])

    def pair_scores(g, p, rows):
        ps = slice(p * LANES, (p + 1) * LANES)
        q2 = q_ref[g, rows, ps].astype(BF16)
        kf = k_ref[g, rows, ps] * DQK ** -0.5
        k2 = kf.astype(BF16)
        q_e = jnp.where(low_l, q2, jnp.zeros_like(q2))
        q_o = jnp.where(low_l, jnp.zeros_like(q2), q2)
        q_st = jnp.concatenate([q_e, q_o], axis=0)
        n_row = n_sc[g][p][...]
        n_b = jnp.broadcast_to(n_row, (LANES, LANES)).astype(BF16)
        k_aug = jnp.concatenate([n_b, k2], axis=0)
        sn = lax.dot_general(q_st, k_aug, (((1,), (1,)), ((), ())),
                             preferred_element_type=F32)
        c_full = c_sc[g][p][...]
        qc = jnp.dot(q_st, c_full.astype(BF16), preferred_element_type=F32)
        return dict(kf=kf, k2=k2, n_row=n_row, c_full=c_full, sn=sn, qc=qc)

    def intra_weights(gt, pr, h):
        rs = slice((h % 2) * L, (h % 2 + 1) * L)
        dmat = gt["neg_mx"][:, h:h + 1] + gt["rT"][h:h + 1, :]
        return pr["sn"][rs, LANES:LANES + L] * jnp.exp(jnp.where(causal, dmat, -jnp.inf))

    def head_output(gt, pr, s, intra, o_sig, h):
        hh = h % 2
        rs = slice(hh * L, (hh + 1) * L)
        hs = slice(h * DV, (h + 1) * DV)
        wi = gt["w_inter"][:, h:h + 1]
        num = wi * pr["qc"][rs, hh * DV:(hh + 1) * DV] + intra
        den = wi * pr["sn"][rs, 0:LANES] + jnp.sum(s, axis=-1, keepdims=True)
        hv = num / jnp.maximum(jnp.abs(den), gt["em"][:, h:h + 1])
        ms = jnp.mean(hv * hv, axis=-1, keepdims=True)
        hn = hv * lax.rsqrt(ms + EPS) * hg_ref[0:1, hs]
        return (hn * o_sig).astype(BF16)

    def state_update(gt, pr, v2, p):
        gs = gt["gs"]
        kgw = jnp.where(low_l, gs[:, 2 * p:2 * p + 1], gs[:, 2 * p + 1:2 * p + 2])
        kg = pr["kf"] * kgw
        upd = jnp.dot(_pad_rows(kg, L).T.astype(BF16), _pad_rows(v2, L),
                      preferred_element_type=F32)
        ge = gt["g_inter"][0:1, 2 * p:2 * p + 1]
        go = gt["g_inter"][0:1, 2 * p + 1:2 * p + 2]
        c_new = jnp.where(krow < DQK, ge, go) * pr["c_full"] + upd
        kn = pr["k2"].astype(F32) * kgw.astype(BF16).astype(F32)
        n_new = jnp.where(lane1 < DQK, ge, go) * pr["n_row"] + jnp.sum(kn, axis=0, keepdims=True)
        return c_new, n_new

    lax.fori_loop(0, tb // L, chunk, 0)

    @pl.when(t == nt - 1)
    def _():
        for g in range(G):
            for cg in range(CONV_GROUPS):
                cs = slice(cg * CONV_GROUP_COLS, (cg + 1) * CONV_GROUP_COLS)
                convn_ref[g, :, cs] = carry_sc[g][cg][6:8, :]
            for p in range(N_PAIRS):
                cf = c_sc[g][p][...]
                cn_ref[g, 2 * p] = cf[0:DQK, 0:DV]
                cn_ref[g, 2 * p + 1] = cf[DQK:2 * DQK, DV:2 * DV]
                nn_ref[g, p:p + 1, :] = n_sc[g][p][...]
            mn_ref[g] = m_sc[g][...]


def _mixer(pf, pb, li, lf, conv_w, head_gain, conv0, c0, n0p, m0p, bsz, seq):
    G = 1
    L = min(seq, CHUNK)
    tb = min(seq, 512)
    nt = seq // tb
    kern = functools.partial(_mixer_kernel, tb=tb, L=L, G=G)
    wide = lambda cb: pl.BlockSpec((G, tb, 1024), lambda b, t: (b, t, cb))
    half = lambda cb: pl.BlockSpec((G, tb, 512), lambda b, t: (b, t, cb))
    gate = pl.BlockSpec((G, tb, LANES), lambda b, t: (b, t, 0))
    st = lambda *shape: pl.BlockSpec((G,) + shape, lambda b, t: (b,) + (0,) * len(shape))
    return pl.pallas_call(
        kern,
        grid=(bsz // G, nt),
        in_specs=[
            wide(0), wide(1), wide(2), half(8), half(9), wide(0), wide(3), gate, gate,
            pl.BlockSpec((CONV_WIDTH, CONV_DIM), lambda b, t: (0, 0)),
            pl.BlockSpec((1, MLSTM_DIM), lambda b, t: (0, 0)),
            st(2, CONV_DIM), st(N_HEADS, DQK, DV), st(N_PAIRS, LANES), st(1, LANES),
        ],
        out_specs=[
            pl.BlockSpec((G, tb, D_MODEL), lambda b, t: (b, t, 0)),
            st(2, CONV_DIM), st(N_HEADS, DQK, DV), st(N_PAIRS, LANES), st(1, LANES),
        ],
        out_shape=[
            jax.ShapeDtypeStruct((bsz, seq, D_MODEL), BF16),
            jax.ShapeDtypeStruct((bsz, 2, CONV_DIM), F32),
            jax.ShapeDtypeStruct((bsz, N_HEADS, DQK, DV), F32),
            jax.ShapeDtypeStruct((bsz, N_PAIRS, LANES), F32),
            jax.ShapeDtypeStruct((bsz, 1, LANES), F32),
        ],
        scratch_shapes=(
            [pltpu.VMEM((LANES, 2 * DV), F32)] * (G * N_PAIRS)
            + [pltpu.VMEM((1, LANES), F32)] * (G * N_PAIRS)
            + [pltpu.VMEM((1, LANES), F32)] * G
            + [pltpu.VMEM((8, CONV_GROUP_COLS), F32)] * (G * CONV_GROUPS)
        ),
        compiler_params=_cparams(("arbitrary", "arbitrary")),
        name="mixer",
    )(pf, pf, pf, pf, pf, pb, pf, li, lf, conv_w, head_gain, conv0, c0, n0p, m0p)


def _outproj_kernel(mix_ref, w_ref, x_ref, g_ref, wr_ref, br_ref,
                    xo_ref, xq_ref, ri_ref, rg_ref, cnt_ref, base_sc, *, tm):
    i = pl.program_id(0)

    @pl.when(i == 0)
    def _():
        base_sc[...] = jnp.zeros(base_sc.shape, F32)

    x = x_ref[...] + jnp.dot(mix_ref[...], w_ref[...], preferred_element_type=F32)
    xo_ref[...] = x
    ms = jnp.mean(x * x, axis=-1, keepdims=True)
    xn = x * lax.rsqrt(ms + EPS) * g_ref[...]
    xq_ref[...] = _pack_halves(xn)
    lg = jnp.dot(xn.astype(BF16), wr_ref[...], preferred_element_type=F32) + br_ref[...]

    lane = lax.broadcasted_iota(jnp.int32, (tm, LANES), 1)
    lanef = lane.astype(F32)
    big = jnp.float32(1e9)
    ninf = -jnp.inf
    is_g = (lane >= N_EXPERTS) & (lane < N_EXPERTS + N_GROUPS)
    glog = jnp.where(is_g, lg, ninf)
    gmax = jnp.max(glog, axis=1, keepdims=True)
    gi = jnp.min(jnp.where(glog == gmax, lanef, big), axis=1, keepdims=True) - N_EXPERTS
    pgi = 1.0 / jnp.sum(jnp.where(is_g, jnp.exp(lg - gmax), 0.0), axis=1, keepdims=True)
    lo = gi * EXPERTS_PER_GROUP
    in_grp = (lanef >= lo) & (lanef < lo + EXPERTS_PER_GROUP)
    el = jnp.where(in_grp, lg, ninf)
    v1 = jnp.max(el, axis=1, keepdims=True)
    i1 = jnp.min(jnp.where(el == v1, lanef, big), axis=1, keepdims=True)
    el2 = jnp.where(lanef == i1, ninf, el)
    v2 = jnp.max(el2, axis=1, keepdims=True)
    i2 = jnp.min(jnp.where(el2 == v2, lanef, big), axis=1, keepdims=True)
    e21 = jnp.exp(v2 - v1)
    g1 = pgi / (1.0 + e21)
    g2 = pgi * e21 / (1.0 + e21)
    sel1 = lanef == i1
    sel2 = lanef == i2
    oh = jnp.where(sel1 | sel2, 1.0, 0.0)
    r_i = lax.broadcasted_iota(jnp.int32, (tm, tm), 0)
    c_i = lax.broadcasted_iota(jnp.int32, (tm, tm), 1)
    stril = jnp.where(c_i < r_i, 1.0, 0.0).astype(BF16)
    tot = jnp.dot(stril, oh.astype(BF16), preferred_element_type=F32) + base_sc[...]
    r1 = jnp.sum(jnp.where(sel1, tot, 0.0), axis=1, keepdims=True)
    r2 = jnp.sum(jnp.where(sel2, tot, 0.0), axis=1, keepdims=True)
    base = base_sc[...] + jnp.sum(oh, axis=0, keepdims=True)
    base_sc[...] = base
    cnt_ref[...] = base
    ri = jnp.where(lane == 0, i1, jnp.where(lane == 1, i2, jnp.where(lane == 2, r1,
                   jnp.where(lane == 3, r2, 0.0))))
    ri_t = jnp.concatenate([ri[c * LANES:(c + 1) * LANES, :].T[0:8, :] for c in range(tm // LANES)], axis=1)
    ri_ref[...] = ri_t.astype(jnp.int32)
    rg_ref[...] = jnp.where(lane == 0, g1, jnp.where(lane == 1, g2, 0.0))


def _outproj(mix, w_out, x2d, g, wr, br):
    n = x2d.shape[0]
    tm = min(n, TOKEN_TILE)
    assert tm % LANES == 0
    kern = functools.partial(_outproj_kernel, tm=tm)
    rowblk = lambda w: pl.BlockSpec((tm, w), lambda i: (i, 0))
    const = lambda r, c: pl.BlockSpec((r, c), lambda i: (0, 0))
    return pl.pallas_call(
        kern,
        grid=(n // tm,),
        in_specs=[rowblk(D_MODEL), const(D_MODEL, D_MODEL), rowblk(D_MODEL), const(1, D_MODEL),
                  const(D_MODEL, LANES), const(1, LANES)],
        out_specs=[rowblk(D_MODEL), rowblk(D_MODEL // 2), pl.BlockSpec((None, 8, tm), lambda i: (i, 0, 0)),
                   rowblk(LANES), const(1, LANES)],
        out_shape=[
            jax.ShapeDtypeStruct((n, D_MODEL), F32),
            jax.ShapeDtypeStruct((n, D_MODEL // 2), jnp.uint32),
            jax.ShapeDtypeStruct((n // tm, 8, tm), jnp.int32),
            jax.ShapeDtypeStruct((n, LANES), F32),
            jax.ShapeDtypeStruct((1, LANES), F32),
        ],
        scratch_shapes=[pltpu.VMEM((1, LANES), F32)],
        compiler_params=_cparams(("arbitrary",)),
        name="outproj",
    )(mix, w_out, x2d, g, wr, br)


def _row_copy(src_ref, src_row, dst_ref, dst_row, sem):
    return pltpu.make_async_copy(src_ref.at[pl.ds(src_row, 1), :], dst_ref.at[pl.ds(dst_row, 1), :], sem)


def _dispatch_kernel(pad_end_ref, padded_ref, nused_ref, da_ref, db_ref, xa_ref, xb_ref, xs_ref, zero_sc, sem_z,
                     sem, *, bm, nb, tiles_a):
    i = pl.program_id(0)

    @pl.when(i == 0)
    def _():
        zero_sc[...] = jnp.zeros(zero_sc.shape, jnp.uint32)

        def zcopy(start):
            return pltpu.make_async_copy(zero_sc, xs_ref.at[pl.ds(pl.multiple_of(start, bm), bm), :], sem_z)

        def zstart(e, c):
            @pl.when(padded_ref[e] > 0)
            def _():
                zcopy(pad_end_ref[e] - bm).start()
            return c

        def zwait(e, c):
            @pl.when(padded_ref[e] > 0)
            def _():
                zcopy(pad_end_ref[e] - bm).wait()
            return c

        def tstart(b, c):
            zcopy(b * bm).start()
            return c

        def twait(b, c):
            zcopy(b * bm).wait()
            return c

        lax.fori_loop(0, N_EXPERTS, zstart, 0)
        lax.fori_loop(nused_ref[0], nb, tstart, 0)
        lax.fori_loop(0, N_EXPERTS, zwait, 0)
        lax.fori_loop(nused_ref[0], nb, twait, 0)

    def scatter_tile(src_ref, dest_ref):
        tm = src_ref.shape[0]

        def start(j, c):
            _row_copy(src_ref, j, xs_ref, dest_ref[0, 0, j], sem).start()
            _row_copy(src_ref, j, xs_ref, dest_ref[0, 0, tm + j], sem).start()
            return c

        lax.fori_loop(0, tm, start, 0, unroll=8)
        for _ in range(TOP_K):
            pltpu.make_async_copy(src_ref, xs_ref.at[pl.ds(0, tm), :], sem).wait()

    @pl.when(i < tiles_a)
    def _():
        scatter_tile(xa_ref, da_ref)

    @pl.when(i >= tiles_a)
    def _():
        scatter_tile(xb_ref, db_ref)


def _dispatch(xq_a, xq_b, dest_a, dest_b, pad_end, padded, nused, p_rows, bm):
    tm_a, tm_b = dest_a.shape[2] // 2, dest_b.shape[2] // 2
    tiles_a, tiles_b = dest_a.shape[0], dest_b.shape[0]
    kern = functools.partial(_dispatch_kernel, bm=bm, nb=p_rows // bm, tiles_a=tiles_a)
    in_a = lambda i, pe, pd, nu: jnp.minimum(i, tiles_a - 1)
    in_b = lambda i, pe, pd, nu: jnp.maximum(i - tiles_a, 0)
    return pl.pallas_call(
        kern,
        grid_spec=pltpu.PrefetchScalarGridSpec(
            num_scalar_prefetch=3,
            grid=(tiles_a + tiles_b,),
            in_specs=[
                pl.BlockSpec((1, 1, 2 * tm_a), lambda *a: (in_a(*a), 0, 0), memory_space=pltpu.SMEM),
                pl.BlockSpec((1, 1, 2 * tm_b), lambda *a: (in_b(*a), 0, 0), memory_space=pltpu.SMEM),
                pl.BlockSpec((tm_a, ROW_WORDS), lambda *a: (in_a(*a), 0)),
                pl.BlockSpec((tm_b, ROW_WORDS), lambda *a: (in_b(*a), 0)),
            ],
            out_specs=pl.BlockSpec(memory_space=pl.ANY),
            scratch_shapes=[pltpu.VMEM((bm, ROW_WORDS), jnp.uint32), pltpu.SemaphoreType.DMA(()),
                            pltpu.SemaphoreType.DMA(())],
        ),
        out_shape=jax.ShapeDtypeStruct((p_rows, ROW_WORDS), jnp.uint32),
        compiler_params=_cparams(("arbitrary",)),
        name="dispatch",
    )(pad_end, padded, nused, dest_a, dest_b, xq_a, xq_b)


def _experts_kernel(blk_e_ref, nused_ref, xs_ref, w1_ref, w3_ref, w2_ref, ys_ref, w1b, w3b, w2b):
    i = pl.program_id(0)

    @pl.when((i == 0) | (blk_e_ref[i] != blk_e_ref[jnp.maximum(i - 1, 0)]))
    def _():
        def cast_in(r, c):
            sl = pl.ds(pl.multiple_of(r * 256, 256), 256)
            w1b[sl, :] = w1_ref[sl, :].astype(BF16)
            w3b[sl, :] = w3_ref[sl, :].astype(BF16)
            return c

        def cast_out(r, c):
            sl = pl.ds(pl.multiple_of(r * 64, 64), 64)
            w2b[sl, :] = w2_ref[sl, :].astype(BF16)
            return c

        lax.fori_loop(0, D_MODEL // 256, cast_in, 0)
        lax.fori_loop(0, D_EXPERT // 64, cast_out, 0)

    @pl.when(i < nused_ref[0])
    def _():
        lo, hi = _unpack_halves(xs_ref[...])
        lo = lo.astype(BF16)
        hi = hi.astype(BF16)
        h1 = (jnp.dot(lo, w1b[0:ROW_WORDS, :], preferred_element_type=F32)
              + jnp.dot(hi, w1b[ROW_WORDS:D_MODEL, :], preferred_element_type=F32))
        h3 = (jnp.dot(lo, w3b[0:ROW_WORDS, :], preferred_element_type=F32)
              + jnp.dot(hi, w3b[ROW_WORDS:D_MODEL, :], preferred_element_type=F32))
        hb = (h1 * (1.0 / (1.0 + jnp.exp(-h1)))) * h3
        ys_ref[...] = _pack_halves(jnp.dot(hb.astype(BF16), w2b[...], preferred_element_type=F32))

    @pl.when(i >= nused_ref[0])
    def _():
        ys_ref[...] = jnp.zeros(ys_ref.shape, jnp.uint32)


def _experts(xs, blk_e, nused, w1, w3, w2, layer, bm):
    p_rows = xs.shape[0]
    nb = p_rows // bm
    rowmap = lambda i, be, nu: (jnp.minimum(i, nu[0] - 1), 0)
    wmap = lambda i, be, nu: (layer, be[i], 0, 0)
    return pl.pallas_call(
        _experts_kernel,
        grid_spec=pltpu.PrefetchScalarGridSpec(
            num_scalar_prefetch=2,
            grid=(nb,),
            in_specs=[
                pl.BlockSpec((bm, ROW_WORDS), rowmap),
                pl.BlockSpec((None, None, D_MODEL, D_EXPERT), wmap),
                pl.BlockSpec((None, None, D_MODEL, D_EXPERT), wmap),
                pl.BlockSpec((None, None, D_EXPERT, D_MODEL), wmap),
            ],
            out_specs=pl.BlockSpec((bm, ROW_WORDS), lambda i, be, nu: (i, 0)),
            scratch_shapes=[pltpu.VMEM((D_MODEL, D_EXPERT), BF16), pltpu.VMEM((D_MODEL, D_EXPERT), BF16),
                            pltpu.VMEM((D_EXPERT, D_MODEL), BF16)],
        ),
        out_shape=jax.ShapeDtypeStruct((p_rows, ROW_WORDS), jnp.uint32),
        compiler_params=_cparams(("arbitrary",)),
        name="experts",
    )(blk_e, nused, xs, w1, w3, w2)


def _combine_kernel(dest_ref, dnext_ref, x_ref, rg_ref, g_ref, ys_ref, out_ref, ybuf, sem, *, tm, rows, final):
    i = pl.program_id(0)
    nt = pl.num_programs(0)
    slot = lax.rem(i, 2)
    other = 1 - slot

    def gather_rows(d_ref, base, buf_slot):
        for jj in range(rows):
            j = base + jj
            _row_copy(ys_ref, d_ref[0, 0, j], ybuf.at[buf_slot, 0], j, sem.at[buf_slot]).start()
            _row_copy(ys_ref, d_ref[0, 0, tm + j], ybuf.at[buf_slot, 1], j, sem.at[buf_slot]).start()

    @pl.when(i == 0)
    def _():
        def first(r, c):
            gather_rows(dest_ref, pl.multiple_of(r * rows, rows), 0)
            return c

        lax.fori_loop(0, tm // rows, first, 0)

    for k in range(TOP_K):
        pltpu.make_async_copy(ys_ref.at[pl.ds(0, tm), :], ybuf.at[slot, k], sem.at[slot]).wait()

    def combine_rows(base):
        sl = pl.ds(base, rows)
        rg = rg_ref[sl, :]
        g1 = rg[:, 0:1]
        g2 = rg[:, 1:2]
        lo1, hi1 = _unpack_halves(ybuf[slot, 0, sl, :])
        lo2, hi2 = _unpack_halves(ybuf[slot, 1, sl, :])
        xa = x_ref[sl, 0:ROW_WORDS] + (g1 * lo1 + g2 * lo2)
        xb = x_ref[sl, ROW_WORDS:D_MODEL] + (g1 * hi1 + g2 * hi2)
        if final:
            ss = jnp.sum(xa * xa, axis=-1, keepdims=True) + jnp.sum(xb * xb, axis=-1, keepdims=True)
            sc = lax.rsqrt(ss / D_MODEL + EPS)
            xa = xa * sc * g_ref[:, 0:ROW_WORDS]
            xb = xb * sc * g_ref[:, ROW_WORDS:D_MODEL]
        out_ref[sl, 0:ROW_WORDS] = xa
        out_ref[sl, ROW_WORDS:D_MODEL] = xb

    @pl.when(i + 1 < nt)
    def _():
        def body(r, c):
            base = pl.multiple_of(r * rows, rows)
            gather_rows(dnext_ref, base, other)
            combine_rows(base)
            return c

        lax.fori_loop(0, tm // rows, body, 0, unroll=2)

    @pl.when(i + 1 == nt)
    def _():
        def body(r, c):
            combine_rows(pl.multiple_of(r * rows, rows))
            return c

        lax.fori_loop(0, tm // rows, body, 0, unroll=2)


def _combine(x_new, ys, dest3, rg, g_final, final):
    n = x_new.shape[0]
    tm = dest3.shape[2] // 2
    nt = n // tm
    kern = functools.partial(_combine_kernel, tm=tm, rows=16, final=final)
    return pl.pallas_call(
        kern,
        grid=(nt,),
        in_specs=[
            pl.BlockSpec((1, 1, 2 * tm), lambda i: (i, 0, 0), memory_space=pltpu.SMEM),
            pl.BlockSpec((1, 1, 2 * tm), lambda i: (jnp.minimum(i + 1, nt - 1), 0, 0), memory_space=pltpu.SMEM),
            pl.BlockSpec((tm, D_MODEL), lambda i: (i, 0)),
            pl.BlockSpec((tm, LANES), lambda i: (i, 0)),
            pl.BlockSpec((1, D_MODEL), lambda i: (0, 0)),
            pl.BlockSpec(memory_space=pl.ANY),
        ],
        out_specs=pl.BlockSpec((tm, D_MODEL), lambda i: (i, 0)),
        out_shape=jax.ShapeDtypeStruct((n, D_MODEL), F32),
        scratch_shapes=[pltpu.VMEM((2, TOP_K, tm, ROW_WORDS), jnp.uint32), pltpu.SemaphoreType.DMA((2,))],
        compiler_params=_cparams(("arbitrary",)),
        name="combine",
    )(dest3, dest3, x_new, rg, g_final, ys)


def _dest_kernel(first_ref, ri_ref, dest_ref, *, tiles, tm):
    for t in range(tiles):
        e1, e2 = ri_ref[t, 0:1, :], ri_ref[t, 1:2, :]
        s1 = jnp.zeros_like(e1)
        s2 = jnp.zeros_like(e2)
        for e in range(N_EXPERTS):
            s1 = jnp.where(e1 == e, first_ref[e], s1)
            s2 = jnp.where(e2 == e, first_ref[e], s2)
        dest_ref[t, :, 0:tm] = s1 + ri_ref[t, 2:3, :]
        dest_ref[t, :, tm:2 * tm] = s2 + ri_ref[t, 3:4, :]


def _dest_rows(first, ri_t):
    n_tiles, _, tm = ri_t.shape
    tiles = min(n_tiles, 16)
    return pl.pallas_call(
        functools.partial(_dest_kernel, tiles=tiles, tm=tm),
        grid_spec=pltpu.PrefetchScalarGridSpec(
            num_scalar_prefetch=1,
            grid=(n_tiles // tiles,),
            in_specs=[pl.BlockSpec((tiles, 8, tm), lambda i, f: (i, 0, 0))],
            out_specs=pl.BlockSpec((tiles, 1, 2 * tm), lambda i, f: (i, 0, 0)),
        ),
        out_shape=jax.ShapeDtypeStruct((n_tiles, 1, 2 * tm), jnp.int32),
        compiler_params=_cparams(("arbitrary",)),
        name="dest_rows",
    )(first, ri_t)


def _route_tables(ri_ts, counts, n_total, bm):
    cnts = [c[0, :N_EXPERTS].astype(jnp.int32) for c in counts]
    cnt = sum(cnts)
    padded = (cnt + bm - 1) // bm * bm
    pad_end = jnp.cumsum(padded)
    pad_start = pad_end - padded
    dests, first = [], pad_start
    for ri_t, c in zip(ri_ts, cnts):
        dests.append(_dest_rows(first.astype(jnp.int32), ri_t))
        first = first + c
    nb = -(-(n_total * TOP_K) // bm) + N_EXPERTS
    nused = (pad_end[-1] // bm).astype(jnp.int32)
    blk = jnp.minimum(jnp.arange(nb, dtype=jnp.int32), nused - 1) * bm
    blk_e = jnp.sum((pad_end[None, :] <= blk[:, None]).astype(jnp.int32), axis=1)
    blk_e = jnp.minimum(blk_e, N_EXPERTS - 1)
    return dests, pad_end.astype(jnp.int32), padded.astype(jnp.int32), blk_e, nused.reshape(1), nb * bm


def _mix_and_route(x2d, bsz, seq, conv0, c0, n0, m0, wts):
    n = bsz * seq
    pf, pb, li, lf = _inproj(x2d, wts["g_mix"], wts["w_main"], wts["wg"], wts["b_if"])
    n0p = n0.reshape(bsz, N_PAIRS, LANES)
    m0p = jnp.pad(m0, ((0, 0), (0, LANES - N_HEADS))).reshape(bsz, 1, LANES)
    by_row = lambda a: a.reshape(bsz, seq, a.shape[-1])
    mix, conv_n, c_n, n_n, m_n = _mixer(by_row(pf), by_row(pb), by_row(li), by_row(lf), wts["conv_w"],
                                         wts["head_gain"], conv0, c0, n0p, m0p, bsz, seq)
    x_new, xq, ri_t, rg, counts = _outproj(mix.reshape(n, D_MODEL), wts["w_out"], x2d, wts["g_ffn"],
                                           wts["wr"], wts["br"])
    states = (conv_n, c_n, n_n.reshape(bsz, N_HEADS, DQK), m_n[:, 0, :N_HEADS])
    return dict(x_new=x_new, xq=xq, ri_t=ri_t, rg=rg, counts=counts, states=states)


def _moe(groups, wts, g_final, final):
    bm = EXPERT_BLOCK
    n_total = sum(g["x_new"].shape[0] for g in groups)
    dests, pad_end, padded, blk_e, nused, p_rows = _route_tables(
        [g["ri_t"] for g in groups], [g["counts"] for g in groups], n_total, bm)
    xs = _dispatch(groups[0]["xq"], groups[1]["xq"], dests[0], dests[1], pad_end, padded, nused, p_rows, bm)
    ys = _experts(xs, blk_e, nused, wts["w1"], wts["w3"], wts["w2"], wts["layer"], bm)
    return [_combine(g["x_new"], ys, d, g["rg"], g_final, final) for g, d in zip(groups, dests)]


def _prep_weights(l, norm_mix, w_in, b_if, conv_w, head_gain, w_out, norm_ffn, w_router_group,
                  b_router_group, w_router_expert, b_router_expert, w1, w3, w2):
    wi = w_in[l]
    w_main = wi.astype(BF16)
    wg = jnp.pad(wi[:, MAIN_COLS:], ((0, 0), (0, LANES - 2 * N_HEADS))).astype(BF16)
    wr = jnp.pad(jnp.concatenate([w_router_expert[l], w_router_group[l]], axis=1),
                 ((0, 0), (0, LANES - N_EXPERTS - N_GROUPS))).astype(BF16)
    br = jnp.pad(jnp.concatenate([b_router_expert[l], b_router_group[l]]),
                 (0, LANES - N_EXPERTS - N_GROUPS)).reshape(1, LANES)
    return dict(
        g_mix=norm_mix[l].reshape(1, D_MODEL),
        w_main=w_main, wg=wg,
        b_if=jnp.pad(b_if[l], (0, LANES - 2 * N_HEADS)).reshape(1, LANES),
        conv_w=conv_w[l], head_gain=head_gain[l].reshape(1, MLSTM_DIM),
        w_out=w_out[l].astype(BF16),
        g_ffn=norm_ffn[l].reshape(1, D_MODEL),
        wr=wr, br=br,
        w1=w1, w3=w3, w2=w2, layer=l,
    )


def _trunks(xs_in, states_in, wts, g_final):
    depth = len(wts)
    shapes = [x.shape for x in xs_in]
    x2ds = [x.reshape(x.shape[0] * x.shape[1], D_MODEL) for x in xs_in]
    new_states = [[] for _ in xs_in]
    for l in range(depth):
        groups = []
        for gi, (x2d, shp, st) in enumerate(zip(x2ds, shapes, states_in)):
            grp = _mix_and_route(x2d, shp[0], shp[1], st[0][l], st[1][l], st[2][l], st[3][l], wts[l])
            new_states[gi].append(grp["states"])
            groups.append(grp)
        x2ds = _moe(groups, wts[l], g_final, l == depth - 1)
    outs = []
    for x2d, shp, sts in zip(x2ds, shapes, new_states):
        outs.append((x2d.reshape(shp),) + tuple(jnp.stack([s[k] for s in sts]) for k in range(4)))
    return outs


def kernel(x_prompt, x_sample, state_conv, state_mlstm_C, state_mlstm_n, state_mlstm_m,
           norm_mix, w_in, b_if, conv_w, head_gain, w_out, norm_ffn,
           w_router_group, b_router_group, w_router_expert, b_router_expert,
           w1, w3, w2, norm_final):
    depth = w_in.shape[0]
    wts = [_prep_weights(l, norm_mix, w_in, b_if, conv_w, head_gain, w_out, norm_ffn, w_router_group,
                         b_router_group, w_router_expert, b_router_expert, w1, w3, w2)
           for l in range(depth)]
    g_final = norm_final.reshape(1, D_MODEL)
    b = x_prompt.shape[0]
    conv0 = jnp.zeros((depth, b, CONV_WIDTH - 1, CONV_DIM), F32)
    c0 = jnp.zeros((depth, b, N_HEADS, DQK, DV), F32)
    n0 = jnp.zeros((depth, b, N_HEADS, DQK), F32)
    m0 = jnp.full((depth, b, N_HEADS), M_INIT, F32)
    (y_p, conv_p, c_p, n_p, m_p), (y_s, conv_s, c_s, n_s, m_s) = _trunks(
        [x_prompt, x_sample],
        [(conv0, c0, n0, m0), (state_conv, state_mlstm_C, state_mlstm_n, state_mlstm_m)],
        wts, g_final)
    return (y_p, y_s, conv_p, c_p, n_p, m_p, conv_s, c_s, n_s, m_s)
```

```python
import functools

import jax
import jax.numpy as jnp
from jax import lax
from jax.experimental import pallas as pl
from jax.experimental.pallas import tpu as pltpu

F32 = jnp.float32
BF16 = jnp.bfloat16

D_MODEL = 2048
CONV_DIM = 1024
N_HEADS = 8
N_PAIRS = N_HEADS // 2
DV = 128
DQK = 64
QK_DIM = N_HEADS * DQK
MLSTM_DIM = N_HEADS * DV
N_GROUPS = 4
EXPERTS_PER_GROUP = 8
N_EXPERTS = 32
TOP_K = 2
D_EXPERT = 512
EPS = 1e-6
M_INIT = -1e30
CONV_WIDTH = 3
LANES = 128
MAIN_COLS = 3 * CONV_DIM + 2 * QK_DIM + 2 * MLSTM_DIM
ROW_WORDS = D_MODEL // 2
PROJ_BLOCK = 1024
F32_SOURCE_BLOCKS = (0, 1, 2, 5, 3)
V_SOURCE_BLOCK = 4
CHUNK = 64
EXPERT_BLOCK = 512
TOKEN_TILE = 512
CONV_GROUP_COLS = 256
CONV_GROUPS = CONV_DIM // CONV_GROUP_COLS
VMEM_LIMIT = 56 * 1024 * 1024


def _cparams(sem):
    return pltpu.CompilerParams(dimension_semantics=sem, vmem_limit_bytes=VMEM_LIMIT)


def _pack_halves(x):
    half = x.shape[1] // 2
    lo = pltpu.bitcast(x[:, :half].astype(BF16).astype(F32), jnp.uint32)
    hi = pltpu.bitcast(x[:, half:].astype(BF16).astype(F32), jnp.uint32)
    return (lo >> 16) | (hi & jnp.uint32(0xFFFF0000))


def _unpack_halves(w):
    lo = pltpu.bitcast(w << 16, F32)
    hi = pltpu.bitcast(w & jnp.uint32(0xFFFF0000), F32)
    return lo, hi


def _split3(x):
    hi = x.astype(BF16)
    r1 = x - hi.astype(F32)
    mid = r1.astype(BF16)
    lo = (r1 - mid.astype(F32)).astype(BF16)
    return hi, mid, lo


def _inproj_kernel(x_ref, g_ref, w_ref, wg_ref, b_ref, pf_ref, pb_ref, li_ref, lf_ref, xh_ref,
                   *, tm, rows):
    def body(r, c):
        sl = pl.ds(pl.multiple_of(r * rows, rows), rows)
        x = x_ref[sl, :]
        ms = jnp.mean(x * x, axis=-1, keepdims=True)
        xh_ref[sl, :] = (x * lax.rsqrt(ms + EPS) * g_ref[...]).astype(BF16)
        return c

    lax.fori_loop(0, tm // rows, body, 0, unroll=4)
    xh = xh_ref[...]
    gt = jnp.dot(xh, wg_ref[...], preferred_element_type=F32) + b_ref[...]
    lane = lax.broadcasted_iota(jnp.int32, gt.shape, 1)
    valid = lane < N_HEADS
    li_ref[...] = jnp.where(valid, gt, 0.0)
    fg = pltpu.roll(gt, LANES - N_HEADS, axis=1)
    lf = jnp.minimum(fg, 0.0) - jnp.log1p(jnp.exp(-jnp.abs(fg)))
    lf_ref[...] = jnp.where(valid, lf, 0.0)

    def block(src):
        return jnp.dot(xh, w_ref[:, src * PROJ_BLOCK:(src + 1) * PROJ_BLOCK], preferred_element_type=F32)

    for dst, src in enumerate(F32_SOURCE_BLOCKS):
        pf_ref[:, dst * PROJ_BLOCK:(dst + 1) * PROJ_BLOCK] = block(src)
    pb_ref[...] = block(V_SOURCE_BLOCK).astype(BF16)


def _inproj(x2d, g, w_all, wg, b_pad):
    n = x2d.shape[0]
    tm = min(n, 256)
    kern = functools.partial(_inproj_kernel, tm=tm, rows=32)
    const = lambda shape: pl.BlockSpec(shape, lambda i: (0, 0))
    return pl.pallas_call(
        kern,
        grid=(n // tm,),
        in_specs=[
            pl.BlockSpec((tm, D_MODEL), lambda i: (i, 0)),
            const((1, D_MODEL)),
            pl.BlockSpec(w_all.shape, lambda i: (0, 0), pipeline_mode=pl.Buffered(1)),
            const((D_MODEL, LANES)),
            const((1, LANES)),
        ],
        out_specs=[
            pl.BlockSpec((tm, len(F32_SOURCE_BLOCKS) * PROJ_BLOCK), lambda i: (i, 0)),
            pl.BlockSpec((tm, PROJ_BLOCK), lambda i: (i, 0)),
            pl.BlockSpec((tm, LANES), lambda i: (i, 0)),
            pl.BlockSpec((tm, LANES), lambda i: (i, 0)),
        ],
        out_shape=[
            jax.ShapeDtypeStruct((n, len(F32_SOURCE_BLOCKS) * PROJ_BLOCK), F32),
            jax.ShapeDtypeStruct((n, PROJ_BLOCK), BF16),
            jax.ShapeDtypeStruct((n, LANES), F32),
            jax.ShapeDtypeStruct((n, LANES), F32),
        ],
        scratch_shapes=[pltpu.VMEM((tm, D_MODEL), BF16)],
        compiler_params=_cparams(("arbitrary",)),
        name="inproj",
    )(x2d, g, w_all, wg, b_pad)


def _cummax_rows(x, length):
    row = lax.broadcasted_iota(jnp.int32, x.shape, 0)
    d = 1
    while d < length:
        shifted = pltpu.roll(x, d, axis=0)
        x = jnp.maximum(x, jnp.where(row >= d, shifted, -jnp.inf))
        d *= 2
    return x


def _pad_rows(x, length):
    if length == LANES:
        return x
    return jnp.concatenate([x, jnp.zeros((LANES - length, x.shape[1]), x.dtype)], axis=0)


def _mixer_kernel(u_ref, gc_ref, gb_ref, q_ref, k_ref, v_ref, o_ref, li_ref, lf_ref,
                  cw_ref, hg_ref, conv0_ref, c0_ref, n0_ref, m0_ref,
                  mix_ref, convn_ref, cn_ref, nn_ref, mn_ref,
                  *scratch, tb, L, G):
    t = pl.program_id(1)
    nt = pl.num_programs(1)
    n_state = G * N_PAIRS
    c_sc = [scratch[g * N_PAIRS:(g + 1) * N_PAIRS] for g in range(G)]
    n_sc = [scratch[n_state + g * N_PAIRS:n_state + (g + 1) * N_PAIRS] for g in range(G)]
    m_sc = scratch[2 * n_state:2 * n_state + G]
    carry_sc = [scratch[2 * n_state + G + g * CONV_GROUPS:2 * n_state + G + (g + 1) * CONV_GROUPS]
                for g in range(G)]

    @pl.when(t == 0)
    def _():
        zero = jnp.zeros((DQK, DV), F32)
        for g in range(G):
            for p in range(N_PAIRS):
                top = jnp.concatenate([c0_ref[g, 2 * p], zero], axis=1)
                bot = jnp.concatenate([zero, c0_ref[g, 2 * p + 1]], axis=1)
                c_sc[g][p][...] = jnp.concatenate([top, bot], axis=0)
                n_sc[g][p][...] = n0_ref[g, p:p + 1, :]
            m_sc[g][...] = m0_ref[g]
            for cg in range(CONV_GROUPS):
                cs = slice(cg * CONV_GROUP_COLS, (cg + 1) * CONV_GROUP_COLS)
                carry_sc[g][cg][...] = jnp.zeros((8, CONV_GROUP_COLS), F32)
                carry_sc[g][cg][6:8, :] = conv0_ref[g, :, cs]

    row = lax.broadcasted_iota(jnp.int32, (L, L), 0)
    col = lax.broadcasted_iota(jnp.int32, (L, L), 1)
    causal = col <= row
    tril = jnp.where(causal, 1.0, 0.0).astype(BF16)
    lane_l = lax.broadcasted_iota(jnp.int32, (L, LANES), 1)
    low_l = lane_l < DQK
    krow = lax.broadcasted_iota(jnp.int32, (LANES, 2 * DV), 0)
    lane1 = lax.broadcasted_iota(jnp.int32, (1, LANES), 1)

    def chunk(c, carry):
        rows = pl.ds(pl.multiple_of(c * L, L), L)
        units = [(g, p) for g in range(G) for p in range(N_PAIRS)]
        heads = [(g, h) for g in range(G) for h in range(N_HEADS)]
        convs = [conv_chunk(g, rows) for g in range(G)]
        gates = [gate_algebra(g, rows) for g in range(G)]
        outs, states = {}, {}
        for g, p in units:
            pr = pair_scores(g, p, rows)
            v2 = v_ref[g, rows, p * 2 * DV:(p + 1) * 2 * DV]
            for hh in range(2):
                h = 2 * p + hh
                s = intra_weights(gates[g], pr, h)
                intra = jnp.dot(s.astype(BF16), v2[:, hh * DV:(hh + 1) * DV], preferred_element_type=F32)
                o_sig = 1.0 / (1.0 + jnp.exp(-o_ref[g, rows, h * DV:(h + 1) * DV]))
                outs[(g, h)] = head_output(gates[g], pr, s, intra, o_sig, h)
            states[(g, p)] = state_update(gates[g], pr, v2, p)
        for g in range(G):
            for cg in range(CONV_GROUPS):
                y_conv, z_tail = convs[g][cg]
                mix_ref[g, rows, cg * CONV_GROUP_COLS:(cg + 1) * CONV_GROUP_COLS] = y_conv
                carry_sc[g][cg][6:8, :] = z_tail
            m_sc[g][...] = gates[g]["m_new"]
        for g, h in heads:
            mix_ref[g, rows, CONV_DIM + h * DV:CONV_DIM + (h + 1) * DV] = outs[(g, h)]
        for g, p in units:
            c_sc[g][p][...], n_sc[g][p][...] = states[(g, p)]
        return carry

    def conv_chunk(g, rows):
        cw = CONV_GROUP_COLS
        res = []
        for cg in range(CONV_GROUPS):
            cs = slice(cg * cw, (cg + 1) * cw)
            z = gc_ref[g, rows, cs] * u_ref[g, rows, cs]
            prev = carry_sc[g][cg][...]
            p1 = prev[7:8, :]
            p2 = prev[6:7, :]
            rw = lax.broadcasted_iota(jnp.int32, (L, cw), 0)
            z1 = jnp.where(rw >= 1, pltpu.roll(z, 1, axis=0), p1)
            z2 = jnp.where(rw >= 2, pltpu.roll(z, 2, axis=0), jnp.where(rw == 1, p1, p2))
            y = z2 * cw_ref[0:1, cs] + z1 * cw_ref[1:2, cs] + z * cw_ref[2:3, cs]
            res.append(((gb_ref[g, rows, cs] * y).astype(BF16), z[L - 2:L, :]))
        return res

    def gate_algebra(g, rows):
        li = li_ref[g, rows, :]
        lf = lf_ref[g, rows, :]
        hi, mid, lo = _split3(lf)
        F = (jnp.dot(tril, hi, preferred_element_type=F32)
             + jnp.dot(tril, mid, preferred_element_type=F32)
             + jnp.dot(tril, lo, preferred_element_type=F32))
        r = li - F
        cm = _cummax_rows(r, L)
        mprev = m_sc[g][...]
        mx = jnp.maximum(mprev, cm)
        M = F + mx
        neg_mx = -mx
        w_inter = jnp.exp(mprev - mx)
        em = jnp.exp(-M)
        gs = jnp.exp(r - mx[L - 1:L, :])
        g_inter = w_inter[L - 1:L, :]
        rT = _pad_rows(r, L).T[:, 0:L]
        return dict(neg_mx=neg_mx, w_inter=w_inter, em=em, gs=gs, g_inter=g_inter, rT=rT, m_new=M[L - 1:L, :])

    def pair_scores(g, p, rows):
        ps = slice(p * LANES, (p + 1) * LANES)
        q2 = q_ref[g, rows, ps].astype(BF16)
        kf = k_ref[g, rows, ps] * DQK ** -0.5
        k2 = kf.astype(BF16)
        q_e = jnp.where(low_l, q2, jnp.zeros_like(q2))
        q_o = jnp.where(low_l, jnp.zeros_like(q2), q2)
        q_st = jnp.concatenate([q_e, q_o], axis=0)
        n_row = n_sc[g][p][...]
        n_b = jnp.broadcast_to(n_row, (LANES, LANES)).astype(BF16)
        k_aug = jnp.concatenate([n_b, k2], axis=0)
        sn = lax.dot_general(q_st, k_aug, (((1,), (1,)), ((), ())),
                             preferred_element_type=F32)
        c_full = c_sc[g][p][...]
        qc = jnp.dot(q_st, c_full.astype(BF16), preferred_element_type=F32)
        return dict(kf=kf, k2=k2, n_row=n_row, c_full=c_full, sn=sn, qc=qc)

    def intra_weights(gt, pr, h):
        rs = slice((h % 2) * L, (h % 2 + 1) * L)
        dmat = gt["neg_mx"][:, h:h + 1] + gt["rT"][h:h + 1, :]
        return pr["sn"][rs, LANES:LANES + L] * jnp.exp(jnp.where(causal, dmat, -jnp.inf))

    def head_output(gt, pr, s, intra, o_sig, h):
        hh = h % 2
        rs = slice(hh * L, (hh + 1) * L)
        hs = slice(h * DV, (h + 1) * DV)
        wi = gt["w_inter"][:, h:h + 1]
        num = wi * pr["qc"][rs, hh * DV:(hh + 1) * DV] + intra
        den = wi * pr["sn"][rs, 0:LANES] + jnp.sum(s, axis=-1, keepdims=True)
        hv = num / jnp.maximum(jnp.abs(den), gt["em"][:, h:h + 1])
        ms = jnp.mean(hv * hv, axis=-1, keepdims=True)
        hn = hv * lax.rsqrt(ms + EPS) * hg_ref[0:1, hs]
        return (hn * o_sig).astype(BF16)

    def state_update(gt, pr, v2, p):
        gs = gt["gs"]
        kgw = jnp.where(low_l, gs[:, 2 * p:2 * p + 1], gs[:, 2 * p + 1:2 * p + 2])
        kg = pr["kf"] * kgw
        upd = jnp.dot(_pad_rows(kg, L).T.astype(BF16), _pad_rows(v2, L),
                      preferred_element_type=F32)
        ge = gt["g_inter"][0:1, 2 * p:2 * p + 1]
        go = gt["g_inter"][0:1, 2 * p + 1:2 * p + 2]
        c_new = jnp.where(krow < DQK, ge, go) * pr["c_full"] + upd
        kn = pr["k2"].astype(F32) * kgw.astype(BF16).astype(F32)
        n_new = jnp.where(lane1 < DQK, ge, go) * pr["n_row"] + jnp.sum(kn, axis=0, keepdims=True)
        return c_new, n_new

    lax.fori_loop(0, tb // L, chunk, 0)

    @pl.when(t == nt - 1)
    def _():
        for g in range(G):
            for cg in range(CONV_GROUPS):
                cs = slice(cg * CONV_GROUP_COLS, (cg + 1) * CONV_GROUP_COLS)
                convn_ref[g, :, cs] = carry_sc[g][cg][6:8, :]
            for p in range(N_PAIRS):
                cf = c_sc[g][p][...]
                cn_ref[g, 2 * p] = cf[0:DQK, 0:DV]
                cn_ref[g, 2 * p + 1] = cf[DQK:2 * DQK, DV:2 * DV]
                nn_ref[g, p:p + 1, :] = n_sc[g][p][...]
            mn_ref[g] = m_sc[g][...]


def _mixer(pf, pb, li, lf, conv_w, head_gain, conv0, c0, n0p, m0p, bsz, seq):
    G = 1
    L = min(seq, CHUNK)
    tb = min(seq, 512)
    nt = seq // tb
    kern = functools.partial(_mixer_kernel, tb=tb, L=L, G=G)
    wide = lambda cb: pl.BlockSpec((G, tb, 1024), lambda b, t: (b, t, cb))
    half = lambda cb: pl.BlockSpec((G, tb, 512), lambda b, t: (b, t, cb))
    gate = pl.BlockSpec((G, tb, LANES), lambda b, t: (b, t, 0))
    st = lambda *shape: pl.BlockSpec((G,) + shape, lambda b, t: (b,) + (0,) * len(shape))
    return pl.pallas_call(
        kern,
        grid=(bsz // G, nt),
        in_specs=[
            wide(0), wide(1), wide(2), half(8), half(9), wide(0), wide(3), gate, gate,
            pl.BlockSpec((CONV_WIDTH, CONV_DIM), lambda b, t: (0, 0)),
            pl.BlockSpec((1, MLSTM_DIM), lambda b, t: (0, 0)),
            st(2, CONV_DIM), st(N_HEADS, DQK, DV), st(N_PAIRS, LANES), st(1, LANES),
        ],
        out_specs=[
            pl.BlockSpec((G, tb, D_MODEL), lambda b, t: (b, t, 0)),
            st(2, CONV_DIM), st(N_HEADS, DQK, DV), st(N_PAIRS, LANES), st(1, LANES),
        ],
        out_shape=[
            jax.ShapeDtypeStruct((bsz, seq, D_MODEL), BF16),
            jax.ShapeDtypeStruct((bsz, 2, CONV_DIM), F32),
            jax.ShapeDtypeStruct((bsz, N_HEADS, DQK, DV), F32),
            jax.ShapeDtypeStruct((bsz, N_PAIRS, LANES), F32),
            jax.ShapeDtypeStruct((bsz, 1, LANES), F32),
        ],
        scratch_shapes=(
            [pltpu.VMEM((LANES, 2 * DV), F32)] * (G * N_PAIRS)
            + [pltpu.VMEM((1, LANES), F32)] * (G * N_PAIRS)
            + [pltpu.VMEM((1, LANES), F32)] * G
            + [pltpu.VMEM((8, CONV_GROUP_COLS), F32)] * (G * CONV_GROUPS)
        ),
        compiler_params=_cparams(("arbitrary", "arbitrary")),
        name="mixer",
    )(pf, pf, pf, pf, pf, pb, pf, li, lf, conv_w, head_gain, conv0, c0, n0p, m0p)


def _outproj_kernel(mix_ref, w_ref, x_ref, g_ref, wr_ref, br_ref,
                    xo_ref, xq_ref, ri_ref, rg_ref, cnt_ref, base_sc, *, tm):
    i = pl.program_id(0)

    @pl.when(i == 0)
    def _():
        base_sc[...] = jnp.zeros(base_sc.shape, F32)

    x = x_ref[...] + jnp.dot(mix_ref[...], w_ref[...], preferred_element_type=F32)
    xo_ref[...] = x
    ms = jnp.mean(x * x, axis=-1, keepdims=True)
    xn = x * lax.rsqrt(ms + EPS) * g_ref[...]
    xq_ref[...] = _pack_halves(xn)
    lg = jnp.dot(xn.astype(BF16), wr_ref[...], preferred_element_type=F32) + br_ref[...]

    lane = lax.broadcasted_iota(jnp.int32, (tm, LANES), 1)
    lanef = lane.astype(F32)
    big = jnp.float32(1e9)
    ninf = -jnp.inf
    is_g = (lane >= N_EXPERTS) & (lane < N_EXPERTS + N_GROUPS)
    glog = jnp.where(is_g, lg, ninf)
    gmax = jnp.max(glog, axis=1, keepdims=True)
    gi = jnp.min(jnp.where(glog == gmax, lanef, big), axis=1, keepdims=True) - N_EXPERTS
    pgi = 1.0 / jnp.sum(jnp.where(is_g, jnp.exp(lg - gmax), 0.0), axis=1, keepdims=True)
    lo = gi * EXPERTS_PER_GROUP
    in_grp = (lanef >= lo) & (lanef < lo + EXPERTS_PER_GROUP)
    el = jnp.where(in_grp, lg, ninf)
    v1 = jnp.max(el, axis=1, keepdims=True)
    i1 = jnp.min(jnp.where(el == v1, lanef, big), axis=1, keepdims=True)
    el2 = jnp.where(lanef == i1, ninf, el)
    v2 = jnp.max(el2, axis=1, keepdims=True)
    i2 = jnp.min(jnp.where(el2 == v2, lanef, big), axis=1, keepdims=True)
    e21 = jnp.exp(v2 - v1)
    g1 = pgi / (1.0 + e21)
    g2 = pgi * e21 / (1.0 + e21)
    sel1 = lanef == i1
    sel2 = lanef == i2
    oh = jnp.where(sel1 | sel2, 1.0, 0.0)
    r_i = lax.broadcasted_iota(jnp.int32, (tm, tm), 0)
    c_i = lax.broadcasted_iota(jnp.int32, (tm, tm), 1)
    stril = jnp.where(c_i < r_i, 1.0, 0.0).astype(BF16)
    tot = jnp.dot(stril, oh.astype(BF16), preferred_element_type=F32) + base_sc[...]
    r1 = jnp.sum(jnp.where(sel1, tot, 0.0), axis=1, keepdims=True)
    r2 = jnp.sum(jnp.where(sel2, tot, 0.0), axis=1, keepdims=True)
    base = base_sc[...] + jnp.sum(oh, axis=0, keepdims=True)
    base_sc[...] = base
    cnt_ref[...] = base
    ri = jnp.where(lane == 0, i1, jnp.where(lane == 1, i2, jnp.where(lane == 2, r1,
                   jnp.where(lane == 3, r2, 0.0))))
    ri_t = jnp.concatenate([ri[c * LANES:(c + 1) * LANES, :].T[0:8, :] for c in range(tm // LANES)], axis=1)
    ri_ref[...] = ri_t.astype(jnp.int32)
    rg_ref[...] = jnp.where(lane == 0, g1, jnp.where(lane == 1, g2, 0.0))


def _outproj(mix, w_out, x2d, g, wr, br):
    n = x2d.shape[0]
    tm = min(n, TOKEN_TILE)
    assert tm % LANES == 0
    kern = functools.partial(_outproj_kernel, tm=tm)
    rowblk = lambda w: pl.BlockSpec((tm, w), lambda i: (i, 0))
    const = lambda r, c: pl.BlockSpec((r, c), lambda i: (0, 0))
    return pl.pallas_call(
        kern,
        grid=(n // tm,),
        in_specs=[rowblk(D_MODEL), const(D_MODEL, D_MODEL), rowblk(D_MODEL), const(1, D_MODEL),
                  const(D_MODEL, LANES), const(1, LANES)],
        out_specs=[rowblk(D_MODEL), rowblk(D_MODEL // 2), pl.BlockSpec((None, 8, tm), lambda i: (i, 0, 0)),
                   rowblk(LANES), const(1, LANES)],
        out_shape=[
            jax.ShapeDtypeStruct((n, D_MODEL), F32),
            jax.ShapeDtypeStruct((n, D_MODEL // 2), jnp.uint32),
            jax.ShapeDtypeStruct((n // tm, 8, tm), jnp.int32),
            jax.ShapeDtypeStruct((n, LANES), F32),
            jax.ShapeDtypeStruct((1, LANES), F32),
        ],
        scratch_shapes=[pltpu.VMEM((1, LANES), F32)],
        compiler_params=_cparams(("arbitrary",)),
        name="outproj",
    )(mix, w_out, x2d, g, wr, br)


def _row_copy(src_ref, src_row, dst_ref, dst_row, sem):
    return pltpu.make_async_copy(src_ref.at[pl.ds(src_row, 1), :], dst_ref.at[pl.ds(dst_row, 1), :], sem)


def _dispatch_kernel(pad_end_ref, padded_ref, nused_ref, da_ref, db_ref, xa_ref, xb_ref, xs_ref, zero_sc, sem_z,
                     sem, *, bm, nb, tiles_a):
    i = pl.program_id(0)

    @pl.when(i == 0)
    def _():
        zero_sc[...] = jnp.zeros(zero_sc.shape, jnp.uint32)

        def zcopy(start):
            return pltpu.make_async_copy(zero_sc, xs_ref.at[pl.ds(pl.multiple_of(start, bm), bm), :], sem_z)

        def zstart(e, c):
            @pl.when(padded_ref[e] > 0)
            def _():
                zcopy(pad_end_ref[e] - bm).start()
            return c

        def zwait(e, c):
            @pl.when(padded_ref[e] > 0)
            def _():
                zcopy(pad_end_ref[e] - bm).wait()
            return c

        def tstart(b, c):
            zcopy(b * bm).start()
            return c

        def twait(b, c):
            zcopy(b * bm).wait()
            return c

        lax.fori_loop(0, N_EXPERTS, zstart, 0)
        lax.fori_loop(nused_ref[0], nb, tstart, 0)
        lax.fori_loop(0, N_EXPERTS, zwait, 0)
        lax.fori_loop(nused_ref[0], nb, twait, 0)

    def scatter_tile(src_ref, dest_ref):
        tm = src_ref.shape[0]

        def start(j, c):
            _row_copy(src_ref, j, xs_ref, dest_ref[0, 0, j], sem).start()
            _row_copy(src_ref, j, xs_ref, dest_ref[0, 0, tm + j], sem).start()
            return c

        lax.fori_loop(0, tm, start, 0, unroll=8)
        for _ in range(TOP_K):
            pltpu.make_async_copy(src_ref, xs_ref.at[pl.ds(0, tm), :], sem).wait()

    @pl.when(i < tiles_a)
    def _():
        scatter_tile(xa_ref, da_ref)

    @pl.when(i >= tiles_a)
    def _():
        scatter_tile(xb_ref, db_ref)


def _dispatch(xq_a, xq_b, dest_a, dest_b, pad_end, padded, nused, p_rows, bm):
    tm_a, tm_b = dest_a.shape[2] // 2, dest_b.shape[2] // 2
    tiles_a, tiles_b = dest_a.shape[0], dest_b.shape[0]
    kern = functools.partial(_dispatch_kernel, bm=bm, nb=p_rows // bm, tiles_a=tiles_a)
    in_a = lambda i, pe, pd, nu: jnp.minimum(i, tiles_a - 1)
    in_b = lambda i, pe, pd, nu: jnp.maximum(i - tiles_a, 0)
    return pl.pallas_call(
        kern,
        grid_spec=pltpu.PrefetchScalarGridSpec(
            num_scalar_prefetch=3,
            grid=(tiles_a + tiles_b,),
            in_specs=[
                pl.BlockSpec((1, 1, 2 * tm_a), lambda *a: (in_a(*a), 0, 0), memory_space=pltpu.SMEM),
                pl.BlockSpec((1, 1, 2 * tm_b), lambda *a: (in_b(*a), 0, 0), memory_space=pltpu.SMEM),
                pl.BlockSpec((tm_a, ROW_WORDS), lambda *a: (in_a(*a), 0)),
                pl.BlockSpec((tm_b, ROW_WORDS), lambda *a: (in_b(*a), 0)),
            ],
            out_specs=pl.BlockSpec(memory_space=pl.ANY),
            scratch_shapes=[pltpu.VMEM((bm, ROW_WORDS), jnp.uint32), pltpu.SemaphoreType.DMA(()),
                            pltpu.SemaphoreType.DMA(())],
        ),
        out_shape=jax.ShapeDtypeStruct((p_rows, ROW_WORDS), jnp.uint32),
        compiler_params=_cparams(("arbitrary",)),
        name="dispatch",
    )(pad_end, padded, nused, dest_a, dest_b, xq_a, xq_b)


def _experts_kernel(blk_e_ref, nused_ref, xs_ref, w1_ref, w3_ref, w2_ref, ys_ref, w1b, w3b, w2b):
    i = pl.program_id(0)

    @pl.when((i == 0) | (blk_e_ref[i] != blk_e_ref[jnp.maximum(i - 1, 0)]))
    def _():
        def cast_in(r, c):
            sl = pl.ds(pl.multiple_of(r * 256, 256), 256)
            w1b[sl, :] = w1_ref[sl, :].astype(BF16)
            w3b[sl, :] = w3_ref[sl, :].astype(BF16)
            return c

        def cast_out(r, c):
            sl = pl.ds(pl.multiple_of(r * 64, 64), 64)
            w2b[sl, :] = w2_ref[sl, :].astype(BF16)
            return c

        lax.fori_loop(0, D_MODEL // 256, cast_in, 0)
        lax.fori_loop(0, D_EXPERT // 64, cast_out, 0)

    @pl.when(i < nused_ref[0])
    def _():
        lo, hi = _unpack_halves(xs_ref[...])
        lo = lo.astype(BF16)
        hi = hi.astype(BF16)
        h1 = (jnp.dot(lo, w1b[0:ROW_WORDS, :], preferred_element_type=F32)
              + jnp.dot(hi, w1b[ROW_WORDS:D_MODEL, :], preferred_element_type=F32))
        h3 = (jnp.dot(lo, w3b[0:ROW_WORDS, :], preferred_element_type=F32)
              + jnp.dot(hi, w3b[ROW_WORDS:D_MODEL, :], preferred_element_type=F32))
        hb = (h1 * (1.0 / (1.0 + jnp.exp(-h1)))) * h3
        ys_ref[...] = _pack_halves(jnp.dot(hb.astype(BF16), w2b[...], preferred_element_type=F32))

    @pl.when(i >= nused_ref[0])
    def _():
        ys_ref[...] = jnp.zeros(ys_ref.shape, jnp.uint32)


def _experts(xs, blk_e, nused, w1, w3, w2, layer, bm):
    p_rows = xs.shape[0]
    nb = p_rows // bm
    rowmap = lambda i, be, nu: (jnp.minimum(i, nu[0] - 1), 0)
    wmap = lambda i, be, nu: (layer, be[i], 0, 0)
    return pl.pallas_call(
        _experts_kernel,
        grid_spec=pltpu.PrefetchScalarGridSpec(
            num_scalar_prefetch=2,
            grid=(nb,),
            in_specs=[
                pl.BlockSpec((bm, ROW_WORDS), rowmap),
                pl.BlockSpec((None, None, D_MODEL, D_EXPERT), wmap),
                pl.BlockSpec((None, None, D_MODEL, D_EXPERT), wmap),
                pl.BlockSpec((None, None, D_EXPERT, D_MODEL), wmap),
            ],
            out_specs=pl.BlockSpec((bm, ROW_WORDS), lambda i, be, nu: (i, 0)),
            scratch_shapes=[pltpu.VMEM((D_MODEL, D_EXPERT), BF16), pltpu.VMEM((D_MODEL, D_EXPERT), BF16),
                            pltpu.VMEM((D_EXPERT, D_MODEL), BF16)],
        ),
        out_shape=jax.ShapeDtypeStruct((p_rows, ROW_WORDS), jnp.uint32),
        compiler_params=_cparams(("arbitrary",)),
        name="experts",
    )(blk_e, nused, xs, w1, w3, w2)


def _combine_kernel(dest_ref, dnext_ref, x_ref, rg_ref, g_ref, ys_ref, out_ref, ybuf, sem, *, tm, rows, final):
    i = pl.program_id(0)
    nt = pl.num_programs(0)
    slot = lax.rem(i, 2)
    other = 1 - slot

    def gather_rows(d_ref, base, buf_slot):
        for jj in range(rows):
            j = base + jj
            _row_copy(ys_ref, d_ref[0, 0, j], ybuf.at[buf_slot, 0], j, sem.at[buf_slot]).start()
            _row_copy(ys_ref, d_ref[0, 0, tm + j], ybuf.at[buf_slot, 1], j, sem.at[buf_slot]).start()

    @pl.when(i == 0)
    def _():
        def first(r, c):
            gather_rows(dest_ref, pl.multiple_of(r * rows, rows), 0)
            return c

        lax.fori_loop(0, tm // rows, first, 0)

    for k in range(TOP_K):
        pltpu.make_async_copy(ys_ref.at[pl.ds(0, tm), :], ybuf.at[slot, k], sem.at[slot]).wait()

    def combine_rows(base):
        sl = pl.ds(base, rows)
        rg = rg_ref[sl, :]
        g1 = rg[:, 0:1]
        g2 = rg[:, 1:2]
        lo1, hi1 = _unpack_halves(ybuf[slot, 0, sl, :])
        lo2, hi2 = _unpack_halves(ybuf[slot, 1, sl, :])
        xa = x_ref[sl, 0:ROW_WORDS] + (g1 * lo1 + g2 * lo2)
        xb = x_ref[sl, ROW_WORDS:D_MODEL] + (g1 * hi1 + g2 * hi2)
        if final:
            ss = jnp.sum(xa * xa, axis=-1, keepdims=True) + jnp.sum(xb * xb, axis=-1, keepdims=True)
            sc = lax.rsqrt(ss / D_MODEL + EPS)
            xa = xa * sc * g_ref[:, 0:ROW_WORDS]
            xb = xb * sc * g_ref[:, ROW_WORDS:D_MODEL]
        out_ref[sl, 0:ROW_WORDS] = xa
        out_ref[sl, ROW_WORDS:D_MODEL] = xb

    @pl.when(i + 1 < nt)
    def _():
        def body(r, c):
            base = pl.multiple_of(r * rows, rows)
            gather_rows(dnext_ref, base, other)
            combine_rows(base)
            return c

        lax.fori_loop(0, tm // rows, body, 0, unroll=2)

    @pl.when(i + 1 == nt)
    def _():
        def body(r, c):
            combine_rows(pl.multiple_of(r * rows, rows))
            return c

        lax.fori_loop(0, tm // rows, body, 0, unroll=2)


def _combine(x_new, ys, dest3, rg, g_final, final):
    n = x_new.shape[0]
    tm = dest3.shape[2] // 2
    nt = n // tm
    kern = functools.partial(_combine_kernel, tm=tm, rows=16, final=final)
    return pl.pallas_call(
        kern,
        grid=(nt,),
        in_specs=[
            pl.BlockSpec((1, 1, 2 * tm), lambda i: (i, 0, 0), memory_space=pltpu.SMEM),
            pl.BlockSpec((1, 1, 2 * tm), lambda i: (jnp.minimum(i + 1, nt - 1), 0, 0), memory_space=pltpu.SMEM),
            pl.BlockSpec((tm, D_MODEL), lambda i: (i, 0)),
            pl.BlockSpec((tm, LANES), lambda i: (i, 0)),
            pl.BlockSpec((1, D_MODEL), lambda i: (0, 0)),
            pl.BlockSpec(memory_space=pl.ANY),
        ],
        out_specs=pl.BlockSpec((tm, D_MODEL), lambda i: (i, 0)),
        out_shape=jax.ShapeDtypeStruct((n, D_MODEL), F32),
        scratch_shapes=[pltpu.VMEM((2, TOP_K, tm, ROW_WORDS), jnp.uint32), pltpu.SemaphoreType.DMA((2,))],
        compiler_params=_cparams(("arbitrary",)),
        name="combine",
    )(dest3, dest3, x_new, rg, g_final, ys)


def _dest_kernel(first_ref, ri_ref, dest_ref, *, tiles, tm):
    for t in range(tiles):
        e1, e2 = ri_ref[t, 0:1, :], ri_ref[t, 1:2, :]
        s1 = jnp.zeros_like(e1)
        s2 = jnp.zeros_like(e2)
        for e in range(N_EXPERTS):
            s1 = jnp.where(e1 == e, first_ref[e], s1)
            s2 = jnp.where(e2 == e, first_ref[e], s2)
        dest_ref[t, :, 0:tm] = s1 + ri_ref[t, 2:3, :]
        dest_ref[t, :, tm:2 * tm] = s2 + ri_ref[t, 3:4, :]


def _dest_rows(first, ri_t):
    n_tiles, _, tm = ri_t.shape
    tiles = min(n_tiles, 16)
    return pl.pallas_call(
        functools.partial(_dest_kernel, tiles=tiles, tm=tm),
        grid_spec=pltpu.PrefetchScalarGridSpec(
            num_scalar_prefetch=1,
            grid=(n_tiles // tiles,),
            in_specs=[pl.BlockSpec((tiles, 8, tm), lambda i, f: (i, 0, 0))],
            out_specs=pl.BlockSpec((tiles, 1, 2 * tm), lambda i, f: (i, 0, 0)),
        ),
        out_shape=jax.ShapeDtypeStruct((n_tiles, 1, 2 * tm), jnp.int32),
        compiler_params=_cparams(("arbitrary",)),
        name="dest_rows",
    )(first, ri_t)


def _route_tables(ri_ts, counts, n_total, bm):
    cnts = [c[0, :N_EXPERTS].astype(jnp.int32) for c in counts]
    cnt = sum(cnts)
    padded = (cnt + bm - 1) // bm * bm
    pad_end = jnp.cumsum(padded)
    pad_start = pad_end - padded
    dests, first = [], pad_start
    for ri_t, c in zip(ri_ts, cnts):
        dests.append(_dest_rows(first.astype(jnp.int32), ri_t))
        first = first + c
    nb = -(-(n_total * TOP_K) // bm) + N_EXPERTS
    nused = (pad_end[-1] // bm).astype(jnp.int32)
    blk = jnp.minimum(jnp.arange(nb, dtype=jnp.int32), nused - 1) * bm
    blk_e = jnp.sum((pad_end[None, :] <= blk[:, None]).astype(jnp.int32), axis=1)
    blk_e = jnp.minimum(blk_e, N_EXPERTS - 1)
    return dests, pad_end.astype(jnp.int32), padded.astype(jnp.int32), blk_e, nused.reshape(1), nb * bm


def _mix_and_route(x2d, bsz, seq, conv0, c0, n0, m0, wts):
    n = bsz * seq
    pf, pb, li, lf = _inproj(x2d, wts["g_mix"], wts["w_main"], wts["wg"], wts["b_if"])
    n0p = n0.reshape(bsz, N_PAIRS, LANES)
    m0p = jnp.pad(m0, ((0, 0), (0, LANES - N_HEADS))).reshape(bsz, 1, LANES)
    by_row = lambda a: a.reshape(bsz, seq, a.shape[-1])
    mix, conv_n, c_n, n_n, m_n = _mixer(by_row(pf), by_row(pb), by_row(li), by_row(lf), wts["conv_w"],
                                         wts["head_gain"], conv0, c0, n0p, m0p, bsz, seq)
    x_new, xq, ri_t, rg, counts = _outproj(mix.reshape(n, D_MODEL), wts["w_out"], x2d, wts["g_ffn"],
                                           wts["wr"], wts["br"])
    states = (conv_n, c_n, n_n.reshape(bsz, N_HEADS, DQK), m_n[:, 0, :N_HEADS])
    return dict(x_new=x_new, xq=xq, ri_t=ri_t, rg=rg, counts=counts, states=states)


def _moe(groups, wts, g_final, final):
    bm = EXPERT_BLOCK
    n_total = sum(g["x_new"].shape[0] for g in groups)
    dests, pad_end, padded, blk_e, nused, p_rows = _route_tables(
        [g["ri_t"] for g in groups], [g["counts"] for g in groups], n_total, bm)
    xs = _dispatch(groups[0]["xq"], groups[1]["xq"], dests[0], dests[1], pad_end, padded, nused, p_rows, bm)
    ys = _experts(xs, blk_e, nused, wts["w1"], wts["w3"], wts["w2"], wts["layer"], bm)
    return [_combine(g["x_new"], ys, d, g["rg"], g_final, final) for g, d in zip(groups, dests)]


def _prep_weights(l, norm_mix, w_in, b_if, conv_w, head_gain, w_out, norm_ffn, w_router_group,
                  b_router_group, w_router_expert, b_router_expert, w1, w3, w2):
    wi = w_in[l]
    w_main = wi.astype(BF16)
    wg = jnp.pad(wi[:, MAIN_COLS:], ((0, 0), (0, LANES - 2 * N_HEADS))).astype(BF16)
    wr = jnp.pad(jnp.concatenate([w_router_expert[l], w_router_group[l]], axis=1),
                 ((0, 0), (0, LANES - N_EXPERTS - N_GROUPS))).astype(BF16)
    br = jnp.pad(jnp.concatenate([b_router_expert[l], b_router_group[l]]),
                 (0, LANES - N_EXPERTS - N_GROUPS)).reshape(1, LANES)
    return dict(
        g_mix=norm_mix[l].reshape(1, D_MODEL),
        w_main=w_main, wg=wg,
        b_if=jnp.pad(b_if[l], (0, LANES - 2 * N_HEADS)).reshape(1, LANES),
        conv_w=conv_w[l], head_gain=head_gain[l].reshape(1, MLSTM_DIM),
        w_out=w_out[l].astype(BF16),
        g_ffn=norm_ffn[l].reshape(1, D_MODEL),
        wr=wr, br=br,
        w1=w1, w3=w3, w2=w2, layer=l,
    )


def _trunks(xs_in, states_in, wts, g_final):
    depth = len(wts)
    shapes = [x.shape for x in xs_in]
    x2ds = [x.reshape(x.shape[0] * x.shape[1], D_MODEL) for x in xs_in]
    new_states = [[] for _ in xs_in]
    for l in range(depth):
        groups = []
        for gi, (x2d, shp, st) in enumerate(zip(x2ds, shapes, states_in)):
            grp = _mix_and_route(x2d, shp[0], shp[1], st[0][l], st[1][l], st[2][l], st[3][l], wts[l])
            new_states[gi].append(grp["states"])
            groups.append(grp)
        x2ds = _moe(groups, wts[l], g_final, l == depth - 1)
    outs = []
    for x2d, shp, sts in zip(x2ds, shapes, new_states):
        outs.append((x2d.reshape(shp),) + tuple(jnp.stack([s[k] for s in sts]) for k in range(4)))
    return outs


def kernel(x_prompt, x_sample, state_conv, state_mlstm_C, state_mlstm_n, state_mlstm_m,
           norm_mix, w_in, b_if, conv_w, head_gain, w_out, norm_ffn,
           w_router_group, b_router_group, w_router_expert, b_router_expert,
           w1, w3, w2, norm_final):
    depth = w_in.shape[0]
    wts = [_prep_weights(l, norm_mix, w_in, b_if, conv_w, head_gain, w_out, norm_ffn, w_router_group,
                         b_router_group, w_router_expert, b_router_expert, w1, w3, w2)
           for l in range(depth)]
    g_final = norm_final.reshape(1, D_MODEL)
    b = x_prompt.shape[0]
    conv0 = jnp.zeros((depth, b, CONV_WIDTH - 1, CONV_DIM), F32)
    c0 = jnp.zeros((depth, b, N_HEADS, DQK, DV), F32)
    n0 = jnp.zeros((depth, b, N_HEADS, DQK), F32)
    m0 = jnp.full((depth, b, N_HEADS), M_INIT, F32)
    (y_p, conv_p, c_p, n_p, m_p), (y_s, conv_s, c_s, n_s, m_s) = _trunks(
        [x_prompt, x_sample],
        [(conv0, c0, n0, m0), (state_conv, state_mlstm_C, state_mlstm_n, state_mlstm_m)],
        wts, g_final)
    return (y_p, y_s, conv_p, c_p, n_p, m_p, conv_s, c_s, n_s, m_s)
```

```python
import functools

import jax
import jax.numpy as jnp
from jax import lax
from jax.experimental import pallas as pl
from jax.experimental.pallas import tpu as pltpu

F32 = jnp.float32
BF16 = jnp.bfloat16

D_MODEL = 2048
CONV_DIM = 1024
N_HEADS = 8
N_PAIRS = N_HEADS // 2
DV = 128
DQK = 64
QK_DIM = N_HEADS * DQK
MLSTM_DIM = N_HEADS * DV
N_GROUPS = 4
EXPERTS_PER_GROUP = 8
N_EXPERTS = 32
TOP_K = 2
D_EXPERT = 512
EPS = 1e-6
M_INIT = -1e30
CONV_WIDTH = 3
LANES = 128
MAIN_COLS = 3 * CONV_DIM + 2 * QK_DIM + 2 * MLSTM_DIM
ROW_WORDS = D_MODEL // 2
PROJ_BLOCK = 1024
F32_SOURCE_BLOCKS = (0, 1, 2, 5, 3)
V_SOURCE_BLOCK = 4
CHUNK = 64
EXPERT_BLOCK = 512
TOKEN_TILE = 512
CONV_GROUP_COLS = 256
CONV_GROUPS = CONV_DIM // CONV_GROUP_COLS
VMEM_LIMIT = 56 * 1024 * 1024


def _cparams(sem):
    return pltpu.CompilerParams(dimension_semantics=sem, vmem_limit_bytes=VMEM_LIMIT)


def _pack_halves(x):
    half = x.shape[1] // 2
    lo = pltpu.bitcast(x[:, :half].astype(BF16).astype(F32), jnp.uint32)
    hi = pltpu.bitcast(x[:, half:].astype(BF16).astype(F32), jnp.uint32)
    return (lo >> 16) | (hi & jnp.uint32(0xFFFF0000))


def _unpack_halves(w):
    lo = pltpu.bitcast(w << 16, F32)
    hi = pltpu.bitcast(w & jnp.uint32(0xFFFF0000), F32)
    return lo, hi


def _split3(x):
    hi = x.astype(BF16)
    r1 = x - hi.astype(F32)
    mid = r1.astype(BF16)
    lo = (r1 - mid.astype(F32)).astype(BF16)
    return hi, mid, lo


def _inproj_kernel(x_ref, xnext_ref, g_ref, w_ref, wg_ref, b_ref, pf_ref, pb_ref, li_ref, lf_ref, xh_a, xh_b,
                   *, tm, rows):
    i = pl.program_id(0)

    def normalise(src_ref, dst_ref):
        for r in range(tm // rows):
            sl = slice(r * rows, (r + 1) * rows)
            x = src_ref[sl, :]
            ms = jnp.mean(x * x, axis=-1, keepdims=True)
            dst_ref[sl, :] = (x * lax.rsqrt(ms + EPS) * g_ref[...]).astype(BF16)

    @pl.when(i == 0)
    def _():
        normalise(x_ref, xh_a)

    @pl.when(lax.rem(i, 2) == 0)
    def _():
        normalise(xnext_ref, xh_b)
        _inproj_tile(xh_a[...], w_ref, wg_ref, b_ref, pf_ref, pb_ref, li_ref, lf_ref)

    @pl.when(lax.rem(i, 2) == 1)
    def _():
        normalise(xnext_ref, xh_a)
        _inproj_tile(xh_b[...], w_ref, wg_ref, b_ref, pf_ref, pb_ref, li_ref, lf_ref)


def _inproj_tile(xh, w_ref, wg_ref, b_ref, pf_ref, pb_ref, li_ref, lf_ref):
    gt = jnp.dot(xh, wg_ref[...], preferred_element_type=F32) + b_ref[...]
    lane = lax.broadcasted_iota(jnp.int32, gt.shape, 1)
    valid = lane < N_HEADS
    li_ref[...] = jnp.where(valid, gt, 0.0)
    fg = pltpu.roll(gt, LANES - N_HEADS, axis=1)
    lf = jnp.minimum(fg, 0.0) - jnp.log1p(jnp.exp(-jnp.abs(fg)))
    lf_ref[...] = jnp.where(valid, lf, 0.0)

    def block(src):
        return jnp.dot(xh, w_ref[:, src * PROJ_BLOCK:(src + 1) * PROJ_BLOCK], preferred_element_type=F32)

    for dst, src in enumerate(F32_SOURCE_BLOCKS):
        pf_ref[:, dst * PROJ_BLOCK:(dst + 1) * PROJ_BLOCK] = block(src)
    pb_ref[...] = block(V_SOURCE_BLOCK).astype(BF16)


def _inproj(x2d, g, w_all, wg, b_pad):
    n = x2d.shape[0]
    tm = min(n, 256)
    nt = n // tm
    kern = functools.partial(_inproj_kernel, tm=tm, rows=32)
    const = lambda shape: pl.BlockSpec(shape, lambda i: (0, 0))
    return pl.pallas_call(
        kern,
        grid=(nt,),
        in_specs=[
            pl.BlockSpec((tm, D_MODEL), lambda i: (i, 0)),
            pl.BlockSpec((tm, D_MODEL), lambda i: (jnp.minimum(i + 1, nt - 1), 0)),
            const((1, D_MODEL)),
            pl.BlockSpec(w_all.shape, lambda i: (0, 0), pipeline_mode=pl.Buffered(1)),
            const((D_MODEL, LANES)),
            const((1, LANES)),
        ],
        out_specs=[
            pl.BlockSpec((tm, len(F32_SOURCE_BLOCKS) * PROJ_BLOCK), lambda i: (i, 0)),
            pl.BlockSpec((tm, PROJ_BLOCK), lambda i: (i, 0)),
            pl.BlockSpec((tm, LANES), lambda i: (i, 0)),
            pl.BlockSpec((tm, LANES), lambda i: (i, 0)),
        ],
        out_shape=[
            jax.ShapeDtypeStruct((n, len(F32_SOURCE_BLOCKS) * PROJ_BLOCK), F32),
            jax.ShapeDtypeStruct((n, PROJ_BLOCK), BF16),
            jax.ShapeDtypeStruct((n, LANES), F32),
            jax.ShapeDtypeStruct((n, LANES), F32),
        ],
        scratch_shapes=[pltpu.VMEM((tm, D_MODEL), BF16), pltpu.VMEM((tm, D_MODEL), BF16)],
        compiler_params=_cparams(("arbitrary",)),
        name="inproj",
    )(x2d, x2d, g, w_all, wg, b_pad)


def _cummax_rows(x, length):
    row = lax.broadcasted_iota(jnp.int32, x.shape, 0)
    d = 1
    while d < length:
        shifted = pltpu.roll(x, d, axis=0)
        x = jnp.maximum(x, jnp.where(row >= d, shifted, -jnp.inf))
        d *= 2
    return x


def _pad_rows(x, length):
    if length == LANES:
        return x
    return jnp.concatenate([x, jnp.zeros((LANES - length, x.shape[1]), x.dtype)], axis=0)


def _mixer_kernel(u_ref, gc_ref, gb_ref, q_ref, k_ref, v_ref, o_ref, li_ref, lf_ref,
                  cw_ref, hg_ref, conv0_ref, c0_ref, n0_ref, m0_ref,
                  mix_ref, convn_ref, cn_ref, nn_ref, mn_ref,
                  *scratch, tb, L, G):
    t = pl.program_id(1)
    nt = pl.num_programs(1)
    n_state = G * N_PAIRS
    c_sc = [scratch[g * N_PAIRS:(g + 1) * N_PAIRS] for g in range(G)]
    n_sc = [scratch[n_state + g * N_PAIRS:n_state + (g + 1) * N_PAIRS] for g in range(G)]
    m_sc = scratch[2 * n_state:2 * n_state + G]
    carry_sc = [scratch[2 * n_state + G + g * CONV_GROUPS:2 * n_state + G + (g + 1) * CONV_GROUPS]
                for g in range(G)]

    @pl.when(t == 0)
    def _():
        zero = jnp.zeros((DQK, DV), F32)
        for g in range(G):
            for p in range(N_PAIRS):
                top = jnp.concatenate([c0_ref[g, 2 * p], zero], axis=1)
                bot = jnp.concatenate([zero, c0_ref[g, 2 * p + 1]], axis=1)
                c_sc[g][p][...] = jnp.concatenate([top, bot], axis=0)
                n_sc[g][p][...] = n0_ref[g, p:p + 1, :]
            m_sc[g][...] = m0_ref[g]
            for cg in range(CONV_GROUPS):
                cs = slice(cg * CONV_GROUP_COLS, (cg + 1) * CONV_GROUP_COLS)
                carry_sc[g][cg][...] = jnp.zeros((8, CONV_GROUP_COLS), F32)
                carry_sc[g][cg][6:8, :] = conv0_ref[g, :, cs]

    row = lax.broadcasted_iota(jnp.int32, (L, L), 0)
    col = lax.broadcasted_iota(jnp.int32, (L, L), 1)
    causal = col <= row
    tril = jnp.where(causal, 1.0, 0.0).astype(BF16)
    lane_l = lax.broadcasted_iota(jnp.int32, (L, LANES), 1)
    low_l = lane_l < DQK
    krow = lax.broadcasted_iota(jnp.int32, (LANES, 2 * DV), 0)
    lane1 = lax.broadcasted_iota(jnp.int32, (1, LANES), 1)

    def chunk(c, carry):
        rows = pl.ds(pl.multiple_of(c * L, L), L)
        units = [(g, p) for g in range(G) for p in range(N_PAIRS)]
        heads = [(g, h) for g in range(G) for h in range(N_HEADS)]
        convs = [conv_chunk(g, rows) for g in range(G)]
        gates = [gate_algebra(g, rows) for g in range(G)]
        outs, states = {}, {}
        for g, p in units:
            pr = pair_scores(g, p, rows)
            v2 = v_ref[g, rows, p * 2 * DV:(p + 1) * 2 * DV]
            for hh in range(2):
                h = 2 * p + hh
                s = intra_weights(gates[g], pr, h)
                intra = jnp.dot(s.astype(BF16), v2[:, hh * DV:(hh + 1) * DV], preferred_element_type=F32)
                o_sig = 1.0 / (1.0 + jnp.exp(-o_ref[g, rows, h * DV:(h + 1) * DV]))
                outs[(g, h)] = head_output(gates[g], pr, s, intra, o_sig, h)
            states[(g, p)] = state_update(gates[g], pr, v2, p)
        for g in range(G):
            for cg in range(CONV_GROUPS):
                y_conv, z_tail = convs[g][cg]
                mix_ref[g, rows, cg * CONV_GROUP_COLS:(cg + 1) * CONV_GROUP_COLS] = y_conv
                carry_sc[g][cg][6:8, :] = z_tail
            m_sc[g][...] = gates[g]["m_new"]
        for g, h in heads:
            mix_ref[g, rows, CONV_DIM + h * DV:CONV_DIM + (h + 1) * DV] = outs[(g, h)]
        for g, p in units:
            c_sc[g][p][...], n_sc[g][p][...] = states[(g, p)]
        return carry

    def conv_chunk(g, rows):
        cw = CONV_GROUP_COLS
        res = []
        for cg in range(CONV_GROUPS):
            cs = slice(cg * cw, (cg + 1) * cw)
            z = gc_ref[g, rows, cs] * u_ref[g, rows, cs]
            prev = carry_sc[g][cg][...]
            p1 = prev[7:8, :]
            p2 = prev[6:7, :]
            rw = lax.broadcasted_iota(jnp.int32, (L, cw), 0)
            z1 = jnp.where(rw >= 1, pltpu.roll(z, 1, axis=0), p1)
            z2 = jnp.where(rw >= 2, pltpu.roll(z, 2, axis=0), jnp.where(rw == 1, p1, p2))
            y = z2 * cw_ref[0:1, cs] + z1 * cw_ref[1:2, cs] + z * cw_ref[2:3, cs]
            res.append(((gb_ref[g, rows, cs] * y).astype(BF16), z[L - 2:L, :]))
        return res

    def gate_algebra(g, rows):
        li = li_ref[g, rows, :]
        lf = lf_ref[g, rows, :]
        hi, mid, lo = _split3(lf)
        F = (jnp.dot(tril, hi, preferred_element_type=F32)
             + jnp.dot(tril, mid, preferred_element_type=F32)
             + jnp.dot(tril, lo, preferred_element_type=F32))
        r = li - F
        cm = _cummax_rows(r, L)
        mprev = m_sc[g][...]
        mx = jnp.maximum(mprev, cm)
        M = F + mx
        neg_mx = -mx
        w_inter = jnp.exp(mprev - mx)
        em = jnp.exp(-M)
        gs = jnp.exp(r - mx[L - 1:L, :])
        g_inter = w_inter[L - 1:L, :]
        rT = _pad_rows(r, L).T[:, 0:L]
        return dict(neg_mx=neg_mx, w_inter=w_inter, em=em, gs=gs, g_inter=g_inter, rT=rT, m_new=M[L - 1:L, :])

    def pair_scores(g, p, rows):
        ps = slice(p * LANES, (p + 1) * LANES)
        q2 = q_ref[g, rows, ps].astype(BF16)
        kf = k_ref[g, rows, ps] * DQK ** -0.5
        k2 = kf.astype(BF16)
        q_e = jnp.where(low_l, q2, jnp.zeros_like(q2))
        q_o = jnp.where(low_l, jnp.zeros_like(q2), q2)
        q_st = jnp.concatenate([q_e, q_o], axis=0)
        n_row = n_sc[g][p][...]
        n_b = jnp.broadcast_to(n_row, (LANES, LANES)).astype(BF16)
        k_aug = jnp.concatenate([n_b, k2], axis=0)
        sn = lax.dot_general(q_st, k_aug, (((1,), (1,)), ((), ())),
                             preferred_element_type=F32)
        c_full = c_sc[g][p][...]
        qc = jnp.dot(q_st, c_full.astype(BF16), preferred_element_type=F32)
        return dict(kf=kf, k2=k2, n_row=n_row, c_full=c_full, sn=sn, qc=qc)

    def intra_weights(gt, pr, h):
        rs = slice((h % 2) * L, (h % 2 + 1) * L)
        dmat = gt["neg_mx"][:, h:h + 1] + gt["rT"][h:h + 1, :]
        return pr["sn"][rs, LANES:LANES + L] * jnp.exp(jnp.where(causal, dmat, -jnp.inf))

    def head_output(gt, pr, s, intra, o_sig, h):
        hh = h % 2
        rs = slice(hh * L, (hh + 1) * L)
        hs = slice(h * DV, (h + 1) * DV)
        wi = gt["w_inter"][:, h:h + 1]
        num = wi * pr["qc"][rs, hh * DV:(hh + 1) * DV] + intra
        den = wi * pr["sn"][rs, 0:LANES] + jnp.sum(s, axis=-1, keepdims=True)
        hv = num / jnp.maximum(jnp.abs(den), gt["em"][:, h:h + 1])
        ms = jnp.mean(hv * hv, axis=-1, keepdims=True)
        hn = hv * lax.rsqrt(ms + EPS) * hg_ref[0:1, hs]
        return (hn * o_sig).astype(BF16)

    def state_update(gt, pr, v2, p):
        gs = gt["gs"]
        kgw = jnp.where(low_l, gs[:, 2 * p:2 * p + 1], gs[:, 2 * p + 1:2 * p + 2])
        kg = pr["kf"] * kgw
        upd = jnp.dot(_pad_rows(kg, L).T.astype(BF16), _pad_rows(v2, L),
                      preferred_element_type=F32)
        ge = gt["g_inter"][0:1, 2 * p:2 * p + 1]
        go = gt["g_inter"][0:1, 2 * p + 1:2 * p + 2]
        c_new = jnp.where(krow < DQK, ge, go) * pr["c_full"] + upd
        kn = pr["k2"].astype(F32) * kgw.astype(BF16).astype(F32)
        n_new = jnp.where(lane1 < DQK, ge, go) * pr["n_row"] + jnp.sum(kn, axis=0, keepdims=True)
        return c_new, n_new

    lax.fori_loop(0, tb // L, chunk, 0)

    @pl.when(t == nt - 1)
    def _():
        for g in range(G):
            for cg in range(CONV_GROUPS):
                cs = slice(cg * CONV_GROUP_COLS, (cg + 1) * CONV_GROUP_COLS)
                convn_ref[g, :, cs] = carry_sc[g][cg][6:8, :]
            for p in range(N_PAIRS):
                cf = c_sc[g][p][...]
                cn_ref[g, 2 * p] = cf[0:DQK, 0:DV]
                cn_ref[g, 2 * p + 1] = cf[DQK:2 * DQK, DV:2 * DV]
                nn_ref[g, p:p + 1, :] = n_sc[g][p][...]
            mn_ref[g] = m_sc[g][...]


def _mixer(pf, pb, li, lf, conv_w, head_gain, conv0, c0, n0p, m0p, bsz, seq):
    G = 1
    L = min(seq, CHUNK)
    tb = min(seq, 512)
    nt = seq // tb
    kern = functools.partial(_mixer_kernel, tb=tb, L=L, G=G)
    wide = lambda cb: pl.BlockSpec((G, tb, 1024), lambda b, t: (b, t, cb))
    half = lambda cb: pl.BlockSpec((G, tb, 512), lambda b, t: (b, t, cb))
    gate = pl.BlockSpec((G, tb, LANES), lambda b, t: (b, t, 0))
    st = lambda *shape: pl.BlockSpec((G,) + shape, lambda b, t: (b,) + (0,) * len(shape))
    return pl.pallas_call(
        kern,
        grid=(bsz // G, nt),
        in_specs=[
            wide(0), wide(1), wide(2), half(8), half(9), wide(0), wide(3), gate, gate,
            pl.BlockSpec((CONV_WIDTH, CONV_DIM), lambda b, t: (0, 0)),
            pl.BlockSpec((1, MLSTM_DIM), lambda b, t: (0, 0)),
            st(2, CONV_DIM), st(N_HEADS, DQK, DV), st(N_PAIRS, LANES), st(1, LANES),
        ],
        out_specs=[
            pl.BlockSpec((G, tb, D_MODEL), lambda b, t: (b, t, 0)),
            st(2, CONV_DIM), st(N_HEADS, DQK, DV), st(N_PAIRS, LANES), st(1, LANES),
        ],
        out_shape=[
            jax.ShapeDtypeStruct((bsz, seq, D_MODEL), BF16),
            jax.ShapeDtypeStruct((bsz, 2, CONV_DIM), F32),
            jax.ShapeDtypeStruct((bsz, N_HEADS, DQK, DV), F32),
            jax.ShapeDtypeStruct((bsz, N_PAIRS, LANES), F32),
            jax.ShapeDtypeStruct((bsz, 1, LANES), F32),
        ],
        scratch_shapes=(
            [pltpu.VMEM((LANES, 2 * DV), F32)] * (G * N_PAIRS)
            + [pltpu.VMEM((1, LANES), F32)] * (G * N_PAIRS)
            + [pltpu.VMEM((1, LANES), F32)] * G
            + [pltpu.VMEM((8, CONV_GROUP_COLS), F32)] * (G * CONV_GROUPS)
        ),
        compiler_params=_cparams(("arbitrary", "arbitrary")),
        name="mixer",
    )(pf, pf, pf, pf, pf, pb, pf, li, lf, conv_w, head_gain, conv0, c0, n0p, m0p)


def _outproj_kernel(mix_ref, w_ref, x_ref, g_ref, wr_ref, br_ref,
                    xo_ref, xq_ref, ri_ref, rg_ref, cnt_ref, base_sc, x_even, x_odd, *, tm):
    s = pl.program_id(0)

    @pl.when(s == 0)
    def _():
        base_sc[...] = jnp.zeros(base_sc.shape, F32)
        x_odd[...] = jnp.zeros(x_odd.shape, F32)

    refs = (mix_ref, w_ref, x_ref, g_ref, wr_ref, br_ref, xo_ref, xq_ref, ri_ref, rg_ref, cnt_ref, base_sc)

    @pl.when(lax.rem(s, 2) == 0)
    def _():
        _outproj_step(s, x_even, x_odd, *refs, tm=tm)

    @pl.when(lax.rem(s, 2) == 1)
    def _():
        _outproj_step(s, x_odd, x_even, *refs, tm=tm)


def _outproj_step(s, x_this, x_prev, mix_ref, w_ref, x_ref, g_ref, wr_ref, br_ref,
                  xo_ref, xq_ref, ri_ref, rg_ref, cnt_ref, base_sc, *, tm):
    x_new = x_ref[...] + jnp.dot(mix_ref[...], w_ref[...], preferred_element_type=F32)
    xo_ref[...] = x_new
    x_this[...] = x_new

    counted = jnp.where(s > 0, 1.0, 0.0)
    x = x_prev[...]
    ms = jnp.mean(x * x, axis=-1, keepdims=True)
    xn = x * lax.rsqrt(ms + EPS) * g_ref[...]
    xq_ref[...] = _pack_halves(xn)
    lg = jnp.dot(xn.astype(BF16), wr_ref[...], preferred_element_type=F32) + br_ref[...]

    lane = lax.broadcasted_iota(jnp.int32, (tm, LANES), 1)
    lanef = lane.astype(F32)
    big = jnp.float32(1e9)
    ninf = -jnp.inf
    is_g = (lane >= N_EXPERTS) & (lane < N_EXPERTS + N_GROUPS)
    glog = jnp.where(is_g, lg, ninf)
    gmax = jnp.max(glog, axis=1, keepdims=True)
    gi = jnp.min(jnp.where(glog == gmax, lanef, big), axis=1, keepdims=True) - N_EXPERTS
    pgi = 1.0 / jnp.sum(jnp.where(is_g, jnp.exp(lg - gmax), 0.0), axis=1, keepdims=True)
    lo = gi * EXPERTS_PER_GROUP
    in_grp = (lanef >= lo) & (lanef < lo + EXPERTS_PER_GROUP)
    el = jnp.where(in_grp, lg, ninf)
    v1 = jnp.max(el, axis=1, keepdims=True)
    i1 = jnp.min(jnp.where(el == v1, lanef, big), axis=1, keepdims=True)
    el2 = jnp.where(lanef == i1, ninf, el)
    v2 = jnp.max(el2, axis=1, keepdims=True)
    i2 = jnp.min(jnp.where(el2 == v2, lanef, big), axis=1, keepdims=True)
    e21 = jnp.exp(v2 - v1)
    g1 = pgi / (1.0 + e21)
    g2 = pgi * e21 / (1.0 + e21)
    sel1 = lanef == i1
    sel2 = lanef == i2
    oh = jnp.where(sel1 | sel2, counted, 0.0)
    r_i = lax.broadcasted_iota(jnp.int32, (tm, tm), 0)
    c_i = lax.broadcasted_iota(jnp.int32, (tm, tm), 1)
    stril = jnp.where(c_i < r_i, 1.0, 0.0).astype(BF16)
    tot = jnp.dot(stril, oh.astype(BF16), preferred_element_type=F32) + base_sc[...]
    r1 = jnp.sum(jnp.where(sel1, tot, 0.0), axis=1, keepdims=True)
    r2 = jnp.sum(jnp.where(sel2, tot, 0.0), axis=1, keepdims=True)
    base = base_sc[...] + jnp.sum(oh, axis=0, keepdims=True)
    base_sc[...] = base
    cnt_ref[...] = base
    ri = jnp.where(lane == 0, i1, jnp.where(lane == 1, i2, jnp.where(lane == 2, r1,
                   jnp.where(lane == 3, r2, 0.0))))
    ri_t = jnp.concatenate([ri[c * LANES:(c + 1) * LANES, :].T[0:8, :] for c in range(tm // LANES)], axis=1)
    ri_ref[...] = ri_t.astype(jnp.int32)
    rg_ref[...] = jnp.where(lane == 0, g1, jnp.where(lane == 1, g2, 0.0))


def _outproj(mix, w_out, x2d, g, wr, br):
    n = x2d.shape[0]
    tm = min(n, TOKEN_TILE)
    assert tm % LANES == 0
    nt = n // tm
    kern = functools.partial(_outproj_kernel, tm=tm)
    cur = lambda s: jnp.minimum(s, nt - 1)
    prev = lambda s: jnp.maximum(s - 1, 0)
    const = lambda r, c, **kw: pl.BlockSpec((r, c), lambda s: (0, 0), **kw)
    return pl.pallas_call(
        kern,
        grid=(nt + 1,),
        in_specs=[pl.BlockSpec((tm, D_MODEL), lambda s: (cur(s), 0)),
                  const(D_MODEL, D_MODEL, pipeline_mode=pl.Buffered(1)),
                  pl.BlockSpec((tm, D_MODEL), lambda s: (cur(s), 0)), const(1, D_MODEL),
                  const(D_MODEL, LANES), const(1, LANES)],
        out_specs=[pl.BlockSpec((tm, D_MODEL), lambda s: (cur(s), 0)),
                   pl.BlockSpec((tm, D_MODEL // 2), lambda s: (prev(s), 0)),
                   pl.BlockSpec((None, 8, tm), lambda s: (prev(s), 0, 0)),
                   pl.BlockSpec((tm, LANES), lambda s: (prev(s), 0)), const(1, LANES)],
        out_shape=[
            jax.ShapeDtypeStruct((n, D_MODEL), F32),
            jax.ShapeDtypeStruct((n, D_MODEL // 2), jnp.uint32),
            jax.ShapeDtypeStruct((n // tm, 8, tm), jnp.int32),
            jax.ShapeDtypeStruct((n, LANES), F32),
            jax.ShapeDtypeStruct((1, LANES), F32),
        ],
        scratch_shapes=[pltpu.VMEM((1, LANES), F32), pltpu.VMEM((tm, D_MODEL), F32),
                        pltpu.VMEM((tm, D_MODEL), F32)],
        compiler_params=_cparams(("arbitrary",)),
        name="outproj",
    )(mix, w_out, x2d, g, wr, br)


def _row_copy(src_ref, src_row, dst_ref, dst_row, sem):
    return pltpu.make_async_copy(src_ref.at[pl.ds(src_row, 1), :], dst_ref.at[pl.ds(dst_row, 1), :], sem)


def _dispatch_kernel(pad_end_ref, padded_ref, nused_ref, da_ref, db_ref, xa_ref, xb_ref, xs_ref, zero_sc, sem_z,
                     sem, *, bm, nb, tiles_a):
    i = pl.program_id(0)

    @pl.when(i == 0)
    def _():
        zero_sc[...] = jnp.zeros(zero_sc.shape, jnp.uint32)

        def zcopy(start):
            return pltpu.make_async_copy(zero_sc, xs_ref.at[pl.ds(pl.multiple_of(start, bm), bm), :], sem_z)

        def zstart(e, c):
            @pl.when(padded_ref[e] > 0)
            def _():
                zcopy(pad_end_ref[e] - bm).start()
            return c

        def zwait(e, c):
            @pl.when(padded_ref[e] > 0)
            def _():
                zcopy(pad_end_ref[e] - bm).wait()
            return c

        def tstart(b, c):
            zcopy(b * bm).start()
            return c

        def twait(b, c):
            zcopy(b * bm).wait()
            return c

        lax.fori_loop(0, N_EXPERTS, zstart, 0)
        lax.fori_loop(nused_ref[0], nb, tstart, 0)
        lax.fori_loop(0, N_EXPERTS, zwait, 0)
        lax.fori_loop(nused_ref[0], nb, twait, 0)

    def scatter_tile(src_ref, dest_ref):
        tm = src_ref.shape[0]

        def start(j, c):
            _row_copy(src_ref, j, xs_ref, dest_ref[0, 0, j], sem).start()
            _row_copy(src_ref, j, xs_ref, dest_ref[0, 0, tm + j], sem).start()
            return c

        lax.fori_loop(0, tm, start, 0, unroll=8)
        for _ in range(TOP_K):
            pltpu.make_async_copy(src_ref, xs_ref.at[pl.ds(0, tm), :], sem).wait()

    @pl.when(i < tiles_a)
    def _():
        scatter_tile(xa_ref, da_ref)

    @pl.when(i >= tiles_a)
    def _():
        scatter_tile(xb_ref, db_ref)


def _dispatch(xq_a, xq_b, dest_a, dest_b, pad_end, padded, nused, p_rows, bm):
    tm_a, tm_b = dest_a.shape[2] // 2, dest_b.shape[2] // 2
    tiles_a, tiles_b = dest_a.shape[0], dest_b.shape[0]
    kern = functools.partial(_dispatch_kernel, bm=bm, nb=p_rows // bm, tiles_a=tiles_a)
    in_a = lambda i, pe, pd, nu: jnp.minimum(i, tiles_a - 1)
    in_b = lambda i, pe, pd, nu: jnp.maximum(i - tiles_a, 0)
    return pl.pallas_call(
        kern,
        grid_spec=pltpu.PrefetchScalarGridSpec(
            num_scalar_prefetch=3,
            grid=(tiles_a + tiles_b,),
            in_specs=[
                pl.BlockSpec((1, 1, 2 * tm_a), lambda *a: (in_a(*a), 0, 0), memory_space=pltpu.SMEM),
                pl.BlockSpec((1, 1, 2 * tm_b), lambda *a: (in_b(*a), 0, 0), memory_space=pltpu.SMEM),
                pl.BlockSpec((tm_a, ROW_WORDS), lambda *a: (in_a(*a), 0)),
                pl.BlockSpec((tm_b, ROW_WORDS), lambda *a: (in_b(*a), 0)),
            ],
            out_specs=pl.BlockSpec(memory_space=pl.ANY),
            scratch_shapes=[pltpu.VMEM((bm, ROW_WORDS), jnp.uint32), pltpu.SemaphoreType.DMA(()),
                            pltpu.SemaphoreType.DMA(())],
        ),
        out_shape=jax.ShapeDtypeStruct((p_rows, ROW_WORDS), jnp.uint32),
        compiler_params=_cparams(("arbitrary",)),
        name="dispatch",
    )(pad_end, padded, nused, dest_a, dest_b, xq_a, xq_b)


def _experts_kernel(blk_e_ref, nused_ref, xs_ref, w1_ref, w3_ref, w2_ref, ys_ref, w1b, w3b, w2b):
    i = pl.program_id(0)

    @pl.when((i == 0) | (blk_e_ref[i] != blk_e_ref[jnp.maximum(i - 1, 0)]))
    def _():
        def cast_in(r, c):
            sl = pl.ds(pl.multiple_of(r * 256, 256), 256)
            w1b[sl, :] = w1_ref[sl, :].astype(BF16)
            w3b[sl, :] = w3_ref[sl, :].astype(BF16)
            return c

        def cast_out(r, c):
            sl = pl.ds(pl.multiple_of(r * 64, 64), 64)
            w2b[sl, :] = w2_ref[sl, :].astype(BF16)
            return c

        lax.fori_loop(0, D_MODEL // 256, cast_in, 0)
        lax.fori_loop(0, D_EXPERT // 64, cast_out, 0)

    @pl.when(i < nused_ref[0])
    def _():
        lo, hi = _unpack_halves(xs_ref[...])
        lo = lo.astype(BF16)
        hi = hi.astype(BF16)
        h1 = (jnp.dot(lo, w1b[0:ROW_WORDS, :], preferred_element_type=F32)
              + jnp.dot(hi, w1b[ROW_WORDS:D_MODEL, :], preferred_element_type=F32))
        h3 = (jnp.dot(lo, w3b[0:ROW_WORDS, :], preferred_element_type=F32)
              + jnp.dot(hi, w3b[ROW_WORDS:D_MODEL, :], preferred_element_type=F32))
        hb = (h1 * (1.0 / (1.0 + jnp.exp(-h1)))) * h3
        ys_ref[...] = _pack_halves(jnp.dot(hb.astype(BF16), w2b[...], preferred_element_type=F32))

    @pl.when(i >= nused_ref[0])
    def _():
        ys_ref[...] = jnp.zeros(ys_ref.shape, jnp.uint32)


def _experts(xs, blk_e, nused, w1, w3, w2, layer, bm):
    p_rows = xs.shape[0]
    nb = p_rows // bm
    rowmap = lambda i, be, nu: (jnp.minimum(i, nu[0] - 1), 0)
    wmap = lambda i, be, nu: (layer, be[i], 0, 0)
    return pl.pallas_call(
        _experts_kernel,
        grid_spec=pltpu.PrefetchScalarGridSpec(
            num_scalar_prefetch=2,
            grid=(nb,),
            in_specs=[
                pl.BlockSpec((bm, ROW_WORDS), rowmap),
                pl.BlockSpec((None, None, D_MODEL, D_EXPERT), wmap),
                pl.BlockSpec((None, None, D_MODEL, D_EXPERT), wmap),
                pl.BlockSpec((None, None, D_EXPERT, D_MODEL), wmap),
            ],
            out_specs=pl.BlockSpec((bm, ROW_WORDS), lambda i, be, nu: (i, 0)),
            scratch_shapes=[pltpu.VMEM((D_MODEL, D_EXPERT), BF16), pltpu.VMEM((D_MODEL, D_EXPERT), BF16),
                            pltpu.VMEM((D_EXPERT, D_MODEL), BF16)],
        ),
        out_shape=jax.ShapeDtypeStruct((p_rows, ROW_WORDS), jnp.uint32),
        compiler_params=_cparams(("arbitrary",)),
        name="experts",
    )(blk_e, nused, xs, w1, w3, w2)


def _combine_kernel(dest_ref, dnext_ref, x_ref, rg_ref, g_ref, ys_ref, out_ref, ybuf, sem, *, tm, rows, final):
    i = pl.program_id(0)
    nt = pl.num_programs(0)
    slot = lax.rem(i, 2)
    other = 1 - slot

    def gather_rows(d_ref, base, buf_slot):
        for jj in range(rows):
            j = base + jj
            _row_copy(ys_ref, d_ref[0, 0, j], ybuf.at[buf_slot, 0], j, sem.at[buf_slot]).start()
            _row_copy(ys_ref, d_ref[0, 0, tm + j], ybuf.at[buf_slot, 1], j, sem.at[buf_slot]).start()

    @pl.when(i == 0)
    def _():
        def first(r, c):
            gather_rows(dest_ref, pl.multiple_of(r * rows, rows), 0)
            return c

        lax.fori_loop(0, tm // rows, first, 0)

    for k in range(TOP_K):
        pltpu.make_async_copy(ys_ref.at[pl.ds(0, tm), :], ybuf.at[slot, k], sem.at[slot]).wait()

    def combine_rows(base):
        sl = pl.ds(base, rows)
        rg = rg_ref[sl, :]
        g1 = rg[:, 0:1]
        g2 = rg[:, 1:2]
        lo1, hi1 = _unpack_halves(ybuf[slot, 0, sl, :])
        lo2, hi2 = _unpack_halves(ybuf[slot, 1, sl, :])
        xa = x_ref[sl, 0:ROW_WORDS] + (g1 * lo1 + g2 * lo2)
        xb = x_ref[sl, ROW_WORDS:D_MODEL] + (g1 * hi1 + g2 * hi2)
        if final:
            ss = jnp.sum(xa * xa, axis=-1, keepdims=True) + jnp.sum(xb * xb, axis=-1, keepdims=True)
            sc = lax.rsqrt(ss / D_MODEL + EPS)
            xa = xa * sc * g_ref[:, 0:ROW_WORDS]
            xb = xb * sc * g_ref[:, ROW_WORDS:D_MODEL]
        out_ref[sl, 0:ROW_WORDS] = xa
        out_ref[sl, ROW_WORDS:D_MODEL] = xb

    @pl.when(i + 1 < nt)
    def _():
        def body(r, c):
            base = pl.multiple_of(r * rows, rows)
            gather_rows(dnext_ref, base, other)
            combine_rows(base)
            return c

        lax.fori_loop(0, tm // rows, body, 0, unroll=2)

    @pl.when(i + 1 == nt)
    def _():
        def body(r, c):
            combine_rows(pl.multiple_of(r * rows, rows))
            return c

        lax.fori_loop(0, tm // rows, body, 0, unroll=2)


def _combine(x_new, ys, dest3, rg, g_final, final):
    n = x_new.shape[0]
    tm = dest3.shape[2] // 2
    nt = n // tm
    kern = functools.partial(_combine_kernel, tm=tm, rows=16, final=final)
    return pl.pallas_call(
        kern,
        grid=(nt,),
        in_specs=[
            pl.BlockSpec((1, 1, 2 * tm), lambda i: (i, 0, 0), memory_space=pltpu.SMEM),
            pl.BlockSpec((1, 1, 2 * tm), lambda i: (jnp.minimum(i + 1, nt - 1), 0, 0), memory_space=pltpu.SMEM),
            pl.BlockSpec((tm, D_MODEL), lambda i: (i, 0)),
            pl.BlockSpec((tm, LANES), lambda i: (i, 0)),
            pl.BlockSpec((1, D_MODEL), lambda i: (0, 0)),
            pl.BlockSpec(memory_space=pl.ANY),
        ],
        out_specs=pl.BlockSpec((tm, D_MODEL), lambda i: (i, 0)),
        out_shape=jax.ShapeDtypeStruct((n, D_MODEL), F32),
        scratch_shapes=[pltpu.VMEM((2, TOP_K, tm, ROW_WORDS), jnp.uint32), pltpu.SemaphoreType.DMA((2,))],
        compiler_params=_cparams(("arbitrary",)),
        name="combine",
    )(dest3, dest3, x_new, rg, g_final, ys)


def _dest_kernel(first_ref, ri_ref, dest_ref, *, tiles, tm):
    for t in range(tiles):
        e1, e2 = ri_ref[t, 0:1, :], ri_ref[t, 1:2, :]
        s1 = jnp.zeros_like(e1)
        s2 = jnp.zeros_like(e2)
        for e in range(N_EXPERTS):
            s1 = jnp.where(e1 == e, first_ref[e], s1)
            s2 = jnp.where(e2 == e, first_ref[e], s2)
        dest_ref[t, :, 0:tm] = s1 + ri_ref[t, 2:3, :]
        dest_ref[t, :, tm:2 * tm] = s2 + ri_ref[t, 3:4, :]


def _dest_rows(first, ri_t):
    n_tiles, _, tm = ri_t.shape
    tiles = min(n_tiles, 16)
    return pl.pallas_call(
        functools.partial(_dest_kernel, tiles=tiles, tm=tm),
        grid_spec=pltpu.PrefetchScalarGridSpec(
            num_scalar_prefetch=1,
            grid=(n_tiles // tiles,),
            in_specs=[pl.BlockSpec((tiles, 8, tm), lambda i, f: (i, 0, 0))],
            out_specs=pl.BlockSpec((tiles, 1, 2 * tm), lambda i, f: (i, 0, 0)),
        ),
        out_shape=jax.ShapeDtypeStruct((n_tiles, 1, 2 * tm), jnp.int32),
        compiler_params=_cparams(("arbitrary",)),
        name="dest_rows",
    )(first, ri_t)


def _route_tables(ri_ts, counts, n_total, bm):
    cnts = [c[0, :N_EXPERTS].astype(jnp.int32) for c in counts]
    cnt = sum(cnts)
    padded = (cnt + bm - 1) // bm * bm
    pad_end = jnp.cumsum(padded)
    pad_start = pad_end - padded
    dests, first = [], pad_start
    for ri_t, c in zip(ri_ts, cnts):
        dests.append(_dest_rows(first.astype(jnp.int32), ri_t))
        first = first + c
    nb = -(-(n_total * TOP_K) // bm) + N_EXPERTS
    nused = (pad_end[-1] // bm).astype(jnp.int32)
    blk = jnp.minimum(jnp.arange(nb, dtype=jnp.int32), nused - 1) * bm
    blk_e = jnp.sum((pad_end[None, :] <= blk[:, None]).astype(jnp.int32), axis=1)
    blk_e = jnp.minimum(blk_e, N_EXPERTS - 1)
    return dests, pad_end.astype(jnp.int32), padded.astype(jnp.int32), blk_e, nused.reshape(1), nb * bm


def _mix_and_route(x2d, bsz, seq, conv0, c0, n0, m0, wts):
    n = bsz * seq
    pf, pb, li, lf = _inproj(x2d, wts["g_mix"], wts["w_main"], wts["wg"], wts["b_if"])
    n0p = n0.reshape(bsz, N_PAIRS, LANES)
    m0p = jnp.pad(m0, ((0, 0), (0, LANES - N_HEADS))).reshape(bsz, 1, LANES)
    by_row = lambda a: a.reshape(bsz, seq, a.shape[-1])
    mix, conv_n, c_n, n_n, m_n = _mixer(by_row(pf), by_row(pb), by_row(li), by_row(lf), wts["conv_w"],
                                         wts["head_gain"], conv0, c0, n0p, m0p, bsz, seq)
    x_new, xq, ri_t, rg, counts = _outproj(mix.reshape(n, D_MODEL), wts["w_out"], x2d, wts["g_ffn"],
                                           wts["wr"], wts["br"])
    states = (conv_n, c_n, n_n.reshape(bsz, N_HEADS, DQK), m_n[:, 0, :N_HEADS])
    return dict(x_new=x_new, xq=xq, ri_t=ri_t, rg=rg, counts=counts, states=states)


def _moe(groups, wts, g_final, final):
    bm = EXPERT_BLOCK
    n_total = sum(g["x_new"].shape[0] for g in groups)
    dests, pad_end, padded, blk_e, nused, p_rows = _route_tables(
        [g["ri_t"] for g in groups], [g["counts"] for g in groups], n_total, bm)
    xs = _dispatch(groups[0]["xq"], groups[1]["xq"], dests[0], dests[1], pad_end, padded, nused, p_rows, bm)
    ys = _experts(xs, blk_e, nused, wts["w1"], wts["w3"], wts["w2"], wts["layer"], bm)
    return [_combine(g["x_new"], ys, d, g["rg"], g_final, final) for g, d in zip(groups, dests)]


def _prep_weights(l, norm_mix, w_in, b_if, conv_w, head_gain, w_out, norm_ffn, w_router_group,
                  b_router_group, w_router_expert, b_router_expert, w1, w3, w2):
    wi = w_in[l]
    w_main = wi.astype(BF16)
    wg = jnp.pad(wi[:, MAIN_COLS:], ((0, 0), (0, LANES - 2 * N_HEADS))).astype(BF16)
    wr = jnp.pad(jnp.concatenate([w_router_expert[l], w_router_group[l]], axis=1),
                 ((0, 0), (0, LANES - N_EXPERTS - N_GROUPS))).astype(BF16)
    br = jnp.pad(jnp.concatenate([b_router_expert[l], b_router_group[l]]),
                 (0, LANES - N_EXPERTS - N_GROUPS)).reshape(1, LANES)
    return dict(
        g_mix=norm_mix[l].reshape(1, D_MODEL),
        w_main=w_main, wg=wg,
        b_if=jnp.pad(b_if[l], (0, LANES - 2 * N_HEADS)).reshape(1, LANES),
        conv_w=conv_w[l], head_gain=head_gain[l].reshape(1, MLSTM_DIM),
        w_out=w_out[l].astype(BF16),
        g_ffn=norm_ffn[l].reshape(1, D_MODEL),
        wr=wr, br=br,
        w1=w1, w3=w3, w2=w2, layer=l,
    )


def _trunks(xs_in, states_in, wts, g_final):
    depth = len(wts)
    shapes = [x.shape for x in xs_in]
    x2ds = [x.reshape(x.shape[0] * x.shape[1], D_MODEL) for x in xs_in]
    new_states = [[] for _ in xs_in]
    for l in range(depth):
        groups = []
        for gi, (x2d, shp, st) in enumerate(zip(x2ds, shapes, states_in)):
            grp = _mix_and_route(x2d, shp[0], shp[1], st[0][l], st[1][l], st[2][l], st[3][l], wts[l])
            new_states[gi].append(grp["states"])
            groups.append(grp)
        x2ds = _moe(groups, wts[l], g_final, l == depth - 1)
    outs = []
    for x2d, shp, sts in zip(x2ds, shapes, new_states):
        outs.append((x2d.reshape(shp),) + tuple(jnp.stack([s[k] for s in sts]) for k in range(4)))
    return outs


def kernel(x_prompt, x_sample, state_conv, state_mlstm_C, state_mlstm_n, state_mlstm_m,
           norm_mix, w_in, b_if, conv_w, head_gain, w_out, norm_ffn,
           w_router_group, b_router_group, w_router_expert, b_router_expert,
           w1, w3, w2, norm_final):
    depth = w_in.shape[0]
    wts = [_prep_weights(l, norm_mix, w_in, b_if, conv_w, head_gain, w_out, norm_ffn, w_router_group,
                         b_router_group, w_router_expert, b_router_expert, w1, w3, w2)
           for l in range(depth)]
    g_final = norm_final.reshape(1, D_MODEL)
    b = x_prompt.shape[0]
    conv0 = jnp.zeros((depth, b, CONV_WIDTH - 1, CONV_DIM), F32)
    c0 = jnp.zeros((depth, b, N_HEADS, DQK, DV), F32)
    n0 = jnp.zeros((depth, b, N_HEADS, DQK), F32)
    m0 = jnp.full((depth, b, N_HEADS), M_INIT, F32)
    (y_p, conv_p, c_p, n_p, m_p), (y_s, conv_s, c_s, n_s, m_s) = _trunks(
        [x_prompt, x_sample],
        [(conv0, c0, n0, m0), (state_conv, state_mlstm_C, state_mlstm_n, state_mlstm_m)],
        wts, g_final)
    return (y_p, y_s, conv_p, c_p, n_p, m_p, conv_s, c_s, n_s, m_s)
```

```python
import functools

import jax
import jax.numpy as jnp
from jax import lax
from jax.experimental import pallas as pl
from jax.experimental.pallas import tpu as pltpu

F32 = jnp.float32
BF16 = jnp.bfloat16

D_MODEL = 2048
CONV_DIM = 1024
N_HEADS = 8
N_PAIRS = N_HEADS // 2
DV = 128
DQK = 64
QK_DIM = N_HEADS * DQK
MLSTM_DIM = N_HEADS * DV
N_GROUPS = 4
EXPERTS_PER_GROUP = 8
N_EXPERTS = 32
TOP_K = 2
D_EXPERT = 512
EPS = 1e-6
M_INIT = -1e30
CONV_WIDTH = 3
LANES = 128
MAIN_COLS = 3 * CONV_DIM + 2 * QK_DIM + 2 * MLSTM_DIM
ROW_WORDS = D_MODEL // 2
PROJ_BLOCK = 1024
F32_SOURCE_BLOCKS = (0, 1, 2, 5, 3)
V_SOURCE_BLOCK = 4
CHUNK = 64
EXPERT_BLOCK = 512
TOKEN_TILE = 512
CONV_GROUP_COLS = 256
CONV_GROUPS = CONV_DIM // CONV_GROUP_COLS
VMEM_LIMIT = 56 * 1024 * 1024


def _cparams(sem):
    return pltpu.CompilerParams(dimension_semantics=sem, vmem_limit_bytes=VMEM_LIMIT)


def _pack_halves(x):
    half = x.shape[1] // 2
    lo = pltpu.bitcast(x[:, :half].astype(BF16).astype(F32), jnp.uint32)
    hi = pltpu.bitcast(x[:, half:].astype(BF16).astype(F32), jnp.uint32)
    return (lo >> 16) | (hi & jnp.uint32(0xFFFF0000))


def _unpack_halves(w):
    lo = pltpu.bitcast(w << 16, F32)
    hi = pltpu.bitcast(w & jnp.uint32(0xFFFF0000), F32)
    return lo, hi


def _split3(x):
    hi = x.astype(BF16)
    r1 = x - hi.astype(F32)
    mid = r1.astype(BF16)
    lo = (r1 - mid.astype(F32)).astype(BF16)
    return hi, mid, lo


def _inproj_kernel(x_ref, g_ref, w_ref, wg_ref, b_ref, pf_ref, pb_ref, li_ref, lf_ref, xh_ref,
                   *, tm, rows):
    def body(r, c):
        sl = pl.ds(pl.multiple_of(r * rows, rows), rows)
        x = x_ref[sl, :]
        ms = jnp.mean(x * x, axis=-1, keepdims=True)
        xh_ref[sl, :] = (x * lax.rsqrt(ms + EPS) * g_ref[...]).astype(BF16)
        return c

    lax.fori_loop(0, tm // rows, body, 0, unroll=4)
    xh = xh_ref[...]
    gt = jnp.dot(xh, wg_ref[...], preferred_element_type=F32) + b_ref[...]
    lane = lax.broadcasted_iota(jnp.int32, gt.shape, 1)
    valid = lane < N_HEADS
    li_ref[...] = jnp.where(valid, gt, 0.0)
    fg = pltpu.roll(gt, LANES - N_HEADS, axis=1)
    lf = jnp.minimum(fg, 0.0) - jnp.log1p(jnp.exp(-jnp.abs(fg)))
    lf_ref[...] = jnp.where(valid, lf, 0.0)

    def block(src):
        return jnp.dot(xh, w_ref[:, src * PROJ_BLOCK:(src + 1) * PROJ_BLOCK], preferred_element_type=F32)

    for dst, src in enumerate(F32_SOURCE_BLOCKS):
        pf_ref[:, dst * PROJ_BLOCK:(dst + 1) * PROJ_BLOCK] = block(src)
    pb_ref[...] = block(V_SOURCE_BLOCK).astype(BF16)


def _inproj(x2d, g, w_all, wg, b_pad):
    n = x2d.shape[0]
    tm = min(n, 256)
    kern = functools.partial(_inproj_kernel, tm=tm, rows=32)
    const = lambda shape: pl.BlockSpec(shape, lambda i: (0, 0))
    return pl.pallas_call(
        kern,
        grid=(n // tm,),
        in_specs=[
            pl.BlockSpec((tm, D_MODEL), lambda i: (i, 0)),
            const((1, D_MODEL)),
            pl.BlockSpec(w_all.shape, lambda i: (0, 0), pipeline_mode=pl.Buffered(1)),
            const((D_MODEL, LANES)),
            const((1, LANES)),
        ],
        out_specs=[
            pl.BlockSpec((tm, len(F32_SOURCE_BLOCKS) * PROJ_BLOCK), lambda i: (i, 0)),
            pl.BlockSpec((tm, PROJ_BLOCK), lambda i: (i, 0)),
            pl.BlockSpec((tm, LANES), lambda i: (i, 0)),
            pl.BlockSpec((tm, LANES), lambda i: (i, 0)),
        ],
        out_shape=[
            jax.ShapeDtypeStruct((n, len(F32_SOURCE_BLOCKS) * PROJ_BLOCK), F32),
            jax.ShapeDtypeStruct((n, PROJ_BLOCK), BF16),
            jax.ShapeDtypeStruct((n, LANES), F32),
            jax.ShapeDtypeStruct((n, LANES), F32),
        ],
        scratch_shapes=[pltpu.VMEM((tm, D_MODEL), BF16)],
        compiler_params=_cparams(("arbitrary",)),
        name="inproj",
    )(x2d, g, w_all, wg, b_pad)


def _cummax_rows(x, length):
    row = lax.broadcasted_iota(jnp.int32, x.shape, 0)
    d = 1
    while d < length:
        shifted = pltpu.roll(x, d, axis=0)
        x = jnp.maximum(x, jnp.where(row >= d, shifted, -jnp.inf))
        d *= 2
    return x


def _pad_rows(x, length):
    if length == LANES:
        return x
    return jnp.concatenate([x, jnp.zeros((LANES - length, x.shape[1]), x.dtype)], axis=0)


def _mixer_kernel(u_ref, gc_ref, gb_ref, q_ref, k_ref, v_ref, o_ref, li_ref, lf_ref,
                  cw_ref, hg_ref, conv0_ref, c0_ref, n0_ref, m0_ref,
                  mix_ref, convn_ref, cn_ref, nn_ref, mn_ref,
                  *scratch, tb, L, G):
    t = pl.program_id(1)
    nt = pl.num_programs(1)
    n_state = G * N_PAIRS
    c_sc = [scratch[g * N_PAIRS:(g + 1) * N_PAIRS] for g in range(G)]
    n_sc = [scratch[n_state + g * N_PAIRS:n_state + (g + 1) * N_PAIRS] for g in range(G)]
    m_sc = scratch[2 * n_state:2 * n_state + G]
    carry_sc = [scratch[2 * n_state + G + g * CONV_GROUPS:2 * n_state + G + (g + 1) * CONV_GROUPS]
                for g in range(G)]

    @pl.when(t == 0)
    def _():
        zero = jnp.zeros((DQK, DV), F32)
        for g in range(G):
            for p in range(N_PAIRS):
                top = jnp.concatenate([c0_ref[g, 2 * p], zero], axis=1)
                bot = jnp.concatenate([zero, c0_ref[g, 2 * p + 1]], axis=1)
                c_sc[g][p][...] = jnp.concatenate([top, bot], axis=0)
                n_sc[g][p][...] = n0_ref[g, p:p + 1, :]
            m_sc[g][...] = m0_ref[g]
            for cg in range(CONV_GROUPS):
                cs = slice(cg * CONV_GROUP_COLS, (cg + 1) * CONV_GROUP_COLS)
                carry_sc[g][cg][...] = jnp.zeros((8, CONV_GROUP_COLS), F32)
                carry_sc[g][cg][6:8, :] = conv0_ref[g, :, cs]

    row = lax.broadcasted_iota(jnp.int32, (L, L), 0)
    col = lax.broadcasted_iota(jnp.int32, (L, L), 1)
    causal = col <= row
    tril = jnp.where(causal, 1.0, 0.0).astype(BF16)
    lane_l = lax.broadcasted_iota(jnp.int32, (L, LANES), 1)
    low_l = lane_l < DQK
    krow = lax.broadcasted_iota(jnp.int32, (LANES, 2 * DV), 0)
    lane1 = lax.broadcasted_iota(jnp.int32, (1, LANES), 1)

    def chunk(c, carry):
        rows = pl.ds(pl.multiple_of(c * L, L), L)
        units = [(g, p) for g in range(G) for p in range(N_PAIRS)]
        heads = [(g, h) for g in range(G) for h in range(N_HEADS)]
        convs = [conv_chunk(g, rows) for g in range(G)]
        gates = [gate_algebra(g, rows) for g in range(G)]
        outs, states = {}, {}
        for g, p in units:
            pr = pair_scores(g, p, rows)
            v2 = v_ref[g, rows, p * 2 * DV:(p + 1) * 2 * DV]
            for hh in range(2):
                h = 2 * p + hh
                s = intra_weights(gates[g], pr, h)
                intra = jnp.dot(s.astype(BF16), v2[:, hh * DV:(hh + 1) * DV], preferred_element_type=F32)
                o_sig = 1.0 / (1.0 + jnp.exp(-o_ref[g, rows, h * DV:(h + 1) * DV]))
                outs[(g, h)] = head_output(gates[g], pr, s, intra, o_sig, h)
            states[(g, p)] = state_update(gates[g], pr, v2, p)
        for g in range(G):
            for cg in range(CONV_GROUPS):
                y_conv, z_tail = convs[g][cg]
                mix_ref[g, rows, cg * CONV_GROUP_COLS:(cg + 1) * CONV_GROUP_COLS] = y_conv
                carry_sc[g][cg][6:8, :] = z_tail
            m_sc[g][...] = gates[g]["m_new"]
        for g, h in heads:
            mix_ref[g, rows, CONV_DIM + h * DV:CONV_DIM + (h + 1) * DV] = outs[(g, h)]
        for g, p in units:
            c_sc[g][p][...], n_sc[g][p][...] = states[(g, p)]
        return carry

    def conv_chunk(g, rows):
        cw = CONV_GROUP_COLS
        res = []
        for cg in range(CONV_GROUPS):
            cs = slice(cg * cw, (cg + 1) * cw)
            z = gc_ref[g, rows, cs] * u_ref[g, rows, cs]
            prev = carry_sc[g][cg][...]
            p1 = prev[7:8, :]
            p2 = prev[6:7, :]
            rw = lax.broadcasted_iota(jnp.int32, (L, cw), 0)
            z1 = jnp.where(rw >= 1, pltpu.roll(z, 1, axis=0), p1)
            z2 = jnp.where(rw >= 2, pltpu.roll(z, 2, axis=0), jnp.where(rw == 1, p1, p2))
            y = z2 * cw_ref[0:1, cs] + z1 * cw_ref[1:2, cs] + z * cw_ref[2:3, cs]
            res.append(((gb_ref[g, rows, cs] * y).astype(BF16), z[L - 2:L, :]))
        return res

    def gate_algebra(g, rows):
        li = li_ref[g, rows, :]
        lf = lf_ref[g, rows, :]
        hi, mid, lo = _split3(lf)
        F = (jnp.dot(tril, hi, preferred_element_type=F32)
             + jnp.dot(tril, mid, preferred_element_type=F32)
             + jnp.dot(tril, lo, preferred_element_type=F32))
        r = li - F
        cm = _cummax_rows(r, L)
        mprev = m_sc[g][...]
        mx = jnp.maximum(mprev, cm)
        M = F + mx
        neg_mx = -mx
        w_inter = jnp.exp(mprev - mx)
        em = jnp.exp(-M)
        gs = jnp.exp(r - mx[L - 1:L, :])
        g_inter = w_inter[L - 1:L, :]
        rT = _pad_rows(r, L).T[:, 0:L]
        return dict(neg_mx=neg_mx, w_inter=w_inter, em=em, gs=gs, g_inter=g_inter, rT=rT, m_new=M[L - 1:L, :])

    def pair_scores(g, p, rows):
        ps = slice(p * LANES, (p + 1) * LANES)
        q2 = q_ref[g, rows, ps].astype(BF16)
        kf = k_ref[g, rows, ps] * DQK ** -0.5
        k2 = kf.astype(BF16)
        q_e = jnp.where(low_l, q2, jnp.zeros_like(q2))
        q_o = jnp.where(low_l, jnp.zeros_like(q2), q2)
        q_st = jnp.concatenate([q_e, q_o], axis=0)
        n_row = n_sc[g][p][...]
        n_b = jnp.broadcast_to(n_row, (LANES, LANES)).astype(BF16)
        k_aug = jnp.concatenate([n_b, k2], axis=0)
        sn = lax.dot_general(q_st, k_aug, (((1,), (1,)), ((), ())),
                             preferred_element_type=F32)
        c_full = c_sc[g][p][...]
        qc = jnp.dot(q_st, c_full.astype(BF16), preferred_element_type=F32)
        return dict(kf=kf, k2=k2, n_row=n_row, c_full=c_full, sn=sn, qc=qc)

    def intra_weights(gt, pr, h):
        rs = slice((h % 2) * L, (h % 2 + 1) * L)
        dmat = gt["neg_mx"][:, h:h + 1] + gt["rT"][h:h + 1, :]
        return pr["sn"][rs, LANES:LANES + L] * jnp.exp(jnp.where(causal, dmat, -jnp.inf))

    def head_output(gt, pr, s, intra, o_sig, h):
        hh = h % 2
        rs = slice(hh * L, (hh + 1) * L)
        hs = slice(h * DV, (h + 1) * DV)
        wi = gt["w_inter"][:, h:h + 1]
        num = wi * pr["qc"][rs, hh * DV:(hh + 1) * DV] + intra
        den = wi * pr["sn"][rs, 0:LANES] + jnp.sum(s, axis=-1, keepdims=True)
        hv = num / jnp.maximum(jnp.abs(den), gt["em"][:, h:h + 1])
        ms = jnp.mean(hv * hv, axis=-1, keepdims=True)
        hn = hv * lax.rsqrt(ms + EPS) * hg_ref[0:1, hs]
        return (hn * o_sig).astype(BF16)

    def state_update(gt, pr, v2, p):
        gs = gt["gs"]
        kgw = jnp.where(low_l, gs[:, 2 * p:2 * p + 1], gs[:, 2 * p + 1:2 * p + 2])
        kg = pr["kf"] * kgw
        upd = jnp.dot(_pad_rows(kg, L).T.astype(BF16), _pad_rows(v2, L),
                      preferred_element_type=F32)
        ge = gt["g_inter"][0:1, 2 * p:2 * p + 1]
        go = gt["g_inter"][0:1, 2 * p + 1:2 * p + 2]
        c_new = jnp.where(krow < DQK, ge, go) * pr["c_full"] + upd
        kn = pr["k2"].astype(F32) * kgw.astype(BF16).astype(F32)
        n_new = jnp.where(lane1 < DQK, ge, go) * pr["n_row"] + jnp.sum(kn, axis=0, keepdims=True)
        return c_new, n_new

    lax.fori_loop(0, tb // L, chunk, 0)

    @pl.when(t == nt - 1)
    def _():
        for g in range(G):
            for cg in range(CONV_GROUPS):
                cs = slice(cg * CONV_GROUP_COLS, (cg + 1) * CONV_GROUP_COLS)
                convn_ref[g, :, cs] = carry_sc[g][cg][6:8, :]
            for p in range(N_PAIRS):
                cf = c_sc[g][p][...]
                cn_ref[g, 2 * p] = cf[0:DQK, 0:DV]
                cn_ref[g, 2 * p + 1] = cf[DQK:2 * DQK, DV:2 * DV]
                nn_ref[g, p:p + 1, :] = n_sc[g][p][...]
            mn_ref[g] = m_sc[g][...]


def _mixer(pf, pb, li, lf, conv_w, head_gain, conv0, c0, n0p, m0p, bsz, seq):
    G = 1
    L = min(seq, CHUNK)
    tb = min(seq, 512)
    nt = seq // tb
    kern = functools.partial(_mixer_kernel, tb=tb, L=L, G=G)
    wide = lambda cb: pl.BlockSpec((G, tb, 1024), lambda b, t: (b, t, cb))
    half = lambda cb: pl.BlockSpec((G, tb, 512), lambda b, t: (b, t, cb))
    gate = pl.BlockSpec((G, tb, LANES), lambda b, t: (b, t, 0))
    st = lambda *shape: pl.BlockSpec((G,) + shape, lambda b, t: (b,) + (0,) * len(shape))
    return pl.pallas_call(
        kern,
        grid=(bsz // G, nt),
        in_specs=[
            wide(0), wide(1), wide(2), half(8), half(9), wide(0), wide(3), gate, gate,
            pl.BlockSpec((CONV_WIDTH, CONV_DIM), lambda b, t: (0, 0)),
            pl.BlockSpec((1, MLSTM_DIM), lambda b, t: (0, 0)),
            st(2, CONV_DIM), st(N_HEADS, DQK, DV), st(N_PAIRS, LANES), st(1, LANES),
        ],
        out_specs=[
            pl.BlockSpec((G, tb, D_MODEL), lambda b, t: (b, t, 0)),
            st(2, CONV_DIM), st(N_HEADS, DQK, DV), st(N_PAIRS, LANES), st(1, LANES),
        ],
        out_shape=[
            jax.ShapeDtypeStruct((bsz, seq, D_MODEL), BF16),
            jax.ShapeDtypeStruct((bsz, 2, CONV_DIM), F32),
            jax.ShapeDtypeStruct((bsz, N_HEADS, DQK, DV), F32),
            jax.ShapeDtypeStruct((bsz, N_PAIRS, LANES), F32),
            jax.ShapeDtypeStruct((bsz, 1, LANES), F32),
        ],
        scratch_shapes=(
            [pltpu.VMEM((LANES, 2 * DV), F32)] * (G * N_PAIRS)
            + [pltpu.VMEM((1, LANES), F32)] * (G * N_PAIRS)
            + [pltpu.VMEM((1, LANES), F32)] * G
            + [pltpu.VMEM((8, CONV_GROUP_COLS), F32)] * (G * CONV_GROUPS)
        ),
        compiler_params=_cparams(("arbitrary", "arbitrary")),
        name="mixer",
    )(pf, pf, pf, pf, pf, pb, pf, li, lf, conv_w, head_gain, conv0, c0, n0p, m0p)


def _outproj_kernel(mix_ref, w_ref, x_ref, g_ref, wr_ref, br_ref,
                    xo_ref, xq_ref, ri_ref, rg_ref, cnt_ref, base_sc, x_even, x_odd, *, tm):
    s = pl.program_id(0)

    @pl.when(s == 0)
    def _():
        base_sc[...] = jnp.zeros(base_sc.shape, F32)
        x_odd[...] = jnp.zeros(x_odd.shape, F32)

    refs = (mix_ref, w_ref, x_ref, g_ref, wr_ref, br_ref, xo_ref, xq_ref, ri_ref, rg_ref, cnt_ref, base_sc)

    @pl.when(lax.rem(s, 2) == 0)
    def _():
        _outproj_step(s, x_even, x_odd, *refs, tm=tm)

    @pl.when(lax.rem(s, 2) == 1)
    def _():
        _outproj_step(s, x_odd, x_even, *refs, tm=tm)


def _outproj_step(s, x_this, x_prev, mix_ref, w_ref, x_ref, g_ref, wr_ref, br_ref,
                  xo_ref, xq_ref, ri_ref, rg_ref, cnt_ref, base_sc, *, tm):
    x_new = x_ref[...] + jnp.dot(mix_ref[...], w_ref[...], preferred_element_type=F32)
    xo_ref[...] = x_new
    x_this[...] = x_new

    counted = jnp.where(s > 0, 1.0, 0.0)
    x = x_prev[...]
    ms = jnp.mean(x * x, axis=-1, keepdims=True)
    xn = x * lax.rsqrt(ms + EPS) * g_ref[...]
    xq_ref[...] = _pack_halves(xn)
    lg = jnp.dot(xn.astype(BF16), wr_ref[...], preferred_element_type=F32) + br_ref[...]

    lane = lax.broadcasted_iota(jnp.int32, (tm, LANES), 1)
    lanef = lane.astype(F32)
    big = jnp.float32(1e9)
    ninf = -jnp.inf
    is_g = (lane >= N_EXPERTS) & (lane < N_EXPERTS + N_GROUPS)
    glog = jnp.where(is_g, lg, ninf)
    gmax = jnp.max(glog, axis=1, keepdims=True)
    gi = jnp.min(jnp.where(glog == gmax, lanef, big), axis=1, keepdims=True) - N_EXPERTS
    pgi = 1.0 / jnp.sum(jnp.where(is_g, jnp.exp(lg - gmax), 0.0), axis=1, keepdims=True)
    lo = gi * EXPERTS_PER_GROUP
    in_grp = (lanef >= lo) & (lanef < lo + EXPERTS_PER_GROUP)
    el = jnp.where(in_grp, lg, ninf)
    v1 = jnp.max(el, axis=1, keepdims=True)
    i1 = jnp.min(jnp.where(el == v1, lanef, big), axis=1, keepdims=True)
    el2 = jnp.where(lanef == i1, ninf, el)
    v2 = jnp.max(el2, axis=1, keepdims=True)
    i2 = jnp.min(jnp.where(el2 == v2, lanef, big), axis=1, keepdims=True)
    e21 = jnp.exp(v2 - v1)
    g1 = pgi / (1.0 + e21)
    g2 = pgi * e21 / (1.0 + e21)
    sel1 = lanef == i1
    sel2 = lanef == i2
    oh = jnp.where(sel1 | sel2, counted, 0.0)
    r_i = lax.broadcasted_iota(jnp.int32, (tm, tm), 0)
    c_i = lax.broadcasted_iota(jnp.int32, (tm, tm), 1)
    stril = jnp.where(c_i < r_i, 1.0, 0.0).astype(BF16)
    tot = jnp.dot(stril, oh.astype(BF16), preferred_element_type=F32) + base_sc[...]
    r1 = jnp.sum(jnp.where(sel1, tot, 0.0), axis=1, keepdims=True)
    r2 = jnp.sum(jnp.where(sel2, tot, 0.0), axis=1, keepdims=True)
    base = base_sc[...] + jnp.sum(oh, axis=0, keepdims=True)
    base_sc[...] = base
    cnt_ref[...] = base
    ri = jnp.where(lane == 0, i1, jnp.where(lane == 1, i2, jnp.where(lane == 2, r1,
                   jnp.where(lane == 3, r2, 0.0))))
    ri_t = jnp.concatenate([ri[c * LANES:(c + 1) * LANES, :].T[0:8, :] for c in range(tm // LANES)], axis=1)
    ri_ref[...] = ri_t.astype(jnp.int32)
    rg_ref[...] = jnp.where(lane == 0, g1, jnp.where(lane == 1, g2, 0.0))


def _outproj(mix, w_out, x2d, g, wr, br):
    n = x2d.shape[0]
    tm = min(n, TOKEN_TILE)
    assert tm % LANES == 0
    nt = n // tm
    kern = functools.partial(_outproj_kernel, tm=tm)
    cur = lambda s: jnp.minimum(s, nt - 1)
    prev = lambda s: jnp.maximum(s - 1, 0)
    const = lambda r, c, **kw: pl.BlockSpec((r, c), lambda s: (0, 0), **kw)
    return pl.pallas_call(
        kern,
        grid=(nt + 1,),
        in_specs=[pl.BlockSpec((tm, D_MODEL), lambda s: (cur(s), 0)),
                  const(D_MODEL, D_MODEL, pipeline_mode=pl.Buffered(1)),
                  pl.BlockSpec((tm, D_MODEL), lambda s: (cur(s), 0)), const(1, D_MODEL),
                  const(D_MODEL, LANES), const(1, LANES)],
        out_specs=[pl.BlockSpec((tm, D_MODEL), lambda s: (cur(s), 0)),
                   pl.BlockSpec((tm, D_MODEL // 2), lambda s: (prev(s), 0)),
                   pl.BlockSpec((None, 8, tm), lambda s: (prev(s), 0, 0)),
                   pl.BlockSpec((tm, LANES), lambda s: (prev(s), 0)), const(1, LANES)],
        out_shape=[
            jax.ShapeDtypeStruct((n, D_MODEL), F32),
            jax.ShapeDtypeStruct((n, D_MODEL // 2), jnp.uint32),
            jax.ShapeDtypeStruct((n // tm, 8, tm), jnp.int32),
            jax.ShapeDtypeStruct((n, LANES), F32),
            jax.ShapeDtypeStruct((1, LANES), F32),
        ],
        scratch_shapes=[pltpu.VMEM((1, LANES), F32), pltpu.VMEM((tm, D_MODEL), F32),
                        pltpu.VMEM((tm, D_MODEL), F32)],
        compiler_params=_cparams(("arbitrary",)),
        name="outproj",
    )(mix, w_out, x2d, g, wr, br)


def _row_copy(src_ref, src_row, dst_ref, dst_row, sem):
    return pltpu.make_async_copy(src_ref.at[pl.ds(src_row, 1), :], dst_ref.at[pl.ds(dst_row, 1), :], sem)


def _dispatch_kernel(pad_end_ref, padded_ref, nused_ref, da_ref, db_ref, xa_ref, xb_ref, xs_ref, zero_sc, sem_z,
                     sem, *, bm, nb, tiles_a):
    i = pl.program_id(0)

    @pl.when(i == 0)
    def _():
        zero_sc[...] = jnp.zeros(zero_sc.shape, jnp.uint32)

        def zcopy(start):
            return pltpu.make_async_copy(zero_sc, xs_ref.at[pl.ds(pl.multiple_of(start, bm), bm), :], sem_z)

        def zstart(e, c):
            @pl.when(padded_ref[e] > 0)
            def _():
                zcopy(pad_end_ref[e] - bm).start()
            return c

        def zwait(e, c):
            @pl.when(padded_ref[e] > 0)
            def _():
                zcopy(pad_end_ref[e] - bm).wait()
            return c

        def tstart(b, c):
            zcopy(b * bm).start()
            return c

        def twait(b, c):
            zcopy(b * bm).wait()
            return c

        lax.fori_loop(0, N_EXPERTS, zstart, 0)
        lax.fori_loop(nused_ref[0], nb, tstart, 0)
        lax.fori_loop(0, N_EXPERTS, zwait, 0)
        lax.fori_loop(nused_ref[0], nb, twait, 0)

    def scatter_tile(src_ref, dest_ref):
        tm = src_ref.shape[0]

        def start(j, c):
            _row_copy(src_ref, j, xs_ref, dest_ref[0, 0, j], sem).start()
            _row_copy(src_ref, j, xs_ref, dest_ref[0, 0, tm + j], sem).start()
            return c

        lax.fori_loop(0, tm, start, 0, unroll=8)
        for _ in range(TOP_K):
            pltpu.make_async_copy(src_ref, xs_ref.at[pl.ds(0, tm), :], sem).wait()

    @pl.when(i < tiles_a)
    def _():
        scatter_tile(xa_ref, da_ref)

    @pl.when(i >= tiles_a)
    def _():
        scatter_tile(xb_ref, db_ref)


def _dispatch(xq_a, xq_b, dest_a, dest_b, pad_end, padded, nused, p_rows, bm):
    tm_a, tm_b = dest_a.shape[2] // 2, dest_b.shape[2] // 2
    tiles_a, tiles_b = dest_a.shape[0], dest_b.shape[0]
    kern = functools.partial(_dispatch_kernel, bm=bm, nb=p_rows // bm, tiles_a=tiles_a)
    in_a = lambda i, pe, pd, nu: jnp.minimum(i, tiles_a - 1)
    in_b = lambda i, pe, pd, nu: jnp.maximum(i - tiles_a, 0)
    return pl.pallas_call(
        kern,
        grid_spec=pltpu.PrefetchScalarGridSpec(
            num_scalar_prefetch=3,
            grid=(tiles_a + tiles_b,),
            in_specs=[
                pl.BlockSpec((1, 1, 2 * tm_a), lambda *a: (in_a(*a), 0, 0), memory_space=pltpu.SMEM),
                pl.BlockSpec((1, 1, 2 * tm_b), lambda *a: (in_b(*a), 0, 0), memory_space=pltpu.SMEM),
                pl.BlockSpec((tm_a, ROW_WORDS), lambda *a: (in_a(*a), 0)),
                pl.BlockSpec((tm_b, ROW_WORDS), lambda *a: (in_b(*a), 0)),
            ],
            out_specs=pl.BlockSpec(memory_space=pl.ANY),
            scratch_shapes=[pltpu.VMEM((bm, ROW_WORDS), jnp.uint32), pltpu.SemaphoreType.DMA(()),
                            pltpu.SemaphoreType.DMA(())],
        ),
        out_shape=jax.ShapeDtypeStruct((p_rows, ROW_WORDS), jnp.uint32),
        compiler_params=_cparams(("arbitrary",)),
        name="dispatch",
    )(pad_end, padded, nused, dest_a, dest_b, xq_a, xq_b)


def _experts_kernel(blk_e_ref, nused_ref, xs_ref, w1_ref, w3_ref, w2_ref, ys_ref, w1b, w3b, w2b):
    i = pl.program_id(0)

    @pl.when((i == 0) | (blk_e_ref[i] != blk_e_ref[jnp.maximum(i - 1, 0)]))
    def _():
        def cast_in(r, c):
            sl = pl.ds(pl.multiple_of(r * 256, 256), 256)
            w1b[sl, :] = w1_ref[sl, :].astype(BF16)
            w3b[sl, :] = w3_ref[sl, :].astype(BF16)
            return c

        def cast_out(r, c):
            sl = pl.ds(pl.multiple_of(r * 64, 64), 64)
            w2b[sl, :] = w2_ref[sl, :].astype(BF16)
            return c

        lax.fori_loop(0, D_MODEL // 256, cast_in, 0)
        lax.fori_loop(0, D_EXPERT // 64, cast_out, 0)

    @pl.when(i < nused_ref[0])
    def _():
        lo, hi = _unpack_halves(xs_ref[...])
        lo = lo.astype(BF16)
        hi = hi.astype(BF16)
        h1 = (jnp.dot(lo, w1b[0:ROW_WORDS, :], preferred_element_type=F32)
              + jnp.dot(hi, w1b[ROW_WORDS:D_MODEL, :], preferred_element_type=F32))
        h3 = (jnp.dot(lo, w3b[0:ROW_WORDS, :], preferred_element_type=F32)
              + jnp.dot(hi, w3b[ROW_WORDS:D_MODEL, :], preferred_element_type=F32))
        hb = (h1 * (1.0 / (1.0 + jnp.exp(-h1)))) * h3
        ys_ref[...] = _pack_halves(jnp.dot(hb.astype(BF16), w2b[...], preferred_element_type=F32))

    @pl.when(i >= nused_ref[0])
    def _():
        ys_ref[...] = jnp.zeros(ys_ref.shape, jnp.uint32)


def _experts(xs, blk_e, nused, w1, w3, w2, layer, bm):
    p_rows = xs.shape[0]
    nb = p_rows // bm
    rowmap = lambda i, be, nu: (jnp.minimum(i, nu[0] - 1), 0)
    wmap = lambda i, be, nu: (layer, be[i], 0, 0)
    return pl.pallas_call(
        _experts_kernel,
        grid_spec=pltpu.PrefetchScalarGridSpec(
            num_scalar_prefetch=2,
            grid=(nb,),
            in_specs=[
                pl.BlockSpec((bm, ROW_WORDS), rowmap),
                pl.BlockSpec((None, None, D_MODEL, D_EXPERT), wmap),
                pl.BlockSpec((None, None, D_MODEL, D_EXPERT), wmap),
                pl.BlockSpec((None, None, D_EXPERT, D_MODEL), wmap),
            ],
            out_specs=pl.BlockSpec((bm, ROW_WORDS), lambda i, be, nu: (i, 0)),
            scratch_shapes=[pltpu.VMEM((D_MODEL, D_EXPERT), BF16), pltpu.VMEM((D_MODEL, D_EXPERT), BF16),
                            pltpu.VMEM((D_EXPERT, D_MODEL), BF16)],
        ),
        out_shape=jax.ShapeDtypeStruct((p_rows, ROW_WORDS), jnp.uint32),
        compiler_params=_cparams(("arbitrary",)),
        name="experts",
    )(blk_e, nused, xs, w1, w3, w2)


def _combine_kernel(dest_ref, dnext_ref, x_ref, rg_ref, g_ref, ys_ref, out_ref, ybuf, sem, *, tm, rows, final):
    i = pl.program_id(0)
    nt = pl.num_programs(0)
    slot = lax.rem(i, 2)
    other = 1 - slot

    def gather_rows(d_ref, base, buf_slot):
        for jj in range(rows):
            j = base + jj
            _row_copy(ys_ref, d_ref[0, 0, j], ybuf.at[buf_slot, 0], j, sem.at[buf_slot]).start()
            _row_copy(ys_ref, d_ref[0, 0, tm + j], ybuf.at[buf_slot, 1], j, sem.at[buf_slot]).start()

    @pl.when(i == 0)
    def _():
        def first(r, c):
            gather_rows(dest_ref, pl.multiple_of(r * rows, rows), 0)
            return c

        lax.fori_loop(0, tm // rows, first, 0)

    for k in range(TOP_K):
        pltpu.make_async_copy(ys_ref.at[pl.ds(0, tm), :], ybuf.at[slot, k], sem.at[slot]).wait()

    def combine_rows(base):
        sl = pl.ds(base, rows)
        rg = rg_ref[sl, :]
        g1 = rg[:, 0:1]
        g2 = rg[:, 1:2]
        lo1, hi1 = _unpack_halves(ybuf[slot, 0, sl, :])
        lo2, hi2 = _unpack_halves(ybuf[slot, 1, sl, :])
        xa = x_ref[sl, 0:ROW_WORDS] + (g1 * lo1 + g2 * lo2)
        xb = x_ref[sl, ROW_WORDS:D_MODEL] + (g1 * hi1 + g2 * hi2)
        if final:
            ss = jnp.sum(xa * xa, axis=-1, keepdims=True) + jnp.sum(xb * xb, axis=-1, keepdims=True)
            sc = lax.rsqrt(ss / D_MODEL + EPS)
            xa = xa * sc * g_ref[:, 0:ROW_WORDS]
            xb = xb * sc * g_ref[:, ROW_WORDS:D_MODEL]
        out_ref[sl, 0:ROW_WORDS] = xa
        out_ref[sl, ROW_WORDS:D_MODEL] = xb

    @pl.when(i + 1 < nt)
    def _():
        def body(r, c):
            base = pl.multiple_of(r * rows, rows)
            gather_rows(dnext_ref, base, other)
            combine_rows(base)
            return c

        lax.fori_loop(0, tm // rows, body, 0, unroll=2)

    @pl.when(i + 1 == nt)
    def _():
        def body(r, c):
            combine_rows(pl.multiple_of(r * rows, rows))
            return c

        lax.fori_loop(0, tm // rows, body, 0, unroll=2)


def _combine(x_new, ys, dest3, rg, g_final, final):
    n = x_new.shape[0]
    tm = dest3.shape[2] // 2
    nt = n // tm
    kern = functools.partial(_combine_kernel, tm=tm, rows=32, final=final)
    return pl.pallas_call(
        kern,
        grid=(nt,),
        in_specs=[
            pl.BlockSpec((1, 1, 2 * tm), lambda i: (i, 0, 0), memory_space=pltpu.SMEM),
            pl.BlockSpec((1, 1, 2 * tm), lambda i: (jnp.minimum(i + 1, nt - 1), 0, 0), memory_space=pltpu.SMEM),
            pl.BlockSpec((tm, D_MODEL), lambda i: (i, 0)),
            pl.BlockSpec((tm, LANES), lambda i: (i, 0)),
            pl.BlockSpec((1, D_MODEL), lambda i: (0, 0)),
            pl.BlockSpec(memory_space=pl.ANY),
        ],
        out_specs=pl.BlockSpec((tm, D_MODEL), lambda i: (i, 0)),
        out_shape=jax.ShapeDtypeStruct((n, D_MODEL), F32),
        scratch_shapes=[pltpu.VMEM((2, TOP_K, tm, ROW_WORDS), jnp.uint32), pltpu.SemaphoreType.DMA((2,))],
        compiler_params=_cparams(("arbitrary",)),
        name="combine",
    )(dest3, dest3, x_new, rg, g_final, ys)


def _dest_kernel(first_ref, ri_ref, dest_ref, *, tiles, tm):
    for t in range(tiles):
        e1, e2 = ri_ref[t, 0:1, :], ri_ref[t, 1:2, :]
        s1 = jnp.zeros_like(e1)
        s2 = jnp.zeros_like(e2)
        for e in range(N_EXPERTS):
            s1 = jnp.where(e1 == e, first_ref[e], s1)
            s2 = jnp.where(e2 == e, first_ref[e], s2)
        dest_ref[t, :, 0:tm] = s1 + ri_ref[t, 2:3, :]
        dest_ref[t, :, tm:2 * tm] = s2 + ri_ref[t, 3:4, :]


def _dest_rows(first, ri_t):
    n_tiles, _, tm = ri_t.shape
    tiles = min(n_tiles, 16)
    return pl.pallas_call(
        functools.partial(_dest_kernel, tiles=tiles, tm=tm),
        grid_spec=pltpu.PrefetchScalarGridSpec(
            num_scalar_prefetch=1,
            grid=(n_tiles // tiles,),
            in_specs=[pl.BlockSpec((tiles, 8, tm), lambda i, f: (i, 0, 0))],
            out_specs=pl.BlockSpec((tiles, 1, 2 * tm), lambda i, f: (i, 0, 0)),
        ),
        out_shape=jax.ShapeDtypeStruct((n_tiles, 1, 2 * tm), jnp.int32),
        compiler_params=_cparams(("arbitrary",)),
        name="dest_rows",
    )(first, ri_t)


def _route_tables(ri_ts, counts, n_total, bm):
    cnts = [c[0, :N_EXPERTS].astype(jnp.int32) for c in counts]
    cnt = sum(cnts)
    padded = (cnt + bm - 1) // bm * bm
    pad_end = jnp.cumsum(padded)
    pad_start = pad_end - padded
    dests, first = [], pad_start
    for ri_t, c in zip(ri_ts, cnts):
        dests.append(_dest_rows(first.astype(jnp.int32), ri_t))
        first = first + c
    nb = -(-(n_total * TOP_K) // bm) + N_EXPERTS
    nused = (pad_end[-1] // bm).astype(jnp.int32)
    blk = jnp.minimum(jnp.arange(nb, dtype=jnp.int32), nused - 1) * bm
    blk_e = jnp.sum((pad_end[None, :] <= blk[:, None]).astype(jnp.int32), axis=1)
    blk_e = jnp.minimum(blk_e, N_EXPERTS - 1)
    return dests, pad_end.astype(jnp.int32), padded.astype(jnp.int32), blk_e, nused.reshape(1), nb * bm


def _mix_and_route(x2d, bsz, seq, conv0, c0, n0, m0, wts):
    n = bsz * seq
    pf, pb, li, lf = _inproj(x2d, wts["g_mix"], wts["w_main"], wts["wg"], wts["b_if"])
    n0p = n0.reshape(bsz, N_PAIRS, LANES)
    m0p = jnp.pad(m0, ((0, 0), (0, LANES - N_HEADS))).reshape(bsz, 1, LANES)
    by_row = lambda a: a.reshape(bsz, seq, a.shape[-1])
    mix, conv_n, c_n, n_n, m_n = _mixer(by_row(pf), by_row(pb), by_row(li), by_row(lf), wts["conv_w"],
                                         wts["head_gain"], conv0, c0, n0p, m0p, bsz, seq)
    x_new, xq, ri_t, rg, counts = _outproj(mix.reshape(n, D_MODEL), wts["w_out"], x2d, wts["g_ffn"],
                                           wts["wr"], wts["br"])
    states = (conv_n, c_n, n_n.reshape(bsz, N_HEADS, DQK), m_n[:, 0, :N_HEADS])
    return dict(x_new=x_new, xq=xq, ri_t=ri_t, rg=rg, counts=counts, states=states)


def _moe(groups, wts, g_final, final):
    bm = EXPERT_BLOCK
    n_total = sum(g["x_new"].shape[0] for g in groups)
    dests, pad_end, padded, blk_e, nused, p_rows = _route_tables(
        [g["ri_t"] for g in groups], [g["counts"] for g in groups], n_total, bm)
    xs = _dispatch(groups[0]["xq"], groups[1]["xq"], dests[0], dests[1], pad_end, padded, nused, p_rows, bm)
    ys = _experts(xs, blk_e, nused, wts["w1"], wts["w3"], wts["w2"], wts["layer"], bm)
    return [_combine(g["x_new"], ys, d, g["rg"], g_final, final) for g, d in zip(groups, dests)]


def _prep_weights(l, norm_mix, w_in, b_if, conv_w, head_gain, w_out, norm_ffn, w_router_group,
                  b_router_group, w_router_expert, b_router_expert, w1, w3, w2):
    wi = w_in[l]
    w_main = wi.astype(BF16)
    wg = jnp.pad(wi[:, MAIN_COLS:], ((0, 0), (0, LANES - 2 * N_HEADS))).astype(BF16)
    wr = jnp.pad(jnp.concatenate([w_router_expert[l], w_router_group[l]], axis=1),
                 ((0, 0), (0, LANES - N_EXPERTS - N_GROUPS))).astype(BF16)
    br = jnp.pad(jnp.concatenate([b_router_expert[l], b_router_group[l]]),
                 (0, LANES - N_EXPERTS - N_GROUPS)).reshape(1, LANES)
    return dict(
        g_mix=norm_mix[l].reshape(1, D_MODEL),
        w_main=w_main, wg=wg,
        b_if=jnp.pad(b_if[l], (0, LANES - 2 * N_HEADS)).reshape(1, LANES),
        conv_w=conv_w[l], head_gain=head_gain[l].reshape(1, MLSTM_DIM),
        w_out=w_out[l].astype(BF16),
        g_ffn=norm_ffn[l].reshape(1, D_MODEL),
        wr=wr, br=br,
        w1=w1, w3=w3, w2=w2, layer=l,
    )


def _trunks(xs_in, states_in, wts, g_final):
    depth = len(wts)
    shapes = [x.shape for x in xs_in]
    x2ds = [x.reshape(x.shape[0] * x.shape[1], D_MODEL) for x in xs_in]
    new_states = [[] for _ in xs_in]
    for l in range(depth):
        groups = []
        for gi, (x2d, shp, st) in enumerate(zip(x2ds, shapes, states_in)):
            grp = _mix_and_route(x2d, shp[0], shp[1], st[0][l], st[1][l], st[2][l], st[3][l], wts[l])
            new_states[gi].append(grp["states"])
            groups.append(grp)
        x2ds = _moe(groups, wts[l], g_final, l == depth - 1)
    outs = []
    for x2d, shp, sts in zip(x2ds, shapes, new_states):
        outs.append((x2d.reshape(shp),) + tuple(jnp.stack([s[k] for s in sts]) for k in range(4)))
    return outs


def kernel(x_prompt, x_sample, state_conv, state_mlstm_C, state_mlstm_n, state_mlstm_m,
           norm_mix, w_in, b_if, conv_w, head_gain, w_out, norm_ffn,
           w_router_group, b_router_group, w_router_expert, b_router_expert,
           w1, w3, w2, norm_final):
    depth = w_in.shape[0]
    wts = [_prep_weights(l, norm_mix, w_in, b_if, conv_w, head_gain, w_out, norm_ffn, w_router_group,
                         b_router_group, w_router_expert, b_router_expert, w1, w3, w2)
           for l in range(depth)]
    g_final = norm_final.reshape(1, D_MODEL)
    b = x_prompt.shape[0]
    conv0 = jnp.zeros((depth, b, CONV_WIDTH - 1, CONV_DIM), F32)
    c0 = jnp.zeros((depth, b, N_HEADS, DQK, DV), F32)
    n0 = jnp.zeros((depth, b, N_HEADS, DQK), F32)
    m0 = jnp.full((depth, b, N_HEADS), M_INIT, F32)
    (y_p, conv_p, c_p, n_p, m_p), (y_s, conv_s, c_s, n_s, m_s) = _trunks(
        [x_prompt, x_sample],
        [(conv0, c0, n0, m0), (state_conv, state_mlstm_C, state_mlstm_n, state_mlstm_m)],
        wts, g_final)
    return (y_p, y_s, conv_p, c_p, n_p, m_p, conv_s, c_s, n_s, m_s)
```

```python
import functools

import jax
import jax.numpy as jnp
from jax import lax
from jax.experimental import pallas as pl
from jax.experimental.pallas import tpu as pltpu

F32 = jnp.float32
BF16 = jnp.bfloat16

D_MODEL = 2048
CONV_DIM = 1024
N_HEADS = 8
N_PAIRS = N_HEADS // 2
DV = 128
DQK = 64
QK_DIM = N_HEADS * DQK
MLSTM_DIM = N_HEADS * DV
N_GROUPS = 4
EXPERTS_PER_GROUP = 8
N_EXPERTS = 32
TOP_K = 2
D_EXPERT = 512
EPS = 1e-6
M_INIT = -1e30
CONV_WIDTH = 3
LANES = 128
MAIN_COLS = 3 * CONV_DIM + 2 * QK_DIM + 2 * MLSTM_DIM
ROW_WORDS = D_MODEL // 2
PROJ_BLOCK = 1024
F32_SOURCE_BLOCKS = (0, 1, 2, 5, 3)
V_SOURCE_BLOCK = 4
CHUNK = 64
EXPERT_BLOCK = 512
TOKEN_TILE = 512
ROW_TILE = 1024
CONV_GROUP_COLS = 256
CONV_GROUPS = CONV_DIM // CONV_GROUP_COLS
VMEM_LIMIT = 56 * 1024 * 1024


def _cparams(sem):
    return pltpu.CompilerParams(dimension_semantics=sem, vmem_limit_bytes=VMEM_LIMIT)


def _pack_halves(x):
    half = x.shape[1] // 2
    lo = pltpu.bitcast(x[:, :half].astype(BF16).astype(F32), jnp.uint32)
    hi = pltpu.bitcast(x[:, half:].astype(BF16).astype(F32), jnp.uint32)
    return (lo >> 16) | (hi & jnp.uint32(0xFFFF0000))


def _unpack_halves(w):
    lo = pltpu.bitcast(w << 16, F32)
    hi = pltpu.bitcast(w & jnp.uint32(0xFFFF0000), F32)
    return lo, hi


def _split3(x):
    hi = x.astype(BF16)
    r1 = x - hi.astype(F32)
    mid = r1.astype(BF16)
    lo = (r1 - mid.astype(F32)).astype(BF16)
    return hi, mid, lo


def _inproj_kernel(x_ref, g_ref, w_ref, wg_ref, b_ref, pf_ref, pb_ref, li_ref, lf_ref, xh_ref,
                   *, tm, rows):
    def body(r, c):
        sl = pl.ds(pl.multiple_of(r * rows, rows), rows)
        x = x_ref[sl, :]
        ms = jnp.mean(x * x, axis=-1, keepdims=True)
        xh_ref[sl, :] = (x * lax.rsqrt(ms + EPS) * g_ref[...]).astype(BF16)
        return c

    lax.fori_loop(0, tm // rows, body, 0, unroll=4)
    xh = xh_ref[...]
    gt = jnp.dot(xh, wg_ref[...], preferred_element_type=F32) + b_ref[...]
    lane = lax.broadcasted_iota(jnp.int32, gt.shape, 1)
    valid = lane < N_HEADS
    li_ref[...] = jnp.where(valid, gt, 0.0)
    fg = pltpu.roll(gt, LANES - N_HEADS, axis=1)
    lf = jnp.minimum(fg, 0.0) - jnp.log1p(jnp.exp(-jnp.abs(fg)))
    lf_ref[...] = jnp.where(valid, lf, 0.0)

    def block(src):
        return jnp.dot(xh, w_ref[:, src * PROJ_BLOCK:(src + 1) * PROJ_BLOCK], preferred_element_type=F32)

    for dst, src in enumerate(F32_SOURCE_BLOCKS):
        pf_ref[:, dst * PROJ_BLOCK:(dst + 1) * PROJ_BLOCK] = block(src)
    pb_ref[...] = block(V_SOURCE_BLOCK).astype(BF16)


def _inproj(x2d, g, w_all, wg, b_pad):
    n = x2d.shape[0]
    tm = min(n, 256)
    kern = functools.partial(_inproj_kernel, tm=tm, rows=32)
    const = lambda shape: pl.BlockSpec(shape, lambda i: (0, 0))
    return pl.pallas_call(
        kern,
        grid=(n // tm,),
        in_specs=[
            pl.BlockSpec((tm, D_MODEL), lambda i: (i, 0)),
            const((1, D_MODEL)),
            pl.BlockSpec(w_all.shape, lambda i: (0, 0), pipeline_mode=pl.Buffered(1)),
            const((D_MODEL, LANES)),
            const((1, LANES)),
        ],
        out_specs=[
            pl.BlockSpec((tm, len(F32_SOURCE_BLOCKS) * PROJ_BLOCK), lambda i: (i, 0)),
            pl.BlockSpec((tm, PROJ_BLOCK), lambda i: (i, 0)),
            pl.BlockSpec((tm, LANES), lambda i: (i, 0)),
            pl.BlockSpec((tm, LANES), lambda i: (i, 0)),
        ],
        out_shape=[
            jax.ShapeDtypeStruct((n, len(F32_SOURCE_BLOCKS) * PROJ_BLOCK), F32),
            jax.ShapeDtypeStruct((n, PROJ_BLOCK), BF16),
            jax.ShapeDtypeStruct((n, LANES), F32),
            jax.ShapeDtypeStruct((n, LANES), F32),
        ],
        scratch_shapes=[pltpu.VMEM((tm, D_MODEL), BF16)],
        compiler_params=_cparams(("arbitrary",)),
        name="inproj",
    )(x2d, g, w_all, wg, b_pad)


def _cummax_rows(x, length):
    row = lax.broadcasted_iota(jnp.int32, x.shape, 0)
    d = 1
    while d < length:
        shifted = pltpu.roll(x, d, axis=0)
        x = jnp.maximum(x, jnp.where(row >= d, shifted, -jnp.inf))
        d *= 2
    return x


def _pad_rows(x, length):
    if length == LANES:
        return x
    return jnp.concatenate([x, jnp.zeros((LANES - length, x.shape[1]), x.dtype)], axis=0)


def _mixer_kernel(u_ref, gc_ref, gb_ref, q_ref, k_ref, v_ref, o_ref, li_ref, lf_ref,
                  cw_ref, hg_ref, conv0_ref, c0_ref, n0_ref, m0_ref,
                  mix_ref, convn_ref, cn_ref, nn_ref, mn_ref,
                  *scratch, tb, L, G):
    t = pl.program_id(1)
    nt = pl.num_programs(1)
    n_state = G * N_PAIRS
    c_sc = [scratch[g * N_PAIRS:(g + 1) * N_PAIRS] for g in range(G)]
    n_sc = [scratch[n_state + g * N_PAIRS:n_state + (g + 1) * N_PAIRS] for g in range(G)]
    m_sc = scratch[2 * n_state:2 * n_state + G]
    carry_sc = [scratch[2 * n_state + G + g * CONV_GROUPS:2 * n_state + G + (g + 1) * CONV_GROUPS]
                for g in range(G)]

    @pl.when(t == 0)
    def _():
        zero = jnp.zeros((DQK, DV), F32)
        for g in range(G):
            for p in range(N_PAIRS):
                top = jnp.concatenate([c0_ref[g, 2 * p], zero], axis=1)
                bot = jnp.concatenate([zero, c0_ref[g, 2 * p + 1]], axis=1)
                c_sc[g][p][...] = jnp.concatenate([top, bot], axis=0)
                n_sc[g][p][...] = n0_ref[g, p:p + 1, :]
            m_sc[g][...] = m0_ref[g]
            for cg in range(CONV_GROUPS):
                cs = slice(cg * CONV_GROUP_COLS, (cg + 1) * CONV_GROUP_COLS)
                carry_sc[g][cg][...] = jnp.zeros((8, CONV_GROUP_COLS), F32)
                carry_sc[g][cg][6:8, :] = conv0_ref[g, :, cs]

    row = lax.broadcasted_iota(jnp.int32, (L, L), 0)
    col = lax.broadcasted_iota(jnp.int32, (L, L), 1)
    causal = col <= row
    tril = jnp.where(causal, 1.0, 0.0).astype(BF16)
    lane_l = lax.broadcasted_iota(jnp.int32, (L, LANES), 1)
    low_l = lane_l < DQK
    krow = lax.broadcasted_iota(jnp.int32, (LANES, 2 * DV), 0)
    lane1 = lax.broadcasted_iota(jnp.int32, (1, LANES), 1)

    def chunk(c, carry):
        rows = pl.ds(pl.multiple_of(c * L, L), L)
        units = [(g, p) for g in range(G) for p in range(N_PAIRS)]
        heads = [(g, h) for g in range(G) for h in range(N_HEADS)]
        convs = [conv_chunk(g, rows) for g in range(G)]
        gates = [gate_algebra(g, rows) for g in range(G)]
        outs, states = {}, {}
        for g, p in units:
            pr = pair_scores(g, p, rows)
            v2 = v_ref[g, rows, p * 2 * DV:(p + 1) * 2 * DV]
            for hh in range(2):
                h = 2 * p + hh
                s = intra_weights(gates[g], pr, h)
                intra = jnp.dot(s.astype(BF16), v2[:, hh * DV:(hh + 1) * DV], preferred_element_type=F32)
                o_sig = 1.0 / (1.0 + jnp.exp(-o_ref[g, rows, h * DV:(h + 1) * DV]))
                outs[(g, h)] = head_output(gates[g], pr, s, intra, o_sig, h)
            states[(g, p)] = state_update(gates[g], pr, v2, p)
        for g in range(G):
            for cg in range(CONV_GROUPS):
                y_conv, z_tail = convs[g][cg]
                mix_ref[g, rows, cg * CONV_GROUP_COLS:(cg + 1) * CONV_GROUP_COLS] = y_conv
                carry_sc[g][cg][6:8, :] = z_tail
            m_sc[g][...] = gates[g]["m_new"]
        for g, h in heads:
            mix_ref[g, rows, CONV_DIM + h * DV:CONV_DIM + (h + 1) * DV] = outs[(g, h)]
        for g, p in units:
            c_sc[g][p][...], n_sc[g][p][...] = states[(g, p)]
        return carry

    def conv_chunk(g, rows):
        cw = CONV_GROUP_COLS
        res = []
        for cg in range(CONV_GROUPS):
            cs = slice(cg * cw, (cg + 1) * cw)
            z = gc_ref[g, rows, cs] * u_ref[g, rows, cs]
            prev = carry_sc[g][cg][...]
            p1 = prev[7:8, :]
            p2 = prev[6:7, :]
            rw = lax.broadcasted_iota(jnp.int32, (L, cw), 0)
            z1 = jnp.where(rw >= 1, pltpu.roll(z, 1, axis=0), p1)
            z2 = jnp.where(rw >= 2, pltpu.roll(z, 2, axis=0), jnp.where(rw == 1, p1, p2))
            y = z2 * cw_ref[0:1, cs] + z1 * cw_ref[1:2, cs] + z * cw_ref[2:3, cs]
            res.append(((gb_ref[g, rows, cs] * y).astype(BF16), z[L - 2:L, :]))
        return res

    def gate_algebra(g, rows):
        li = li_ref[g, rows, :]
        lf = lf_ref[g, rows, :]
        hi, mid, lo = _split3(lf)
        F = (jnp.dot(tril, hi, preferred_element_type=F32)
             + jnp.dot(tril, mid, preferred_element_type=F32)
             + jnp.dot(tril, lo, preferred_element_type=F32))
        r = li - F
        cm = _cummax_rows(r, L)
        mprev = m_sc[g][...]
        mx = jnp.maximum(mprev, cm)
        M = F + mx
        neg_mx = -mx
        w_inter = jnp.exp(mprev - mx)
        em = jnp.exp(-M)
        gs = jnp.exp(r - mx[L - 1:L, :])
        g_inter = w_inter[L - 1:L, :]
        rT = _pad_rows(r, L).T[:, 0:L]
        return dict(neg_mx=neg_mx, w_inter=w_inter, em=em, gs=gs, g_inter=g_inter, rT=rT, m_new=M[L - 1:L, :])

    def pair_scores(g, p, rows):
        ps = slice(p * LANES, (p + 1) * LANES)
        q2 = q_ref[g, rows, ps].astype(BF16)
        kf = k_ref[g, rows, ps] * DQK ** -0.5
        k2 = kf.astype(BF16)
        q_e = jnp.where(low_l, q2, jnp.zeros_like(q2))
        q_o = jnp.where(low_l, jnp.zeros_like(q2), q2)
        q_st = jnp.concatenate([q_e, q_o], axis=0)
        n_row = n_sc[g][p][...]
        n_b = jnp.broadcast_to(n_row, (LANES, LANES)).astype(BF16)
        k_aug = jnp.concatenate([n_b, k2], axis=0)
        sn = lax.dot_general(q_st, k_aug, (((1,), (1,)), ((), ())),
                             preferred_element_type=F32)
        c_full = c_sc[g][p][...]
        qc = jnp.dot(q_st, c_full.astype(BF16), preferred_element_type=F32)
        return dict(kf=kf, k2=k2, n_row=n_row, c_full=c_full, sn=sn, qc=qc)

    def intra_weights(gt, pr, h):
        rs = slice((h % 2) * L, (h % 2 + 1) * L)
        dmat = gt["neg_mx"][:, h:h + 1] + gt["rT"][h:h + 1, :]
        return pr["sn"][rs, LANES:LANES + L] * jnp.exp(jnp.where(causal, dmat, -jnp.inf))

    def head_output(gt, pr, s, intra, o_sig, h):
        hh = h % 2
        rs = slice(hh * L, (hh + 1) * L)
        hs = slice(h * DV, (h + 1) * DV)
        wi = gt["w_inter"][:, h:h + 1]
        num = wi * pr["qc"][rs, hh * DV:(hh + 1) * DV] + intra
        den = wi * pr["sn"][rs, 0:LANES] + jnp.sum(s, axis=-1, keepdims=True)
        hv = num / jnp.maximum(jnp.abs(den), gt["em"][:, h:h + 1])
        ms = jnp.mean(hv * hv, axis=-1, keepdims=True)
        hn = hv * lax.rsqrt(ms + EPS) * hg_ref[0:1, hs]
        return (hn * o_sig).astype(BF16)

    def state_update(gt, pr, v2, p):
        gs = gt["gs"]
        kgw = jnp.where(low_l, gs[:, 2 * p:2 * p + 1], gs[:, 2 * p + 1:2 * p + 2])
        kg = pr["kf"] * kgw
        upd = jnp.dot(_pad_rows(kg, L).T.astype(BF16), _pad_rows(v2, L),
                      preferred_element_type=F32)
        ge = gt["g_inter"][0:1, 2 * p:2 * p + 1]
        go = gt["g_inter"][0:1, 2 * p + 1:2 * p + 2]
        c_new = jnp.where(krow < DQK, ge, go) * pr["c_full"] + upd
        kn = pr["k2"].astype(F32) * kgw.astype(BF16).astype(F32)
        n_new = jnp.where(lane1 < DQK, ge, go) * pr["n_row"] + jnp.sum(kn, axis=0, keepdims=True)
        return c_new, n_new

    lax.fori_loop(0, tb // L, chunk, 0)

    @pl.when(t == nt - 1)
    def _():
        for g in range(G):
            for cg in range(CONV_GROUPS):
                cs = slice(cg * CONV_GROUP_COLS, (cg + 1) * CONV_GROUP_COLS)
                convn_ref[g, :, cs] = carry_sc[g][cg][6:8, :]
            for p in range(N_PAIRS):
                cf = c_sc[g][p][...]
                cn_ref[g, 2 * p] = cf[0:DQK, 0:DV]
                cn_ref[g, 2 * p + 1] = cf[DQK:2 * DQK, DV:2 * DV]
                nn_ref[g, p:p + 1, :] = n_sc[g][p][...]
            mn_ref[g] = m_sc[g][...]


def _mixer(pf, pb, li, lf, conv_w, head_gain, conv0, c0, n0p, m0p, bsz, seq):
    G = 1
    L = min(seq, CHUNK)
    tb = min(seq, 512)
    nt = seq // tb
    kern = functools.partial(_mixer_kernel, tb=tb, L=L, G=G)
    wide = lambda cb: pl.BlockSpec((G, tb, 1024), lambda b, t: (b, t, cb))
    half = lambda cb: pl.BlockSpec((G, tb, 512), lambda b, t: (b, t, cb))
    gate = pl.BlockSpec((G, tb, LANES), lambda b, t: (b, t, 0))
    st = lambda *shape: pl.BlockSpec((G,) + shape, lambda b, t: (b,) + (0,) * len(shape))
    return pl.pallas_call(
        kern,
        grid=(bsz // G, nt),
        in_specs=[
            wide(0), wide(1), wide(2), half(8), half(9), wide(0), wide(3), gate, gate,
            pl.BlockSpec((CONV_WIDTH, CONV_DIM), lambda b, t: (0, 0)),
            pl.BlockSpec((1, MLSTM_DIM), lambda b, t: (0, 0)),
            st(2, CONV_DIM), st(N_HEADS, DQK, DV), st(N_PAIRS, LANES), st(1, LANES),
        ],
        out_specs=[
            pl.BlockSpec((G, tb, D_MODEL), lambda b, t: (b, t, 0)),
            st(2, CONV_DIM), st(N_HEADS, DQK, DV), st(N_PAIRS, LANES), st(1, LANES),
        ],
        out_shape=[
            jax.ShapeDtypeStruct((bsz, seq, D_MODEL), BF16),
            jax.ShapeDtypeStruct((bsz, 2, CONV_DIM), F32),
            jax.ShapeDtypeStruct((bsz, N_HEADS, DQK, DV), F32),
            jax.ShapeDtypeStruct((bsz, N_PAIRS, LANES), F32),
            jax.ShapeDtypeStruct((bsz, 1, LANES), F32),
        ],
        scratch_shapes=(
            [pltpu.VMEM((LANES, 2 * DV), F32)] * (G * N_PAIRS)
            + [pltpu.VMEM((1, LANES), F32)] * (G * N_PAIRS)
            + [pltpu.VMEM((1, LANES), F32)] * G
            + [pltpu.VMEM((8, CONV_GROUP_COLS), F32)] * (G * CONV_GROUPS)
        ),
        compiler_params=_cparams(("arbitrary", "arbitrary")),
        name="mixer",
    )(pf, pf, pf, pf, pf, pb, pf, li, lf, conv_w, head_gain, conv0, c0, n0p, m0p)


def _outproj_kernel(mix_ref, w_ref, x_ref, g_ref, wr_ref, br_ref,
                    xo_ref, xq_ref, ri_ref, rg_ref, cnt_ref, base_sc, x_even, x_odd, *, tm):
    s = pl.program_id(0)

    @pl.when(s == 0)
    def _():
        base_sc[...] = jnp.zeros(base_sc.shape, F32)
        x_odd[...] = jnp.zeros(x_odd.shape, F32)

    refs = (mix_ref, w_ref, x_ref, g_ref, wr_ref, br_ref, xo_ref, xq_ref, ri_ref, rg_ref, cnt_ref, base_sc)

    @pl.when(lax.rem(s, 2) == 0)
    def _():
        _outproj_step(s, x_even, x_odd, *refs, tm=tm)

    @pl.when(lax.rem(s, 2) == 1)
    def _():
        _outproj_step(s, x_odd, x_even, *refs, tm=tm)


def _outproj_step(s, x_this, x_prev, mix_ref, w_ref, x_ref, g_ref, wr_ref, br_ref,
                  xo_ref, xq_ref, ri_ref, rg_ref, cnt_ref, base_sc, *, tm):
    x_new = x_ref[...] + jnp.dot(mix_ref[...], w_ref[...], preferred_element_type=F32)
    xo_ref[...] = x_new
    x_this[...] = x_new

    counted = jnp.where(s > 0, 1.0, 0.0)
    x = x_prev[...]
    ms = jnp.mean(x * x, axis=-1, keepdims=True)
    xn = x * lax.rsqrt(ms + EPS) * g_ref[...]
    xq_ref[...] = _pack_halves(xn)
    lg = jnp.dot(xn.astype(BF16), wr_ref[...], preferred_element_type=F32) + br_ref[...]

    lane = lax.broadcasted_iota(jnp.int32, (tm, LANES), 1)
    lanef = lane.astype(F32)
    big = jnp.float32(1e9)
    ninf = -jnp.inf
    is_g = (lane >= N_EXPERTS) & (lane < N_EXPERTS + N_GROUPS)
    glog = jnp.where(is_g, lg, ninf)
    gmax = jnp.max(glog, axis=1, keepdims=True)
    gi = jnp.min(jnp.where(glog == gmax, lanef, big), axis=1, keepdims=True) - N_EXPERTS
    pgi = 1.0 / jnp.sum(jnp.where(is_g, jnp.exp(lg - gmax), 0.0), axis=1, keepdims=True)
    lo = gi * EXPERTS_PER_GROUP
    in_grp = (lanef >= lo) & (lanef < lo + EXPERTS_PER_GROUP)
    el = jnp.where(in_grp, lg, ninf)
    v1 = jnp.max(el, axis=1, keepdims=True)
    i1 = jnp.min(jnp.where(el == v1, lanef, big), axis=1, keepdims=True)
    el2 = jnp.where(lanef == i1, ninf, el)
    v2 = jnp.max(el2, axis=1, keepdims=True)
    i2 = jnp.min(jnp.where(el2 == v2, lanef, big), axis=1, keepdims=True)
    e21 = jnp.exp(v2 - v1)
    g1 = pgi / (1.0 + e21)
    g2 = pgi * e21 / (1.0 + e21)
    sel1 = lanef == i1
    sel2 = lanef == i2
    oh = jnp.where(sel1 | sel2, counted, 0.0)
    r_i = lax.broadcasted_iota(jnp.int32, (tm, tm), 0)
    c_i = lax.broadcasted_iota(jnp.int32, (tm, tm), 1)
    stril = jnp.where(c_i < r_i, 1.0, 0.0).astype(BF16)
    tot = jnp.dot(stril, oh.astype(BF16), preferred_element_type=F32) + base_sc[...]
    r1 = jnp.sum(jnp.where(sel1, tot, 0.0), axis=1, keepdims=True)
    r2 = jnp.sum(jnp.where(sel2, tot, 0.0), axis=1, keepdims=True)
    base = base_sc[...] + jnp.sum(oh, axis=0, keepdims=True)
    base_sc[...] = base
    cnt_ref[...] = base
    ri = jnp.where(lane == 0, i1, jnp.where(lane == 1, i2, jnp.where(lane == 2, r1,
                   jnp.where(lane == 3, r2, 0.0))))
    ri_t = jnp.concatenate([ri[c * LANES:(c + 1) * LANES, :].T[0:8, :] for c in range(tm // LANES)], axis=1)
    ri_ref[...] = ri_t.astype(jnp.int32)
    rg_ref[...] = jnp.where(lane == 0, g1, jnp.where(lane == 1, g2, 0.0))


def _outproj(mix, w_out, x2d, g, wr, br):
    n = x2d.shape[0]
    tm = min(n, TOKEN_TILE)
    assert tm % LANES == 0
    nt = n // tm
    kern = functools.partial(_outproj_kernel, tm=tm)
    cur = lambda s: jnp.minimum(s, nt - 1)
    prev = lambda s: jnp.maximum(s - 1, 0)
    const = lambda r, c, **kw: pl.BlockSpec((r, c), lambda s: (0, 0), **kw)
    return pl.pallas_call(
        kern,
        grid=(nt + 1,),
        in_specs=[pl.BlockSpec((tm, D_MODEL), lambda s: (cur(s), 0)),
                  const(D_MODEL, D_MODEL, pipeline_mode=pl.Buffered(1)),
                  pl.BlockSpec((tm, D_MODEL), lambda s: (cur(s), 0)), const(1, D_MODEL),
                  const(D_MODEL, LANES), const(1, LANES)],
        out_specs=[pl.BlockSpec((tm, D_MODEL), lambda s: (cur(s), 0)),
                   pl.BlockSpec((tm, D_MODEL // 2), lambda s: (prev(s), 0)),
                   pl.BlockSpec((None, 8, tm), lambda s: (prev(s), 0, 0)),
                   pl.BlockSpec((tm, LANES), lambda s: (prev(s), 0)), const(1, LANES)],
        out_shape=[
            jax.ShapeDtypeStruct((n, D_MODEL), F32),
            jax.ShapeDtypeStruct((n, D_MODEL // 2), jnp.uint32),
            jax.ShapeDtypeStruct((n // tm, 8, tm), jnp.int32),
            jax.ShapeDtypeStruct((n, LANES), F32),
            jax.ShapeDtypeStruct((1, LANES), F32),
        ],
        scratch_shapes=[pltpu.VMEM((1, LANES), F32), pltpu.VMEM((tm, D_MODEL), F32),
                        pltpu.VMEM((tm, D_MODEL), F32)],
        compiler_params=_cparams(("arbitrary",)),
        name="outproj",
    )(mix, w_out, x2d, g, wr, br)


def _row_copy(src_ref, src_row, dst_ref, dst_row, sem):
    return pltpu.make_async_copy(src_ref.at[pl.ds(src_row, 1), :], dst_ref.at[pl.ds(dst_row, 1), :], sem)


def _dispatch_kernel(pad_end_ref, padded_ref, nused_ref, da_ref, db_ref, xa_ref, xb_ref, xs_ref, zero_sc, sem_z,
                     sem, *, bm, nb, tiles_a):
    i = pl.program_id(0)

    @pl.when(i == 0)
    def _():
        zero_sc[...] = jnp.zeros(zero_sc.shape, jnp.uint32)

        def zcopy(start):
            return pltpu.make_async_copy(zero_sc, xs_ref.at[pl.ds(pl.multiple_of(start, bm), bm), :], sem_z)

        def zstart(e, c):
            @pl.when(padded_ref[e] > 0)
            def _():
                zcopy(pad_end_ref[e] - bm).start()
            return c

        def zwait(e, c):
            @pl.when(padded_ref[e] > 0)
            def _():
                zcopy(pad_end_ref[e] - bm).wait()
            return c

        def tstart(b, c):
            zcopy(b * bm).start()
            return c

        def twait(b, c):
            zcopy(b * bm).wait()
            return c

        lax.fori_loop(0, N_EXPERTS, zstart, 0)
        lax.fori_loop(nused_ref[0], nb, tstart, 0)
        lax.fori_loop(0, N_EXPERTS, zwait, 0)
        lax.fori_loop(nused_ref[0], nb, twait, 0)

    def scatter_tile(src_ref, dest_ref):
        tm = src_ref.shape[0]

        def start(j, c):
            _row_copy(src_ref, j, xs_ref, dest_ref[0, 0, j], sem).start()
            _row_copy(src_ref, j, xs_ref, dest_ref[0, 0, tm + j], sem).start()
            return c

        lax.fori_loop(0, tm, start, 0, unroll=8)
        for _ in range(TOP_K):
            pltpu.make_async_copy(src_ref, xs_ref.at[pl.ds(0, tm), :], sem).wait()

    @pl.when(i < tiles_a)
    def _():
        scatter_tile(xa_ref, da_ref)

    @pl.when(i >= tiles_a)
    def _():
        scatter_tile(xb_ref, db_ref)


def _dispatch(xq_a, xq_b, dest_a, dest_b, pad_end, padded, nused, p_rows, bm):
    tm_a, tm_b = dest_a.shape[2] // 2, dest_b.shape[2] // 2
    tiles_a, tiles_b = dest_a.shape[0], dest_b.shape[0]
    kern = functools.partial(_dispatch_kernel, bm=bm, nb=p_rows // bm, tiles_a=tiles_a)
    in_a = lambda i, pe, pd, nu: jnp.minimum(i, tiles_a - 1)
    in_b = lambda i, pe, pd, nu: jnp.maximum(i - tiles_a, 0)
    return pl.pallas_call(
        kern,
        grid_spec=pltpu.PrefetchScalarGridSpec(
            num_scalar_prefetch=3,
            grid=(tiles_a + tiles_b,),
            in_specs=[
                pl.BlockSpec((1, 1, 2 * tm_a), lambda *a: (in_a(*a), 0, 0), memory_space=pltpu.SMEM),
                pl.BlockSpec((1, 1, 2 * tm_b), lambda *a: (in_b(*a), 0, 0), memory_space=pltpu.SMEM),
                pl.BlockSpec((tm_a, ROW_WORDS), lambda *a: (in_a(*a), 0)),
                pl.BlockSpec((tm_b, ROW_WORDS), lambda *a: (in_b(*a), 0)),
            ],
            out_specs=pl.BlockSpec(memory_space=pl.ANY),
            scratch_shapes=[pltpu.VMEM((bm, ROW_WORDS), jnp.uint32), pltpu.SemaphoreType.DMA(()),
                            pltpu.SemaphoreType.DMA(())],
        ),
        out_shape=jax.ShapeDtypeStruct((p_rows, ROW_WORDS), jnp.uint32),
        compiler_params=_cparams(("arbitrary",)),
        name="dispatch",
    )(pad_end, padded, nused, dest_a, dest_b, xq_a, xq_b)


def _experts_kernel(blk_e_ref, nused_ref, xs_ref, w1_ref, w3_ref, w2_ref, ys_ref, w1b, w3b, w2b):
    i = pl.program_id(0)

    @pl.when((i == 0) | (blk_e_ref[i] != blk_e_ref[jnp.maximum(i - 1, 0)]))
    def _():
        def cast_in(r, c):
            sl = pl.ds(pl.multiple_of(r * 256, 256), 256)
            w1b[sl, :] = w1_ref[sl, :].astype(BF16)
            w3b[sl, :] = w3_ref[sl, :].astype(BF16)
            return c

        def cast_out(r, c):
            sl = pl.ds(pl.multiple_of(r * 64, 64), 64)
            w2b[sl, :] = w2_ref[sl, :].astype(BF16)
            return c

        lax.fori_loop(0, D_MODEL // 256, cast_in, 0)
        lax.fori_loop(0, D_EXPERT // 64, cast_out, 0)

    @pl.when(i < nused_ref[0])
    def _():
        lo, hi = _unpack_halves(xs_ref[...])
        lo = lo.astype(BF16)
        hi = hi.astype(BF16)
        h1 = (jnp.dot(lo, w1b[0:ROW_WORDS, :], preferred_element_type=F32)
              + jnp.dot(hi, w1b[ROW_WORDS:D_MODEL, :], preferred_element_type=F32))
        h3 = (jnp.dot(lo, w3b[0:ROW_WORDS, :], preferred_element_type=F32)
              + jnp.dot(hi, w3b[ROW_WORDS:D_MODEL, :], preferred_element_type=F32))
        hb = (h1 * (1.0 / (1.0 + jnp.exp(-h1)))) * h3
        ys_ref[...] = _pack_halves(jnp.dot(hb.astype(BF16), w2b[...], preferred_element_type=F32))

    @pl.when(i >= nused_ref[0])
    def _():
        ys_ref[...] = jnp.zeros(ys_ref.shape, jnp.uint32)


def _experts(xs, blk_e, nused, w1, w3, w2, layer, bm):
    p_rows = xs.shape[0]
    nb = p_rows // bm
    rowmap = lambda i, be, nu: (jnp.minimum(i, nu[0] - 1), 0)
    wmap = lambda i, be, nu: (layer, be[i], 0, 0)
    return pl.pallas_call(
        _experts_kernel,
        grid_spec=pltpu.PrefetchScalarGridSpec(
            num_scalar_prefetch=2,
            grid=(nb,),
            in_specs=[
                pl.BlockSpec((bm, ROW_WORDS), rowmap),
                pl.BlockSpec((None, None, D_MODEL, D_EXPERT), wmap),
                pl.BlockSpec((None, None, D_MODEL, D_EXPERT), wmap),
                pl.BlockSpec((None, None, D_EXPERT, D_MODEL), wmap),
            ],
            out_specs=pl.BlockSpec((bm, ROW_WORDS), lambda i, be, nu: (i, 0)),
            scratch_shapes=[pltpu.VMEM((D_MODEL, D_EXPERT), BF16), pltpu.VMEM((D_MODEL, D_EXPERT), BF16),
                            pltpu.VMEM((D_EXPERT, D_MODEL), BF16)],
        ),
        out_shape=jax.ShapeDtypeStruct((p_rows, ROW_WORDS), jnp.uint32),
        compiler_params=_cparams(("arbitrary",)),
        name="experts",
    )(blk_e, nused, xs, w1, w3, w2)


def _combine_kernel(dest_ref, dnext_ref, x_ref, rg_ref, g_ref, ys_ref, out_ref, ybuf, sem, *, tm, rows, final):
    i = pl.program_id(0)
    nt = pl.num_programs(0)
    slot = lax.rem(i, 2)
    other = 1 - slot

    def gather_rows(d_ref, base, buf_slot):
        for jj in range(rows):
            j = base + jj
            _row_copy(ys_ref, d_ref[0, 0, j], ybuf.at[buf_slot, 0], j, sem.at[buf_slot]).start()
            _row_copy(ys_ref, d_ref[0, 0, tm + j], ybuf.at[buf_slot, 1], j, sem.at[buf_slot]).start()

    @pl.when(i == 0)
    def _():
        def first(r, c):
            gather_rows(dest_ref, pl.multiple_of(r * rows, rows), 0)
            return c

        lax.fori_loop(0, tm // rows, first, 0)

    for k in range(TOP_K):
        pltpu.make_async_copy(ys_ref.at[pl.ds(0, tm), :], ybuf.at[slot, k], sem.at[slot]).wait()

    def combine_rows(base):
        sl = pl.ds(base, rows)
        rg = rg_ref[sl, :]
        g1 = rg[:, 0:1]
        g2 = rg[:, 1:2]
        lo1, hi1 = _unpack_halves(ybuf[slot, 0, sl, :])
        lo2, hi2 = _unpack_halves(ybuf[slot, 1, sl, :])
        xa = x_ref[sl, 0:ROW_WORDS] + (g1 * lo1 + g2 * lo2)
        xb = x_ref[sl, ROW_WORDS:D_MODEL] + (g1 * hi1 + g2 * hi2)
        if final:
            ss = jnp.sum(xa * xa, axis=-1, keepdims=True) + jnp.sum(xb * xb, axis=-1, keepdims=True)
            sc = lax.rsqrt(ss / D_MODEL + EPS)
            xa = xa * sc * g_ref[:, 0:ROW_WORDS]
            xb = xb * sc * g_ref[:, ROW_WORDS:D_MODEL]
        out_ref[sl, 0:ROW_WORDS] = xa
        out_ref[sl, ROW_WORDS:D_MODEL] = xb

    @pl.when(i + 1 < nt)
    def _():
        def body(r, c):
            base = pl.multiple_of(r * rows, rows)
            gather_rows(dnext_ref, base, other)
            combine_rows(base)
            return c

        lax.fori_loop(0, tm // rows, body, 0, unroll=2)

    @pl.when(i + 1 == nt)
    def _():
        def body(r, c):
            combine_rows(pl.multiple_of(r * rows, rows))
            return c

        lax.fori_loop(0, tm // rows, body, 0, unroll=2)


def _combine(x_new, ys, dest3, rg, g_final, final):
    n = x_new.shape[0]
    tm = dest3.shape[2] // 2
    nt = n // tm
    kern = functools.partial(_combine_kernel, tm=tm, rows=32, final=final)
    return pl.pallas_call(
        kern,
        grid=(nt,),
        in_specs=[
            pl.BlockSpec((1, 1, 2 * tm), lambda i: (i, 0, 0), memory_space=pltpu.SMEM),
            pl.BlockSpec((1, 1, 2 * tm), lambda i: (jnp.minimum(i + 1, nt - 1), 0, 0), memory_space=pltpu.SMEM),
            pl.BlockSpec((tm, D_MODEL), lambda i: (i, 0)),
            pl.BlockSpec((tm, LANES), lambda i: (i, 0)),
            pl.BlockSpec((1, D_MODEL), lambda i: (0, 0)),
            pl.BlockSpec(memory_space=pl.ANY),
        ],
        out_specs=pl.BlockSpec((tm, D_MODEL), lambda i: (i, 0)),
        out_shape=jax.ShapeDtypeStruct((n, D_MODEL), F32),
        scratch_shapes=[pltpu.VMEM((2, TOP_K, tm, ROW_WORDS), jnp.uint32), pltpu.SemaphoreType.DMA((2,))],
        compiler_params=_cparams(("arbitrary",)),
        name="combine",
    )(dest3, dest3, x_new, rg, g_final, ys)


def _dest_kernel(first_ref, ri_ref, dest_ref, *, tiles, tm, fold):
    wide = fold * tm
    for t in range(tiles):
        e1, e2 = ri_ref[t, 0:1, :], ri_ref[t, 1:2, :]
        s1 = jnp.zeros_like(e1)
        s2 = jnp.zeros_like(e2)
        for e in range(N_EXPERTS):
            s1 = jnp.where(e1 == e, first_ref[e], s1)
            s2 = jnp.where(e2 == e, first_ref[e], s2)
        out_t, j = divmod(t, fold)
        dest_ref[out_t, :, j * tm:(j + 1) * tm] = s1 + ri_ref[t, 2:3, :]
        dest_ref[out_t, :, wide + j * tm:wide + (j + 1) * tm] = s2 + ri_ref[t, 3:4, :]


def _dest_rows(first, ri_t, row_tile):
    n_tiles, _, tm = ri_t.shape
    fold = row_tile // tm
    tiles = min(n_tiles, 16)
    assert tiles % fold == 0 and n_tiles % tiles == 0
    return pl.pallas_call(
        functools.partial(_dest_kernel, tiles=tiles, tm=tm, fold=fold),
        grid_spec=pltpu.PrefetchScalarGridSpec(
            num_scalar_prefetch=1,
            grid=(n_tiles // tiles,),
            in_specs=[pl.BlockSpec((tiles, 8, tm), lambda i, f: (i, 0, 0))],
            out_specs=pl.BlockSpec((tiles // fold, 1, 2 * row_tile), lambda i, f: (i, 0, 0)),
        ),
        out_shape=jax.ShapeDtypeStruct((n_tiles // fold, 1, 2 * row_tile), jnp.int32),
        compiler_params=_cparams(("arbitrary",)),
        name="dest_rows",
    )(first, ri_t)


def _route_tables(ri_ts, counts, n_total, bm):
    cnts = [c[0, :N_EXPERTS].astype(jnp.int32) for c in counts]
    cnt = sum(cnts)
    padded = (cnt + bm - 1) // bm * bm
    pad_end = jnp.cumsum(padded)
    pad_start = pad_end - padded
    dests, first = [], pad_start
    for ri_t, c in zip(ri_ts, cnts):
        row_tile = min(ri_t.shape[0] * ri_t.shape[2], ROW_TILE)
        dests.append(_dest_rows(first.astype(jnp.int32), ri_t, row_tile))
        first = first + c
    nb = -(-(n_total * TOP_K) // bm) + N_EXPERTS
    nused = (pad_end[-1] // bm).astype(jnp.int32)
    blk = jnp.minimum(jnp.arange(nb, dtype=jnp.int32), nused - 1) * bm
    blk_e = jnp.sum((pad_end[None, :] <= blk[:, None]).astype(jnp.int32), axis=1)
    blk_e = jnp.minimum(blk_e, N_EXPERTS - 1)
    return dests, pad_end.astype(jnp.int32), padded.astype(jnp.int32), blk_e, nused.reshape(1), nb * bm


def _mix_and_route(x2d, bsz, seq, conv0, c0, n0, m0, wts):
    n = bsz * seq
    pf, pb, li, lf = _inproj(x2d, wts["g_mix"], wts["w_main"], wts["wg"], wts["b_if"])
    n0p = n0.reshape(bsz, N_PAIRS, LANES)
    m0p = jnp.pad(m0, ((0, 0), (0, LANES - N_HEADS))).reshape(bsz, 1, LANES)
    by_row = lambda a: a.reshape(bsz, seq, a.shape[-1])
    mix, conv_n, c_n, n_n, m_n = _mixer(by_row(pf), by_row(pb), by_row(li), by_row(lf), wts["conv_w"],
                                         wts["head_gain"], conv0, c0, n0p, m0p, bsz, seq)
    x_new, xq, ri_t, rg, counts = _outproj(mix.reshape(n, D_MODEL), wts["w_out"], x2d, wts["g_ffn"],
                                           wts["wr"], wts["br"])
    states = (conv_n, c_n, n_n.reshape(bsz, N_HEADS, DQK), m_n[:, 0, :N_HEADS])
    return dict(x_new=x_new, xq=xq, ri_t=ri_t, rg=rg, counts=counts, states=states)


def _moe(groups, wts, g_final, final):
    bm = EXPERT_BLOCK
    n_total = sum(g["x_new"].shape[0] for g in groups)
    dests, pad_end, padded, blk_e, nused, p_rows = _route_tables(
        [g["ri_t"] for g in groups], [g["counts"] for g in groups], n_total, bm)
    xs = _dispatch(groups[0]["xq"], groups[1]["xq"], dests[0], dests[1], pad_end, padded, nused, p_rows, bm)
    ys = _experts(xs, blk_e, nused, wts["w1"], wts["w3"], wts["w2"], wts["layer"], bm)
    return [_combine(g["x_new"], ys, d, g["rg"], g_final, final) for g, d in zip(groups, dests)]


def _prep_weights(l, norm_mix, w_in, b_if, conv_w, head_gain, w_out, norm_ffn, w_router_group,
                  b_router_group, w_router_expert, b_router_expert, w1, w3, w2):
    wi = w_in[l]
    w_main = wi.astype(BF16)
    wg = jnp.pad(wi[:, MAIN_COLS:], ((0, 0), (0, LANES - 2 * N_HEADS))).astype(BF16)
    wr = jnp.pad(jnp.concatenate([w_router_expert[l], w_router_group[l]], axis=1),
                 ((0, 0), (0, LANES - N_EXPERTS - N_GROUPS))).astype(BF16)
    br = jnp.pad(jnp.concatenate([b_router_expert[l], b_router_group[l]]),
                 (0, LANES - N_EXPERTS - N_GROUPS)).reshape(1, LANES)
    return dict(
        g_mix=norm_mix[l].reshape(1, D_MODEL),
        w_main=w_main, wg=wg,
        b_if=jnp.pad(b_if[l], (0, LANES - 2 * N_HEADS)).reshape(1, LANES),
        conv_w=conv_w[l], head_gain=head_gain[l].reshape(1, MLSTM_DIM),
        w_out=w_out[l].astype(BF16),
        g_ffn=norm_ffn[l].reshape(1, D_MODEL),
        wr=wr, br=br,
        w1=w1, w3=w3, w2=w2, layer=l,
    )


def _trunks(xs_in, states_in, wts, g_final):
    depth = len(wts)
    shapes = [x.shape for x in xs_in]
    x2ds = [x.reshape(x.shape[0] * x.shape[1], D_MODEL) for x in xs_in]
    new_states = [[] for _ in xs_in]
    for l in range(depth):
        groups = []
        for gi, (x2d, shp, st) in enumerate(zip(x2ds, shapes, states_in)):
            grp = _mix_and_route(x2d, shp[0], shp[1], st[0][l], st[1][l], st[2][l], st[3][l], wts[l])
            new_states[gi].append(grp["states"])
            groups.append(grp)
        x2ds = _moe(groups, wts[l], g_final, l == depth - 1)
    outs = []
    for x2d, shp, sts in zip(x2ds, shapes, new_states):
        outs.append((x2d.reshape(shp),) + tuple(jnp.stack([s[k] for s in sts]) for k in range(4)))
    return outs


def kernel(x_prompt, x_sample, state_conv, state_mlstm_C, state_mlstm_n, state_mlstm_m,
           norm_mix, w_in, b_if, conv_w, head_gain, w_out, norm_ffn,
           w_router_group, b_router_group, w_router_expert, b_router_expert,
           w1, w3, w2, norm_final):
    depth = w_in.shape[0]
    wts = [_prep_weights(l, norm_mix, w_in, b_if, conv_w, head_gain, w_out, norm_ffn, w_router_group,
                         b_router_group, w_router_expert, b_router_expert, w1, w3, w2)
           for l in range(depth)]
    g_final = norm_final.reshape(1, D_MODEL)
    b = x_prompt.shape[0]
    conv0 = jnp.zeros((depth, b, CONV_WIDTH - 1, CONV_DIM), F32)
    c0 = jnp.zeros((depth, b, N_HEADS, DQK, DV), F32)
    n0 = jnp.zeros((depth, b, N_HEADS, DQK), F32)
    m0 = jnp.full((depth, b, N_HEADS), M_INIT, F32)
    (y_p, conv_p, c_p, n_p, m_p), (y_s, conv_s, c_s, n_s, m_s) = _trunks(
        [x_prompt, x_sample],
        [(conv0, c0, n0, m0), (state_conv, state_mlstm_C, state_mlstm_n, state_mlstm_m)],
        wts, g_final)
    return (y_p, y_s, conv_p, c_p, n_p, m_p, conv_s, c_s, n_s, m_s)
```

```python
import functools

import jax
import jax.numpy as jnp
from jax import lax
from jax.experimental import pallas as pl
from jax.experimental.pallas import tpu as pltpu

F32 = jnp.float32
BF16 = jnp.bfloat16

D_MODEL = 2048
CONV_DIM = 1024
N_HEADS = 8
N_PAIRS = N_HEADS // 2
DV = 128
DQK = 64
QK_DIM = N_HEADS * DQK
MLSTM_DIM = N_HEADS * DV
N_GROUPS = 4
EXPERTS_PER_GROUP = 8
N_EXPERTS = 32
TOP_K = 2
D_EXPERT = 512
EPS = 1e-6
M_INIT = -1e30
CONV_WIDTH = 3
LANES = 128
MAIN_COLS = 3 * CONV_DIM + 2 * QK_DIM + 2 * MLSTM_DIM
ROW_WORDS = D_MODEL // 2
PROJ_BLOCK = 1024
F32_SOURCE_BLOCKS = (0, 1, 2, 5, 3)
V_SOURCE_BLOCK = 4
CHUNK = 64
EXPERT_BLOCK = 512
TOKEN_TILE = 512
DISPATCH_TILE = 1024
COMBINE_TILE = 512
CONV_GROUP_COLS = 256
CONV_GROUPS = CONV_DIM // CONV_GROUP_COLS
VMEM_LIMIT = 56 * 1024 * 1024


def _cparams(sem):
    return pltpu.CompilerParams(dimension_semantics=sem, vmem_limit_bytes=VMEM_LIMIT)


def _pack_halves(x):
    half = x.shape[1] // 2
    lo = pltpu.bitcast(x[:, :half].astype(BF16).astype(F32), jnp.uint32)
    hi = pltpu.bitcast(x[:, half:].astype(BF16).astype(F32), jnp.uint32)
    return (lo >> 16) | (hi & jnp.uint32(0xFFFF0000))


def _unpack_halves(w):
    lo = pltpu.bitcast(w << 16, F32)
    hi = pltpu.bitcast(w & jnp.uint32(0xFFFF0000), F32)
    return lo, hi


def _split3(x):
    hi = x.astype(BF16)
    r1 = x - hi.astype(F32)
    mid = r1.astype(BF16)
    lo = (r1 - mid.astype(F32)).astype(BF16)
    return hi, mid, lo


def _inproj_kernel(x_ref, g_ref, w_ref, wg_ref, b_ref, pf_ref, pb_ref, li_ref, lf_ref, xh_ref,
                   *, tm, rows):
    def body(r, c):
        sl = pl.ds(pl.multiple_of(r * rows, rows), rows)
        x = x_ref[sl, :]
        ms = jnp.mean(x * x, axis=-1, keepdims=True)
        xh_ref[sl, :] = (x * lax.rsqrt(ms + EPS) * g_ref[...]).astype(BF16)
        return c

    lax.fori_loop(0, tm // rows, body, 0, unroll=4)
    xh = xh_ref[...]
    gt = jnp.dot(xh, wg_ref[...], preferred_element_type=F32) + b_ref[...]
    lane = lax.broadcasted_iota(jnp.int32, gt.shape, 1)
    valid = lane < N_HEADS
    li_ref[...] = jnp.where(valid, gt, 0.0)
    fg = pltpu.roll(gt, LANES - N_HEADS, axis=1)
    lf = jnp.minimum(fg, 0.0) - jnp.log1p(jnp.exp(-jnp.abs(fg)))
    lf_ref[...] = jnp.where(valid, lf, 0.0)

    def block(src):
        return jnp.dot(xh, w_ref[:, src * PROJ_BLOCK:(src + 1) * PROJ_BLOCK], preferred_element_type=F32)

    for dst, src in enumerate(F32_SOURCE_BLOCKS):
        pf_ref[:, dst * PROJ_BLOCK:(dst + 1) * PROJ_BLOCK] = block(src)
    pb_ref[...] = block(V_SOURCE_BLOCK).astype(BF16)


def _inproj(x2d, g, w_all, wg, b_pad):
    n = x2d.shape[0]
    tm = min(n, 256)
    kern = functools.partial(_inproj_kernel, tm=tm, rows=32)
    const = lambda shape: pl.BlockSpec(shape, lambda i: (0, 0))
    return pl.pallas_call(
        kern,
        grid=(n // tm,),
        in_specs=[
            pl.BlockSpec((tm, D_MODEL), lambda i: (i, 0)),
            const((1, D_MODEL)),
            pl.BlockSpec(w_all.shape, lambda i: (0, 0), pipeline_mode=pl.Buffered(1)),
            const((D_MODEL, LANES)),
            const((1, LANES)),
        ],
        out_specs=[
            pl.BlockSpec((tm, len(F32_SOURCE_BLOCKS) * PROJ_BLOCK), lambda i: (i, 0)),
            pl.BlockSpec((tm, PROJ_BLOCK), lambda i: (i, 0)),
            pl.BlockSpec((tm, LANES), lambda i: (i, 0)),
            pl.BlockSpec((tm, LANES), lambda i: (i, 0)),
        ],
        out_shape=[
            jax.ShapeDtypeStruct((n, len(F32_SOURCE_BLOCKS) * PROJ_BLOCK), F32),
            jax.ShapeDtypeStruct((n, PROJ_BLOCK), BF16),
            jax.ShapeDtypeStruct((n, LANES), F32),
            jax.ShapeDtypeStruct((n, LANES), F32),
        ],
        scratch_shapes=[pltpu.VMEM((tm, D_MODEL), BF16)],
        compiler_params=_cparams(("arbitrary",)),
        name="inproj",
    )(x2d, g, w_all, wg, b_pad)


def _cummax_rows(x, length):
    row = lax.broadcasted_iota(jnp.int32, x.shape, 0)
    d = 1
    while d < length:
        shifted = pltpu.roll(x, d, axis=0)
        x = jnp.maximum(x, jnp.where(row >= d, shifted, -jnp.inf))
        d *= 2
    return x


def _pad_rows(x, length):
    if length == LANES:
        return x
    return jnp.concatenate([x, jnp.zeros((LANES - length, x.shape[1]), x.dtype)], axis=0)


def _mixer_kernel(u_ref, gc_ref, gb_ref, q_ref, k_ref, v_ref, o_ref, li_ref, lf_ref,
                  cw_ref, hg_ref, conv0_ref, c0_ref, n0_ref, m0_ref,
                  mix_ref, convn_ref, cn_ref, nn_ref, mn_ref,
                  *scratch, tb, L, G):
    t = pl.program_id(1)
    nt = pl.num_programs(1)
    n_state = G * N_PAIRS
    c_sc = [scratch[g * N_PAIRS:(g + 1) * N_PAIRS] for g in range(G)]
    n_sc = [scratch[n_state + g * N_PAIRS:n_state + (g + 1) * N_PAIRS] for g in range(G)]
    m_sc = scratch[2 * n_state:2 * n_state + G]
    carry_sc = [scratch[2 * n_state + G + g * CONV_GROUPS:2 * n_state + G + (g + 1) * CONV_GROUPS]
                for g in range(G)]

    @pl.when(t == 0)
    def _():
        zero = jnp.zeros((DQK, DV), F32)
        for g in range(G):
            for p in range(N_PAIRS):
                top = jnp.concatenate([c0_ref[g, 2 * p], zero], axis=1)
                bot = jnp.concatenate([zero, c0_ref[g, 2 * p + 1]], axis=1)
                c_sc[g][p][...] = jnp.concatenate([top, bot], axis=0)
                n_sc[g][p][...] = n0_ref[g, p:p + 1, :]
            m_sc[g][...] = m0_ref[g]
            for cg in range(CONV_GROUPS):
                cs = slice(cg * CONV_GROUP_COLS, (cg + 1) * CONV_GROUP_COLS)
                carry_sc[g][cg][...] = jnp.zeros((8, CONV_GROUP_COLS), F32)
                carry_sc[g][cg][6:8, :] = conv0_ref[g, :, cs]

    row = lax.broadcasted_iota(jnp.int32, (L, L), 0)
    col = lax.broadcasted_iota(jnp.int32, (L, L), 1)
    causal = col <= row
    tril = jnp.where(causal, 1.0, 0.0).astype(BF16)
    lane_l = lax.broadcasted_iota(jnp.int32, (L, LANES), 1)
    low_l = lane_l < DQK
    krow = lax.broadcasted_iota(jnp.int32, (LANES, 2 * DV), 0)
    lane1 = lax.broadcasted_iota(jnp.int32, (1, LANES), 1)

    def chunk(c, carry):
        rows = pl.ds(pl.multiple_of(c * L, L), L)
        units = [(g, p) for g in range(G) for p in range(N_PAIRS)]
        heads = [(g, h) for g in range(G) for h in range(N_HEADS)]
        convs = [conv_chunk(g, rows) for g in range(G)]
        gates = [gate_algebra(g, rows) for g in range(G)]
        outs, states = {}, {}
        for g, p in units:
            pr = pair_scores(g, p, rows)
            v2 = v_ref[g, rows, p * 2 * DV:(p + 1) * 2 * DV]
            for hh in range(2):
                h = 2 * p + hh
                s = intra_weights(gates[g], pr, h)
                intra = jnp.dot(s.astype(BF16), v2[:, hh * DV:(hh + 1) * DV], preferred_element_type=F32)
                o_sig = 1.0 / (1.0 + jnp.exp(-o_ref[g, rows, h * DV:(h + 1) * DV]))
                outs[(g, h)] = head_output(gates[g], pr, s, intra, o_sig, h)
            states[(g, p)] = state_update(gates[g], pr, v2, p)
        for g in range(G):
            for cg in range(CONV_GROUPS):
                y_conv, z_tail = convs[g][cg]
                mix_ref[g, rows, cg * CONV_GROUP_COLS:(cg + 1) * CONV_GROUP_COLS] = y_conv
                carry_sc[g][cg][6:8, :] = z_tail
            m_sc[g][...] = gates[g]["m_new"]
        for g, h in heads:
            mix_ref[g, rows, CONV_DIM + h * DV:CONV_DIM + (h + 1) * DV] = outs[(g, h)]
        for g, p in units:
            c_sc[g][p][...], n_sc[g][p][...] = states[(g, p)]
        return carry

    def conv_chunk(g, rows):
        cw = CONV_GROUP_COLS
        res = []
        for cg in range(CONV_GROUPS):
            cs = slice(cg * cw, (cg + 1) * cw)
            z = gc_ref[g, rows, cs] * u_ref[g, rows, cs]
            prev = carry_sc[g][cg][...]
            p1 = prev[7:8, :]
            p2 = prev[6:7, :]
            rw = lax.broadcasted_iota(jnp.int32, (L, cw), 0)
            z1 = jnp.where(rw >= 1, pltpu.roll(z, 1, axis=0), p1)
            z2 = jnp.where(rw >= 2, pltpu.roll(z, 2, axis=0), jnp.where(rw == 1, p1, p2))
            y = z2 * cw_ref[0:1, cs] + z1 * cw_ref[1:2, cs] + z * cw_ref[2:3, cs]
            res.append(((gb_ref[g, rows, cs] * y).astype(BF16), z[L - 2:L, :]))
        return res

    def gate_algebra(g, rows):
        li = li_ref[g, rows, :]
        lf = lf_ref[g, rows, :]
        hi, mid, lo = _split3(lf)
        F = (jnp.dot(tril, hi, preferred_element_type=F32)
             + jnp.dot(tril, mid, preferred_element_type=F32)
             + jnp.dot(tril, lo, preferred_element_type=F32))
        r = li - F
        cm = _cummax_rows(r, L)
        mprev = m_sc[g][...]
        mx = jnp.maximum(mprev, cm)
        M = F + mx
        neg_mx = -mx
        w_inter = jnp.exp(mprev - mx)
        em = jnp.exp(-M)
        gs = jnp.exp(r - mx[L - 1:L, :])
        g_inter = w_inter[L - 1:L, :]
        rT = _pad_rows(r, L).T[:, 0:L]
        return dict(neg_mx=neg_mx, w_inter=w_inter, em=em, gs=gs, g_inter=g_inter, rT=rT, m_new=M[L - 1:L, :])

    def pair_scores(g, p, rows):
        ps = slice(p * LANES, (p + 1) * LANES)
        q2 = q_ref[g, rows, ps].astype(BF16)
        kf = k_ref[g, rows, ps] * DQK ** -0.5
        k2 = kf.astype(BF16)
        q_e = jnp.where(low_l, q2, jnp.zeros_like(q2))
        q_o = jnp.where(low_l, jnp.zeros_like(q2), q2)
        q_st = jnp.concatenate([q_e, q_o], axis=0)
        n_row = n_sc[g][p][...]
        n_b = jnp.broadcast_to(n_row, (LANES, LANES)).astype(BF16)
        k_aug = jnp.concatenate([n_b, k2], axis=0)
        sn = lax.dot_general(q_st, k_aug, (((1,), (1,)), ((), ())),
                             preferred_element_type=F32)
        c_full = c_sc[g][p][...]
        qc = jnp.dot(q_st, c_full.astype(BF16), preferred_element_type=F32)
        return dict(kf=kf, k2=k2, n_row=n_row, c_full=c_full, sn=sn, qc=qc)

    def intra_weights(gt, pr, h):
        rs = slice((h % 2) * L, (h % 2 + 1) * L)
        dmat = gt["neg_mx"][:, h:h + 1] + gt["rT"][h:h + 1, :]
        return pr["sn"][rs, LANES:LANES + L] * jnp.exp(jnp.where(causal, dmat, -jnp.inf))

    def head_output(gt, pr, s, intra, o_sig, h):
        hh = h % 2
        rs = slice(hh * L, (hh + 1) * L)
        hs = slice(h * DV, (h + 1) * DV)
        wi = gt["w_inter"][:, h:h + 1]
        num = wi * pr["qc"][rs, hh * DV:(hh + 1) * DV] + intra
        den = wi * pr["sn"][rs, 0:LANES] + jnp.sum(s, axis=-1, keepdims=True)
        hv = num / jnp.maximum(jnp.abs(den), gt["em"][:, h:h + 1])
        ms = jnp.mean(hv * hv, axis=-1, keepdims=True)
        hn = hv * lax.rsqrt(ms + EPS) * hg_ref[0:1, hs]
        return (hn * o_sig).astype(BF16)

    def state_update(gt, pr, v2, p):
        gs = gt["gs"]
        kgw = jnp.where(low_l, gs[:, 2 * p:2 * p + 1], gs[:, 2 * p + 1:2 * p + 2])
        kg = pr["kf"] * kgw
        upd = jnp.dot(_pad_rows(kg, L).T.astype(BF16), _pad_rows(v2, L),
                      preferred_element_type=F32)
        ge = gt["g_inter"][0:1, 2 * p:2 * p + 1]
        go = gt["g_inter"][0:1, 2 * p + 1:2 * p + 2]
        c_new = jnp.where(krow < DQK, ge, go) * pr["c_full"] + upd
        kn = pr["k2"].astype(F32) * kgw.astype(BF16).astype(F32)
        n_new = jnp.where(lane1 < DQK, ge, go) * pr["n_row"] + jnp.sum(kn, axis=0, keepdims=True)
        return c_new, n_new

    lax.fori_loop(0, tb // L, chunk, 0)

    @pl.when(t == nt - 1)
    def _():
        for g in range(G):
            for cg in range(CONV_GROUPS):
                cs = slice(cg * CONV_GROUP_COLS, (cg + 1) * CONV_GROUP_COLS)
                convn_ref[g, :, cs] = carry_sc[g][cg][6:8, :]
            for p in range(N_PAIRS):
                cf = c_sc[g][p][...]
                cn_ref[g, 2 * p] = cf[0:DQK, 0:DV]
                cn_ref[g, 2 * p + 1] = cf[DQK:2 * DQK, DV:2 * DV]
                nn_ref[g, p:p + 1, :] = n_sc[g][p][...]
            mn_ref[g] = m_sc[g][...]


def _mixer(pf, pb, li, lf, conv_w, head_gain, conv0, c0, n0p, m0p, bsz, seq):
    G = 1
    L = min(seq, CHUNK)
    tb = min(seq, 512)
    nt = seq // tb
    kern = functools.partial(_mixer_kernel, tb=tb, L=L, G=G)
    wide = lambda cb: pl.BlockSpec((G, tb, 1024), lambda b, t: (b, t, cb))
    half = lambda cb: pl.BlockSpec((G, tb, 512), lambda b, t: (b, t, cb))
    gate = pl.BlockSpec((G, tb, LANES), lambda b, t: (b, t, 0))
    st = lambda *shape: pl.BlockSpec((G,) + shape, lambda b, t: (b,) + (0,) * len(shape))
    return pl.pallas_call(
        kern,
        grid=(bsz // G, nt),
        in_specs=[
            wide(0), wide(1), wide(2), half(8), half(9), wide(0), wide(3), gate, gate,
            pl.BlockSpec((CONV_WIDTH, CONV_DIM), lambda b, t: (0, 0)),
            pl.BlockSpec((1, MLSTM_DIM), lambda b, t: (0, 0)),
            st(2, CONV_DIM), st(N_HEADS, DQK, DV), st(N_PAIRS, LANES), st(1, LANES),
        ],
        out_specs=[
            pl.BlockSpec((G, tb, D_MODEL), lambda b, t: (b, t, 0)),
            st(2, CONV_DIM), st(N_HEADS, DQK, DV), st(N_PAIRS, LANES), st(1, LANES),
        ],
        out_shape=[
            jax.ShapeDtypeStruct((bsz, seq, D_MODEL), BF16),
            jax.ShapeDtypeStruct((bsz, 2, CONV_DIM), F32),
            jax.ShapeDtypeStruct((bsz, N_HEADS, DQK, DV), F32),
            jax.ShapeDtypeStruct((bsz, N_PAIRS, LANES), F32),
            jax.ShapeDtypeStruct((bsz, 1, LANES), F32),
        ],
        scratch_shapes=(
            [pltpu.VMEM((LANES, 2 * DV), F32)] * (G * N_PAIRS)
            + [pltpu.VMEM((1, LANES), F32)] * (G * N_PAIRS)
            + [pltpu.VMEM((1, LANES), F32)] * G
            + [pltpu.VMEM((8, CONV_GROUP_COLS), F32)] * (G * CONV_GROUPS)
        ),
        compiler_params=_cparams(("arbitrary", "arbitrary")),
        name="mixer",
    )(pf, pf, pf, pf, pf, pb, pf, li, lf, conv_w, head_gain, conv0, c0, n0p, m0p)


def _outproj_kernel(mix_ref, w_ref, x_ref, g_ref, wr_ref, br_ref,
                    xo_ref, xq_ref, ri_ref, rg_ref, cnt_ref, base_sc, x_even, x_odd, *, tm):
    s = pl.program_id(0)

    @pl.when(s == 0)
    def _():
        base_sc[...] = jnp.zeros(base_sc.shape, F32)
        x_odd[...] = jnp.zeros(x_odd.shape, F32)

    refs = (mix_ref, w_ref, x_ref, g_ref, wr_ref, br_ref, xo_ref, xq_ref, ri_ref, rg_ref, cnt_ref, base_sc)

    @pl.when(lax.rem(s, 2) == 0)
    def _():
        _outproj_step(s, x_even, x_odd, *refs, tm=tm)

    @pl.when(lax.rem(s, 2) == 1)
    def _():
        _outproj_step(s, x_odd, x_even, *refs, tm=tm)


def _outproj_step(s, x_this, x_prev, mix_ref, w_ref, x_ref, g_ref, wr_ref, br_ref,
                  xo_ref, xq_ref, ri_ref, rg_ref, cnt_ref, base_sc, *, tm):
    x_new = x_ref[...] + jnp.dot(mix_ref[...], w_ref[...], preferred_element_type=F32)
    xo_ref[...] = x_new
    x_this[...] = x_new

    counted = jnp.where(s > 0, 1.0, 0.0)
    x = x_prev[...]
    ms = jnp.mean(x * x, axis=-1, keepdims=True)
    xn = x * lax.rsqrt(ms + EPS) * g_ref[...]
    xq_ref[...] = _pack_halves(xn)
    lg = jnp.dot(xn.astype(BF16), wr_ref[...], preferred_element_type=F32) + br_ref[...]

    lane = lax.broadcasted_iota(jnp.int32, (tm, LANES), 1)
    lanef = lane.astype(F32)
    big = jnp.float32(1e9)
    ninf = -jnp.inf
    is_g = (lane >= N_EXPERTS) & (lane < N_EXPERTS + N_GROUPS)
    glog = jnp.where(is_g, lg, ninf)
    gmax = jnp.max(glog, axis=1, keepdims=True)
    gi = jnp.min(jnp.where(glog == gmax, lanef, big), axis=1, keepdims=True) - N_EXPERTS
    pgi = 1.0 / jnp.sum(jnp.where(is_g, jnp.exp(lg - gmax), 0.0), axis=1, keepdims=True)
    lo = gi * EXPERTS_PER_GROUP
    in_grp = (lanef >= lo) & (lanef < lo + EXPERTS_PER_GROUP)
    el = jnp.where(in_grp, lg, ninf)
    v1 = jnp.max(el, axis=1, keepdims=True)
    i1 = jnp.min(jnp.where(el == v1, lanef, big), axis=1, keepdims=True)
    el2 = jnp.where(lanef == i1, ninf, el)
    v2 = jnp.max(el2, axis=1, keepdims=True)
    i2 = jnp.min(jnp.where(el2 == v2, lanef, big), axis=1, keepdims=True)
    e21 = jnp.exp(v2 - v1)
    g1 = pgi / (1.0 + e21)
    g2 = pgi * e21 / (1.0 + e21)
    sel1 = lanef == i1
    sel2 = lanef == i2
    oh = jnp.where(sel1 | sel2, counted, 0.0)
    r_i = lax.broadcasted_iota(jnp.int32, (tm, tm), 0)
    c_i = lax.broadcasted_iota(jnp.int32, (tm, tm), 1)
    stril = jnp.where(c_i < r_i, 1.0, 0.0).astype(BF16)
    tot = jnp.dot(stril, oh.astype(BF16), preferred_element_type=F32) + base_sc[...]
    r1 = jnp.sum(jnp.where(sel1, tot, 0.0), axis=1, keepdims=True)
    r2 = jnp.sum(jnp.where(sel2, tot, 0.0), axis=1, keepdims=True)
    base = base_sc[...] + jnp.sum(oh, axis=0, keepdims=True)
    base_sc[...] = base
    cnt_ref[...] = base
    ri = jnp.where(lane == 0, i1, jnp.where(lane == 1, i2, jnp.where(lane == 2, r1,
                   jnp.where(lane == 3, r2, 0.0))))
    ri_t = jnp.concatenate([ri[c * LANES:(c + 1) * LANES, :].T[0:8, :] for c in range(tm // LANES)], axis=1)
    ri_ref[...] = ri_t.astype(jnp.int32)
    rg_ref[...] = jnp.where(lane == 0, g1, jnp.where(lane == 1, g2, 0.0))


def _outproj(mix, w_out, x2d, g, wr, br):
    n = x2d.shape[0]
    tm = min(n, TOKEN_TILE)
    assert tm % LANES == 0
    nt = n // tm
    kern = functools.partial(_outproj_kernel, tm=tm)
    cur = lambda s: jnp.minimum(s, nt - 1)
    prev = lambda s: jnp.maximum(s - 1, 0)
    const = lambda r, c, **kw: pl.BlockSpec((r, c), lambda s: (0, 0), **kw)
    return pl.pallas_call(
        kern,
        grid=(nt + 1,),
        in_specs=[pl.BlockSpec((tm, D_MODEL), lambda s: (cur(s), 0)),
                  const(D_MODEL, D_MODEL, pipeline_mode=pl.Buffered(1)),
                  pl.BlockSpec((tm, D_MODEL), lambda s: (cur(s), 0)), const(1, D_MODEL),
                  const(D_MODEL, LANES), const(1, LANES)],
        out_specs=[pl.BlockSpec((tm, D_MODEL), lambda s: (cur(s), 0)),
                   pl.BlockSpec((tm, D_MODEL // 2), lambda s: (prev(s), 0)),
                   pl.BlockSpec((None, 8, tm), lambda s: (prev(s), 0, 0)),
                   pl.BlockSpec((tm, LANES), lambda s: (prev(s), 0)), const(1, LANES)],
        out_shape=[
            jax.ShapeDtypeStruct((n, D_MODEL), F32),
            jax.ShapeDtypeStruct((n, D_MODEL // 2), jnp.uint32),
            jax.ShapeDtypeStruct((n // tm, 8, tm), jnp.int32),
            jax.ShapeDtypeStruct((n, LANES), F32),
            jax.ShapeDtypeStruct((1, LANES), F32),
        ],
        scratch_shapes=[pltpu.VMEM((1, LANES), F32), pltpu.VMEM((tm, D_MODEL), F32),
                        pltpu.VMEM((tm, D_MODEL), F32)],
        compiler_params=_cparams(("arbitrary",)),
        name="outproj",
    )(mix, w_out, x2d, g, wr, br)


def _row_copy(src_ref, src_row, dst_ref, dst_row, sem):
    return pltpu.make_async_copy(src_ref.at[pl.ds(src_row, 1), :], dst_ref.at[pl.ds(dst_row, 1), :], sem)


def _dispatch_kernel(pad_end_ref, padded_ref, nused_ref, da_ref, db_ref, xa_ref, xb_ref, xs_ref, zero_sc, sem_z,
                     sem, *, bm, nb, tiles_a):
    i = pl.program_id(0)

    @pl.when(i == 0)
    def _():
        zero_sc[...] = jnp.zeros(zero_sc.shape, jnp.uint32)

        def zcopy(start):
            return pltpu.make_async_copy(zero_sc, xs_ref.at[pl.ds(pl.multiple_of(start, bm), bm), :], sem_z)

        def zstart(e, c):
            @pl.when(padded_ref[e] > 0)
            def _():
                zcopy(pad_end_ref[e] - bm).start()
            return c

        def zwait(e, c):
            @pl.when(padded_ref[e] > 0)
            def _():
                zcopy(pad_end_ref[e] - bm).wait()
            return c

        def tstart(b, c):
            zcopy(b * bm).start()
            return c

        def twait(b, c):
            zcopy(b * bm).wait()
            return c

        lax.fori_loop(0, N_EXPERTS, zstart, 0)
        lax.fori_loop(nused_ref[0], nb, tstart, 0)
        lax.fori_loop(0, N_EXPERTS, zwait, 0)
        lax.fori_loop(nused_ref[0], nb, twait, 0)

    def scatter_tile(src_ref, dest_ref):
        tm = src_ref.shape[0]

        def start(j, c):
            _row_copy(src_ref, j, xs_ref, dest_ref[0, 0, j], sem).start()
            _row_copy(src_ref, j, xs_ref, dest_ref[0, 0, tm + j], sem).start()
            return c

        lax.fori_loop(0, tm, start, 0, unroll=8)
        for _ in range(TOP_K):
            pltpu.make_async_copy(src_ref, xs_ref.at[pl.ds(0, tm), :], sem).wait()

    @pl.when(i < tiles_a)
    def _():
        scatter_tile(xa_ref, da_ref)

    @pl.when(i >= tiles_a)
    def _():
        scatter_tile(xb_ref, db_ref)


def _dispatch(xq_a, xq_b, dest_a, dest_b, pad_end, padded, nused, p_rows, bm):
    tm_a, tm_b = dest_a.shape[2] // 2, dest_b.shape[2] // 2
    tiles_a, tiles_b = dest_a.shape[0], dest_b.shape[0]
    kern = functools.partial(_dispatch_kernel, bm=bm, nb=p_rows // bm, tiles_a=tiles_a)
    in_a = lambda i, pe, pd, nu: jnp.minimum(i, tiles_a - 1)
    in_b = lambda i, pe, pd, nu: jnp.maximum(i - tiles_a, 0)
    return pl.pallas_call(
        kern,
        grid_spec=pltpu.PrefetchScalarGridSpec(
            num_scalar_prefetch=3,
            grid=(tiles_a + tiles_b,),
            in_specs=[
                pl.BlockSpec((1, 1, 2 * tm_a), lambda *a: (in_a(*a), 0, 0), memory_space=pltpu.SMEM),
                pl.BlockSpec((1, 1, 2 * tm_b), lambda *a: (in_b(*a), 0, 0), memory_space=pltpu.SMEM),
                pl.BlockSpec((tm_a, ROW_WORDS), lambda *a: (in_a(*a), 0)),
                pl.BlockSpec((tm_b, ROW_WORDS), lambda *a: (in_b(*a), 0)),
            ],
            out_specs=pl.BlockSpec(memory_space=pl.ANY),
            scratch_shapes=[pltpu.VMEM((bm, ROW_WORDS), jnp.uint32), pltpu.SemaphoreType.DMA(()),
                            pltpu.SemaphoreType.DMA(())],
        ),
        out_shape=jax.ShapeDtypeStruct((p_rows, ROW_WORDS), jnp.uint32),
        compiler_params=_cparams(("arbitrary",)),
        name="dispatch",
    )(pad_end, padded, nused, dest_a, dest_b, xq_a, xq_b)


def _experts_kernel(blk_e_ref, nused_ref, xs_ref, w1_ref, w3_ref, w2_ref, ys_ref, w1b, w3b, w2b):
    i = pl.program_id(0)

    @pl.when((i == 0) | (blk_e_ref[i] != blk_e_ref[jnp.maximum(i - 1, 0)]))
    def _():
        def cast_in(r, c):
            sl = pl.ds(pl.multiple_of(r * 256, 256), 256)
            w1b[sl, :] = w1_ref[sl, :].astype(BF16)
            w3b[sl, :] = w3_ref[sl, :].astype(BF16)
            return c

        def cast_out(r, c):
            sl = pl.ds(pl.multiple_of(r * 64, 64), 64)
            w2b[sl, :] = w2_ref[sl, :].astype(BF16)
            return c

        lax.fori_loop(0, D_MODEL // 256, cast_in, 0)
        lax.fori_loop(0, D_EXPERT // 64, cast_out, 0)

    @pl.when(i < nused_ref[0])
    def _():
        lo, hi = _unpack_halves(xs_ref[...])
        lo = lo.astype(BF16)
        hi = hi.astype(BF16)
        h1 = (jnp.dot(lo, w1b[0:ROW_WORDS, :], preferred_element_type=F32)
              + jnp.dot(hi, w1b[ROW_WORDS:D_MODEL, :], preferred_element_type=F32))
        h3 = (jnp.dot(lo, w3b[0:ROW_WORDS, :], preferred_element_type=F32)
              + jnp.dot(hi, w3b[ROW_WORDS:D_MODEL, :], preferred_element_type=F32))
        hb = (h1 * (1.0 / (1.0 + jnp.exp(-h1)))) * h3
        ys_ref[...] = _pack_halves(jnp.dot(hb.astype(BF16), w2b[...], preferred_element_type=F32))

    @pl.when(i >= nused_ref[0])
    def _():
        ys_ref[...] = jnp.zeros(ys_ref.shape, jnp.uint32)


def _experts(xs, blk_e, nused, w1, w3, w2, layer, bm):
    p_rows = xs.shape[0]
    nb = p_rows // bm
    rowmap = lambda i, be, nu: (jnp.minimum(i, nu[0] - 1), 0)
    wmap = lambda i, be, nu: (layer, be[i], 0, 0)
    return pl.pallas_call(
        _experts_kernel,
        grid_spec=pltpu.PrefetchScalarGridSpec(
            num_scalar_prefetch=2,
            grid=(nb,),
            in_specs=[
                pl.BlockSpec((bm, ROW_WORDS), rowmap),
                pl.BlockSpec((None, None, D_MODEL, D_EXPERT), wmap),
                pl.BlockSpec((None, None, D_MODEL, D_EXPERT), wmap),
                pl.BlockSpec((None, None, D_EXPERT, D_MODEL), wmap),
            ],
            out_specs=pl.BlockSpec((bm, ROW_WORDS), lambda i, be, nu: (i, 0)),
            scratch_shapes=[pltpu.VMEM((D_MODEL, D_EXPERT), BF16), pltpu.VMEM((D_MODEL, D_EXPERT), BF16),
                            pltpu.VMEM((D_EXPERT, D_MODEL), BF16)],
        ),
        out_shape=jax.ShapeDtypeStruct((p_rows, ROW_WORDS), jnp.uint32),
        compiler_params=_cparams(("arbitrary",)),
        name="experts",
    )(blk_e, nused, xs, w1, w3, w2)


def _combine_kernel(dest_ref, dnext_ref, x_ref, rg_ref, g_ref, ys_ref, out_ref, ybuf, sem, *, tm, rows, final):
    i = pl.program_id(0)
    nt = pl.num_programs(0)
    slot = lax.rem(i, 2)
    other = 1 - slot

    def gather_rows(d_ref, base, buf_slot):
        for jj in range(rows):
            j = base + jj
            _row_copy(ys_ref, d_ref[0, 0, j], ybuf.at[buf_slot, 0], j, sem.at[buf_slot]).start()
            _row_copy(ys_ref, d_ref[0, 0, tm + j], ybuf.at[buf_slot, 1], j, sem.at[buf_slot]).start()

    @pl.when(i == 0)
    def _():
        def first(r, c):
            gather_rows(dest_ref, pl.multiple_of(r * rows, rows), 0)
            return c

        lax.fori_loop(0, tm // rows, first, 0)

    for k in range(TOP_K):
        pltpu.make_async_copy(ys_ref.at[pl.ds(0, tm), :], ybuf.at[slot, k], sem.at[slot]).wait()

    def combine_rows(base):
        sl = pl.ds(base, rows)
        rg = rg_ref[sl, :]
        g1 = rg[:, 0:1]
        g2 = rg[:, 1:2]
        lo1, hi1 = _unpack_halves(ybuf[slot, 0, sl, :])
        lo2, hi2 = _unpack_halves(ybuf[slot, 1, sl, :])
        xa = x_ref[sl, 0:ROW_WORDS] + (g1 * lo1 + g2 * lo2)
        xb = x_ref[sl, ROW_WORDS:D_MODEL] + (g1 * hi1 + g2 * hi2)
        if final:
            ss = jnp.sum(xa * xa, axis=-1, keepdims=True) + jnp.sum(xb * xb, axis=-1, keepdims=True)
            sc = lax.rsqrt(ss / D_MODEL + EPS)
            xa = xa * sc * g_ref[:, 0:ROW_WORDS]
            xb = xb * sc * g_ref[:, ROW_WORDS:D_MODEL]
        out_ref[sl, 0:ROW_WORDS] = xa
        out_ref[sl, ROW_WORDS:D_MODEL] = xb

    @pl.when(i + 1 < nt)
    def _():
        def body(r, c):
            base = pl.multiple_of(r * rows, rows)
            gather_rows(dnext_ref, base, other)
            combine_rows(base)
            return c

        lax.fori_loop(0, tm // rows, body, 0, unroll=2)

    @pl.when(i + 1 == nt)
    def _():
        def body(r, c):
            combine_rows(pl.multiple_of(r * rows, rows))
            return c

        lax.fori_loop(0, tm // rows, body, 0, unroll=2)


def _combine(x_new, ys, dest3, rg, g_final, final):
    n = x_new.shape[0]
    tm = dest3.shape[2] // 2
    nt = n // tm
    kern = functools.partial(_combine_kernel, tm=tm, rows=32, final=final)
    return pl.pallas_call(
        kern,
        grid=(nt,),
        in_specs=[
            pl.BlockSpec((1, 1, 2 * tm), lambda i: (i, 0, 0), memory_space=pltpu.SMEM),
            pl.BlockSpec((1, 1, 2 * tm), lambda i: (jnp.minimum(i + 1, nt - 1), 0, 0), memory_space=pltpu.SMEM),
            pl.BlockSpec((tm, D_MODEL), lambda i: (i, 0)),
            pl.BlockSpec((tm, LANES), lambda i: (i, 0)),
            pl.BlockSpec((1, D_MODEL), lambda i: (0, 0)),
            pl.BlockSpec(memory_space=pl.ANY),
        ],
        out_specs=pl.BlockSpec((tm, D_MODEL), lambda i: (i, 0)),
        out_shape=jax.ShapeDtypeStruct((n, D_MODEL), F32),
        scratch_shapes=[pltpu.VMEM((2, TOP_K, tm, ROW_WORDS), jnp.uint32), pltpu.SemaphoreType.DMA((2,))],
        compiler_params=_cparams(("arbitrary",)),
        name="combine",
    )(dest3, dest3, x_new, rg, g_final, ys)


def _dest_kernel(first_ref, ri_ref, dest_ref, *, tiles, tm, fold):
    wide = fold * tm
    for t in range(tiles):
        e1, e2 = ri_ref[t, 0:1, :], ri_ref[t, 1:2, :]
        s1 = jnp.zeros_like(e1)
        s2 = jnp.zeros_like(e2)
        for e in range(N_EXPERTS):
            s1 = jnp.where(e1 == e, first_ref[e], s1)
            s2 = jnp.where(e2 == e, first_ref[e], s2)
        out_t, j = divmod(t, fold)
        dest_ref[out_t, :, j * tm:(j + 1) * tm] = s1 + ri_ref[t, 2:3, :]
        dest_ref[out_t, :, wide + j * tm:wide + (j + 1) * tm] = s2 + ri_ref[t, 3:4, :]


def _dest_rows(first, ri_t, row_tile):
    n_tiles, _, tm = ri_t.shape
    fold = row_tile // tm
    tiles = min(n_tiles, 16)
    assert tiles % fold == 0 and n_tiles % tiles == 0
    return pl.pallas_call(
        functools.partial(_dest_kernel, tiles=tiles, tm=tm, fold=fold),
        grid_spec=pltpu.PrefetchScalarGridSpec(
            num_scalar_prefetch=1,
            grid=(n_tiles // tiles,),
            in_specs=[pl.BlockSpec((tiles, 8, tm), lambda i, f: (i, 0, 0))],
            out_specs=pl.BlockSpec((tiles // fold, 1, 2 * row_tile), lambda i, f: (i, 0, 0)),
        ),
        out_shape=jax.ShapeDtypeStruct((n_tiles // fold, 1, 2 * row_tile), jnp.int32),
        compiler_params=_cparams(("arbitrary",)),
        name="dest_rows",
    )(first, ri_t)


def _route_tables(ri_ts, counts, n_total, bm):
    cnts = [c[0, :N_EXPERTS].astype(jnp.int32) for c in counts]
    cnt = sum(cnts)
    padded = (cnt + bm - 1) // bm * bm
    pad_end = jnp.cumsum(padded)
    pad_start = pad_end - padded
    dests, first = [], pad_start
    for ri_t, c in zip(ri_ts, cnts):
        n_group = ri_t.shape[0] * ri_t.shape[2]
        dests.append(tuple(_dest_rows(first.astype(jnp.int32), ri_t, min(n_group, tile))
                           for tile in (DISPATCH_TILE, COMBINE_TILE)))
        first = first + c
    nb = -(-(n_total * TOP_K) // bm) + N_EXPERTS
    nused = (pad_end[-1] // bm).astype(jnp.int32)
    blk = jnp.minimum(jnp.arange(nb, dtype=jnp.int32), nused - 1) * bm
    blk_e = jnp.sum((pad_end[None, :] <= blk[:, None]).astype(jnp.int32), axis=1)
    blk_e = jnp.minimum(blk_e, N_EXPERTS - 1)
    return dests, pad_end.astype(jnp.int32), padded.astype(jnp.int32), blk_e, nused.reshape(1), nb * bm


def _mix_and_route(x2d, bsz, seq, conv0, c0, n0, m0, wts):
    n = bsz * seq
    pf, pb, li, lf = _inproj(x2d, wts["g_mix"], wts["w_main"], wts["wg"], wts["b_if"])
    n0p = n0.reshape(bsz, N_PAIRS, LANES)
    m0p = jnp.pad(m0, ((0, 0), (0, LANES - N_HEADS))).reshape(bsz, 1, LANES)
    by_row = lambda a: a.reshape(bsz, seq, a.shape[-1])
    mix, conv_n, c_n, n_n, m_n = _mixer(by_row(pf), by_row(pb), by_row(li), by_row(lf), wts["conv_w"],
                                         wts["head_gain"], conv0, c0, n0p, m0p, bsz, seq)
    x_new, xq, ri_t, rg, counts = _outproj(mix.reshape(n, D_MODEL), wts["w_out"], x2d, wts["g_ffn"],
                                           wts["wr"], wts["br"])
    states = (conv_n, c_n, n_n.reshape(bsz, N_HEADS, DQK), m_n[:, 0, :N_HEADS])
    return dict(x_new=x_new, xq=xq, ri_t=ri_t, rg=rg, counts=counts, states=states)


def _moe(groups, wts, g_final, final):
    bm = EXPERT_BLOCK
    n_total = sum(g["x_new"].shape[0] for g in groups)
    dests, pad_end, padded, blk_e, nused, p_rows = _route_tables(
        [g["ri_t"] for g in groups], [g["counts"] for g in groups], n_total, bm)
    xs = _dispatch(groups[0]["xq"], groups[1]["xq"], dests[0][0], dests[1][0], pad_end, padded, nused, p_rows, bm)
    ys = _experts(xs, blk_e, nused, wts["w1"], wts["w3"], wts["w2"], wts["layer"], bm)
    return [_combine(g["x_new"], ys, d[1], g["rg"], g_final, final) for g, d in zip(groups, dests)]


def _prep_weights(l, norm_mix, w_in, b_if, conv_w, head_gain, w_out, norm_ffn, w_router_group,
                  b_router_group, w_router_expert, b_router_expert, w1, w3, w2):
    wi = w_in[l]
    w_main = wi.astype(BF16)
    wg = jnp.pad(wi[:, MAIN_COLS:], ((0, 0), (0, LANES - 2 * N_HEADS))).astype(BF16)
    wr = jnp.pad(jnp.concatenate([w_router_expert[l], w_router_group[l]], axis=1),
                 ((0, 0), (0, LANES - N_EXPERTS - N_GROUPS))).astype(BF16)
    br = jnp.pad(jnp.concatenate([b_router_expert[l], b_router_group[l]]),
                 (0, LANES - N_EXPERTS - N_GROUPS)).reshape(1, LANES)
    return dict(
        g_mix=norm_mix[l].reshape(1, D_MODEL),
        w_main=w_main, wg=wg,
        b_if=jnp.pad(b_if[l], (0, LANES - 2 * N_HEADS)).reshape(1, LANES),
        conv_w=conv_w[l], head_gain=head_gain[l].reshape(1, MLSTM_DIM),
        w_out=w_out[l].astype(BF16),
        g_ffn=norm_ffn[l].reshape(1, D_MODEL),
        wr=wr, br=br,
        w1=w1, w3=w3, w2=w2, layer=l,
    )


def _trunks(xs_in, states_in, wts, g_final):
    depth = len(wts)
    shapes = [x.shape for x in xs_in]
    x2ds = [x.reshape(x.shape[0] * x.shape[1], D_MODEL) for x in xs_in]
    new_states = [[] for _ in xs_in]
    for l in range(depth):
        groups = []
        for gi, (x2d, shp, st) in enumerate(zip(x2ds, shapes, states_in)):
            grp = _mix_and_route(x2d, shp[0], shp[1], st[0][l], st[1][l], st[2][l], st[3][l], wts[l])
            new_states[gi].append(grp["states"])
            groups.append(grp)
        x2ds = _moe(groups, wts[l], g_final, l == depth - 1)
    outs = []
    for x2d, shp, sts in zip(x2ds, shapes, new_states):
        outs.append((x2d.reshape(shp),) + tuple(jnp.stack([s[k] for s in sts]) for k in range(4)))
    return outs


def kernel(x_prompt, x_sample, state_conv, state_mlstm_C, state_mlstm_n, state_mlstm_m,
           norm_mix, w_in, b_if, conv_w, head_gain, w_out, norm_ffn,
           w_router_group, b_router_group, w_router_expert, b_router_expert,
           w1, w3, w2, norm_final):
    depth = w_in.shape[0]
    wts = [_prep_weights(l, norm_mix, w_in, b_if, conv_w, head_gain, w_out, norm_ffn, w_router_group,
                         b_router_group, w_router_expert, b_router_expert, w1, w3, w2)
           for l in range(depth)]
    g_final = norm_final.reshape(1, D_MODEL)
    b = x_prompt.shape[0]
    conv0 = jnp.zeros((depth, b, CONV_WIDTH - 1, CONV_DIM), F32)
    c0 = jnp.zeros((depth, b, N_HEADS, DQK, DV), F32)
    n0 = jnp.zeros((depth, b, N_HEADS, DQK), F32)
    m0 = jnp.full((depth, b, N_HEADS), M_INIT, F32)
    (y_p, conv_p, c_p, n_p, m_p), (y_s, conv_s, c_s, n_s, m_s) = _trunks(
        [x_prompt, x_sample],
        [(conv0, c0, n0, m0), (state_conv, state_mlstm_C, state_mlstm_n, state_mlstm_m)],
        wts, g_final)
    return (y_p, y_s, conv_p, c_p, n_p, m_p, conv_s, c_s, n_s, m_s)
```

```python
import functools

import jax
import jax.numpy as jnp
from jax import lax
from jax.experimental import pallas as pl
from jax.experimental.pallas import tpu as pltpu

F32 = jnp.float32
BF16 = jnp.bfloat16

D_MODEL = 2048
CONV_DIM = 1024
N_HEADS = 8
N_PAIRS = N_HEADS // 2
DV = 128
DQK = 64
QK_DIM = N_HEADS * DQK
MLSTM_DIM = N_HEADS * DV
N_GROUPS = 4
EXPERTS_PER_GROUP = 8
N_EXPERTS = 32
TOP_K = 2
D_EXPERT = 512
EPS = 1e-6
M_INIT = -1e30
CONV_WIDTH = 3
LANES = 128
MAIN_COLS = 3 * CONV_DIM + 2 * QK_DIM + 2 * MLSTM_DIM
ROW_WORDS = D_MODEL // 2
PROJ_BLOCK = 1024
F32_SOURCE_BLOCKS = (0, 1, 2, 5, 3)
V_SOURCE_BLOCK = 4
CHUNK = 64
EXPERT_BLOCK = 512
TOKEN_TILE = 512
DISPATCH_TILE = 1024
COMBINE_TILE = 512
CONV_GROUP_COLS = 256
CONV_GROUPS = CONV_DIM // CONV_GROUP_COLS
VMEM_LIMIT = 56 * 1024 * 1024


def _cparams(sem):
    return pltpu.CompilerParams(dimension_semantics=sem, vmem_limit_bytes=VMEM_LIMIT)


def _pack_halves(x):
    half = x.shape[1] // 2
    lo = pltpu.bitcast(x[:, :half].astype(BF16).astype(F32), jnp.uint32)
    hi = pltpu.bitcast(x[:, half:].astype(BF16).astype(F32), jnp.uint32)
    return (lo >> 16) | (hi & jnp.uint32(0xFFFF0000))


def _unpack_halves(w):
    lo = pltpu.bitcast(w << 16, F32)
    hi = pltpu.bitcast(w & jnp.uint32(0xFFFF0000), F32)
    return lo, hi


def _split3(x):
    hi = x.astype(BF16)
    r1 = x - hi.astype(F32)
    mid = r1.astype(BF16)
    lo = (r1 - mid.astype(F32)).astype(BF16)
    return hi, mid, lo


def _inproj_kernel(x_ref, g_ref, w_ref, wg_ref, b_ref, pf_ref, pb_ref, li_ref, lf_ref, xh_ref,
                   *, tm, rows):
    def body(r, c):
        sl = pl.ds(pl.multiple_of(r * rows, rows), rows)
        x = x_ref[sl, :]
        ms = jnp.mean(x * x, axis=-1, keepdims=True)
        xh_ref[sl, :] = (x * lax.rsqrt(ms + EPS) * g_ref[...]).astype(BF16)
        return c

    lax.fori_loop(0, tm // rows, body, 0, unroll=4)
    xh = xh_ref[...]
    gt = jnp.dot(xh, wg_ref[...], preferred_element_type=F32) + b_ref[...]
    lane = lax.broadcasted_iota(jnp.int32, gt.shape, 1)
    valid = lane < N_HEADS
    li_ref[...] = jnp.where(valid, gt, 0.0)
    fg = pltpu.roll(gt, LANES - N_HEADS, axis=1)
    lf = jnp.minimum(fg, 0.0) - jnp.log1p(jnp.exp(-jnp.abs(fg)))
    lf_ref[...] = jnp.where(valid, lf, 0.0)

    def block(src):
        return jnp.dot(xh, w_ref[:, src * PROJ_BLOCK:(src + 1) * PROJ_BLOCK], preferred_element_type=F32)

    for dst, src in enumerate(F32_SOURCE_BLOCKS):
        pf_ref[:, dst * PROJ_BLOCK:(dst + 1) * PROJ_BLOCK] = block(src)
    pb_ref[...] = block(V_SOURCE_BLOCK).astype(BF16)


def _inproj(x2d, g, w_all, wg, b_pad):
    n = x2d.shape[0]
    tm = min(n, 256)
    kern = functools.partial(_inproj_kernel, tm=tm, rows=32)
    const = lambda shape: pl.BlockSpec(shape, lambda i: (0, 0))
    return pl.pallas_call(
        kern,
        grid=(n // tm,),
        in_specs=[
            pl.BlockSpec((tm, D_MODEL), lambda i: (i, 0)),
            const((1, D_MODEL)),
            pl.BlockSpec(w_all.shape, lambda i: (0, 0), pipeline_mode=pl.Buffered(1)),
            const((D_MODEL, LANES)),
            const((1, LANES)),
        ],
        out_specs=[
            pl.BlockSpec((tm, len(F32_SOURCE_BLOCKS) * PROJ_BLOCK), lambda i: (i, 0)),
            pl.BlockSpec((tm, PROJ_BLOCK), lambda i: (i, 0)),
            pl.BlockSpec((tm, LANES), lambda i: (i, 0)),
            pl.BlockSpec((tm, LANES), lambda i: (i, 0)),
        ],
        out_shape=[
            jax.ShapeDtypeStruct((n, len(F32_SOURCE_BLOCKS) * PROJ_BLOCK), F32),
            jax.ShapeDtypeStruct((n, PROJ_BLOCK), BF16),
            jax.ShapeDtypeStruct((n, LANES), F32),
            jax.ShapeDtypeStruct((n, LANES), F32),
        ],
        scratch_shapes=[pltpu.VMEM((tm, D_MODEL), BF16)],
        compiler_params=_cparams(("arbitrary",)),
        name="inproj",
    )(x2d, g, w_all, wg, b_pad)


def _cummax_rows(x, length):
    row = lax.broadcasted_iota(jnp.int32, x.shape, 0)
    d = 1
    while d < length:
        shifted = pltpu.roll(x, d, axis=0)
        x = jnp.maximum(x, jnp.where(row >= d, shifted, -jnp.inf))
        d *= 2
    return x


def _pad_rows(x, length):
    if length == LANES:
        return x
    return jnp.concatenate([x, jnp.zeros((LANES - length, x.shape[1]), x.dtype)], axis=0)


def _mixer_kernel(u_ref, gc_ref, gb_ref, q_ref, k_ref, v_ref, o_ref, li_ref, lf_ref,
                  cw_ref, hg_ref, conv0_ref, c0_ref, n0_ref, m0_ref,
                  mix_ref, convn_ref, cn_ref, nn_ref, mn_ref,
                  *scratch, tb, L, G):
    t = pl.program_id(1)
    nt = pl.num_programs(1)
    n_state = G * N_PAIRS
    c_sc = [scratch[g * N_PAIRS:(g + 1) * N_PAIRS] for g in range(G)]
    n_sc = [scratch[n_state + g * N_PAIRS:n_state + (g + 1) * N_PAIRS] for g in range(G)]
    m_sc = scratch[2 * n_state:2 * n_state + G]
    carry_sc = [scratch[2 * n_state + G + g * CONV_GROUPS:2 * n_state + G + (g + 1) * CONV_GROUPS]
                for g in range(G)]

    @pl.when(t == 0)
    def _():
        zero = jnp.zeros((DQK, DV), F32)
        for g in range(G):
            for p in range(N_PAIRS):
                top = jnp.concatenate([c0_ref[g, 2 * p], zero], axis=1)
                bot = jnp.concatenate([zero, c0_ref[g, 2 * p + 1]], axis=1)
                c_sc[g][p][...] = jnp.concatenate([top, bot], axis=0)
                n_sc[g][p][...] = n0_ref[g, p:p + 1, :]
            m_sc[g][...] = m0_ref[g]
            for cg in range(CONV_GROUPS):
                cs = slice(cg * CONV_GROUP_COLS, (cg + 1) * CONV_GROUP_COLS)
                carry_sc[g][cg][...] = jnp.zeros((8, CONV_GROUP_COLS), F32)
                carry_sc[g][cg][6:8, :] = conv0_ref[g, :, cs]

    row = lax.broadcasted_iota(jnp.int32, (L, L), 0)
    col = lax.broadcasted_iota(jnp.int32, (L, L), 1)
    causal = col <= row
    tril = jnp.where(causal, 1.0, 0.0).astype(BF16)
    lane_l = lax.broadcasted_iota(jnp.int32, (L, LANES), 1)
    low_l = lane_l < DQK
    krow = lax.broadcasted_iota(jnp.int32, (LANES, 2 * DV), 0)
    lane1 = lax.broadcasted_iota(jnp.int32, (1, LANES), 1)

    def chunk(c, carry):
        rows = pl.ds(pl.multiple_of(c * L, L), L)
        units = [(g, p) for g in range(G) for p in range(N_PAIRS)]
        heads = [(g, h) for g in range(G) for h in range(N_HEADS)]
        convs = [conv_chunk(g, rows) for g in range(G)]
        gates = [gate_algebra(g, rows) for g in range(G)]
        outs, states = {}, {}
        for g, p in units:
            pr = pair_scores(g, p, rows)
            v2 = v_ref[g, rows, p * 2 * DV:(p + 1) * 2 * DV]
            for hh in range(2):
                h = 2 * p + hh
                s = intra_weights(gates[g], pr, h)
                intra = jnp.dot(s.astype(BF16), v2[:, hh * DV:(hh + 1) * DV], preferred_element_type=F32)
                o_sig = 1.0 / (1.0 + jnp.exp(-o_ref[g, rows, h * DV:(h + 1) * DV]))
                outs[(g, h)] = head_output(gates[g], pr, s, intra, o_sig, h)
            states[(g, p)] = state_update(gates[g], pr, v2, p)
        for g in range(G):
            for cg in range(CONV_GROUPS):
                y_conv, z_tail = convs[g][cg]
                mix_ref[g, rows, cg * CONV_GROUP_COLS:(cg + 1) * CONV_GROUP_COLS] = y_conv
                carry_sc[g][cg][6:8, :] = z_tail
            m_sc[g][...] = gates[g]["m_new"]
        for g, h in heads:
            mix_ref[g, rows, CONV_DIM + h * DV:CONV_DIM + (h + 1) * DV] = outs[(g, h)]
        for g, p in units:
            c_sc[g][p][...], n_sc[g][p][...] = states[(g, p)]
        return carry

    def conv_chunk(g, rows):
        cw = CONV_GROUP_COLS
        res = []
        for cg in range(CONV_GROUPS):
            cs = slice(cg * cw, (cg + 1) * cw)
            z = gc_ref[g, rows, cs] * u_ref[g, rows, cs]
            prev = carry_sc[g][cg][...]
            p1 = prev[7:8, :]
            p2 = prev[6:7, :]
            rw = lax.broadcasted_iota(jnp.int32, (L, cw), 0)
            z1 = jnp.where(rw >= 1, pltpu.roll(z, 1, axis=0), p1)
            z2 = jnp.where(rw >= 2, pltpu.roll(z, 2, axis=0), jnp.where(rw == 1, p1, p2))
            y = z2 * cw_ref[0:1, cs] + z1 * cw_ref[1:2, cs] + z * cw_ref[2:3, cs]
            res.append(((gb_ref[g, rows, cs] * y).astype(BF16), z[L - 2:L, :]))
        return res

    def gate_algebra(g, rows):
        li = li_ref[g, rows, :]
        lf = lf_ref[g, rows, :]
        hi, mid, lo = _split3(lf)
        F = (jnp.dot(tril, hi, preferred_element_type=F32)
             + jnp.dot(tril, mid, preferred_element_type=F32)
             + jnp.dot(tril, lo, preferred_element_type=F32))
        r = li - F
        cm = _cummax_rows(r, L)
        mprev = m_sc[g][...]
        mx = jnp.maximum(mprev, cm)
        M = F + mx
        neg_mx = -mx
        w_inter = jnp.exp(mprev - mx)
        em = jnp.exp(-M)
        gs = jnp.exp(r - mx[L - 1:L, :])
        g_inter = w_inter[L - 1:L, :]
        rT = _pad_rows(r, L).T[:, 0:L]
        return dict(neg_mx=neg_mx, w_inter=w_inter, em=em, gs=gs, g_inter=g_inter, rT=rT, m_new=M[L - 1:L, :])

    def pair_scores(g, p, rows):
        ps = slice(p * LANES, (p + 1) * LANES)
        q2 = q_ref[g, rows, ps].astype(BF16)
        kf = k_ref[g, rows, ps] * DQK ** -0.5
        k2 = kf.astype(BF16)
        q_e = jnp.where(low_l, q2, jnp.zeros_like(q2))
        q_o = jnp.where(low_l, jnp.zeros_like(q2), q2)
        q_st = jnp.concatenate([q_e, q_o], axis=0)
        n_row = n_sc[g][p][...]
        n_b = jnp.broadcast_to(n_row, (LANES, LANES)).astype(BF16)
        k_aug = jnp.concatenate([n_b, k2], axis=0)
        sn = lax.dot_general(q_st, k_aug, (((1,), (1,)), ((), ())),
                             preferred_element_type=F32)
        c_full = c_sc[g][p][...]
        qc = jnp.dot(q_st, c_full.astype(BF16), preferred_element_type=F32)
        return dict(kf=kf, k2=k2, n_row=n_row, c_full=c_full, sn=sn, qc=qc)

    def intra_weights(gt, pr, h):
        rs = slice((h % 2) * L, (h % 2 + 1) * L)
        dmat = gt["neg_mx"][:, h:h + 1] + gt["rT"][h:h + 1, :]
        return pr["sn"][rs, LANES:LANES + L] * jnp.exp(jnp.where(causal, dmat, -jnp.inf))

    def head_output(gt, pr, s, intra, o_sig, h):
        hh = h % 2
        rs = slice(hh * L, (hh + 1) * L)
        hs = slice(h * DV, (h + 1) * DV)
        wi = gt["w_inter"][:, h:h + 1]
        num = wi * pr["qc"][rs, hh * DV:(hh + 1) * DV] + intra
        den = wi * pr["sn"][rs, 0:LANES] + jnp.sum(s, axis=-1, keepdims=True)
        hv = num / jnp.maximum(jnp.abs(den), gt["em"][:, h:h + 1])
        ms = jnp.mean(hv * hv, axis=-1, keepdims=True)
        hn = hv * lax.rsqrt(ms + EPS) * hg_ref[0:1, hs]
        return (hn * o_sig).astype(BF16)

    def state_update(gt, pr, v2, p):
        gs = gt["gs"]
        kgw = jnp.where(low_l, gs[:, 2 * p:2 * p + 1], gs[:, 2 * p + 1:2 * p + 2])
        kg = pr["kf"] * kgw
        upd = jnp.dot(_pad_rows(kg, L).T.astype(BF16), _pad_rows(v2, L),
                      preferred_element_type=F32)
        ge = gt["g_inter"][0:1, 2 * p:2 * p + 1]
        go = gt["g_inter"][0:1, 2 * p + 1:2 * p + 2]
        c_new = jnp.where(krow < DQK, ge, go) * pr["c_full"] + upd
        kn = pr["k2"].astype(F32) * kgw.astype(BF16).astype(F32)
        n_new = jnp.where(lane1 < DQK, ge, go) * pr["n_row"] + jnp.sum(kn, axis=0, keepdims=True)
        return c_new, n_new

    lax.fori_loop(0, tb // L, chunk, 0)

    @pl.when(t == nt - 1)
    def _():
        for g in range(G):
            for cg in range(CONV_GROUPS):
                cs = slice(cg * CONV_GROUP_COLS, (cg + 1) * CONV_GROUP_COLS)
                convn_ref[g, :, cs] = carry_sc[g][cg][6:8, :]
            for p in range(N_PAIRS):
                cf = c_sc[g][p][...]
                cn_ref[g, 2 * p] = cf[0:DQK, 0:DV]
                cn_ref[g, 2 * p + 1] = cf[DQK:2 * DQK, DV:2 * DV]
                nn_ref[g, p:p + 1, :] = n_sc[g][p][...]
            mn_ref[g] = m_sc[g][...]


def _mixer(pf, pb, li, lf, conv_w, head_gain, conv0, c0, n0p, m0p, bsz, seq):
    G = 1
    L = min(seq, CHUNK)
    tb = min(seq, 512)
    nt = seq // tb
    kern = functools.partial(_mixer_kernel, tb=tb, L=L, G=G)
    wide = lambda cb: pl.BlockSpec((G, tb, 1024), lambda b, t: (b, t, cb))
    half = lambda cb: pl.BlockSpec((G, tb, 512), lambda b, t: (b, t, cb))
    gate = pl.BlockSpec((G, tb, LANES), lambda b, t: (b, t, 0))
    st = lambda *shape: pl.BlockSpec((G,) + shape, lambda b, t: (b,) + (0,) * len(shape))
    return pl.pallas_call(
        kern,
        grid=(bsz // G, nt),
        in_specs=[
            wide(0), wide(1), wide(2), half(8), half(9), wide(0), wide(3), gate, gate,
            pl.BlockSpec((CONV_WIDTH, CONV_DIM), lambda b, t: (0, 0)),
            pl.BlockSpec((1, MLSTM_DIM), lambda b, t: (0, 0)),
            st(2, CONV_DIM), st(N_HEADS, DQK, DV), st(N_PAIRS, LANES), st(1, LANES),
        ],
        out_specs=[
            pl.BlockSpec((G, tb, D_MODEL), lambda b, t: (b, t, 0)),
            st(2, CONV_DIM), st(N_HEADS, DQK, DV), st(N_PAIRS, LANES), st(1, LANES),
        ],
        out_shape=[
            jax.ShapeDtypeStruct((bsz, seq, D_MODEL), BF16),
            jax.ShapeDtypeStruct((bsz, 2, CONV_DIM), F32),
            jax.ShapeDtypeStruct((bsz, N_HEADS, DQK, DV), F32),
            jax.ShapeDtypeStruct((bsz, N_PAIRS, LANES), F32),
            jax.ShapeDtypeStruct((bsz, 1, LANES), F32),
        ],
        scratch_shapes=(
            [pltpu.VMEM((LANES, 2 * DV), F32)] * (G * N_PAIRS)
            + [pltpu.VMEM((1, LANES), F32)] * (G * N_PAIRS)
            + [pltpu.VMEM((1, LANES), F32)] * G
            + [pltpu.VMEM((8, CONV_GROUP_COLS), F32)] * (G * CONV_GROUPS)
        ),
        compiler_params=_cparams(("arbitrary", "arbitrary")),
        name="mixer",
    )(pf, pf, pf, pf, pf, pb, pf, li, lf, conv_w, head_gain, conv0, c0, n0p, m0p)


def _outproj_kernel(mix_ref, w_ref, x_ref, g_ref, wr_ref, br_ref,
                    xo_ref, xq_ref, ri_ref, rg_ref, cnt_ref, base_sc, x_even, x_odd, *, tm):
    s = pl.program_id(0)

    @pl.when(s == 0)
    def _():
        base_sc[...] = jnp.zeros(base_sc.shape, F32)
        x_odd[...] = jnp.zeros(x_odd.shape, F32)

    refs = (mix_ref, w_ref, x_ref, g_ref, wr_ref, br_ref, xo_ref, xq_ref, ri_ref, rg_ref, cnt_ref, base_sc)

    @pl.when(lax.rem(s, 2) == 0)
    def _():
        _outproj_step(s, x_even, x_odd, *refs, tm=tm)

    @pl.when(lax.rem(s, 2) == 1)
    def _():
        _outproj_step(s, x_odd, x_even, *refs, tm=tm)


def _outproj_step(s, x_this, x_prev, mix_ref, w_ref, x_ref, g_ref, wr_ref, br_ref,
                  xo_ref, xq_ref, ri_ref, rg_ref, cnt_ref, base_sc, *, tm):
    x_new = x_ref[...] + jnp.dot(mix_ref[...], w_ref[...], preferred_element_type=F32)
    xo_ref[...] = x_new
    x_this[...] = x_new

    counted = jnp.where(s > 0, 1.0, 0.0)
    x = x_prev[...]
    ms = jnp.mean(x * x, axis=-1, keepdims=True)
    xn = x * lax.rsqrt(ms + EPS) * g_ref[...]
    xq_ref[...] = _pack_halves(xn)
    lg = jnp.dot(xn.astype(BF16), wr_ref[...], preferred_element_type=F32) + br_ref[...]

    lane = lax.broadcasted_iota(jnp.int32, (tm, LANES), 1)
    lanef = lane.astype(F32)
    big = jnp.float32(1e9)
    ninf = -jnp.inf
    is_g = (lane >= N_EXPERTS) & (lane < N_EXPERTS + N_GROUPS)
    glog = jnp.where(is_g, lg, ninf)
    gmax = jnp.max(glog, axis=1, keepdims=True)
    gi = jnp.min(jnp.where(glog == gmax, lanef, big), axis=1, keepdims=True) - N_EXPERTS
    pgi = 1.0 / jnp.sum(jnp.where(is_g, jnp.exp(lg - gmax), 0.0), axis=1, keepdims=True)
    lo = gi * EXPERTS_PER_GROUP
    in_grp = (lanef >= lo) & (lanef < lo + EXPERTS_PER_GROUP)
    el = jnp.where(in_grp, lg, ninf)
    v1 = jnp.max(el, axis=1, keepdims=True)
    i1 = jnp.min(jnp.where(el == v1, lanef, big), axis=1, keepdims=True)
    el2 = jnp.where(lanef == i1, ninf, el)
    v2 = jnp.max(el2, axis=1, keepdims=True)
    i2 = jnp.min(jnp.where(el2 == v2, lanef, big), axis=1, keepdims=True)
    e21 = jnp.exp(v2 - v1)
    g1 = pgi / (1.0 + e21)
    g2 = pgi * e21 / (1.0 + e21)
    sel1 = lanef == i1
    sel2 = lanef == i2
    oh = jnp.where(sel1 | sel2, counted, 0.0)
    r_i = lax.broadcasted_iota(jnp.int32, (tm, tm), 0)
    c_i = lax.broadcasted_iota(jnp.int32, (tm, tm), 1)
    stril = jnp.where(c_i < r_i, 1.0, 0.0).astype(BF16)
    tot = jnp.dot(stril, oh.astype(BF16), preferred_element_type=F32) + base_sc[...]
    r1 = jnp.sum(jnp.where(sel1, tot, 0.0), axis=1, keepdims=True)
    r2 = jnp.sum(jnp.where(sel2, tot, 0.0), axis=1, keepdims=True)
    base = base_sc[...] + jnp.sum(oh, axis=0, keepdims=True)
    base_sc[...] = base
    cnt_ref[...] = base
    ri = jnp.where(lane == 0, i1, jnp.where(lane == 1, i2, jnp.where(lane == 2, r1,
                   jnp.where(lane == 3, r2, 0.0))))
    ri_t = jnp.concatenate([ri[c * LANES:(c + 1) * LANES, :].T[0:8, :] for c in range(tm // LANES)], axis=1)
    ri_ref[...] = ri_t.astype(jnp.int32)
    rg_ref[...] = jnp.where(lane == 0, g1, jnp.where(lane == 1, g2, 0.0))


def _outproj(mix, w_out, x2d, g, wr, br):
    n = x2d.shape[0]
    tm = min(n, TOKEN_TILE)
    assert tm % LANES == 0
    nt = n // tm
    kern = functools.partial(_outproj_kernel, tm=tm)
    cur = lambda s: jnp.minimum(s, nt - 1)
    prev = lambda s: jnp.maximum(s - 1, 0)
    const = lambda r, c, **kw: pl.BlockSpec((r, c), lambda s: (0, 0), **kw)
    return pl.pallas_call(
        kern,
        grid=(nt + 1,),
        in_specs=[pl.BlockSpec((tm, D_MODEL), lambda s: (cur(s), 0)),
                  const(D_MODEL, D_MODEL, pipeline_mode=pl.Buffered(1)),
                  pl.BlockSpec((tm, D_MODEL), lambda s: (cur(s), 0)), const(1, D_MODEL),
                  const(D_MODEL, LANES), const(1, LANES)],
        out_specs=[pl.BlockSpec((tm, D_MODEL), lambda s: (cur(s), 0)),
                   pl.BlockSpec((tm, D_MODEL // 2), lambda s: (prev(s), 0)),
                   pl.BlockSpec((None, 8, tm), lambda s: (prev(s), 0, 0)),
                   pl.BlockSpec((tm, LANES), lambda s: (prev(s), 0)), const(1, LANES)],
        out_shape=[
            jax.ShapeDtypeStruct((n, D_MODEL), F32),
            jax.ShapeDtypeStruct((n, D_MODEL // 2), jnp.uint32),
            jax.ShapeDtypeStruct((n // tm, 8, tm), jnp.int32),
            jax.ShapeDtypeStruct((n, LANES), F32),
            jax.ShapeDtypeStruct((1, LANES), F32),
        ],
        scratch_shapes=[pltpu.VMEM((1, LANES), F32), pltpu.VMEM((tm, D_MODEL), F32),
                        pltpu.VMEM((tm, D_MODEL), F32)],
        compiler_params=_cparams(("arbitrary",)),
        name="outproj",
    )(mix, w_out, x2d, g, wr, br)


def _row_copy(src_ref, src_row, dst_ref, dst_row, sem):
    return pltpu.make_async_copy(src_ref.at[pl.ds(src_row, 1), :], dst_ref.at[pl.ds(dst_row, 1), :], sem)


def _dispatch_kernel(pad_end_ref, padded_ref, nused_ref, da_ref, db_ref, xa_ref, xb_ref, xs_ref, zero_sc, sem_z,
                     sem, *, bm, nb, tiles_a):
    i = pl.program_id(0)

    @pl.when(i == 0)
    def _():
        zero_sc[...] = jnp.zeros(zero_sc.shape, jnp.uint32)

        def zcopy(start):
            return pltpu.make_async_copy(zero_sc, xs_ref.at[pl.ds(pl.multiple_of(start, bm), bm), :], sem_z)

        def zstart(e, c):
            @pl.when(padded_ref[e] > 0)
            def _():
                zcopy(pad_end_ref[e] - bm).start()
            return c

        def zwait(e, c):
            @pl.when(padded_ref[e] > 0)
            def _():
                zcopy(pad_end_ref[e] - bm).wait()
            return c

        def tstart(b, c):
            zcopy(b * bm).start()
            return c

        def twait(b, c):
            zcopy(b * bm).wait()
            return c

        lax.fori_loop(0, N_EXPERTS, zstart, 0)
        lax.fori_loop(nused_ref[0], nb, tstart, 0)
        lax.fori_loop(0, N_EXPERTS, zwait, 0)
        lax.fori_loop(nused_ref[0], nb, twait, 0)

    def scatter_tile(src_ref, dest_ref):
        tm = src_ref.shape[0]

        def start(j, c):
            _row_copy(src_ref, j, xs_ref, dest_ref[0, 0, j], sem).start()
            _row_copy(src_ref, j, xs_ref, dest_ref[0, 0, tm + j], sem).start()
            return c

        lax.fori_loop(0, tm, start, 0, unroll=8)
        for _ in range(TOP_K):
            pltpu.make_async_copy(src_ref, xs_ref.at[pl.ds(0, tm), :], sem).wait()

    @pl.when(i < tiles_a)
    def _():
        scatter_tile(xa_ref, da_ref)

    @pl.when(i >= tiles_a)
    def _():
        scatter_tile(xb_ref, db_ref)


def _dispatch(xq_a, xq_b, dest_a, dest_b, pad_end, padded, nused, p_rows, bm):
    tm_a, tm_b = dest_a.shape[2] // 2, dest_b.shape[2] // 2
    tiles_a, tiles_b = dest_a.shape[0], dest_b.shape[0]
    kern = functools.partial(_dispatch_kernel, bm=bm, nb=p_rows // bm, tiles_a=tiles_a)
    in_a = lambda i, pe, pd, nu: jnp.minimum(i, tiles_a - 1)
    in_b = lambda i, pe, pd, nu: jnp.maximum(i - tiles_a, 0)
    return pl.pallas_call(
        kern,
        grid_spec=pltpu.PrefetchScalarGridSpec(
            num_scalar_prefetch=3,
            grid=(tiles_a + tiles_b,),
            in_specs=[
                pl.BlockSpec((1, 1, 2 * tm_a), lambda *a: (in_a(*a), 0, 0), memory_space=pltpu.SMEM),
                pl.BlockSpec((1, 1, 2 * tm_b), lambda *a: (in_b(*a), 0, 0), memory_space=pltpu.SMEM),
                pl.BlockSpec((tm_a, ROW_WORDS), lambda *a: (in_a(*a), 0)),
                pl.BlockSpec((tm_b, ROW_WORDS), lambda *a: (in_b(*a), 0)),
            ],
            out_specs=pl.BlockSpec(memory_space=pl.ANY),
            scratch_shapes=[pltpu.VMEM((bm, ROW_WORDS), jnp.uint32), pltpu.SemaphoreType.DMA(()),
                            pltpu.SemaphoreType.DMA(())],
        ),
        out_shape=jax.ShapeDtypeStruct((p_rows, ROW_WORDS), jnp.uint32),
        compiler_params=_cparams(("arbitrary",)),
        name="dispatch",
    )(pad_end, padded, nused, dest_a, dest_b, xq_a, xq_b)


def _experts_kernel(blk_e_ref, nused_ref, first_ref, slot_ref, next_e_ref, xs_ref, w1_ref, w3_ref, w2_ref, ys_ref,
                    wf1, wf3, wf2, w1b, w3b, w2b, sem, *, layer):
    i = pl.program_id(0)

    def fetch(e, sl):
        return (pltpu.make_async_copy(w1_ref.at[layer, e], wf1.at[sl], sem.at[sl]),
                pltpu.make_async_copy(w3_ref.at[layer, e], wf3.at[sl], sem.at[sl]),
                pltpu.make_async_copy(w2_ref.at[layer, e], wf2.at[sl], sem.at[sl]))

    @pl.when(i == 0)
    def _():
        for cp in fetch(blk_e_ref[0], 0):
            cp.start()

    @pl.when(first_ref[i] == 1)
    def _():
        sl = slot_ref[i]

        @pl.when(next_e_ref[i] >= 0)
        def _():
            for cp in fetch(next_e_ref[i], 1 - sl):
                cp.start()

        for cp in fetch(blk_e_ref[i], sl):
            cp.wait()

        def cast_in(r, c):
            rs = pl.ds(pl.multiple_of(r * 256, 256), 256)
            w1b[rs, :] = wf1[sl, rs, :].astype(BF16)
            w3b[rs, :] = wf3[sl, rs, :].astype(BF16)
            return c

        def cast_out(r, c):
            rs = pl.ds(pl.multiple_of(r * 64, 64), 64)
            w2b[rs, :] = wf2[sl, rs, :].astype(BF16)
            return c

        lax.fori_loop(0, D_MODEL // 256, cast_in, 0)
        lax.fori_loop(0, D_EXPERT // 64, cast_out, 0)

    @pl.when(i < nused_ref[0])
    def _():
        lo, hi = _unpack_halves(xs_ref[...])
        lo = lo.astype(BF16)
        hi = hi.astype(BF16)
        h1 = (jnp.dot(lo, w1b[0:ROW_WORDS, :], preferred_element_type=F32)
              + jnp.dot(hi, w1b[ROW_WORDS:D_MODEL, :], preferred_element_type=F32))
        h3 = (jnp.dot(lo, w3b[0:ROW_WORDS, :], preferred_element_type=F32)
              + jnp.dot(hi, w3b[ROW_WORDS:D_MODEL, :], preferred_element_type=F32))
        hb = (h1 * (1.0 / (1.0 + jnp.exp(-h1)))) * h3
        ys_ref[...] = _pack_halves(jnp.dot(hb.astype(BF16), w2b[...], preferred_element_type=F32))

    @pl.when(i >= nused_ref[0])
    def _():
        ys_ref[...] = jnp.zeros(ys_ref.shape, jnp.uint32)


def _experts(xs, blk_e, nused, w1, w3, w2, layer, bm):
    p_rows = xs.shape[0]
    nb = p_rows // bm
    idx = jnp.arange(nb, dtype=jnp.int32)
    first = (idx < nused[0]) & ((idx == 0) | (blk_e != jnp.roll(blk_e, 1)))
    slot = (jnp.cumsum(first.astype(jnp.int32)) - 1) % 2
    pos = jnp.where(first, idx, nb)
    nxt = lax.cummin(jnp.concatenate([pos[1:], jnp.full((1,), nb, jnp.int32)]), reverse=True)
    next_e = jnp.where(nxt < nb, blk_e[jnp.minimum(nxt, nb - 1)], -1).astype(jnp.int32)
    rowmap = lambda i, *_: (jnp.minimum(i, _[1][0] - 1), 0)
    return pl.pallas_call(
        functools.partial(_experts_kernel, layer=layer),
        grid_spec=pltpu.PrefetchScalarGridSpec(
            num_scalar_prefetch=5,
            grid=(nb,),
            in_specs=[
                pl.BlockSpec((bm, ROW_WORDS), rowmap),
                pl.BlockSpec(memory_space=pl.ANY),
                pl.BlockSpec(memory_space=pl.ANY),
                pl.BlockSpec(memory_space=pl.ANY),
            ],
            out_specs=pl.BlockSpec((bm, ROW_WORDS), lambda i, *_: (i, 0)),
            scratch_shapes=[pltpu.VMEM((2, D_MODEL, D_EXPERT), F32), pltpu.VMEM((2, D_MODEL, D_EXPERT), F32),
                            pltpu.VMEM((2, D_EXPERT, D_MODEL), F32),
                            pltpu.VMEM((D_MODEL, D_EXPERT), BF16), pltpu.VMEM((D_MODEL, D_EXPERT), BF16),
                            pltpu.VMEM((D_EXPERT, D_MODEL), BF16), pltpu.SemaphoreType.DMA((2,))],
        ),
        out_shape=jax.ShapeDtypeStruct((p_rows, ROW_WORDS), jnp.uint32),
        compiler_params=_cparams(("arbitrary",)),
        name="experts",
    )(blk_e, nused, first.astype(jnp.int32), slot.astype(jnp.int32), next_e, xs, w1, w3, w2)


def _combine_kernel(dest_ref, dnext_ref, x_ref, rg_ref, g_ref, ys_ref, out_ref, ybuf, sem, *, tm, rows, final):
    i = pl.program_id(0)
    nt = pl.num_programs(0)
    slot = lax.rem(i, 2)
    other = 1 - slot

    def gather_rows(d_ref, base, buf_slot):
        for jj in range(rows):
            j = base + jj
            _row_copy(ys_ref, d_ref[0, 0, j], ybuf.at[buf_slot, 0], j, sem.at[buf_slot]).start()
            _row_copy(ys_ref, d_ref[0, 0, tm + j], ybuf.at[buf_slot, 1], j, sem.at[buf_slot]).start()

    @pl.when(i == 0)
    def _():
        def first(r, c):
            gather_rows(dest_ref, pl.multiple_of(r * rows, rows), 0)
            return c

        lax.fori_loop(0, tm // rows, first, 0)

    for k in range(TOP_K):
        pltpu.make_async_copy(ys_ref.at[pl.ds(0, tm), :], ybuf.at[slot, k], sem.at[slot]).wait()

    def combine_rows(base):
        sl = pl.ds(base, rows)
        rg = rg_ref[sl, :]
        g1 = rg[:, 0:1]
        g2 = rg[:, 1:2]
        lo1, hi1 = _unpack_halves(ybuf[slot, 0, sl, :])
        lo2, hi2 = _unpack_halves(ybuf[slot, 1, sl, :])
        xa = x_ref[sl, 0:ROW_WORDS] + (g1 * lo1 + g2 * lo2)
        xb = x_ref[sl, ROW_WORDS:D_MODEL] + (g1 * hi1 + g2 * hi2)
        if final:
            ss = jnp.sum(xa * xa, axis=-1, keepdims=True) + jnp.sum(xb * xb, axis=-1, keepdims=True)
            sc = lax.rsqrt(ss / D_MODEL + EPS)
            xa = xa * sc * g_ref[:, 0:ROW_WORDS]
            xb = xb * sc * g_ref[:, ROW_WORDS:D_MODEL]
        out_ref[sl, 0:ROW_WORDS] = xa
        out_ref[sl, ROW_WORDS:D_MODEL] = xb

    @pl.when(i + 1 < nt)
    def _():
        def body(r, c):
            base = pl.multiple_of(r * rows, rows)
            gather_rows(dnext_ref, base, other)
            combine_rows(base)
            return c

        lax.fori_loop(0, tm // rows, body, 0, unroll=2)

    @pl.when(i + 1 == nt)
    def _():
        def body(r, c):
            combine_rows(pl.multiple_of(r * rows, rows))
            return c

        lax.fori_loop(0, tm // rows, body, 0, unroll=2)


def _combine(x_new, ys, dest3, rg, g_final, final):
    n = x_new.shape[0]
    tm = dest3.shape[2] // 2
    nt = n // tm
    kern = functools.partial(_combine_kernel, tm=tm, rows=32, final=final)
    return pl.pallas_call(
        kern,
        grid=(nt,),
        in_specs=[
            pl.BlockSpec((1, 1, 2 * tm), lambda i: (i, 0, 0), memory_space=pltpu.SMEM),
            pl.BlockSpec((1, 1, 2 * tm), lambda i: (jnp.minimum(i + 1, nt - 1), 0, 0), memory_space=pltpu.SMEM),
            pl.BlockSpec((tm, D_MODEL), lambda i: (i, 0)),
            pl.BlockSpec((tm, LANES), lambda i: (i, 0)),
            pl.BlockSpec((1, D_MODEL), lambda i: (0, 0)),
            pl.BlockSpec(memory_space=pl.ANY),
        ],
        out_specs=pl.BlockSpec((tm, D_MODEL), lambda i: (i, 0)),
        out_shape=jax.ShapeDtypeStruct((n, D_MODEL), F32),
        scratch_shapes=[pltpu.VMEM((2, TOP_K, tm, ROW_WORDS), jnp.uint32), pltpu.SemaphoreType.DMA((2,))],
        compiler_params=_cparams(("arbitrary",)),
        name="combine",
    )(dest3, dest3, x_new, rg, g_final, ys)


def _dest_kernel(first_ref, ri_ref, dest_ref, *, tiles, tm, fold):
    wide = fold * tm
    for t in range(tiles):
        e1, e2 = ri_ref[t, 0:1, :], ri_ref[t, 1:2, :]
        s1 = jnp.zeros_like(e1)
        s2 = jnp.zeros_like(e2)
        for e in range(N_EXPERTS):
            s1 = jnp.where(e1 == e, first_ref[e], s1)
            s2 = jnp.where(e2 == e, first_ref[e], s2)
        out_t, j = divmod(t, fold)
        dest_ref[out_t, :, j * tm:(j + 1) * tm] = s1 + ri_ref[t, 2:3, :]
        dest_ref[out_t, :, wide + j * tm:wide + (j + 1) * tm] = s2 + ri_ref[t, 3:4, :]


def _dest_rows(first, ri_t, row_tile):
    n_tiles, _, tm = ri_t.shape
    fold = row_tile // tm
    tiles = min(n_tiles, 16)
    assert tiles % fold == 0 and n_tiles % tiles == 0
    return pl.pallas_call(
        functools.partial(_dest_kernel, tiles=tiles, tm=tm, fold=fold),
        grid_spec=pltpu.PrefetchScalarGridSpec(
            num_scalar_prefetch=1,
            grid=(n_tiles // tiles,),
            in_specs=[pl.BlockSpec((tiles, 8, tm), lambda i, f: (i, 0, 0))],
            out_specs=pl.BlockSpec((tiles // fold, 1, 2 * row_tile), lambda i, f: (i, 0, 0)),
        ),
        out_shape=jax.ShapeDtypeStruct((n_tiles // fold, 1, 2 * row_tile), jnp.int32),
        compiler_params=_cparams(("arbitrary",)),
        name="dest_rows",
    )(first, ri_t)


def _route_tables(ri_ts, counts, n_total, bm):
    cnts = [c[0, :N_EXPERTS].astype(jnp.int32) for c in counts]
    cnt = sum(cnts)
    padded = (cnt + bm - 1) // bm * bm
    pad_end = jnp.cumsum(padded)
    pad_start = pad_end - padded
    dests, first = [], pad_start
    for ri_t, c in zip(ri_ts, cnts):
        n_group = ri_t.shape[0] * ri_t.shape[2]
        dests.append(tuple(_dest_rows(first.astype(jnp.int32), ri_t, min(n_group, tile))
                           for tile in (DISPATCH_TILE, COMBINE_TILE)))
        first = first + c
    nb = -(-(n_total * TOP_K) // bm) + N_EXPERTS
    nused = (pad_end[-1] // bm).astype(jnp.int32)
    blk = jnp.minimum(jnp.arange(nb, dtype=jnp.int32), nused - 1) * bm
    blk_e = jnp.sum((pad_end[None, :] <= blk[:, None]).astype(jnp.int32), axis=1)
    blk_e = jnp.minimum(blk_e, N_EXPERTS - 1)
    return dests, pad_end.astype(jnp.int32), padded.astype(jnp.int32), blk_e, nused.reshape(1), nb * bm


def _mix_and_route(x2d, bsz, seq, conv0, c0, n0, m0, wts):
    n = bsz * seq
    pf, pb, li, lf = _inproj(x2d, wts["g_mix"], wts["w_main"], wts["wg"], wts["b_if"])
    n0p = n0.reshape(bsz, N_PAIRS, LANES)
    m0p = jnp.pad(m0, ((0, 0), (0, LANES - N_HEADS))).reshape(bsz, 1, LANES)
    by_row = lambda a: a.reshape(bsz, seq, a.shape[-1])
    mix, conv_n, c_n, n_n, m_n = _mixer(by_row(pf), by_row(pb), by_row(li), by_row(lf), wts["conv_w"],
                                         wts["head_gain"], conv0, c0, n0p, m0p, bsz, seq)
    x_new, xq, ri_t, rg, counts = _outproj(mix.reshape(n, D_MODEL), wts["w_out"], x2d, wts["g_ffn"],
                                           wts["wr"], wts["br"])
    states = (conv_n, c_n, n_n.reshape(bsz, N_HEADS, DQK), m_n[:, 0, :N_HEADS])
    return dict(x_new=x_new, xq=xq, ri_t=ri_t, rg=rg, counts=counts, states=states)


def _moe(groups, wts, g_final, final):
    bm = EXPERT_BLOCK
    n_total = sum(g["x_new"].shape[0] for g in groups)
    dests, pad_end, padded, blk_e, nused, p_rows = _route_tables(
        [g["ri_t"] for g in groups], [g["counts"] for g in groups], n_total, bm)
    xs = _dispatch(groups[0]["xq"], groups[1]["xq"], dests[0][0], dests[1][0], pad_end, padded, nused, p_rows, bm)
    ys = _experts(xs, blk_e, nused, wts["w1"], wts["w3"], wts["w2"], wts["layer"], bm)
    return [_combine(g["x_new"], ys, d[1], g["rg"], g_final, final) for g, d in zip(groups, dests)]


def _prep_weights(l, norm_mix, w_in, b_if, conv_w, head_gain, w_out, norm_ffn, w_router_group,
                  b_router_group, w_router_expert, b_router_expert, w1, w3, w2):
    wi = w_in[l]
    w_main = wi.astype(BF16)
    wg = jnp.pad(wi[:, MAIN_COLS:], ((0, 0), (0, LANES - 2 * N_HEADS))).astype(BF16)
    wr = jnp.pad(jnp.concatenate([w_router_expert[l], w_router_group[l]], axis=1),
                 ((0, 0), (0, LANES - N_EXPERTS - N_GROUPS))).astype(BF16)
    br = jnp.pad(jnp.concatenate([b_router_expert[l], b_router_group[l]]),
                 (0, LANES - N_EXPERTS - N_GROUPS)).reshape(1, LANES)
    return dict(
        g_mix=norm_mix[l].reshape(1, D_MODEL),
        w_main=w_main, wg=wg,
        b_if=jnp.pad(b_if[l], (0, LANES - 2 * N_HEADS)).reshape(1, LANES),
        conv_w=conv_w[l], head_gain=head_gain[l].reshape(1, MLSTM_DIM),
        w_out=w_out[l].astype(BF16),
        g_ffn=norm_ffn[l].reshape(1, D_MODEL),
        wr=wr, br=br,
        w1=w1, w3=w3, w2=w2, layer=l,
    )


def _trunks(xs_in, states_in, wts, g_final):
    depth = len(wts)
    shapes = [x.shape for x in xs_in]
    x2ds = [x.reshape(x.shape[0] * x.shape[1], D_MODEL) for x in xs_in]
    new_states = [[] for _ in xs_in]
    for l in range(depth):
        groups = []
        for gi, (x2d, shp, st) in enumerate(zip(x2ds, shapes, states_in)):
            grp = _mix_and_route(x2d, shp[0], shp[1], st[0][l], st[1][l], st[2][l], st[3][l], wts[l])
            new_states[gi].append(grp["states"])
            groups.append(grp)
        x2ds = _moe(groups, wts[l], g_final, l == depth - 1)
    outs = []
    for x2d, shp, sts in zip(x2ds, shapes, new_states):
        outs.append((x2d.reshape(shp),) + tuple(jnp.stack([s[k] for s in sts]) for k in range(4)))
    return outs


def kernel(x_prompt, x_sample, state_conv, state_mlstm_C, state_mlstm_n, state_mlstm_m,
           norm_mix, w_in, b_if, conv_w, head_gain, w_out, norm_ffn,
           w_router_group, b_router_group, w_router_expert, b_router_expert,
           w1, w3, w2, norm_final):
    depth = w_in.shape[0]
    wts = [_prep_weights(l, norm_mix, w_in, b_if, conv_w, head_gain, w_out, norm_ffn, w_router_group,
                         b_router_group, w_router_expert, b_router_expert, w1, w3, w2)
           for l in range(depth)]
    g_final = norm_final.reshape(1, D_MODEL)
    b = x_prompt.shape[0]
    conv0 = jnp.zeros((depth, b, CONV_WIDTH - 1, CONV_DIM), F32)
    c0 = jnp.zeros((depth, b, N_HEADS, DQK, DV), F32)
    n0 = jnp.zeros((depth, b, N_HEADS, DQK), F32)
    m0 = jnp.full((depth, b, N_HEADS), M_INIT, F32)
    (y_p, conv_p, c_p, n_p, m_p), (y_s, conv_s, c_s, n_s, m_s) = _trunks(
        [x_prompt, x_sample],
        [(conv0, c0, n0, m0), (state_conv, state_mlstm_C, state_mlstm_n, state_mlstm_m)],
        wts, g_final)
    return (y_p, y_s, conv_p, c_p, n_p, m_p, conv_s, c_s, n_s, m_s)
```
